```python
import jax
import jax.numpy as jnp
from jax import lax
import numpy as np

D_MODEL = 1024
BATCH = 4
SEQ = 4096
DEPTH = 2
DEC_BATCH = 32
DEC_SEQ = 1
PAST_LEN = 8192
PAGE_SIZE = 128

HEAD_DIM = 64
ROT_DIM = HEAD_DIM // 4
ROPE_THETA = 500000.0
NSA_HEADS = 8
NSA_GROUPS = 2
NSA_HPG = NSA_HEADS // NSA_GROUPS
CMP_STRIDE = 16
CMP_BLOCK = 2 * CMP_STRIDE
SEL_BLOCK = 64
TOP_N = 16
WINDOW = 512
SB_HEADS = 4
FOX_HEADS = 4
FORGET_BIAS_INIT = 3.0
Q_BLOCK = 128
NSA_W = NSA_HEADS * HEAD_DIM
NSA_KV_W = NSA_GROUPS * HEAD_DIM
SB_W = SB_HEADS * HEAD_DIM
FOX_W = FOX_HEADS * HEAD_DIM
N_IN = NSA_W + 6 * NSA_KV_W + 3 * NSA_HEADS + 3 * SB_W + 3 * FOX_W + FOX_HEADS + 3 * D_MODEL
D_FF = ((8 * D_MODEL + 3 * 256 - 1) // (3 * 256)) * 256
SEL_FORCE_SCORE = 1.0e4
NEG_INF = -1.0e30
RMS_EPS = 1e-6

kernel_name = 'nsa_stickbreak_fox_hybrid_step'


def rmsnorm(x, g):
    xf = x.astype(jnp.float32)
    y = xf * lax.rsqrt(jnp.mean(xf * xf, axis=-1, keepdims=True) + RMS_EPS)
    return (y * g.astype(jnp.float32)).astype(x.dtype)


def rope(x, pos):
    half = ROT_DIM // 2
    inv = ROPE_THETA ** (-jnp.arange(half, dtype=jnp.float32) / half)
    ang = pos.astype(jnp.float32)[:, None] * inv[None, :]
    cos = jnp.cos(ang)[:, None, :]
    sin = jnp.sin(ang)[:, None, :]
    xr = x[..., :ROT_DIM].astype(jnp.float32)
    x1, x2 = xr[..., :half], xr[..., half:]
    rot = jnp.concatenate([x1 * cos - x2 * sin, x2 * cos + x1 * sin], axis=-1)
    return jnp.concatenate([rot.astype(x.dtype), x[..., ROT_DIM:]], axis=-1)


def masked_softmax(s, mask):
    s = jnp.where(mask, s.astype(jnp.float32), NEG_INF)
    e = jnp.where(mask, jnp.exp(s - jnp.max(s, axis=-1, keepdims=True)), 0.0)
    return e / jnp.maximum(jnp.sum(e, axis=-1, keepdims=True), 1e-30)


def pad_rows(a, length):
    return jnp.pad(a, [(0, 0), (0, length - a.shape[1])] + [(0, 0)] * (a.ndim - 2))


def compress(k, pos_emb, w):
    B, L, G, dh = k.shape
    ch = k.reshape(B, L // CMP_STRIDE, CMP_STRIDE, G, dh)
    blocks = jnp.concatenate([ch[:, :-1], ch[:, 1:]], axis=2) + pos_emb[None, None, :, None, :]
    return jnp.einsum('bnlgd,lde->bnge', blocks, w.reshape(CMP_BLOCK, HEAD_DIM, HEAD_DIM))


def sel_blocks(kv):
    B, L = kv.shape[:2]
    return kv.reshape(B, L // SEL_BLOCK, SEL_BLOCK, 2, NSA_GROUPS, HEAD_DIM).transpose(0, 4, 1, 2, 3, 5)


def project(h, pos, w_in, b_forget):
    B, T, _ = h.shape
    p = jnp.einsum('btd,dn->btn', h, w_in)
    sizes = (NSA_W, 2 * NSA_KV_W, 2 * NSA_KV_W, 2 * NSA_KV_W, 3 * NSA_HEADS,
             3 * SB_W, 3 * FOX_W, FOX_HEADS, 3 * D_MODEL)
    parts, start = [], 0
    for s in sizes:
        parts.append(p[..., start:start + s])
        start += s
    pq, pcmp, psel, pwin, pgate, psb, pfox, pf, pmerge = parts
    q_a = rope(pq.reshape(B, T, NSA_HEADS, HEAD_DIM), pos)

    def kv_rot(a):
        a = a.reshape(B, T, 2, NSA_GROUPS, HEAD_DIM)
        return jnp.stack([rope(a[:, :, 0], pos), a[:, :, 1]], axis=2)

    kv_cmp, kv_sel, kv_win = kv_rot(pcmp), kv_rot(psel), kv_rot(pwin)
    g_a = jax.nn.sigmoid(pgate.astype(jnp.float32)).reshape(B, T, NSA_HEADS, 3)
    qkv_b = psb.reshape(B, T, 3, SB_HEADS, HEAD_DIM)
    qkv_c = pfox.reshape(B, T, 3, FOX_HEADS, HEAD_DIM)
    logf = jax.nn.log_sigmoid(pf.astype(jnp.float32) + b_forget.astype(jnp.float32))
    g_m = jax.nn.sigmoid(pmerge.astype(jnp.float32)).reshape(B, T, 3, D_MODEL)
    return q_a, kv_cmp, kv_sel, kv_win, g_a, qkv_b, qkv_c, logf, g_m


def nsa_attend(q, pos_q, gates, ck, cv, kvb, k_win, v_win, pos_win):
    B, T = q.shape[:2]
    scale = HEAD_DIM ** -0.5
    qg = q.reshape(B, T, NSA_GROUPS, NSA_HPG, HEAD_DIM)
    n_cmp = ck.shape[1]
    c_end = jnp.arange(n_cmp, dtype=jnp.int32) * CMP_STRIDE + (CMP_BLOCK - 1)
    c_mask = c_end[None, :] <= pos_q[:, None]
    p_c = masked_softmax(jnp.einsum('btghd,bngd->bghtn', qg, ck).astype(jnp.float32) * scale, c_mask)
    o_c = jnp.einsum('bghtn,bngd->btghd', p_c, cv.astype(jnp.float32))
    p_grp = jnp.sum(p_c, axis=2)
    chunk = (jnp.pad(p_grp, ((0, 0), (0, 0), (0, 0), (1, 0)))
             + jnp.pad(p_grp, ((0, 0), (0, 0), (0, 0), (0, 1))))
    n_sel = kvb.shape[2]
    imp = chunk.reshape(B, NSA_GROUPS, T, n_sel, SEL_BLOCK // CMP_STRIDE).sum(-1)
    blk = jnp.arange(n_sel, dtype=jnp.int32)[None, :]
    cur = (pos_q // SEL_BLOCK)[:, None]
    valid = blk * SEL_BLOCK <= pos_q[:, None]
    forced = (blk == 0) | (blk == cur) | (blk == cur - 1)
    score = jnp.where(valid, jnp.where(forced, SEL_FORCE_SCORE, imp), -1.0)
    n_top = min(TOP_N, n_sel)
    _, idx = lax.top_k(score, n_top)
    bi = jnp.arange(B)[:, None, None, None]
    gi = jnp.arange(NSA_GROUPS)[None, :, None, None]
    kv_g = kvb[bi, gi, idx]
    n_key = n_top * SEL_BLOCK
    k_s = kv_g[..., 0, :].reshape(B, NSA_GROUPS, T, n_key, HEAD_DIM)
    v_s = kv_g[..., 1, :].reshape(B, NSA_GROUPS, T, n_key, HEAD_DIM)
    kpos = (idx[..., None] * SEL_BLOCK + jnp.arange(SEL_BLOCK, dtype=jnp.int32)).reshape(B, NSA_GROUPS, T, n_key)
    s_mask = (kpos <= pos_q[None, None, :, None])[:, :, None]
    p_s = masked_softmax(jnp.einsum('btghd,bgtkd->bghtk', qg, k_s).astype(jnp.float32) * scale, s_mask)
    o_s = jnp.einsum('bghtk,bgtkd->btghd', p_s, v_s.astype(jnp.float32))
    w_mask = ((pos_win[None, :] <= pos_q[:, None]) & (pos_q[:, None] - pos_win[None, :] <= WINDOW)
              & (pos_win[None, :] >= 0))
    p_w = masked_softmax(jnp.einsum('btghd,bsgd->bghts', qg, k_win).astype(jnp.float32) * scale, w_mask)
    o_w = jnp.einsum('bghts,bsgd->btghd', p_w, v_win.astype(jnp.float32))
    g = gates.reshape(B, T, NSA_GROUPS, NSA_HPG, 3)
    o = g[..., 0:1] * o_c + g[..., 1:2] * o_s + g[..., 2:3] * o_w
    return o.reshape(B, T, NSA_W).astype(q.dtype)


def stick_breaking(q, k, v, pos_q, pos_k):
    B, T = q.shape[:2]
    z = jnp.einsum('bthd,bshd->bhts', q, k).astype(jnp.float32) * (HEAD_DIM ** -0.5)
    mask = pos_k[None, :] < pos_q[:, None]
    log_keep = jnp.where(mask, jax.nn.log_sigmoid(-z), 0.0)
    after = lax.cumsum(log_keep, axis=3, reverse=True) - log_keep
    w = jnp.where(mask, jnp.exp(jax.nn.log_sigmoid(z) + after), 0.0)
    o = jnp.einsum('bhts,bshd->bthd', w, v.astype(jnp.float32))
    return o.reshape(B, T, SB_W).astype(q.dtype)


def forgetting_attn(q, k, v, cum_q, cum_k, pos_q, pos_k):
    B, T = q.shape[:2]
    s = (jnp.einsum('bthd,bshd->bhts', q, k).astype(jnp.float32) * (HEAD_DIM ** -0.5)
         + jnp.swapaxes(cum_q, 1, 2)[:, :, :, None] - jnp.swapaxes(cum_k, 1, 2)[:, :, None, :])
    mask = pos_k[None, :] <= pos_q[:, None]
    p = masked_softmax(s, mask)
    o = jnp.einsum('bhts,bshd->bthd', p, v.astype(jnp.float32))
    return o.reshape(B, T, FOX_W).astype(q.dtype)


def merge_branches(o_a, o_b, o_c, g_m, w_up_nsa, w_up_sb, w_up_fox, w_out):
    u = (g_m[:, :, 0] * jnp.einsum('btc,cd->btd', o_a, w_up_nsa)
         + g_m[:, :, 1] * jnp.einsum('btc,cd->btd', o_b, w_up_sb)
         + g_m[:, :, 2] * jnp.einsum('btc,cd->btd', o_c, w_up_fox))
    return jnp.einsum('btd,de->bte', u.astype(w_out.dtype), w_out)


def swiglu(h, w_gate, w_up, w_down):
    return jnp.einsum('btf,fd->btd', jax.nn.silu(jnp.einsum('btd,df->btf', h, w_gate))
                      * jnp.einsum('btd,df->btf', h, w_up), w_down)


def prompt_mixers(h, w_in, b_forget, cmp_pos, w_cmp_k, w_cmp_v):
    B, S, _ = h.shape
    pos = jnp.arange(S, dtype=jnp.int32)
    q_a, kv_cmp, kv_sel, kv_win, g_a, qkv_b, qkv_c, logf, g_m = project(h, pos, w_in, b_forget)
    l_pad = -(-S // SEL_BLOCK) * SEL_BLOCK
    kv_cmp_p = pad_rows(kv_cmp, l_pad)
    ck = compress(kv_cmp_p[:, :, 0], cmp_pos, w_cmp_k)
    cv = compress(kv_cmp_p[:, :, 1], cmp_pos, w_cmp_v)
    kvb = sel_blocks(pad_rows(kv_sel, l_pad))
    win_pad = jnp.pad(kv_win, ((0, 0), (WINDOW, 0), (0, 0), (0, 0), (0, 0)))
    q_b, k_b, v_b = qkv_b[:, :, 0], qkv_b[:, :, 1], qkv_b[:, :, 2]
    q_c, k_c, v_c = qkv_c[:, :, 0], qkv_c[:, :, 1], qkv_c[:, :, 2]
    cum = lax.cumsum(logf, axis=1)

    def block(i):
        q0 = i * Q_BLOCK
        pos_q = q0 + jnp.arange(Q_BLOCK, dtype=jnp.int32)
        rows = lambda a: lax.dynamic_slice_in_dim(a, q0, Q_BLOCK, axis=1)
        wkv = lax.dynamic_slice_in_dim(win_pad, q0, WINDOW + Q_BLOCK, axis=1)
        pos_w = q0 - WINDOW + jnp.arange(WINDOW + Q_BLOCK, dtype=jnp.int32)
        o_a = nsa_attend(rows(q_a), pos_q, rows(g_a), ck, cv, kvb, wkv[:, :, 0], wkv[:, :, 1], pos_w)
        o_b = stick_breaking(rows(q_b), k_b, v_b, pos_q, pos)
        o_c = forgetting_attn(rows(q_c), k_c, v_c, rows(cum), cum, pos_q, pos)
        return o_a, o_b, o_c

    o_a, o_b, o_c = lax.map(block, jnp.arange(S // Q_BLOCK, dtype=jnp.int32))
    unblock = lambda o: jnp.swapaxes(o, 0, 1).reshape(B, S, o.shape[-1])
    win_keep = min(WINDOW, S)
    states = (kv_cmp, kv_sel, qkv_b[:, :, 1:], qkv_c[:, :, 1:], logf, kv_win[:, S - win_keep:])
    return unblock(o_a), unblock(o_b), unblock(o_c), g_m, states


def sample_mixers(h, c_cmp, c_sel, c_sb, c_fox, c_logf, s_win, page_table,
                  w_in, b_forget, cmp_pos, w_cmp_k, w_cmp_v):
    Bd, T, _ = h.shape
    pos = PAST_LEN + jnp.arange(T, dtype=jnp.int32)
    q_a, kv_cmp, kv_sel, kv_win, g_a, qkv_b, qkv_c, logf, g_m = project(h, pos, w_in, b_forget)

    def past(pool):
        g = pool[page_table]
        return g.reshape((Bd, g.shape[1] * g.shape[2]) + pool.shape[2:])

    L = PAST_LEN + T
    l_pad = -(-L // SEL_BLOCK) * SEL_BLOCK
    all_cmp = pad_rows(jnp.concatenate([past(c_cmp), kv_cmp], axis=1), l_pad)
    ck = compress(all_cmp[:, :, 0], cmp_pos, w_cmp_k)
    cv = compress(all_cmp[:, :, 1], cmp_pos, w_cmp_v)
    kvb = sel_blocks(pad_rows(jnp.concatenate([past(c_sel), kv_sel], axis=1), l_pad))
    wb = s_win.shape[1]
    win_all = jnp.concatenate([s_win, kv_win], axis=1)
    pos_w = PAST_LEN - wb + jnp.arange(wb + T, dtype=jnp.int32)
    o_a = nsa_attend(q_a, pos, g_a, ck, cv, kvb, win_all[:, :, 0], win_all[:, :, 1], pos_w)
    pos_k = jnp.arange(L, dtype=jnp.int32)
    sb_all = jnp.concatenate([past(c_sb), qkv_b[:, :, 1:]], axis=1)
    o_b = stick_breaking(qkv_b[:, :, 0], sb_all[:, :, 0], sb_all[:, :, 1], pos, pos_k)
    fox_all = jnp.concatenate([past(c_fox), qkv_c[:, :, 1:]], axis=1)
    cum = lax.cumsum(jnp.concatenate([past(c_logf).astype(jnp.float32), logf], axis=1), axis=1)
    o_c = forgetting_attn(qkv_c[:, :, 0], fox_all[:, :, 0], fox_all[:, :, 1], cum[:, PAST_LEN:], cum, pos, pos_k)
    states = (kv_cmp, kv_sel, qkv_b[:, :, 1:], qkv_c[:, :, 1:], logf, win_all[:, T:])
    return o_a, o_b, o_c, g_m, states


def setup_inputs(seed: int = 0) -> dict:
    key = jax.random.key(seed)
    ks = jax.random.split(key, 32)
    n_pages = PAST_LEN // PAGE_SIZE
    n_pool = (DEC_BATCH * n_pages * 5) // 4
    win_buf = min(WINDOW, PAST_LEN)
    nrm = lambda k, shape, s: jax.random.normal(k, shape, jnp.float32) * s
    page_table = jax.random.permutation(ks[8], n_pool)[:DEC_BATCH * n_pages].reshape(DEC_BATCH, n_pages).astype(jnp.int32)
    return {
        'x_prompt': nrm(ks[0], (BATCH, SEQ, D_MODEL), 1.0),
        'x_sample': nrm(ks[1], (DEC_BATCH, DEC_SEQ, D_MODEL), 1.0),
        'cache_nsa_cmp_kv': nrm(ks[2], (DEPTH, n_pool, PAGE_SIZE, 2, NSA_GROUPS, HEAD_DIM), 1.0),
        'cache_nsa_sel_kv': nrm(ks[3], (DEPTH, n_pool, PAGE_SIZE, 2, NSA_GROUPS, HEAD_DIM), 1.0),
        'cache_sb_kv': nrm(ks[4], (DEPTH, n_pool, PAGE_SIZE, 2, SB_HEADS, HEAD_DIM), 1.0),
        'cache_fox_kv': nrm(ks[5], (DEPTH, n_pool, PAGE_SIZE, 2, FOX_HEADS, HEAD_DIM), 1.0),
        'cache_fox_logf': jax.nn.log_sigmoid(FORGET_BIAS_INIT + nrm(ks[6], (DEPTH, n_pool, PAGE_SIZE, FOX_HEADS), 1.0)),
        'state_nsa_win_kv': nrm(ks[7], (DEPTH, DEC_BATCH, win_buf, 2, NSA_GROUPS, HEAD_DIM), 1.0),
        'page_table': page_table,
        'norm_mix_g': 1.0 + nrm(ks[9], (DEPTH, D_MODEL), 0.02),
        'norm_ffn_g': 1.0 + nrm(ks[10], (DEPTH, D_MODEL), 0.02),
        'norm_final_g': 1.0 + nrm(ks[11], (D_MODEL,), 0.02),
        'w_in': nrm(ks[12], (DEPTH, D_MODEL, N_IN), D_MODEL ** -0.5),
        'b_forget': FORGET_BIAS_INIT + nrm(ks[13], (DEPTH, FOX_HEADS), 0.1),
        'cmp_pos': nrm(ks[14], (DEPTH, CMP_BLOCK, HEAD_DIM), 0.1),
        'w_cmp_k': nrm(ks[15], (DEPTH, CMP_BLOCK * HEAD_DIM, HEAD_DIM), (CMP_BLOCK * HEAD_DIM) ** -0.5),
        'w_cmp_v': nrm(ks[16], (DEPTH, CMP_BLOCK * HEAD_DIM, HEAD_DIM), (CMP_BLOCK * HEAD_DIM) ** -0.5),
        'w_up_nsa': nrm(ks[17], (DEPTH, NSA_W, D_MODEL), NSA_W ** -0.5),
        'w_up_sb': nrm(ks[18], (DEPTH, SB_W, D_MODEL), SB_W ** -0.5),
        'w_up_fox': nrm(ks[19], (DEPTH, FOX_W, D_MODEL), FOX_W ** -0.5),
        'w_out': nrm(ks[20], (DEPTH, D_MODEL, D_MODEL), D_MODEL ** -0.5),
        'w_ffn_gate': nrm(ks[21], (DEPTH, D_MODEL, D_FF), D_MODEL ** -0.5),
        'w_ffn_up': nrm(ks[22], (DEPTH, D_MODEL, D_FF), D_MODEL ** -0.5),
        'w_ffn_down': nrm(ks[23], (DEPTH, D_FF, D_MODEL), D_FF ** -0.5),
    }


def reference(x_prompt, x_sample, cache_nsa_cmp_kv, cache_nsa_sel_kv, cache_sb_kv, cache_fox_kv,
              cache_fox_logf, state_nsa_win_kv, page_table, norm_mix_g, norm_ffn_g, norm_final_g,
              w_in, b_forget, cmp_pos, w_cmp_k, w_cmp_v, w_up_nsa, w_up_sb, w_up_fox, w_out,
              w_ffn_gate, w_ffn_up, w_ffn_down):
    xp, xs = x_prompt, x_sample
    p_states, s_states = [], []
    for l in range(DEPTH):
        hp = rmsnorm(xp, norm_mix_g[l])
        a_p, b_p, c_p, gm_p, st_p = prompt_mixers(hp, w_in[l], b_forget[l], cmp_pos[l], w_cmp_k[l], w_cmp_v[l])
        xp = xp + merge_branches(a_p, b_p, c_p, gm_p, w_up_nsa[l], w_up_sb[l], w_up_fox[l], w_out[l])
        hs = rmsnorm(xs, norm_mix_g[l])
        a_s, b_s, c_s, gm_s, st_s = sample_mixers(
            hs, cache_nsa_cmp_kv[l], cache_nsa_sel_kv[l], cache_sb_kv[l], cache_fox_kv[l],
            cache_fox_logf[l], state_nsa_win_kv[l], page_table,
            w_in[l], b_forget[l], cmp_pos[l], w_cmp_k[l], w_cmp_v[l])
        xs = xs + merge_branches(a_s, b_s, c_s, gm_s, w_up_nsa[l], w_up_sb[l], w_up_fox[l], w_out[l])
        xp = xp + swiglu(rmsnorm(xp, norm_ffn_g[l]), w_ffn_gate[l], w_ffn_up[l], w_ffn_down[l])
        xs = xs + swiglu(rmsnorm(xs, norm_ffn_g[l]), w_ffn_gate[l], w_ffn_up[l], w_ffn_down[l])
        p_states.append(st_p)
        s_states.append(st_s)
    y_prompt = rmsnorm(xp, norm_final_g)
    y_sample = rmsnorm(xs, norm_final_g)
    stk = lambda sts, i: jnp.stack([st[i] for st in sts], axis=0)
    return (y_prompt, y_sample,
            stk(p_states, 0), stk(p_states, 1), stk(p_states, 2), stk(p_states, 3), stk(p_states, 4), stk(p_states, 5),
            stk(s_states, 0), stk(s_states, 1), stk(s_states, 2), stk(s_states, 3), stk(s_states, 4), stk(s_states, 5))
```

```python
import functools

import numpy as np
import jax
import jax.numpy as jnp
from jax import lax
from jax.experimental import pallas as pl
from jax.experimental.pallas import tpu as pltpu

F32, BF16, I32 = jnp.float32, jnp.bfloat16, jnp.int32

HEAD_DIM = 64
ROT_DIM = HEAD_DIM // 4
ROPE_THETA = 500000.0
NSA_HEADS = 8
NSA_GROUPS = 2
NSA_HPG = NSA_HEADS // NSA_GROUPS
CMP_STRIDE = 16
CMP_BLOCK = 32
SEL_BLOCK = 64
TOP_N = 16
WINDOW = 512
SB_HEADS = 4
FOX_HEADS = 4
PAGE = 128
SEL_FORCE_SCORE = 1.0e4
NEG = -1.0e30
RMS_EPS = 1e-6
SCALE = HEAD_DIM ** -0.5

LANES = 128
NSA_W = NSA_HEADS * HEAD_DIM
KV_W = 2 * NSA_GROUPS * HEAD_DIM
SB_W = SB_HEADS * HEAD_DIM
GATE_W = 3 * NSA_HEADS
C_Q, C_CMP, C_SEL, C_WIN, C_SB, C_FOX, C_MISC = 0, 512, 768, 1024, 1280, 2048, 2816
N_PROJ = C_MISC + LANES
LOGF_LANE = GATE_W
VMEM_LIMIT = 56 * 1024 * 1024
PAGES_PER_STEP = 8


def _cp(*sem):
    return pltpu.CompilerParams(dimension_semantics=sem, vmem_limit_bytes=VMEM_LIMIT)


def _dot(a, b):
    return jnp.dot(a, b, preferred_element_type=F32)


def _dot_nt(a, b):
    return lax.dot_general(a, b, (((1,), (1,)), ((), ())), preferred_element_type=F32)


def _split_dot(a, r, parts):
    out, rem = None, a
    for _ in range(parts):
        hi = rem.astype(BF16)
        d = _dot(hi, r)
        out = d if out is None else out + d
        rem = rem - hi.astype(F32)
    return out


def _split_dot_l(l, a, parts):
    out, rem = None, a
    for _ in range(parts):
        hi = rem.astype(BF16)
        d = _dot(l, hi)
        out = d if out is None else out + d
        rem = rem - hi.astype(F32)
    return out


def _rms(x, g):
    return x * lax.rsqrt(jnp.mean(x * x, axis=-1, keepdims=True) + RMS_EPS) * g


def _softplus(z):
    return jnp.maximum(z, 0.0) + jnp.log1p(jnp.exp(-jnp.abs(z)))


def _iota(shape, axis):
    return lax.broadcasted_iota(I32, shape, axis)


def _proj_kernel(x_ref, g_ref, w_ref, rope_ref, b_ref, tri_ref,
                 qn_ref, kvc_ref, kvs_ref, kvw_ref, kvsb_ref, kvfx_ref,
                 kbs_ref, kbw_ref, kbsb_ref, kbfx_ref, qsb_ref, qfx_ref,
                 gate_ref, logf_ref, cum_ref, h_scr, carry_scr, *, tiles_per_seq):
    i = pl.program_id(0)
    tm = x_ref.shape[0]
    h_scr[...] = _rms(x_ref[...], g_ref[...]).astype(BF16)
    hb = h_scr[...]
    cos, sin_lo, sin_hi = rope_ref[:, 0:128], rope_ref[:, 128:256], rope_ref[:, 256:384]

    def rope(seg):
        return seg * cos + pltpu.roll(seg, LANES - 8, 1) * sin_lo + pltpu.roll(seg, 8, 1) * sin_hi

    lane = _iota((tm, LANES), 1)
    pq = _dot(hb, w_ref[:, C_Q:C_Q + NSA_W])
    for s in range(NSA_HPG):
        seg = rope(pq[:, LANES * s:LANES * (s + 1)]) * SCALE
        for g in range(NSA_GROUPS):
            qn_ref[g, s] = jnp.where(lane // HEAD_DIM == g, seg, 0.0).astype(BF16)

    for c0, f_ref, h_ref in ((C_CMP, kvc_ref, None), (C_SEL, kvs_ref, kbs_ref), (C_WIN, kvw_ref, kbw_ref)):
        p = _dot(hb, w_ref[:, c0:c0 + KV_W])
        k, v = rope(p[:, 0:128]), p[:, 128:256]
        f_ref[:, 0:128] = k
        f_ref[:, 128:256] = v
        if h_ref is not None:
            h_ref[:, 0:128] = k.astype(BF16)
            h_ref[:, 128:256] = v.astype(BF16)

    lane_w = _iota((tm, SB_W), 1)
    for c0, q_ref, f_ref, h_ref in ((C_SB, qsb_ref, kvsb_ref, kbsb_ref), (C_FOX, qfx_ref, kvfx_ref, kbfx_ref)):
        p = _dot(hb, w_ref[:, c0:c0 + 3 * SB_W])
        q = p[:, 0:SB_W] * SCALE
        for h in range(SB_HEADS):
            q_ref[h] = jnp.where(lane_w // HEAD_DIM == h, q, 0.0).astype(BF16)
        kv = p[:, SB_W:3 * SB_W]
        f_ref[...] = kv
        h_ref[...] = kv.astype(BF16)

    pm = _dot(hb, w_ref[:, C_MISC:C_MISC + LANES])
    sg = jax.nn.sigmoid(pm)
    gate_ref[0] = sg
    gate_ref[1] = pltpu.roll(sg, LANES - GATE_W // 2, 1)
    lf = -_softplus(-(pm + b_ref[...]))
    logf_ref[...] = lf

    @pl.when(i % tiles_per_seq == 0)
    def _():
        carry_scr[...] = jnp.zeros_like(carry_scr)

    c = _split_dot_l(tri_ref[...], lf, 3) + carry_scr[0:1, :]
    cum_ref[...] = c
    carry_scr[...] = jnp.broadcast_to(c[tm - 1:tm, :], carry_scr.shape)


def _proj(x, gain, w, rope_tab, bias, *, tm, tiles_per_seq):
    m, d = x.shape
    tri = jnp.tril(jnp.ones((tm, tm), F32)).astype(BF16)
    row = lambda wid: pl.BlockSpec((tm, wid), lambda i: (i, 0))
    const = lambda shape: pl.BlockSpec(shape, lambda i: (0,) * len(shape))
    f32o = lambda wid: jax.ShapeDtypeStruct((m, wid), F32)
    b16o = lambda wid: jax.ShapeDtypeStruct((m, wid), BF16)
    out_shape = (
        jax.ShapeDtypeStruct((NSA_GROUPS, NSA_HPG, m, LANES), BF16),
        f32o(KV_W), f32o(KV_W), f32o(KV_W), f32o(2 * SB_W), f32o(2 * SB_W),
        b16o(KV_W), b16o(KV_W), b16o(2 * SB_W), b16o(2 * SB_W),
        jax.ShapeDtypeStruct((SB_HEADS, m, SB_W), BF16), jax.ShapeDtypeStruct((SB_HEADS, m, SB_W), BF16),
        jax.ShapeDtypeStruct((NSA_GROUPS, m, LANES), F32), f32o(LANES), f32o(LANES),
    )
    out_specs = (
        pl.BlockSpec((NSA_GROUPS, NSA_HPG, tm, LANES), lambda i: (0, 0, i, 0)),
        row(KV_W), row(KV_W), row(KV_W), row(2 * SB_W), row(2 * SB_W),
        row(KV_W), row(KV_W), row(2 * SB_W), row(2 * SB_W),
        pl.BlockSpec((SB_HEADS, tm, SB_W), lambda i: (0, i, 0)),
        pl.BlockSpec((SB_HEADS, tm, SB_W), lambda i: (0, i, 0)),
        pl.BlockSpec((NSA_GROUPS, tm, LANES), lambda i: (0, i, 0)),
        row(LANES), row(LANES),
    )
    return pl.pallas_call(
        functools.partial(_proj_kernel, tiles_per_seq=tiles_per_seq),
        grid=(m // tm,),
        in_specs=[row(d), const((1, d)), const((d, N_PROJ)),
                  pl.BlockSpec((tm, 3 * LANES), lambda i: (i % tiles_per_seq, 0)),
                  const((1, LANES)), const((tm, tm))],
        out_specs=out_specs, out_shape=out_shape,
        scratch_shapes=[pltpu.VMEM((tm, d), BF16), pltpu.VMEM((8, LANES), F32)],
        compiler_params=_cp("arbitrary"), name="proj",
    )(x, gain, w, rope_tab, bias, tri)


def _compress_kernel(pt_ref, *refs, pp, nch):
    pages, (w_ref, pe_ref, out_ref, rows_scr, acc_scr) = refs[:pp], refs[pp:]
    j = pl.program_id(1)

    @pl.when(j == 0)
    def _():
        for half in range(KV_W // LANES):
            rows_scr[half, pl.ds(nch * CMP_STRIDE, CMP_STRIDE), :] = jnp.zeros((CMP_STRIDE, LANES), F32)

    for i in range(pp):
        for half in range(KV_W // LANES):
            rows_scr[half, pl.ds(pl.multiple_of((j * pp + i) * PAGE, PAGE), PAGE), :] = (
                pages[i][0, :, half * LANES:(half + 1) * LANES])

    @pl.when(j == pl.num_programs(1) - 1)
    def _():
        acc_scr[...] = jnp.zeros_like(acc_scr)
        for l in range(CMP_BLOCK):
            xl = jnp.concatenate([rows_scr[half, pl.ds(l, nch, stride=CMP_STRIDE), :]
                                  for half in range(KV_W // LANES)], axis=1) + pe_ref[l:l + 1, :]
            acc_scr[...] += _dot(xl.astype(BF16), w_ref[l])
        out_ref[0] = acc_scr[...].astype(BF16)


def _compress(pool, page_table, w_blk, pe4, *, pp):
    b, n_pages = page_table.shape
    nch = n_pages * (PAGE // CMP_STRIDE)
    page_spec = lambda i: pl.BlockSpec((1, PAGE, KV_W), lambda bb, j, pt: (pt[bb, j * pp + i], 0, 0))
    grid_spec = pltpu.PrefetchScalarGridSpec(
        num_scalar_prefetch=1, grid=(b, n_pages // pp),
        in_specs=[page_spec(i) for i in range(pp)] + [
            pl.BlockSpec((CMP_BLOCK, KV_W, KV_W), lambda bb, j, pt: (0, 0, 0)),
            pl.BlockSpec((CMP_BLOCK, KV_W), lambda bb, j, pt: (0, 0))],
        out_specs=pl.BlockSpec((1, nch, KV_W), lambda bb, j, pt: (bb, 0, 0)),
        scratch_shapes=[pltpu.VMEM((KV_W // LANES, nch * CMP_STRIDE + CMP_BLOCK, LANES), F32),
                        pltpu.VMEM((nch, KV_W), F32)])
    return pl.pallas_call(
        functools.partial(_compress_kernel, pp=pp, nch=nch), grid_spec=grid_spec,
        out_shape=jax.ShapeDtypeStruct((b, nch, KV_W), BF16),
        compiler_params=_cp("arbitrary", "arbitrary"), name="compress",
    )(page_table, *([pool] * pp), w_blk, pe4)


def _masked_softmax_rows(s, mask):
    s = jnp.where(mask, s, NEG)
    e = jnp.where(mask, jnp.exp(s - jnp.max(s, axis=1, keepdims=True)), 0.0)
    return e / jnp.maximum(jnp.sum(e, axis=1, keepdims=True), 1e-30)


def _online_step(s, mask, v, m_scr, l_scr, acc_scr):
    if mask is not None:
        s = jnp.where(mask, s, NEG)
    m_prev = m_scr[...]
    m_next = jnp.maximum(m_prev, jnp.max(s, axis=1, keepdims=True))
    alpha = jnp.exp(m_prev - m_next)
    e = jnp.exp(s - m_next)
    if mask is not None:
        e = jnp.where(mask, e, 0.0)
    l_scr[...] = alpha * l_scr[...] + jnp.sum(e, axis=1, keepdims=True)
    reps = acc_scr.shape[1] // LANES
    alpha_w = alpha if reps == 1 else jnp.concatenate([alpha] * reps, axis=1)
    acc_scr[...] = acc_scr[...] * alpha_w + _dot(e.astype(BF16), v)
    m_scr[...] = m_next


def _online_init(m_scr, l_scr, acc_scr):
    m_scr[...] = jnp.full_like(m_scr, NEG)
    l_scr[...] = jnp.zeros_like(l_scr)
    acc_scr[...] = jnp.zeros_like(acc_scr)


def _online_result(l_scr, acc_scr):
    l = jnp.maximum(l_scr[...], 1e-30)
    reps = acc_scr.shape[1] // LANES
    return acc_scr[...] / (l if reps == 1 else jnp.concatenate([l] * reps, axis=1))


def _select_blocks(imp, pos, n_sel):
    blk = _iota(imp.shape, 1)
    cur = pos // SEL_BLOCK
    valid = blk * SEL_BLOCK <= pos
    forced = jnp.where(blk == 0, 1.0, 0.0) + jnp.where(blk == cur, 1.0, 0.0) + jnp.where(blk == cur - 1, 1.0, 0.0)
    score = jnp.where(valid, jnp.where(forced > 0.5, SEL_FORCE_SCORE, imp), -1.0)
    score = jnp.where(blk < n_sel, score, -2.0)
    rank = jnp.zeros(imp.shape, F32)
    for b2 in range(n_sel):
        col = score[:, b2:b2 + 1]
        ge = jnp.where(col >= score, 1.0, 0.0)
        gt = jnp.where(col > score, 1.0, 0.0)
        rank = rank + jnp.where(blk > b2, ge, gt)
    return jnp.where(rank < min(TOP_N, n_sel), 1.0, 0.0)


def _nsa_prompt_kernel(qn_ref, ckv_ref, ks_ref, kw_ref, gate_ref, amat_ref, e_ref, out_ref,
                       m_scr, l_scr, acc_scr, oc_scr, os_scr, *, n_sel):
    g, i = pl.program_id(1), pl.program_id(2)
    tq = qn_ref.shape[2]
    tk = LANES
    rows = NSA_HPG * tq
    q0 = i * tq
    q = qn_ref[0].reshape(rows, LANES)
    pos1 = q0 + _iota((tq, 1), 0)
    pos = jnp.concatenate([pos1] * NSA_HPG, axis=0)

    nc = ckv_ref.shape[1]
    s = _dot_nt(q, ckv_ref[0, :, 0:128])
    cmask = _iota((rows, nc), 1) * CMP_STRIDE + (CMP_BLOCK - 1) <= pos
    p = _masked_softmax_rows(s, cmask)
    oc_scr[...] = _dot(p.astype(BF16), ckv_ref[0, :, 128:256])
    p_grp = p[0:tq]
    for h in range(1, NSA_HPG):
        p_grp = p_grp + p[h * tq:(h + 1) * tq]
    imp = _split_dot(p_grp, amat_ref[...], 2)
    sel = _select_blocks(imp, pos1, n_sel).astype(BF16)

    keyi = _iota((rows, tk), 1)

    _online_init(m_scr, l_scr, acc_scr)

    def sel_body(j, carry):
        kt = ks_ref[0, pl.ds(pl.multiple_of(j * tk, tk), tk), :]
        chosen = _dot(sel, e_ref[j])
        chosen = jnp.concatenate([chosen] * NSA_HPG, axis=0)
        kpos = j * tk + keyi
        mask = jnp.where(kpos <= pos, chosen, 0.0) > 0.5
        _online_step(_dot_nt(q, kt[:, 0:128]), mask, kt[:, 128:256], m_scr, l_scr, acc_scr)
        return carry

    lax.fori_loop(0, i + 1, sel_body, 0)
    os_scr[...] = _online_result(l_scr, acc_scr)

    _online_init(m_scr, l_scr, acc_scr)

    def win_body(j, carry):
        kt = kw_ref[0, pl.ds(pl.multiple_of(j * tk, tk), tk), :]
        kpos = j * tk + keyi
        mask = jnp.where(kpos <= pos, jnp.where(pos - kpos <= WINDOW, 1.0, 0.0), 0.0) > 0.5
        _online_step(_dot_nt(q, kt[:, 0:128]), mask, kt[:, 128:256], m_scr, l_scr, acc_scr)
        return carry

    lax.fori_loop(jnp.maximum(i - WINDOW // tk, 0), i + 1, win_body, 0)
    o_w = _online_result(l_scr, acc_scr)

    gates = gate_ref[0]
    lane = _iota((tq, LANES), 1)
    for h in range(NSA_HPG):
        r = slice(h * tq, (h + 1) * tq)
        o = (gates[:, 3 * h:3 * h + 1] * oc_scr[r, :] + gates[:, 3 * h + 1:3 * h + 2] * os_scr[r, :]
             + gates[:, 3 * h + 2:3 * h + 3] * o_w[r, :])
        out_ref[0, :, LANES * h:LANES * (h + 1)] = jnp.where(lane // HEAD_DIM == g, o, 0.0).astype(BF16)


def _imp_matrix(nc, width):
    n = np.arange(nc)[:, None]
    b = np.arange(width)[None, :]
    sub = SEL_BLOCK // CMP_STRIDE
    return jnp.asarray((n // sub == b).astype(np.float32) + ((n + 1) // sub == b), BF16)


def _expand_tiles(n_tiles, n_blk, tk):
    key = np.arange(n_tiles * tk).reshape(n_tiles, 1, tk)
    return jnp.asarray(key // SEL_BLOCK == np.arange(n_blk)[None, :, None], BF16)


def _nsa_prompt(qn, ckv, kb_sel, kb_win, gates, *, batch, seq, tq):
    nq = seq // tq
    nc = ckv.shape[1]
    n_sel = seq // SEL_BLOCK
    amat = _imp_matrix(nc, LANES)
    etile = _expand_tiles(seq // LANES, LANES, LANES)
    rows = NSA_HPG * tq
    m = batch * seq
    return pl.pallas_call(
        functools.partial(_nsa_prompt_kernel, n_sel=n_sel),
        grid=(batch, NSA_GROUPS, nq),
        in_specs=[
            pl.BlockSpec((1, NSA_HPG, tq, LANES), lambda b, g, i: (g, 0, b * nq + i, 0)),
            pl.BlockSpec((1, nc, KV_W), lambda b, g, i: (b, 0, 0)),
            pl.BlockSpec((1, seq, KV_W), lambda b, g, i: (b, 0, 0)),
            pl.BlockSpec((1, seq, KV_W), lambda b, g, i: (b, 0, 0)),
            pl.BlockSpec((1, tq, LANES), lambda b, g, i: (g, b * nq + i, 0)),
            pl.BlockSpec(amat.shape, lambda b, g, i: (0, 0)),
            pl.BlockSpec(etile.shape, lambda b, g, i: (0, 0, 0)),
        ],
        out_specs=pl.BlockSpec((1, tq, NSA_W), lambda b, g, i: (g, b * nq + i, 0)),
        out_shape=jax.ShapeDtypeStruct((NSA_GROUPS, m, NSA_W), BF16),
        scratch_shapes=[pltpu.VMEM((rows, LANES), F32)] * 5,
        compiler_params=_cp("arbitrary", "arbitrary", "arbitrary"), name="nsa_prompt",
    )(qn, ckv, kb_sel.reshape(batch, seq, KV_W), kb_win.reshape(batch, seq, KV_W), gates, amat, etile)


def _upper_strict(n):
    return jnp.asarray(np.triu(np.ones((n, n), np.float32), 1).T, BF16)


def _sb_prompt_kernel(q_ref, kv_ref, u_ref, out_ref, acc_scr, r_scr, o_scr):
    i = pl.program_id(1)
    tq = q_ref.shape[1]
    tk = LANES
    n_kt = (i + 1) * (tq // tk)
    pos = i * tq + _iota((tq, 1), 0)
    keyi = _iota((tq, tk), 1)
    lane = _iota((tq, SB_W), 1)
    o_scr[...] = jnp.zeros_like(o_scr)

    def head_body(h, carry):
        qh = q_ref[h]
        acc_scr[...] = jnp.zeros_like(acc_scr)
        r_scr[...] = jnp.zeros_like(r_scr)

        def key_body(t, c):
            j = n_kt - 1 - t
            kt = kv_ref[0, pl.ds(pl.multiple_of(j * tk, tk), tk), :]
            z = _dot_nt(qh, kt[:, 0:SB_W])
            mask = j * tk + keyi < pos
            sp = _softplus(z)
            log_keep = jnp.where(mask, -sp, 0.0)
            after = _split_dot(log_keep, u_ref[...], 2) + r_scr[...]
            w = jnp.where(mask, jnp.exp(z - sp + after), 0.0)
            acc_scr[...] += _dot(w.astype(BF16), kt[:, SB_W:2 * SB_W])
            r_scr[...] += jnp.sum(log_keep, axis=1, keepdims=True)
            return c

        lax.fori_loop(0, n_kt, key_body, 0)
        o_scr[...] += jnp.where(lane // HEAD_DIM == h, acc_scr[...], 0.0)
        return carry

    lax.fori_loop(0, SB_HEADS, head_body, 0)
    out_ref[...] = o_scr[...].astype(BF16)


def _sb_prompt(q, kb, *, batch, seq, tq):
    nq = seq // tq
    u = _upper_strict(LANES)
    return pl.pallas_call(
        _sb_prompt_kernel, grid=(batch, nq),
        in_specs=[pl.BlockSpec((SB_HEADS, tq, SB_W), lambda b, i: (0, b * nq + i, 0)),
                  pl.BlockSpec((1, seq, 2 * SB_W), lambda b, i: (b, 0, 0)),
                  pl.BlockSpec(u.shape, lambda b, i: (0, 0))],
        out_specs=pl.BlockSpec((tq, SB_W), lambda b, i: (b * nq + i, 0)),
        out_shape=jax.ShapeDtypeStruct((batch * seq, SB_W), BF16),
        scratch_shapes=[pltpu.VMEM((tq, SB_W), F32), pltpu.VMEM((tq, LANES), F32), pltpu.VMEM((tq, SB_W), F32)],
        compiler_params=_cp("arbitrary", "arbitrary"), name="sb_prompt",
    )(q, kb.reshape(batch, seq, 2 * SB_W), u)


def _fox_prompt_kernel(q_ref, kv_ref, cq_ref, ck_ref, out_ref, m_scr, l_scr, acc_scr, o_scr):
    i = pl.program_id(1)
    tq = q_ref.shape[1]
    tk = LANES
    n_kt = (i + 1) * (tq // tk)
    pos = i * tq + _iota((tq, 1), 0)
    keyi = _iota((tq, tk), 1)
    lane = _iota((tq, SB_W), 1)
    o_scr[...] = jnp.zeros_like(o_scr)

    def head_body(h, carry):
        qh = q_ref[h]
        cq = cq_ref[h]
        _online_init(m_scr, l_scr, acc_scr)

        def key_body(j, c):
            kt = kv_ref[0, pl.ds(pl.multiple_of(j * tk, tk), tk), :]
            s = _dot_nt(qh, kt[:, 0:SB_W]) + (cq - ck_ref[0, h, j])
            _online_step(s, j * tk + keyi <= pos, kt[:, SB_W:2 * SB_W], m_scr, l_scr, acc_scr)
            return c

        lax.fori_loop(0, n_kt, key_body, 0)
        o_scr[...] += jnp.where(lane // HEAD_DIM == h, _online_result(l_scr, acc_scr), 0.0)
        return carry

    lax.fori_loop(0, FOX_HEADS, head_body, 0)
    out_ref[...] = o_scr[...].astype(BF16)


def _fox_prompt(q, kb, cum_q, cum_k, *, batch, seq, tq):
    nq = seq // tq
    nk = seq // LANES
    return pl.pallas_call(
        _fox_prompt_kernel, grid=(batch, nq),
        in_specs=[pl.BlockSpec((FOX_HEADS, tq, SB_W), lambda b, i: (0, b * nq + i, 0)),
                  pl.BlockSpec((1, seq, 2 * SB_W), lambda b, i: (b, 0, 0)),
                  pl.BlockSpec((FOX_HEADS, tq, 1), lambda b, i: (0, b * nq + i, 0)),
                  pl.BlockSpec((1, FOX_HEADS, nk, 1, LANES), lambda b, i: (b, 0, 0, 0, 0))],
        out_specs=pl.BlockSpec((tq, SB_W), lambda b, i: (b * nq + i, 0)),
        out_shape=jax.ShapeDtypeStruct((batch * seq, SB_W), BF16),
        scratch_shapes=[pltpu.VMEM((tq, LANES), F32), pltpu.VMEM((tq, LANES), F32),
                        pltpu.VMEM((tq, SB_W), F32), pltpu.VMEM((tq, SB_W), F32)],
        compiler_params=_cp("arbitrary", "arbitrary"), name="fox_prompt",
    )(q, kb.reshape(batch, seq, 2 * SB_W), cum_q, cum_k)


def _merge_kernel(x_ref, g_ref, oa_ref, ob_ref, oc_ref, wm_ref, wa_ref, wb_ref, wc_ref, wo_ref, out_ref):
    x = x_ref[...]
    d = x.shape[1]
    hb = _rms(x, g_ref[...]).astype(BF16)
    oa = oa_ref[0] + oa_ref[1]
    u = jax.nn.sigmoid(_dot(hb, wm_ref[:, 0:d])) * _dot(oa, wa_ref[...])
    u = u + jax.nn.sigmoid(_dot(hb, wm_ref[:, d:2 * d])) * _dot(ob_ref[...], wb_ref[...])
    u = u + jax.nn.sigmoid(_dot(hb, wm_ref[:, 2 * d:3 * d])) * _dot(oc_ref[...], wc_ref[...])
    out_ref[...] = x + _dot(u.astype(BF16), wo_ref[...])


def _merge(x, gain, o_nsa, o_sb, o_fox, w_merge, w_a, w_b, w_c, w_o, *, tm):
    m, d = x.shape
    row = lambda wid: pl.BlockSpec((tm, wid), lambda i: (i, 0))
    const = lambda a: pl.BlockSpec(a.shape, lambda i: (0,) * a.ndim)
    return pl.pallas_call(
        _merge_kernel, grid=(m // tm,),
        in_specs=[row(d), const(gain), pl.BlockSpec((NSA_GROUPS, tm, NSA_W), lambda i: (0, i, 0)),
                  row(SB_W), row(SB_W), const(w_merge), const(w_a), const(w_b), const(w_c), const(w_o)],
        out_specs=row(d), out_shape=jax.ShapeDtypeStruct((m, d), F32),
        compiler_params=_cp("arbitrary"), name="merge",
    )(x, gain, o_nsa, o_sb, o_fox, w_merge, w_a, w_b, w_c, w_o)


def _ffn_kernel(x_ref, g_ref, wg_ref, wu_ref, wd_ref, out_ref):
    x = x_ref[...]
    hb = _rms(x, g_ref[...]).astype(BF16)
    a = _dot(hb, wg_ref[...])
    act = a * jax.nn.sigmoid(a) * _dot(hb, wu_ref[...])
    out_ref[...] = x + _dot(act.astype(BF16), wd_ref[...])


def _ffn(x, gain, w_gate, w_up, w_down, *, tm):
    m, d = x.shape
    row = pl.BlockSpec((tm, d), lambda i: (i, 0))
    const = lambda a: pl.BlockSpec(a.shape, lambda i: (0,) * a.ndim, pipeline_mode=pl.Buffered(1))
    return pl.pallas_call(
        _ffn_kernel, grid=(m // tm,),
        in_specs=[row, pl.BlockSpec(gain.shape, lambda i: (0, 0)), const(w_gate), const(w_up), const(w_down)],
        out_specs=row, out_shape=jax.ShapeDtypeStruct((m, d), F32),
        compiler_params=_cp("arbitrary"), name="ffn",
    )(x, gain, w_gate, w_up, w_down)


def _norm_kernel(x_ref, g_ref, out_ref):
    out_ref[...] = _rms(x_ref[...], g_ref[...])


def _final_norm(x, gain, *, tm):
    m, d = x.shape
    row = pl.BlockSpec((tm, d), lambda i: (i, 0))
    return pl.pallas_call(
        _norm_kernel, grid=(m // tm,), in_specs=[row, pl.BlockSpec((1, d), lambda i: (0, 0))],
        out_specs=row, out_shape=jax.ShapeDtypeStruct((m, d), F32),
        compiler_params=_cp("arbitrary"), name="final_norm",
    )(x, gain)


def _page_specs(width, n_pages, pp, descending):
    def spec(i):
        if descending:
            return pl.BlockSpec((1, PAGE, width), lambda b, j, pt: (pt[b, n_pages - 1 - (j * pp + i)], 0, 0))
        return pl.BlockSpec((1, PAGE, width), lambda b, j, pt: (pt[b, j * pp + i], 0, 0))
    return [spec(i) for i in range(pp)]


def _one_key_tile(kv_row):
    k = jnp.broadcast_to(kv_row[:, 0:LANES], (LANES, LANES)).astype(BF16)
    v = jnp.broadcast_to(kv_row[:, LANES:2 * LANES], (LANES, LANES)).astype(BF16)
    return k, v


def _head_rows_to_lanes(acc):
    lane, row = _iota(acc.shape, 1), _iota(acc.shape, 0)
    return jnp.sum(jnp.where(lane // HEAD_DIM == row, acc, 0.0), axis=0, keepdims=True)


def _sb_dec_kernel(pt_ref, q_ref, *refs, pp):
    pages, (u_ref, out_ref, acc_scr, r_scr) = refs[:pp], refs[pp:]
    j = pl.program_id(1)

    @pl.when(j == 0)
    def _():
        acc_scr[...] = jnp.zeros_like(acc_scr)
        r_scr[...] = jnp.zeros_like(r_scr)

    q = q_ref[0]
    for i in range(pp):
        pg = pages[i][0]
        z = _dot_nt(q, pg[:, 0:SB_W].astype(BF16))
        sp = _softplus(z)
        after = _split_dot(-sp, u_ref[...], 2) + r_scr[...]
        w = jnp.exp(z - sp + after)
        acc_scr[...] += _dot(w.astype(BF16), pg[:, SB_W:2 * SB_W].astype(BF16))
        r_scr[...] += jnp.sum(-sp, axis=1, keepdims=True)

    @pl.when(j == pl.num_programs(1) - 1)
    def _():
        out_ref[0] = _head_rows_to_lanes(acc_scr[...])


def _sb_dec(q8, pool, page_table, *, pp):
    bd, n_pages = page_table.shape
    u = _upper_strict(PAGE)
    grid_spec = pltpu.PrefetchScalarGridSpec(
        num_scalar_prefetch=1, grid=(bd, n_pages // pp),
        in_specs=[pl.BlockSpec((1, 8, SB_W), lambda b, j, pt: (b, 0, 0))]
        + _page_specs(2 * SB_W, n_pages, pp, True) + [pl.BlockSpec(u.shape, lambda b, j, pt: (0, 0))],
        out_specs=pl.BlockSpec((1, 1, SB_W), lambda b, j, pt: (b, 0, 0)),
        scratch_shapes=[pltpu.VMEM((8, SB_W), F32), pltpu.VMEM((8, LANES), F32)])
    return pl.pallas_call(
        functools.partial(_sb_dec_kernel, pp=pp), grid_spec=grid_spec,
        out_shape=jax.ShapeDtypeStruct((bd, 1, SB_W), F32),
        compiler_params=_cp("arbitrary", "arbitrary"), name="sb_decode",
    )(page_table, q8, *([pool] * pp), u)


def _fox_dec_kernel(pt_ref, q_ref, kvn_ref, lfn_ref, *refs, pp):
    pages, lf_pages = refs[:pp], refs[pp:2 * pp]
    u_ref, out_ref, m_scr, l_scr, acc_scr, r_scr = refs[2 * pp:]
    j = pl.program_id(1)
    q = q_ref[0]

    @pl.when(j == 0)
    def _():
        kvn = kvn_ref[0].astype(BF16).astype(F32)
        m_scr[...] = jnp.broadcast_to(jnp.sum(q.astype(F32) * kvn[:, 0:SB_W], axis=1, keepdims=True), m_scr.shape)
        l_scr[...] = jnp.ones_like(l_scr)
        acc_scr[...] = jnp.broadcast_to(kvn[:, SB_W:2 * SB_W], acc_scr.shape)
        r_scr[...] = lfn_ref[0]

    for i in range(pp):
        pg = pages[i][0]
        lf = lf_pages[i][0]
        decay = _split_dot(lf, u_ref[...], 3) + r_scr[...]
        s = _dot_nt(q, pg[:, 0:SB_W].astype(BF16)) + decay
        _online_step(s, None, pg[:, SB_W:2 * SB_W].astype(BF16), m_scr, l_scr, acc_scr)
        r_scr[...] += jnp.sum(lf, axis=1, keepdims=True)

    @pl.when(j == pl.num_programs(1) - 1)
    def _():
        out_ref[0] = _head_rows_to_lanes(_online_result(l_scr, acc_scr))


def _fox_dec(q8, kv_new, lf_new, pool, lf_pool, page_table, *, pp):
    bd, n_pages = page_table.shape
    u = _upper_strict(PAGE)
    desc = lambda i: (lambda b, j, pt: (pt[b, n_pages - 1 - (j * pp + i)], 0, 0))
    grid_spec = pltpu.PrefetchScalarGridSpec(
        num_scalar_prefetch=1, grid=(bd, n_pages // pp),
        in_specs=[pl.BlockSpec((1, 8, SB_W), lambda b, j, pt: (b, 0, 0)),
                  pl.BlockSpec((1, 1, 2 * SB_W), lambda b, j, pt: (b, 0, 0)),
                  pl.BlockSpec((1, 8, LANES), lambda b, j, pt: (b, 0, 0))]
        + _page_specs(2 * SB_W, n_pages, pp, True)
        + [pl.BlockSpec((1, 8, PAGE), desc(i)) for i in range(pp)]
        + [pl.BlockSpec(u.shape, lambda b, j, pt: (0, 0))],
        out_specs=pl.BlockSpec((1, 1, SB_W), lambda b, j, pt: (b, 0, 0)),
        scratch_shapes=[pltpu.VMEM((8, LANES), F32), pltpu.VMEM((8, LANES), F32),
                        pltpu.VMEM((8, SB_W), F32), pltpu.VMEM((8, LANES), F32)])
    return pl.pallas_call(
        functools.partial(_fox_dec_kernel, pp=pp), grid_spec=grid_spec,
        out_shape=jax.ShapeDtypeStruct((bd, 1, SB_W), F32),
        compiler_params=_cp("arbitrary", "arbitrary"), name="fox_decode",
    )(page_table, q8, kv_new, lf_new, *([pool] * pp), *([lf_pool] * pp), u)


def _nsa_dec_kernel(pt_ref, q_ref, ckv_ref, gate_ref, win_ref, kvs_new_ref, kvw_new_ref, amat_ref, e_ref, *refs,
                    pp, past_len, n_sel):
    pages, (out_ref, win_out_ref, m_scr, l_scr, acc_scr, sel_scr, oc_scr, ow_scr) = refs[:pp], refs[pp:]
    j = pl.program_id(1)
    q = q_ref[0]
    rows = q.shape[0]
    row = _iota((rows, 1), 0)
    pos = jnp.full((rows, 1), past_len, I32)

    @pl.when(j == 0)
    def _():
        nc = ckv_ref.shape[1]
        cmask = _iota((rows, nc), 1) * CMP_STRIDE + (CMP_BLOCK - 1) <= pos
        p = _masked_softmax_rows(_dot_nt(q, ckv_ref[0, :, 0:128]), cmask)
        oc_scr[...] = _dot(p.astype(BF16), ckv_ref[0, :, 128:256])
        g0 = jnp.sum(p[0:NSA_HPG], axis=0, keepdims=True)
        g1 = jnp.sum(p[NSA_HPG:2 * NSA_HPG], axis=0, keepdims=True)
        p_grp = jnp.where(row < NSA_HPG, g0, g1)
        imp = _split_dot(p_grp, amat_ref[...], 2)
        sel_scr[...] = _select_blocks(imp, pos, n_sel)

        wk = win_ref[0]
        wn = kvw_new_ref[0]
        _online_init(m_scr, l_scr, acc_scr)
        n_w = wk.shape[0]
        for t in range(n_w // LANES):
            kt = wk[t * LANES:(t + 1) * LANES, :]
            kpos = past_len - n_w + t * LANES + _iota((rows, LANES), 1)
            mask = jnp.where(kpos >= 0, jnp.where(pos - kpos <= WINDOW, 1.0, 0.0), 0.0) > 0.5
            _online_step(_dot_nt(q, kt[:, 0:128].astype(BF16)), mask, kt[:, 128:256].astype(BF16),
                         m_scr, l_scr, acc_scr)
        kn, vn = _one_key_tile(wn)
        _online_step(_dot_nt(q, kn), _iota((rows, LANES), 1) == 0, vn, m_scr, l_scr, acc_scr)
        ow_scr[...] = _online_result(l_scr, acc_scr)
        ridx = _iota(wk.shape, 0)
        win_out_ref[0] = jnp.where(ridx == n_w - 1, jnp.broadcast_to(wn, wk.shape), pltpu.roll(wk, n_w - 1, 0))
        _online_init(m_scr, l_scr, acc_scr)

    sel = sel_scr[...].astype(BF16)
    for i in range(pp):
        pg = pages[i][0]
        chosen = _dot(sel, e_ref[j * pp + i])
        _online_step(_dot_nt(q, pg[:, 0:128].astype(BF16)), chosen > 0.5, pg[:, 128:256].astype(BF16),
                     m_scr, l_scr, acc_scr)

    @pl.when(j == pl.num_programs(1) - 1)
    def _():
        kn, vn = _one_key_tile(kvs_new_ref[0])
        cur = past_len // SEL_BLOCK
        chosen = jnp.sum(jnp.where(_iota(sel_scr.shape, 1) == cur, sel_scr[...], 0.0), axis=1, keepdims=True)
        mask = jnp.where(_iota((rows, LANES), 1) == 0, chosen, 0.0) > 0.5
        _online_step(_dot_nt(q, kn), mask, vn, m_scr, l_scr, acc_scr)
        o_s = _online_result(l_scr, acc_scr)
        gates = jnp.where(row < NSA_HPG, gate_ref[0, 0:1, :], gate_ref[0, 1:2, :])
        lane = _iota((rows, LANES), 1)
        hh = row % NSA_HPG

        def gate(c):
            return jnp.sum(jnp.where(lane == 3 * hh + c, gates, 0.0), axis=1, keepdims=True)

        o = gate(0) * oc_scr[...] + gate(1) * o_s + gate(2) * ow_scr[...]
        o = jnp.where(lane // HEAD_DIM == row // NSA_HPG, o, 0.0)
        for g in range(NSA_GROUPS):
            for h in range(NSA_HPG):
                r = g * NSA_HPG + h
                out_ref[0, g:g + 1, LANES * h:LANES * (h + 1)] = o[r:r + 1, :]


def _nsa_dec(q8, ckv, gates, win_state, kvs_new, kvw_new, pool, page_table, *, pp):
    bd, n_pages = page_table.shape
    past_len = n_pages * PAGE
    n_sel = -(-(past_len + 1) // SEL_BLOCK)
    nc = ckv.shape[1]
    wid = -(-n_sel // LANES) * LANES
    amat = _imp_matrix(nc, wid)
    etile = _expand_tiles(n_pages, wid, PAGE)
    n_w = win_state.shape[1]
    req = lambda shape: pl.BlockSpec((1,) + shape, lambda b, j, pt: (b,) + (0,) * len(shape))
    const = lambda a: pl.BlockSpec(a.shape, lambda b, j, pt: (0,) * a.ndim)
    grid_spec = pltpu.PrefetchScalarGridSpec(
        num_scalar_prefetch=1, grid=(bd, n_pages // pp),
        in_specs=[req((8, LANES)), req((nc, KV_W)), req((NSA_GROUPS, LANES)), req((n_w, KV_W)),
                  req((1, KV_W)), req((1, KV_W)), const(amat), const(etile)]
        + _page_specs(KV_W, n_pages, pp, False),
        out_specs=(req((NSA_GROUPS, NSA_W)), req((n_w, KV_W))),
        scratch_shapes=[pltpu.VMEM((8, LANES), F32)] * 3 + [pltpu.VMEM((8, wid), F32)]
        + [pltpu.VMEM((8, LANES), F32)] * 2)
    return pl.pallas_call(
        functools.partial(_nsa_dec_kernel, pp=pp, past_len=past_len, n_sel=n_sel), grid_spec=grid_spec,
        out_shape=(jax.ShapeDtypeStruct((bd, NSA_GROUPS, NSA_W), F32), jax.ShapeDtypeStruct((bd, n_w, KV_W), F32)),
        compiler_params=_cp("arbitrary", "arbitrary"), name="nsa_decode",
    )(page_table, q8, ckv, gates, win_state, kvs_new, kvw_new, amat, etile, *([pool] * pp))


def _rope_table(pos):
    half = ROT_DIM // 2
    inv = ROPE_THETA ** (-jnp.arange(half, dtype=F32) / half)
    ang = pos.astype(F32)[:, None] * inv[None, :]
    cos, sin = jnp.cos(ang), jnp.sin(ang)
    n = pos.shape[0]
    pad = HEAD_DIM - ROT_DIM
    one_head = lambda a, b, fill: jnp.concatenate([a, b, jnp.full((n, pad), fill, F32)], axis=1)
    zero = jnp.zeros_like(sin)
    tabs = [one_head(cos, cos, 1.0), one_head(-sin, zero, 0.0), one_head(zero, sin, 0.0)]
    return jnp.concatenate([jnp.tile(t, (1, LANES // HEAD_DIM)) for t in tabs], axis=1)


_Q_HEAD_ORDER = (0, 4, 1, 5, 2, 6, 3, 7)


def _proj_columns():
    q = np.concatenate([np.arange(HEAD_DIM) + HEAD_DIM * h for h in _Q_HEAD_ORDER])
    kv = np.arange(NSA_W, NSA_W + 3 * KV_W)
    gate0 = NSA_W + 3 * KV_W
    sbfox = np.arange(gate0 + GATE_W, gate0 + GATE_W + 6 * SB_W)
    gate = np.arange(gate0, gate0 + GATE_W)
    pf0 = gate0 + GATE_W + 6 * SB_W
    pf = np.arange(pf0, pf0 + FOX_HEADS)
    return np.concatenate([q, kv, sbfox, gate, pf]), pf0 + FOX_HEADS


def _layer_weights(l, w_in, b_forget, cmp_pos, w_cmp_k, w_cmp_v, w_up_nsa):
    cols, merge0 = _proj_columns()
    d = w_in.shape[1]
    w_proj = jnp.take(w_in[l], jnp.asarray(cols), axis=1)
    w_proj = jnp.concatenate([w_proj, jnp.zeros((d, N_PROJ - w_proj.shape[1]), F32)], axis=1).astype(BF16)
    w_merge = w_in[l][:, merge0:].astype(BF16)
    bias = jnp.zeros((1, LANES), F32).at[0, LOGF_LANE:LOGF_LANE + FOX_HEADS].set(b_forget[l])
    wk = w_cmp_k[l].reshape(CMP_BLOCK, HEAD_DIM, HEAD_DIM)
    wv = w_cmp_v[l].reshape(CMP_BLOCK, HEAD_DIM, HEAD_DIM)
    w_blk = jnp.zeros((CMP_BLOCK, KV_W, KV_W), F32)
    for n, wsrc in enumerate((wk, wk, wv, wv)):
        w_blk = w_blk.at[:, HEAD_DIM * n:HEAD_DIM * (n + 1), HEAD_DIM * n:HEAD_DIM * (n + 1)].set(wsrc)
    pe4 = jnp.tile(cmp_pos[l], (1, KV_W // HEAD_DIM))
    rows = np.concatenate([np.arange(HEAD_DIM) + HEAD_DIM * h for h in _Q_HEAD_ORDER])
    w_a = jnp.take(w_up_nsa[l], jnp.asarray(rows), axis=0).astype(BF16)
    return w_proj, w_merge, bias, w_blk.astype(BF16), pe4, w_a


def _pad_heads(a):
    return jnp.concatenate([a, jnp.zeros((a.shape[0], 8 - a.shape[1], a.shape[2]), a.dtype)], axis=1)


def kernel(x_prompt, x_sample, cache_nsa_cmp_kv, cache_nsa_sel_kv, cache_sb_kv, cache_fox_kv, cache_fox_logf,
           state_nsa_win_kv, page_table, norm_mix_g, norm_ffn_g, norm_final_g, w_in, b_forget, cmp_pos,
           w_cmp_k, w_cmp_v, w_up_nsa, w_up_sb, w_up_fox, w_out, w_ffn_gate, w_ffn_up, w_ffn_down):
    batch, seq, d = x_prompt.shape
    bd = x_sample.shape[0]
    depth = w_in.shape[0]
    n_pool = cache_sb_kv.shape[1]
    n_pages = page_table.shape[1]
    past_len = n_pages * PAGE
    m = batch * seq
    tm = min(256, seq)
    tq_nsa = min(128, seq)
    tq = min(256, seq)
    pp = min(PAGES_PER_STEP, n_pages)
    pp_prompt = min(PAGES_PER_STEP, seq // PAGE)

    rope_p = _rope_table(jnp.arange(seq, dtype=I32))
    rope_s = _rope_table(jnp.full((bd,), past_len, I32))
    ident_pt = jnp.arange(batch * (seq // PAGE), dtype=I32).reshape(batch, seq // PAGE)

    xp = x_prompt.reshape(m, d)
    xs = x_sample.reshape(bd, d)
    p_states, s_states = [], []
    for l in range(depth):
        w_proj, w_merge, bias, w_blk, pe4, w_a = _layer_weights(l, w_in, b_forget, cmp_pos, w_cmp_k, w_cmp_v, w_up_nsa)
        w_b, w_c, w_o = w_up_sb[l].astype(BF16), w_up_fox[l].astype(BF16), w_out[l].astype(BF16)
        w_g, w_u, w_d = w_ffn_gate[l].astype(BF16), w_ffn_up[l].astype(BF16), w_ffn_down[l].astype(BF16)
        g_mix, g_ffn = norm_mix_g[l].reshape(1, d), norm_ffn_g[l].reshape(1, d)

        (qn, kv_cmp, kv_sel, kv_win, kv_sb, kv_fox, kb_sel, kb_win, kb_sb, kb_fox, q_sb, q_fox, gates, logf, cum
         ) = _proj(xp, g_mix, w_proj, rope_p, bias, tm=tm, tiles_per_seq=seq // tm)
        ckv = _compress(kv_cmp.reshape(m // PAGE, PAGE, KV_W), ident_pt, w_blk, pe4, pp=pp_prompt)
        o_nsa = _nsa_prompt(qn, ckv, kb_sel, kb_win, gates, batch=batch, seq=seq, tq=tq_nsa)
        o_sb = _sb_prompt(q_sb, kb_sb, batch=batch, seq=seq, tq=tq)
        cum4 = cum[:, LOGF_LANE:LOGF_LANE + FOX_HEADS]
        cum_q = cum4.T[:, :, None]
        cum_k = cum4.reshape(batch, seq, FOX_HEADS).transpose(0, 2, 1).reshape(batch, FOX_HEADS, seq // LANES, 1, LANES)
        o_fox = _fox_prompt(q_fox, kb_fox, cum_q, cum_k, batch=batch, seq=seq, tq=tq)
        xp = _merge(xp, g_mix, o_nsa, o_sb, o_fox, w_merge, w_a, w_b, w_c, w_o, tm=tm)
        xp = _ffn(xp, g_ffn, w_g, w_u, w_d, tm=tm)
        win_keep = min(WINDOW, seq)
        p_states.append((
            kv_cmp.reshape(batch, seq, 2, NSA_GROUPS, HEAD_DIM), kv_sel.reshape(batch, seq, 2, NSA_GROUPS, HEAD_DIM),
            kv_sb.reshape(batch, seq, 2, SB_HEADS, HEAD_DIM), kv_fox.reshape(batch, seq, 2, FOX_HEADS, HEAD_DIM),
            logf[:, LOGF_LANE:LOGF_LANE + FOX_HEADS].reshape(batch, seq, FOX_HEADS),
            kv_win.reshape(batch, seq, 2, NSA_GROUPS, HEAD_DIM)[:, seq - win_keep:]))

        (qn, kv_cmp, kv_sel, kv_win, kv_sb, kv_fox, _, _, _, _, q_sb, q_fox, gates, logf, _
         ) = _proj(xs, g_mix, w_proj, rope_s, bias, tm=bd, tiles_per_seq=1)
        ckv = _compress(cache_nsa_cmp_kv[l].reshape(n_pool, PAGE, KV_W), page_table, w_blk, pe4, pp=pp)
        q8 = qn.transpose(2, 0, 1, 3).reshape(bd, NSA_HEADS, LANES)
        o_nsa, win_new = _nsa_dec(
            q8, ckv, gates.transpose(1, 0, 2), state_nsa_win_kv[l].reshape(bd, -1, KV_W),
            kv_sel.reshape(bd, 1, KV_W), kv_win.reshape(bd, 1, KV_W),
            cache_nsa_sel_kv[l].reshape(n_pool, PAGE, KV_W), page_table, pp=pp)
        o_sb = _sb_dec(_pad_heads(q_sb.transpose(1, 0, 2)), cache_sb_kv[l].reshape(n_pool, PAGE, 2 * SB_W),
                       page_table, pp=pp)
        lf4 = logf[:, LOGF_LANE:LOGF_LANE + FOX_HEADS]
        lf_new = _pad_heads(jnp.broadcast_to(lf4[:, :, None], (bd, FOX_HEADS, LANES)))
        lf_pool = _pad_heads(cache_fox_logf[l].astype(F32).transpose(0, 2, 1))
        o_fox = _fox_dec(_pad_heads(q_fox.transpose(1, 0, 2)), kv_fox.reshape(bd, 1, 2 * SB_W), lf_new,
                         cache_fox_kv[l].reshape(n_pool, PAGE, 2 * SB_W), lf_pool, page_table, pp=pp)
        xs = _merge(xs, g_mix, o_nsa.transpose(1, 0, 2).astype(BF16), o_sb.reshape(bd, SB_W).astype(BF16),
                    o_fox.reshape(bd, SB_W).astype(BF16), w_merge, w_a, w_b, w_c, w_o, tm=bd)
        xs = _ffn(xs, g_ffn, w_g, w_u, w_d, tm=bd)
        s_states.append((
            kv_cmp.reshape(bd, 1, 2, NSA_GROUPS, HEAD_DIM), kv_sel.reshape(bd, 1, 2, NSA_GROUPS, HEAD_DIM),
            kv_sb.reshape(bd, 1, 2, SB_HEADS, HEAD_DIM), kv_fox.reshape(bd, 1, 2, FOX_HEADS, HEAD_DIM),
            lf4.reshape(bd, 1, FOX_HEADS), win_new.reshape(bd, -1, 2, NSA_GROUPS, HEAD_DIM)))

    g_fin = norm_final_g.reshape(1, d)
    y_prompt = _final_norm(xp, g_fin, tm=tm).reshape(batch, seq, d)
    y_sample = _final_norm(xs, g_fin, tm=bd).reshape(bd, 1, d)
    stk = lambda sts, i: jnp.stack([st[i] for st in sts], axis=0)
    return (y_prompt, y_sample) + tuple(stk(p_states, i) for i in range(6)) + tuple(stk(s_states, i) for i in range(6))
```

```python
import functools

import numpy as np
import jax
import jax.numpy as jnp
from jax import lax
from jax.experimental import pallas as pl
from jax.experimental.pallas import tpu as pltpu

F32, BF16, I32 = jnp.float32, jnp.bfloat16, jnp.int32

HEAD_DIM = 64
ROT_DIM = HEAD_DIM // 4
ROPE_THETA = 500000.0
NSA_HEADS = 8
NSA_GROUPS = 2
NSA_HPG = NSA_HEADS // NSA_GROUPS
CMP_STRIDE = 16
CMP_BLOCK = 32
SEL_BLOCK = 64
TOP_N = 16
WINDOW = 512
SB_HEADS = 4
FOX_HEADS = 4
PAGE = 128
SEL_FORCE_SCORE = 1.0e4
NEG = -1.0e30
RMS_EPS = 1e-6
SCALE = HEAD_DIM ** -0.5

LANES = 128
NSA_W = NSA_HEADS * HEAD_DIM
KV_W = 2 * NSA_GROUPS * HEAD_DIM
SB_W = SB_HEADS * HEAD_DIM
GATE_W = 3 * NSA_HEADS
C_Q, C_CMP, C_SEL, C_WIN, C_SB, C_FOX, C_MISC = 0, 512, 768, 1024, 1280, 2048, 2816
N_PROJ = C_MISC + LANES
LOGF_LANE = GATE_W
VMEM_LIMIT = 56 * 1024 * 1024
PAGES_PER_STEP = 8


def _cp(*sem):
    return pltpu.CompilerParams(dimension_semantics=sem, vmem_limit_bytes=VMEM_LIMIT)


def _dot(a, b):
    return jnp.dot(a, b, preferred_element_type=F32)


def _dot_nt(a, b):
    return lax.dot_general(a, b, (((1,), (1,)), ((), ())), preferred_element_type=F32)


def _split_dot(a, r, parts):
    out, rem = None, a
    for _ in range(parts):
        hi = rem.astype(BF16)
        d = _dot(hi, r)
        out = d if out is None else out + d
        rem = rem - hi.astype(F32)
    return out


def _split_dot_l(l, a, parts):
    out, rem = None, a
    for _ in range(parts):
        hi = rem.astype(BF16)
        d = _dot(l, hi)
        out = d if out is None else out + d
        rem = rem - hi.astype(F32)
    return out


def _rms(x, g):
    return x * lax.rsqrt(jnp.mean(x * x, axis=-1, keepdims=True) + RMS_EPS) * g


def _softplus(z):
    return jnp.maximum(z, 0.0) + jnp.log(1.0 + jnp.exp(-jnp.abs(z)))


def _iota(shape, axis):
    return lax.broadcasted_iota(I32, shape, axis)


def _proj_kernel(x_ref, g_ref, w_ref, rope_ref, b_ref, tri_ref,
                 qn_ref, kvc_ref, kvs_ref, kvw_ref, kvsb_ref, kvfx_ref,
                 kbs_ref, kbw_ref, kbsb_ref, kbfx_ref, qsb_ref, qfx_ref,
                 gate_ref, logf_ref, cum_ref, h_scr, carry_scr, *, tiles_per_seq):
    i = pl.program_id(0)
    tm = x_ref.shape[0]
    h_scr[...] = _rms(x_ref[...], g_ref[...]).astype(BF16)
    hb = h_scr[...]
    cos, sin_lo, sin_hi = rope_ref[:, 0:128], rope_ref[:, 128:256], rope_ref[:, 256:384]

    def rope(seg):
        return seg * cos + pltpu.roll(seg, LANES - 8, 1) * sin_lo + pltpu.roll(seg, 8, 1) * sin_hi

    lane = _iota((tm, LANES), 1)
    pq = _dot(hb, w_ref[:, C_Q:C_Q + NSA_W])
    for s in range(NSA_HPG):
        seg = rope(pq[:, LANES * s:LANES * (s + 1)]) * SCALE
        for g in range(NSA_GROUPS):
            qn_ref[g, s] = jnp.where(lane // HEAD_DIM == g, seg, 0.0).astype(BF16)

    for c0, f_ref, h_ref in ((C_CMP, kvc_ref, None), (C_SEL, kvs_ref, kbs_ref), (C_WIN, kvw_ref, kbw_ref)):
        p = _dot(hb, w_ref[:, c0:c0 + KV_W])
        k, v = rope(p[:, 0:128]), p[:, 128:256]
        f_ref[:, 0:128] = k
        f_ref[:, 128:256] = v
        if h_ref is not None:
            h_ref[:, 0:128] = k.astype(BF16)
            h_ref[:, 128:256] = v.astype(BF16)

    lane_w = _iota((tm, SB_W), 1)
    for c0, q_ref, f_ref, h_ref in ((C_SB, qsb_ref, kvsb_ref, kbsb_ref), (C_FOX, qfx_ref, kvfx_ref, kbfx_ref)):
        p = _dot(hb, w_ref[:, c0:c0 + 3 * SB_W])
        q = p[:, 0:SB_W] * SCALE
        for h in range(SB_HEADS):
            q_ref[h] = jnp.where(lane_w // HEAD_DIM == h, q, 0.0).astype(BF16)
        kv = p[:, SB_W:3 * SB_W]
        f_ref[...] = kv
        h_ref[...] = kv.astype(BF16)

    pm = _dot(hb, w_ref[:, C_MISC:C_MISC + LANES])
    sg = jax.nn.sigmoid(pm)
    gate_ref[0] = sg
    gate_ref[1] = pltpu.roll(sg, LANES - GATE_W // 2, 1)
    zf = pm + b_ref[...]
    lf = jnp.minimum(zf, 0.0) - jnp.log1p(jnp.exp(-jnp.abs(zf)))
    logf_ref[...] = lf

    @pl.when(i % tiles_per_seq == 0)
    def _():
        carry_scr[...] = jnp.zeros_like(carry_scr)

    c = _split_dot_l(tri_ref[...], lf, 3) + carry_scr[0:1, :]
    cum_ref[...] = c
    carry_scr[...] = jnp.broadcast_to(c[tm - 1:tm, :], carry_scr.shape)


def _proj(x, gain, w, rope_tab, bias, *, tm, tiles_per_seq):
    m, d = x.shape
    tri = jnp.tril(jnp.ones((tm, tm), F32)).astype(BF16)
    row = lambda wid: pl.BlockSpec((tm, wid), lambda i: (i, 0))
    const = lambda shape: pl.BlockSpec(shape, lambda i: (0,) * len(shape))
    f32o = lambda wid: jax.ShapeDtypeStruct((m, wid), F32)
    b16o = lambda wid: jax.ShapeDtypeStruct((m, wid), BF16)
    out_shape = (
        jax.ShapeDtypeStruct((NSA_GROUPS, NSA_HPG, m, LANES), BF16),
        f32o(KV_W), f32o(KV_W), f32o(KV_W), f32o(2 * SB_W), f32o(2 * SB_W),
        b16o(KV_W), b16o(KV_W), b16o(2 * SB_W), b16o(2 * SB_W),
        jax.ShapeDtypeStruct((SB_HEADS, m, SB_W), BF16), jax.ShapeDtypeStruct((SB_HEADS, m, SB_W), BF16),
        jax.ShapeDtypeStruct((NSA_GROUPS, m, LANES), F32), f32o(LANES), f32o(LANES),
    )
    out_specs = (
        pl.BlockSpec((NSA_GROUPS, NSA_HPG, tm, LANES), lambda i: (0, 0, i, 0)),
        row(KV_W), row(KV_W), row(KV_W), row(2 * SB_W), row(2 * SB_W),
        row(KV_W), row(KV_W), row(2 * SB_W), row(2 * SB_W),
        pl.BlockSpec((SB_HEADS, tm, SB_W), lambda i: (0, i, 0)),
        pl.BlockSpec((SB_HEADS, tm, SB_W), lambda i: (0, i, 0)),
        pl.BlockSpec((NSA_GROUPS, tm, LANES), lambda i: (0, i, 0)),
        row(LANES), row(LANES),
    )
    return pl.pallas_call(
        functools.partial(_proj_kernel, tiles_per_seq=tiles_per_seq),
        grid=(m // tm,),
        in_specs=[row(d), const((1, d)), const((d, N_PROJ)),
                  pl.BlockSpec((tm, 3 * LANES), lambda i: (i % tiles_per_seq, 0)),
                  const((1, LANES)), const((tm, tm))],
        out_specs=out_specs, out_shape=out_shape,
        scratch_shapes=[pltpu.VMEM((tm, d), BF16), pltpu.VMEM((8, LANES), F32)],
        compiler_params=_cp("arbitrary"), name="proj",
    )(x, gain, w, rope_tab, bias, tri)


def _compress_kernel(pt_ref, *refs, pp, nch):
    pages, (w_ref, pe_ref, out_ref, rows_scr, acc_scr) = refs[:pp], refs[pp:]
    j = pl.program_id(1)

    @pl.when(j == 0)
    def _():
        for half in range(KV_W // LANES):
            rows_scr[half, pl.ds(nch * CMP_STRIDE, CMP_STRIDE), :] = jnp.zeros((CMP_STRIDE, LANES), F32)

    for i in range(pp):
        for half in range(KV_W // LANES):
            rows_scr[half, pl.ds(pl.multiple_of((j * pp + i) * PAGE, PAGE), PAGE), :] = (
                pages[i][0, :, half * LANES:(half + 1) * LANES])

    @pl.when(j == pl.num_programs(1) - 1)
    def _():
        acc_scr[...] = jnp.zeros_like(acc_scr)
        for l in range(CMP_BLOCK):
            xl = jnp.concatenate([rows_scr[half, pl.ds(l, nch, stride=CMP_STRIDE), :]
                                  for half in range(KV_W // LANES)], axis=1) + pe_ref[l:l + 1, :]
            acc_scr[...] += _dot(xl.astype(BF16), w_ref[l])
        out_ref[0] = acc_scr[...].astype(BF16)


def _compress(pool, page_table, w_blk, pe4, *, pp):
    b, n_pages = page_table.shape
    nch = n_pages * (PAGE // CMP_STRIDE)
    page_spec = lambda i: pl.BlockSpec((1, PAGE, KV_W), lambda bb, j, pt: (pt[bb, j * pp + i], 0, 0))
    grid_spec = pltpu.PrefetchScalarGridSpec(
        num_scalar_prefetch=1, grid=(b, n_pages // pp),
        in_specs=[page_spec(i) for i in range(pp)] + [
            pl.BlockSpec((CMP_BLOCK, KV_W, KV_W), lambda bb, j, pt: (0, 0, 0)),
            pl.BlockSpec((CMP_BLOCK, KV_W), lambda bb, j, pt: (0, 0))],
        out_specs=pl.BlockSpec((1, nch, KV_W), lambda bb, j, pt: (bb, 0, 0)),
        scratch_shapes=[pltpu.VMEM((KV_W // LANES, nch * CMP_STRIDE + CMP_BLOCK, LANES), F32),
                        pltpu.VMEM((nch, KV_W), F32)])
    return pl.pallas_call(
        functools.partial(_compress_kernel, pp=pp, nch=nch), grid_spec=grid_spec,
        out_shape=jax.ShapeDtypeStruct((b, nch, KV_W), BF16),
        compiler_params=_cp("arbitrary", "arbitrary"), name="compress",
    )(page_table, *([pool] * pp), w_blk, pe4)


def _masked_softmax_rows(s, mask):
    s = jnp.where(mask, s, NEG)
    e = jnp.where(mask, jnp.exp(s - jnp.max(s, axis=1, keepdims=True)), 0.0)
    return e / jnp.maximum(jnp.sum(e, axis=1, keepdims=True), 1e-30)


def _online_step(s, mask, v, m_scr, l_scr, acc_scr):
    if mask is not None:
        s = jnp.where(mask, s, NEG)
    m_prev = m_scr[...]
    m_next = jnp.maximum(m_prev, jnp.max(s, axis=1, keepdims=True))
    alpha = jnp.exp(m_prev - m_next)
    e = jnp.exp(s - m_next)
    if mask is not None:
        e = jnp.where(mask, e, 0.0)
    l_scr[...] = alpha * l_scr[...] + jnp.sum(e, axis=1, keepdims=True)
    reps = acc_scr.shape[1] // LANES
    alpha_w = alpha if reps == 1 else jnp.concatenate([alpha] * reps, axis=1)
    acc_scr[...] = acc_scr[...] * alpha_w + _dot(e.astype(BF16), v)
    m_scr[...] = m_next


def _online_init(m_scr, l_scr, acc_scr):
    m_scr[...] = jnp.full_like(m_scr, NEG)
    l_scr[...] = jnp.zeros_like(l_scr)
    acc_scr[...] = jnp.zeros_like(acc_scr)


def _online_result(l_scr, acc_scr):
    l = jnp.maximum(l_scr[...], 1e-30)
    reps = acc_scr.shape[1] // LANES
    return acc_scr[...] / (l if reps == 1 else jnp.concatenate([l] * reps, axis=1))


def _select_blocks(imp, pos, n_sel):
    blk = _iota(imp.shape, 1)
    cur = pos // SEL_BLOCK
    valid = blk * SEL_BLOCK <= pos
    forced = jnp.where(blk == 0, 1.0, 0.0) + jnp.where(blk == cur, 1.0, 0.0) + jnp.where(blk == cur - 1, 1.0, 0.0)
    score = jnp.where(valid, jnp.where(forced > 0.5, SEL_FORCE_SCORE, imp), -1.0)
    score = jnp.where(blk < n_sel, score, -2.0)
    rank = jnp.zeros(imp.shape, F32)
    for b2 in range(n_sel):
        col = score[:, b2:b2 + 1]
        ge = jnp.where(col >= score, 1.0, 0.0)
        gt = jnp.where(col > score, 1.0, 0.0)
        rank = rank + jnp.where(blk > b2, ge, gt)
    return jnp.where(rank < min(TOP_N, n_sel), 1.0, 0.0)


def _rep(x, n):
    return x if n == 1 else jnp.concatenate([x] * n, axis=1)


def _select_blocks_t(p_grp, amat_t, q0, n_sel):
    t = p_grp.shape[0]
    nb = amat_t.shape[0]
    hi = p_grp.astype(BF16)
    lo = (p_grp - hi.astype(F32)).astype(BF16)
    imp = _dot_nt(amat_t, hi) + _dot_nt(amat_t, lo)
    pos = q0 + _iota((1, t), 1)
    blk = _iota((nb, t), 0)
    cur = pos // SEL_BLOCK
    forced = jnp.where(blk == 0, 1.0, 0.0) + jnp.where(blk == cur, 1.0, 0.0) + jnp.where(blk == cur - 1, 1.0, 0.0)
    score = jnp.where(blk * SEL_BLOCK <= pos, jnp.where(forced > 0.5, SEL_FORCE_SCORE, imp), -1.0)
    score = jnp.where(blk < n_sel, score, -2.0)
    groups = [score[8 * k:8 * (k + 1)] for k in range(nb // 8)]
    ranks = [jnp.zeros((8, t), F32) for _ in groups]
    sub = _iota((8, t), 0)
    for b2 in range(n_sel):
        rowv = score[b2:b2 + 1, :]
        k2, r2 = divmod(b2, 8)
        for k, grp in enumerate(groups):
            if k > k2:
                inc = jnp.where(rowv >= grp, 1.0, 0.0)
            elif k < k2:
                inc = jnp.where(rowv > grp, 1.0, 0.0)
            else:
                inc = jnp.where(sub > r2, jnp.where(rowv >= grp, 1.0, 0.0), jnp.where(rowv > grp, 1.0, 0.0))
            ranks[k] = ranks[k] + inc
    chosen = [jnp.where(rk < min(TOP_N, n_sel), 1.0, 0.0) for rk in ranks]
    pad = [jnp.zeros((LANES - nb, t), F32)] if nb < LANES else []
    return jnp.concatenate(chosen + pad, axis=0)


def _nsa_prompt_kernel(qn_ref, ckv_ref, ks_ref, kw_ref, gate_ref, amat_ref, e_ref, out_ref,
                       m_scr, l_scr, acc_scr, oc_scr, os_scr, *, n_sel):
    g, i = pl.program_id(1), pl.program_id(2)
    t = qn_ref.shape[2]
    q0 = i * t
    rowi, coli = _iota((t, t), 0), _iota((t, t), 1)
    reps = t // LANES

    nc = ckv_ref.shape[1]
    cmask = _iota((t, nc), 1) * CMP_STRIDE + (CMP_BLOCK - 1) <= q0 + _iota((t, 1), 0)
    ck, cv = ckv_ref[0, :, 0:128], ckv_ref[0, :, 128:256]
    p_grp = None
    for h in range(NSA_HPG):
        p = _masked_softmax_rows(_dot_nt(qn_ref[0, h], ck), cmask)
        oc_scr[h] = _dot(p.astype(BF16), cv)
        p_grp = p if p_grp is None else p_grp + p
    sel = _select_blocks_t(p_grp, amat_ref[...], q0, n_sel).T.astype(BF16)

    def online_head(h, s, v):
        m_prev = m_scr[h]
        m_next = jnp.maximum(m_prev, jnp.max(s, axis=1, keepdims=True))
        alpha = jnp.exp(m_prev - m_next)
        e = jnp.exp(s - _rep(m_next, reps))
        l_scr[h] = alpha * l_scr[h] + jnp.sum(e, axis=1, keepdims=True)
        m_scr[h] = m_next
        acc_scr[h] = acc_scr[h] * alpha + _dot(e.astype(BF16), v)

    def reset():
        m_scr[...] = jnp.full_like(m_scr, NEG)
        l_scr[...] = jnp.zeros_like(l_scr)
        acc_scr[...] = jnp.zeros_like(acc_scr)

    def sel_tile(j, diag):
        kt = ks_ref[0, pl.ds(pl.multiple_of(j * t, t), t), :]
        chosen = _dot(sel, e_ref[j])
        if diag:
            chosen = jnp.where(coli <= rowi, chosen, 0.0)
        keep = chosen > 0.5
        for h in range(NSA_HPG):
            online_head(h, jnp.where(keep, _dot_nt(qn_ref[0, h], kt[:, 0:128]), NEG), kt[:, 128:256])

    reset()
    lax.fori_loop(0, i, lambda j, c: (sel_tile(j, False), c)[1], 0)
    sel_tile(i, True)
    for h in range(NSA_HPG):
        os_scr[h] = acc_scr[h] / jnp.maximum(l_scr[h], 1e-30)

    def win_tile(j, c):
        kt = kw_ref[0, pl.ds(pl.multiple_of(j * t, t), t), :]
        d = (i - j) * t + rowi - coli
        keep = jnp.where(d >= 0, jnp.where(d <= WINDOW, 1.0, 0.0), 0.0) > 0.5
        for h in range(NSA_HPG):
            online_head(h, jnp.where(keep, _dot_nt(qn_ref[0, h], kt[:, 0:128]), NEG), kt[:, 128:256])
        return c

    reset()
    lax.fori_loop(jnp.maximum(i - WINDOW // t, 0), i + 1, win_tile, 0)

    gates = gate_ref[0]
    lane = _iota((t, LANES), 1)
    for h in range(NSA_HPG):
        o_w = acc_scr[h] / jnp.maximum(l_scr[h], 1e-30)
        o = (gates[:, 3 * h:3 * h + 1] * oc_scr[h] + gates[:, 3 * h + 1:3 * h + 2] * os_scr[h]
             + gates[:, 3 * h + 2:3 * h + 3] * o_w)
        out_ref[0, :, LANES * h:LANES * (h + 1)] = jnp.where(lane // HEAD_DIM == g, o, 0.0).astype(BF16)


def _imp_matrix(nc, width):
    n = np.arange(nc)[:, None]
    b = np.arange(width)[None, :]
    sub = SEL_BLOCK // CMP_STRIDE
    return jnp.asarray((n // sub == b).astype(np.float32) + ((n + 1) // sub == b), BF16)


def _expand_tiles(n_tiles, n_blk, tk):
    key = np.arange(n_tiles * tk).reshape(n_tiles, 1, tk)
    return jnp.asarray(key // SEL_BLOCK == np.arange(n_blk)[None, :, None], BF16)


def _nsa_prompt(qn, ckv, kb_sel, kb_win, gates, *, batch, seq, tq):
    nq = seq // tq
    nc = ckv.shape[1]
    n_sel = seq // SEL_BLOCK
    assert n_sel <= LANES and WINDOW % tq == 0
    amat = _imp_matrix(nc, -(-n_sel // 8) * 8).T
    etile = _expand_tiles(nq, LANES, tq)
    m = batch * seq
    return pl.pallas_call(
        functools.partial(_nsa_prompt_kernel, n_sel=n_sel),
        grid=(batch, NSA_GROUPS, nq),
        in_specs=[
            pl.BlockSpec((1, NSA_HPG, tq, LANES), lambda b, g, i: (g, 0, b * nq + i, 0)),
            pl.BlockSpec((1, nc, KV_W), lambda b, g, i: (b, 0, 0)),
            pl.BlockSpec((1, seq, KV_W), lambda b, g, i: (b, 0, 0)),
            pl.BlockSpec((1, seq, KV_W), lambda b, g, i: (b, 0, 0)),
            pl.BlockSpec((1, tq, LANES), lambda b, g, i: (g, b * nq + i, 0)),
            pl.BlockSpec(amat.shape, lambda b, g, i: (0, 0)),
            pl.BlockSpec(etile.shape, lambda b, g, i: (0, 0, 0)),
        ],
        out_specs=pl.BlockSpec((1, tq, NSA_W), lambda b, g, i: (g, b * nq + i, 0)),
        out_shape=jax.ShapeDtypeStruct((NSA_GROUPS, m, NSA_W), BF16),
        scratch_shapes=[pltpu.VMEM((NSA_HPG, tq, LANES), F32)] * 5,
        compiler_params=_cp("arbitrary", "arbitrary", "arbitrary"), name="nsa_prompt",
    )(qn, ckv, kb_sel.reshape(batch, seq, KV_W), kb_win.reshape(batch, seq, KV_W), gates, amat, etile)


def _upper_strict(n):
    return jnp.asarray(np.triu(np.ones((n, n), np.float32), 1).T, BF16)


def _stack_heads(v):
    lane = _iota(v.shape, 1)
    return jnp.concatenate([jnp.where(lane // HEAD_DIM == h, v, jnp.zeros_like(v)) for h in range(SB_HEADS)], axis=0)


def _head_lanes(parts):
    lane = _iota(parts[0].shape, 1)
    return jnp.concatenate([jnp.where(lane < HEAD_DIM, parts[0], parts[1]),
                            jnp.where(lane < HEAD_DIM, parts[2], parts[3])], axis=1)


def _sb_prompt_kernel(q_ref, kv_ref, u_ref, out_ref, acc_scr, r_scr):
    i = pl.program_id(1)
    t = q_ref.shape[1]
    rowi, coli = _iota((t, t), 0), _iota((t, t), 1)
    acc_scr[...] = jnp.zeros_like(acc_scr)
    r_scr[...] = jnp.zeros_like(r_scr)

    def tile(j, diag):
        kt = kv_ref[0, pl.ds(pl.multiple_of(j * t, t), t), :]
        ws = []
        for h in range(SB_HEADS):
            z = _dot_nt(q_ref[h], kt[:, 0:SB_W])
            sp = _softplus(z)
            log_keep = -sp
            if diag:
                log_keep = jnp.where(coli < rowi, log_keep, 0.0)
            r = r_scr[h]
            w = jnp.exp(z - sp + _split_dot(log_keep, u_ref[...], 2) + _rep(r, t // LANES))
            if diag:
                w = jnp.where(coli < rowi, w, 0.0)
            ws.append(w.astype(BF16))
            r_scr[h] = r + jnp.sum(log_keep, axis=1, keepdims=True)
        acc_scr[...] += _dot(jnp.concatenate(ws, axis=1), _stack_heads(kt[:, SB_W:2 * SB_W]))

    tile(i, True)
    lax.fori_loop(0, i, lambda n, c: (tile(i - 1 - n, False), c)[1], 0)
    out_ref[...] = acc_scr[...].astype(BF16)


def _sb_prompt(q, kb, *, batch, seq, tq):
    nq = seq // tq
    u = _upper_strict(tq)
    return pl.pallas_call(
        _sb_prompt_kernel, grid=(batch, nq),
        in_specs=[pl.BlockSpec((SB_HEADS, tq, SB_W), lambda b, i: (0, b * nq + i, 0)),
                  pl.BlockSpec((1, seq, 2 * SB_W), lambda b, i: (b, 0, 0)),
                  pl.BlockSpec(u.shape, lambda b, i: (0, 0))],
        out_specs=pl.BlockSpec((tq, SB_W), lambda b, i: (b * nq + i, 0)),
        out_shape=jax.ShapeDtypeStruct((batch * seq, SB_W), BF16),
        scratch_shapes=[pltpu.VMEM((tq, SB_W), F32), pltpu.VMEM((SB_HEADS, tq, LANES), F32)],
        compiler_params=_cp("arbitrary", "arbitrary"), name="sb_prompt",
    )(q, kb.reshape(batch, seq, 2 * SB_W), u)


def _fox_prompt_kernel(q_ref, kv_ref, cq_ref, ck_ref, out_ref, m_scr, l_scr, acc_scr, cq_scr):
    i = pl.program_id(1)
    t = q_ref.shape[1]
    rowi, coli = _iota((t, t), 0), _iota((t, t), 1)
    m_scr[...] = jnp.full_like(m_scr, NEG)
    l_scr[...] = jnp.zeros_like(l_scr)
    acc_scr[...] = jnp.zeros_like(acc_scr)
    for h in range(FOX_HEADS):
        cq_scr[h] = jnp.broadcast_to(cq_ref[h], (t, LANES))

    def tile(j, diag):
        kt = kv_ref[0, pl.ds(pl.multiple_of(j * t, t), t), :]
        es, alphas = [], []
        for h in range(FOX_HEADS):
            s = _dot_nt(q_ref[h], kt[:, 0:SB_W]) + (_rep(cq_scr[h], t // LANES) - ck_ref[0, h, j])
            if diag:
                s = jnp.where(coli <= rowi, s, NEG)
            m_prev = m_scr[h]
            m_next = jnp.maximum(m_prev, jnp.max(s, axis=1, keepdims=True))
            alpha = jnp.exp(m_prev - m_next)
            e = jnp.exp(s - _rep(m_next, t // LANES))
            l_scr[h] = alpha * l_scr[h] + jnp.sum(e, axis=1, keepdims=True)
            m_scr[h] = m_next
            es.append(e.astype(BF16))
            alphas.append(alpha)
        acc_scr[...] = acc_scr[...] * _head_lanes(alphas) + _dot(jnp.concatenate(es, axis=1),
                                                                 _stack_heads(kt[:, SB_W:2 * SB_W]))

    lax.fori_loop(0, i, lambda j, c: (tile(j, False), c)[1], 0)
    tile(i, True)
    out_ref[...] = (acc_scr[...] / _head_lanes([jnp.maximum(l_scr[h], 1e-30) for h in range(FOX_HEADS)])).astype(BF16)


def _fox_prompt(q, kb, cum_q, cum_k, *, batch, seq, tq):
    nq = seq // tq
    nk = seq // tq
    return pl.pallas_call(
        _fox_prompt_kernel, grid=(batch, nq),
        in_specs=[pl.BlockSpec((FOX_HEADS, tq, SB_W), lambda b, i: (0, b * nq + i, 0)),
                  pl.BlockSpec((1, seq, 2 * SB_W), lambda b, i: (b, 0, 0)),
                  pl.BlockSpec((FOX_HEADS, tq, 1), lambda b, i: (0, b * nq + i, 0)),
                  pl.BlockSpec((1, FOX_HEADS, nk, 1, tq), lambda b, i: (b, 0, 0, 0, 0))],
        out_specs=pl.BlockSpec((tq, SB_W), lambda b, i: (b * nq + i, 0)),
        out_shape=jax.ShapeDtypeStruct((batch * seq, SB_W), BF16),
        scratch_shapes=[pltpu.VMEM((FOX_HEADS, tq, LANES), F32), pltpu.VMEM((FOX_HEADS, tq, LANES), F32),
                        pltpu.VMEM((tq, SB_W), F32), pltpu.VMEM((FOX_HEADS, tq, LANES), F32)],
        compiler_params=_cp("arbitrary", "arbitrary"), name="fox_prompt",
    )(q, kb.reshape(batch, seq, 2 * SB_W), cum_q, cum_k)


def _merge_kernel(x_ref, g_ref, oa_ref, ob_ref, oc_ref, wm_ref, wa_ref, wb_ref, wc_ref, wo_ref, out_ref):
    x = x_ref[...]
    d = x.shape[1]
    hb = _rms(x, g_ref[...]).astype(BF16)
    oa = oa_ref[0] + oa_ref[1]
    u = jax.nn.sigmoid(_dot(hb, wm_ref[:, 0:d])) * _dot(oa, wa_ref[...])
    u = u + jax.nn.sigmoid(_dot(hb, wm_ref[:, d:2 * d])) * _dot(ob_ref[...], wb_ref[...])
    u = u + jax.nn.sigmoid(_dot(hb, wm_ref[:, 2 * d:3 * d])) * _dot(oc_ref[...], wc_ref[...])
    out_ref[...] = x + _dot(u.astype(BF16), wo_ref[...])


def _merge(x, gain, o_nsa, o_sb, o_fox, w_merge, w_a, w_b, w_c, w_o, *, tm):
    m, d = x.shape
    row = lambda wid: pl.BlockSpec((tm, wid), lambda i: (i, 0))
    const = lambda a: pl.BlockSpec(a.shape, lambda i: (0,) * a.ndim)
    return pl.pallas_call(
        _merge_kernel, grid=(m // tm,),
        in_specs=[row(d), const(gain), pl.BlockSpec((NSA_GROUPS, tm, NSA_W), lambda i: (0, i, 0)),
                  row(SB_W), row(SB_W), const(w_merge), const(w_a), const(w_b), const(w_c), const(w_o)],
        out_specs=row(d), out_shape=jax.ShapeDtypeStruct((m, d), F32),
        compiler_params=_cp("arbitrary"), name="merge",
    )(x, gain, o_nsa, o_sb, o_fox, w_merge, w_a, w_b, w_c, w_o)


def _ffn_kernel(x_ref, g_ref, wg_ref, wu_ref, wd_ref, out_ref):
    x = x_ref[...]
    hb = _rms(x, g_ref[...]).astype(BF16)
    a = _dot(hb, wg_ref[...])
    act = a * jax.nn.sigmoid(a) * _dot(hb, wu_ref[...])
    out_ref[...] = x + _dot(act.astype(BF16), wd_ref[...])


def _ffn(x, gain, w_gate, w_up, w_down, *, tm):
    m, d = x.shape
    row = pl.BlockSpec((tm, d), lambda i: (i, 0))
    const = lambda a: pl.BlockSpec(a.shape, lambda i: (0,) * a.ndim, pipeline_mode=pl.Buffered(1))
    return pl.pallas_call(
        _ffn_kernel, grid=(m // tm,),
        in_specs=[row, pl.BlockSpec(gain.shape, lambda i: (0, 0)), const(w_gate), const(w_up), const(w_down)],
        out_specs=row, out_shape=jax.ShapeDtypeStruct((m, d), F32),
        compiler_params=_cp("arbitrary"), name="ffn",
    )(x, gain, w_gate, w_up, w_down)


def _norm_kernel(x_ref, g_ref, out_ref):
    out_ref[...] = _rms(x_ref[...], g_ref[...])


def _final_norm(x, gain, *, tm):
    m, d = x.shape
    row = pl.BlockSpec((tm, d), lambda i: (i, 0))
    return pl.pallas_call(
        _norm_kernel, grid=(m // tm,), in_specs=[row, pl.BlockSpec((1, d), lambda i: (0, 0))],
        out_specs=row, out_shape=jax.ShapeDtypeStruct((m, d), F32),
        compiler_params=_cp("arbitrary"), name="final_norm",
    )(x, gain)


def _page_specs(width, n_pages, pp, descending):
    def spec(i):
        if descending:
            return pl.BlockSpec((1, PAGE, width), lambda b, j, pt: (pt[b, n_pages - 1 - (j * pp + i)], 0, 0))
        return pl.BlockSpec((1, PAGE, width), lambda b, j, pt: (pt[b, j * pp + i], 0, 0))
    return [spec(i) for i in range(pp)]


def _one_key_tile(kv_row):
    k = jnp.broadcast_to(kv_row[:, 0:LANES], (LANES, LANES)).astype(BF16)
    v = jnp.broadcast_to(kv_row[:, LANES:2 * LANES], (LANES, LANES)).astype(BF16)
    return k, v


def _head_rows_to_lanes(acc):
    lane, row = _iota(acc.shape, 1), _iota(acc.shape, 0)
    return jnp.sum(jnp.where(lane // HEAD_DIM == row, acc, 0.0), axis=0, keepdims=True)


def _sb_dec_kernel(pt_ref, q_ref, *refs, pp):
    pages, (u_ref, out_ref, acc_scr, r_scr) = refs[:pp], refs[pp:]
    j = pl.program_id(1)

    @pl.when(j == 0)
    def _():
        acc_scr[...] = jnp.zeros_like(acc_scr)
        r_scr[...] = jnp.zeros_like(r_scr)

    q = q_ref[0]
    for i in range(pp):
        pg = pages[i][0]
        z = _dot_nt(q, pg[:, 0:SB_W].astype(BF16))
        sp = _softplus(z)
        after = _split_dot(-sp, u_ref[...], 2) + r_scr[...]
        w = jnp.exp(z - sp + after)
        acc_scr[...] += _dot(w.astype(BF16), pg[:, SB_W:2 * SB_W].astype(BF16))
        r_scr[...] += jnp.sum(-sp, axis=1, keepdims=True)

    @pl.when(j == pl.num_programs(1) - 1)
    def _():
        out_ref[0] = _head_rows_to_lanes(acc_scr[...])


def _sb_dec(q8, pool, page_table, *, pp):
    bd, n_pages = page_table.shape
    u = _upper_strict(PAGE)
    grid_spec = pltpu.PrefetchScalarGridSpec(
        num_scalar_prefetch=1, grid=(bd, n_pages // pp),
        in_specs=[pl.BlockSpec((1, 8, SB_W), lambda b, j, pt: (b, 0, 0))]
        + _page_specs(2 * SB_W, n_pages, pp, True) + [pl.BlockSpec(u.shape, lambda b, j, pt: (0, 0))],
        out_specs=pl.BlockSpec((1, 1, SB_W), lambda b, j, pt: (b, 0, 0)),
        scratch_shapes=[pltpu.VMEM((8, SB_W), F32), pltpu.VMEM((8, LANES), F32)])
    return pl.pallas_call(
        functools.partial(_sb_dec_kernel, pp=pp), grid_spec=grid_spec,
        out_shape=jax.ShapeDtypeStruct((bd, 1, SB_W), F32),
        compiler_params=_cp("arbitrary", "arbitrary"), name="sb_decode",
    )(page_table, q8, *([pool] * pp), u)


def _fox_dec_kernel(pt_ref, q_ref, kvn_ref, lfn_ref, *refs, pp):
    pages, lf_pages = refs[:pp], refs[pp:2 * pp]
    u_ref, out_ref, m_scr, l_scr, acc_scr, r_scr = refs[2 * pp:]
    j = pl.program_id(1)
    q = q_ref[0]

    @pl.when(j == 0)
    def _():
        kvn = kvn_ref[0].astype(BF16).astype(F32)
        m_scr[...] = jnp.broadcast_to(jnp.sum(q.astype(F32) * kvn[:, 0:SB_W], axis=1, keepdims=True), m_scr.shape)
        l_scr[...] = jnp.ones_like(l_scr)
        acc_scr[...] = jnp.broadcast_to(kvn[:, SB_W:2 * SB_W], acc_scr.shape)
        r_scr[...] = lfn_ref[0]

    for i in range(pp):
        pg = pages[i][0]
        lf = lf_pages[i][0]
        decay = _split_dot(lf, u_ref[...], 3) + r_scr[...]
        s = _dot_nt(q, pg[:, 0:SB_W].astype(BF16)) + decay
        _online_step(s, None, pg[:, SB_W:2 * SB_W].astype(BF16), m_scr, l_scr, acc_scr)
        r_scr[...] += jnp.sum(lf, axis=1, keepdims=True)

    @pl.when(j == pl.num_programs(1) - 1)
    def _():
        out_ref[0] = _head_rows_to_lanes(_online_result(l_scr, acc_scr))


def _fox_dec(q8, kv_new, lf_new, pool, lf_pool, page_table, *, pp):
    bd, n_pages = page_table.shape
    u = _upper_strict(PAGE)
    desc = lambda i: (lambda b, j, pt: (pt[b, n_pages - 1 - (j * pp + i)], 0, 0))
    grid_spec = pltpu.PrefetchScalarGridSpec(
        num_scalar_prefetch=1, grid=(bd, n_pages // pp),
        in_specs=[pl.BlockSpec((1, 8, SB_W), lambda b, j, pt: (b, 0, 0)),
                  pl.BlockSpec((1, 1, 2 * SB_W), lambda b, j, pt: (b, 0, 0)),
                  pl.BlockSpec((1, 8, LANES), lambda b, j, pt: (b, 0, 0))]
        + _page_specs(2 * SB_W, n_pages, pp, True)
        + [pl.BlockSpec((1, 8, PAGE), desc(i)) for i in range(pp)]
        + [pl.BlockSpec(u.shape, lambda b, j, pt: (0, 0))],
        out_specs=pl.BlockSpec((1, 1, SB_W), lambda b, j, pt: (b, 0, 0)),
        scratch_shapes=[pltpu.VMEM((8, LANES), F32), pltpu.VMEM((8, LANES), F32),
                        pltpu.VMEM((8, SB_W), F32), pltpu.VMEM((8, LANES), F32)])
    return pl.pallas_call(
        functools.partial(_fox_dec_kernel, pp=pp), grid_spec=grid_spec,
        out_shape=jax.ShapeDtypeStruct((bd, 1, SB_W), F32),
        compiler_params=_cp("arbitrary", "arbitrary"), name="fox_decode",
    )(page_table, q8, kv_new, lf_new, *([pool] * pp), *([lf_pool] * pp), u)


def _nsa_dec_kernel(pt_ref, q_ref, ckv_ref, gate_ref, win_ref, kvs_new_ref, kvw_new_ref, amat_ref, e_ref, *refs,
                    pp, past_len, n_sel):
    pages, (out_ref, win_out_ref, m_scr, l_scr, acc_scr, sel_scr, oc_scr, ow_scr) = refs[:pp], refs[pp:]
    j = pl.program_id(1)
    q = q_ref[0]
    rows = q.shape[0]
    row = _iota((rows, 1), 0)
    pos = jnp.full((rows, 1), past_len, I32)

    @pl.when(j == 0)
    def _():
        nc = ckv_ref.shape[1]
        cmask = _iota((rows, nc), 1) * CMP_STRIDE + (CMP_BLOCK - 1) <= pos
        p = _masked_softmax_rows(_dot_nt(q, ckv_ref[0, :, 0:128]), cmask)
        oc_scr[...] = _dot(p.astype(BF16), ckv_ref[0, :, 128:256])
        g0 = jnp.sum(p[0:NSA_HPG], axis=0, keepdims=True)
        g1 = jnp.sum(p[NSA_HPG:2 * NSA_HPG], axis=0, keepdims=True)
        p_grp = jnp.where(row < NSA_HPG, g0, g1)
        imp = _split_dot(p_grp, amat_ref[...], 2)
        sel_scr[...] = _select_blocks(imp, pos, n_sel)

        wk = win_ref[0]
        wn = kvw_new_ref[0]
        _online_init(m_scr, l_scr, acc_scr)
        n_w = wk.shape[0]
        for t in range(n_w // LANES):
            kt = wk[t * LANES:(t + 1) * LANES, :]
            kpos = past_len - n_w + t * LANES + _iota((rows, LANES), 1)
            mask = jnp.where(kpos >= 0, jnp.where(pos - kpos <= WINDOW, 1.0, 0.0), 0.0) > 0.5
            _online_step(_dot_nt(q, kt[:, 0:128].astype(BF16)), mask, kt[:, 128:256].astype(BF16),
                         m_scr, l_scr, acc_scr)
        kn, vn = _one_key_tile(wn)
        _online_step(_dot_nt(q, kn), _iota((rows, LANES), 1) == 0, vn, m_scr, l_scr, acc_scr)
        ow_scr[...] = _online_result(l_scr, acc_scr)
        ridx = _iota(wk.shape, 0)
        win_out_ref[0] = jnp.where(ridx == n_w - 1, jnp.broadcast_to(wn, wk.shape), pltpu.roll(wk, n_w - 1, 0))
        _online_init(m_scr, l_scr, acc_scr)

    sel = sel_scr[...].astype(BF16)
    for i in range(pp):
        pg = pages[i][0]
        chosen = _dot(sel, e_ref[j * pp + i])
        _online_step(_dot_nt(q, pg[:, 0:128].astype(BF16)), chosen > 0.5, pg[:, 128:256].astype(BF16),
                     m_scr, l_scr, acc_scr)

    @pl.when(j == pl.num_programs(1) - 1)
    def _():
        kn, vn = _one_key_tile(kvs_new_ref[0])
        cur = past_len // SEL_BLOCK
        chosen = jnp.sum(jnp.where(_iota(sel_scr.shape, 1) == cur, sel_scr[...], 0.0), axis=1, keepdims=True)
        mask = jnp.where(_iota((rows, LANES), 1) == 0, chosen, 0.0) > 0.5
        _online_step(_dot_nt(q, kn), mask, vn, m_scr, l_scr, acc_scr)
        o_s = _online_result(l_scr, acc_scr)
        gates = jnp.where(row < NSA_HPG, gate_ref[0, 0:1, :], gate_ref[0, 1:2, :])
        lane = _iota((rows, LANES), 1)
        hh = row % NSA_HPG

        def gate(c):
            return jnp.sum(jnp.where(lane == 3 * hh + c, gates, 0.0), axis=1, keepdims=True)

        o = gate(0) * oc_scr[...] + gate(1) * o_s + gate(2) * ow_scr[...]
        o = jnp.where(lane // HEAD_DIM == row // NSA_HPG, o, 0.0)
        for g in range(NSA_GROUPS):
            for h in range(NSA_HPG):
                r = g * NSA_HPG + h
                out_ref[0, g:g + 1, LANES * h:LANES * (h + 1)] = o[r:r + 1, :]


def _nsa_dec(q8, ckv, gates, win_state, kvs_new, kvw_new, pool, page_table, *, pp):
    bd, n_pages = page_table.shape
    past_len = n_pages * PAGE
    n_sel = -(-(past_len + 1) // SEL_BLOCK)
    nc = ckv.shape[1]
    wid = -(-n_sel // LANES) * LANES
    amat = _imp_matrix(nc, wid)
    etile = _expand_tiles(n_pages, wid, PAGE)
    n_w = win_state.shape[1]
    req = lambda shape: pl.BlockSpec((1,) + shape, lambda b, j, pt: (b,) + (0,) * len(shape))
    const = lambda a: pl.BlockSpec(a.shape, lambda b, j, pt: (0,) * a.ndim)
    grid_spec = pltpu.PrefetchScalarGridSpec(
        num_scalar_prefetch=1, grid=(bd, n_pages // pp),
        in_specs=[req((8, LANES)), req((nc, KV_W)), req((NSA_GROUPS, LANES)), req((n_w, KV_W)),
                  req((1, KV_W)), req((1, KV_W)), const(amat), const(etile)]
        + _page_specs(KV_W, n_pages, pp, False),
        out_specs=(req((NSA_GROUPS, NSA_W)), req((n_w, KV_W))),
        scratch_shapes=[pltpu.VMEM((8, LANES), F32)] * 3 + [pltpu.VMEM((8, wid), F32)]
        + [pltpu.VMEM((8, LANES), F32)] * 2)
    return pl.pallas_call(
        functools.partial(_nsa_dec_kernel, pp=pp, past_len=past_len, n_sel=n_sel), grid_spec=grid_spec,
        out_shape=(jax.ShapeDtypeStruct((bd, NSA_GROUPS, NSA_W), F32), jax.ShapeDtypeStruct((bd, n_w, KV_W), F32)),
        compiler_params=_cp("arbitrary", "arbitrary"), name="nsa_decode",
    )(page_table, q8, ckv, gates, win_state, kvs_new, kvw_new, amat, etile, *([pool] * pp))


def _rope_table(pos):
    half = ROT_DIM // 2
    inv = ROPE_THETA ** (-jnp.arange(half, dtype=F32) / half)
    ang = pos.astype(F32)[:, None] * inv[None, :]
    cos, sin = jnp.cos(ang), jnp.sin(ang)
    n = pos.shape[0]
    pad = HEAD_DIM - ROT_DIM
    one_head = lambda a, b, fill: jnp.concatenate([a, b, jnp.full((n, pad), fill, F32)], axis=1)
    zero = jnp.zeros_like(sin)
    tabs = [one_head(cos, cos, 1.0), one_head(-sin, zero, 0.0), one_head(zero, sin, 0.0)]
    return jnp.concatenate([jnp.tile(t, (1, LANES // HEAD_DIM)) for t in tabs], axis=1)


_Q_HEAD_ORDER = (0, 4, 1, 5, 2, 6, 3, 7)


def _proj_columns():
    q = np.concatenate([np.arange(HEAD_DIM) + HEAD_DIM * h for h in _Q_HEAD_ORDER])
    kv = np.arange(NSA_W, NSA_W + 3 * KV_W)
    gate0 = NSA_W + 3 * KV_W
    sbfox = np.arange(gate0 + GATE_W, gate0 + GATE_W + 6 * SB_W)
    gate = np.arange(gate0, gate0 + GATE_W)
    pf0 = gate0 + GATE_W + 6 * SB_W
    pf = np.arange(pf0, pf0 + FOX_HEADS)
    return np.concatenate([q, kv, sbfox, gate, pf]), pf0 + FOX_HEADS


def _layer_weights(l, w_in, b_forget, cmp_pos, w_cmp_k, w_cmp_v, w_up_nsa):
    cols, merge0 = _proj_columns()
    d = w_in.shape[1]
    w_proj = jnp.take(w_in[l], jnp.asarray(cols), axis=1)
    w_proj = jnp.concatenate([w_proj, jnp.zeros((d, N_PROJ - w_proj.shape[1]), F32)], axis=1).astype(BF16)
    w_merge = w_in[l][:, merge0:].astype(BF16)
    bias = jnp.zeros((1, LANES), F32).at[0, LOGF_LANE:LOGF_LANE + FOX_HEADS].set(b_forget[l])
    wk = w_cmp_k[l].reshape(CMP_BLOCK, HEAD_DIM, HEAD_DIM)
    wv = w_cmp_v[l].reshape(CMP_BLOCK, HEAD_DIM, HEAD_DIM)
    w_blk = jnp.zeros((CMP_BLOCK, KV_W, KV_W), F32)
    for n, wsrc in enumerate((wk, wk, wv, wv)):
        w_blk = w_blk.at[:, HEAD_DIM * n:HEAD_DIM * (n + 1), HEAD_DIM * n:HEAD_DIM * (n + 1)].set(wsrc)
    pe4 = jnp.tile(cmp_pos[l], (1, KV_W // HEAD_DIM))
    rows = np.concatenate([np.arange(HEAD_DIM) + HEAD_DIM * h for h in _Q_HEAD_ORDER])
    w_a = jnp.take(w_up_nsa[l], jnp.asarray(rows), axis=0).astype(BF16)
    return w_proj, w_merge, bias, w_blk.astype(BF16), pe4, w_a


def _pad_heads(a):
    return jnp.concatenate([a, jnp.zeros((a.shape[0], 8 - a.shape[1], a.shape[2]), a.dtype)], axis=1)


def kernel(x_prompt, x_sample, cache_nsa_cmp_kv, cache_nsa_sel_kv, cache_sb_kv, cache_fox_kv, cache_fox_logf,
           state_nsa_win_kv, page_table, norm_mix_g, norm_ffn_g, norm_final_g, w_in, b_forget, cmp_pos,
           w_cmp_k, w_cmp_v, w_up_nsa, w_up_sb, w_up_fox, w_out, w_ffn_gate, w_ffn_up, w_ffn_down):
    batch, seq, d = x_prompt.shape
    bd = x_sample.shape[0]
    depth = w_in.shape[0]
    n_pool = cache_sb_kv.shape[1]
    n_pages = page_table.shape[1]
    past_len = n_pages * PAGE
    m = batch * seq
    tm = min(256, seq)
    tq = min(256, seq)
    pp = min(PAGES_PER_STEP, n_pages)
    pp_prompt = min(PAGES_PER_STEP, seq // PAGE)

    rope_p = _rope_table(jnp.arange(seq, dtype=I32))
    rope_s = _rope_table(jnp.full((bd,), past_len, I32))
    ident_pt = jnp.arange(batch * (seq // PAGE), dtype=I32).reshape(batch, seq // PAGE)

    xp = x_prompt.reshape(m, d)
    xs = x_sample.reshape(bd, d)
    p_states, s_states = [], []
    for l in range(depth):
        w_proj, w_merge, bias, w_blk, pe4, w_a = _layer_weights(l, w_in, b_forget, cmp_pos, w_cmp_k, w_cmp_v, w_up_nsa)
        w_b, w_c, w_o = w_up_sb[l].astype(BF16), w_up_fox[l].astype(BF16), w_out[l].astype(BF16)
        w_g, w_u, w_d = w_ffn_gate[l].astype(BF16), w_ffn_up[l].astype(BF16), w_ffn_down[l].astype(BF16)
        g_mix, g_ffn = norm_mix_g[l].reshape(1, d), norm_ffn_g[l].reshape(1, d)

        (qn, kv_cmp, kv_sel, kv_win, kv_sb, kv_fox, kb_sel, kb_win, kb_sb, kb_fox, q_sb, q_fox, gates, logf, cum
         ) = _proj(xp, g_mix, w_proj, rope_p, bias, tm=tm, tiles_per_seq=seq // tm)
        ckv = _compress(kv_cmp.reshape(m // PAGE, PAGE, KV_W), ident_pt, w_blk, pe4, pp=pp_prompt)
        o_nsa = _nsa_prompt(qn, ckv, kb_sel, kb_win, gates, batch=batch, seq=seq, tq=tq)
        o_sb = _sb_prompt(q_sb, kb_sb, batch=batch, seq=seq, tq=tq)
        cum4 = cum[:, LOGF_LANE:LOGF_LANE + FOX_HEADS]
        cum_q = cum4.T[:, :, None]
        cum_k = cum4.reshape(batch, seq, FOX_HEADS).transpose(0, 2, 1).reshape(batch, FOX_HEADS, seq // tq, 1, tq)
        o_fox = _fox_prompt(q_fox, kb_fox, cum_q, cum_k, batch=batch, seq=seq, tq=tq)
        xp = _merge(xp, g_mix, o_nsa, o_sb, o_fox, w_merge, w_a, w_b, w_c, w_o, tm=tm)
        xp = _ffn(xp, g_ffn, w_g, w_u, w_d, tm=tm)
        win_keep = min(WINDOW, seq)
        p_states.append((
            kv_cmp.reshape(batch, seq, 2, NSA_GROUPS, HEAD_DIM), kv_sel.reshape(batch, seq, 2, NSA_GROUPS, HEAD_DIM),
            kv_sb.reshape(batch, seq, 2, SB_HEADS, HEAD_DIM), kv_fox.reshape(batch, seq, 2, FOX_HEADS, HEAD_DIM),
            logf[:, LOGF_LANE:LOGF_LANE + FOX_HEADS].reshape(batch, seq, FOX_HEADS),
            kv_win.reshape(batch, seq, 2, NSA_GROUPS, HEAD_DIM)[:, seq - win_keep:]))

        (qn, kv_cmp, kv_sel, kv_win, kv_sb, kv_fox, _, _, _, _, q_sb, q_fox, gates, logf, _
         ) = _proj(xs, g_mix, w_proj, rope_s, bias, tm=bd, tiles_per_seq=1)
        ckv = _compress(cache_nsa_cmp_kv[l].reshape(n_pool, PAGE, KV_W), page_table, w_blk, pe4, pp=pp)
        q8 = qn.transpose(2, 0, 1, 3).reshape(bd, NSA_HEADS, LANES)
        o_nsa, win_new = _nsa_dec(
            q8, ckv, gates.transpose(1, 0, 2), state_nsa_win_kv[l].reshape(bd, -1, KV_W),
            kv_sel.reshape(bd, 1, KV_W), kv_win.reshape(bd, 1, KV_W),
            cache_nsa_sel_kv[l].reshape(n_pool, PAGE, KV_W), page_table, pp=pp)
        o_sb = _sb_dec(_pad_heads(q_sb.transpose(1, 0, 2)), cache_sb_kv[l].reshape(n_pool, PAGE, 2 * SB_W),
                       page_table, pp=pp)
        lf4 = logf[:, LOGF_LANE:LOGF_LANE + FOX_HEADS]
        lf_new = _pad_heads(jnp.broadcast_to(lf4[:, :, None], (bd, FOX_HEADS, LANES)))
        lf_pool = _pad_heads(cache_fox_logf[l].astype(F32).transpose(0, 2, 1))
        o_fox = _fox_dec(_pad_heads(q_fox.transpose(1, 0, 2)), kv_fox.reshape(bd, 1, 2 * SB_W), lf_new,
                         cache_fox_kv[l].reshape(n_pool, PAGE, 2 * SB_W), lf_pool, page_table, pp=pp)
        xs = _merge(xs, g_mix, o_nsa.transpose(1, 0, 2).astype(BF16), o_sb.reshape(bd, SB_W).astype(BF16),
                    o_fox.reshape(bd, SB_W).astype(BF16), w_merge, w_a, w_b, w_c, w_o, tm=bd)
        xs = _ffn(xs, g_ffn, w_g, w_u, w_d, tm=bd)
        s_states.append((
            kv_cmp.reshape(bd, 1, 2, NSA_GROUPS, HEAD_DIM), kv_sel.reshape(bd, 1, 2, NSA_GROUPS, HEAD_DIM),
            kv_sb.reshape(bd, 1, 2, SB_HEADS, HEAD_DIM), kv_fox.reshape(bd, 1, 2, FOX_HEADS, HEAD_DIM),
            lf4.reshape(bd, 1, FOX_HEADS), win_new.reshape(bd, -1, 2, NSA_GROUPS, HEAD_DIM)))

    g_fin = norm_final_g.reshape(1, d)
    y_prompt = _final_norm(xp, g_fin, tm=tm).reshape(batch, seq, d)
    y_sample = _final_norm(xs, g_fin, tm=bd).reshape(bd, 1, d)
    stk = lambda sts, i: jnp.stack([st[i] for st in sts], axis=0)
    return (y_prompt, y_sample) + tuple(stk(p_states, i) for i in range(6)) + tuple(stk(s_states, i) for i in range(6))
```

```python
import functools

import numpy as np
import jax
import jax.numpy as jnp
from jax import lax
from jax.experimental import pallas as pl
from jax.experimental.pallas import tpu as pltpu

F32, BF16, I32 = jnp.float32, jnp.bfloat16, jnp.int32

HEAD_DIM = 64
ROT_DIM = HEAD_DIM // 4
ROPE_THETA = 500000.0
NSA_HEADS = 8
NSA_GROUPS = 2
NSA_HPG = NSA_HEADS // NSA_GROUPS
CMP_STRIDE = 16
CMP_BLOCK = 32
SEL_BLOCK = 64
TOP_N = 16
WINDOW = 512
SB_HEADS = 4
FOX_HEADS = 4
PAGE = 128
SEL_FORCE_SCORE = 1.0e4
NEG = -1.0e30
RMS_EPS = 1e-6
SCALE = HEAD_DIM ** -0.5

LANES = 128
NSA_W = NSA_HEADS * HEAD_DIM
KV_W = 2 * NSA_GROUPS * HEAD_DIM
SB_W = SB_HEADS * HEAD_DIM
GATE_W = 3 * NSA_HEADS
C_Q, C_CMP, C_SEL, C_WIN, C_SB, C_FOX, C_MISC = 0, 512, 768, 1024, 1280, 2048, 2816
N_PROJ = C_MISC + LANES
LOGF_LANE = GATE_W
VMEM_LIMIT = 56 * 1024 * 1024
PAGES_PER_STEP = 8


def _cp(*sem):
    return pltpu.CompilerParams(dimension_semantics=sem, vmem_limit_bytes=VMEM_LIMIT)


def _dot(a, b):
    return jnp.dot(a, b, preferred_element_type=F32)


def _dot_nt(a, b):
    return lax.dot_general(a, b, (((1,), (1,)), ((), ())), preferred_element_type=F32)


def _split_dot(a, r, parts):
    out, rem = None, a
    for _ in range(parts):
        hi = rem.astype(BF16)
        d = _dot(hi, r)
        out = d if out is None else out + d
        rem = rem - hi.astype(F32)
    return out


def _split_dot_l(l, a, parts):
    out, rem = None, a
    for _ in range(parts):
        hi = rem.astype(BF16)
        d = _dot(l, hi)
        out = d if out is None else out + d
        rem = rem - hi.astype(F32)
    return out


def _rms(x, g):
    return x * lax.rsqrt(jnp.mean(x * x, axis=-1, keepdims=True) + RMS_EPS) * g


def _softplus(z):
    return jnp.maximum(z, 0.0) + jnp.log(1.0 + jnp.exp(-jnp.abs(z)))


def _iota(shape, axis):
    return lax.broadcasted_iota(I32, shape, axis)


def _proj_kernel(x_ref, g_ref, w_ref, rope_ref, b_ref, tri_ref,
                 qn_ref, kvc_ref, kvs_ref, kvw_ref, kvsb_ref, kvfx_ref,
                 kbs_ref, kbw_ref, kbsb_ref, kbfx_ref, qsb_ref, qfx_ref,
                 gate_ref, logf_ref, cum_ref, h_scr, carry_scr, *, tiles_per_seq):
    i = pl.program_id(0)
    tm = x_ref.shape[0]
    h_scr[...] = _rms(x_ref[...], g_ref[...]).astype(BF16)
    hb = h_scr[...]
    cos, sin_lo, sin_hi = rope_ref[:, 0:128], rope_ref[:, 128:256], rope_ref[:, 256:384]

    def rope(seg):
        return seg * cos + pltpu.roll(seg, LANES - 8, 1) * sin_lo + pltpu.roll(seg, 8, 1) * sin_hi

    lane = _iota((tm, LANES), 1)
    pq = _dot(hb, w_ref[:, C_Q:C_Q + NSA_W])
    for s in range(NSA_HPG):
        seg = rope(pq[:, LANES * s:LANES * (s + 1)]) * SCALE
        for g in range(NSA_GROUPS):
            qn_ref[g, s] = jnp.where(lane // HEAD_DIM == g, seg, 0.0).astype(BF16)

    for c0, f_ref, h_ref in ((C_CMP, kvc_ref, None), (C_SEL, kvs_ref, kbs_ref), (C_WIN, kvw_ref, kbw_ref)):
        p = _dot(hb, w_ref[:, c0:c0 + KV_W])
        k, v = rope(p[:, 0:128]), p[:, 128:256]
        f_ref[:, 0:128] = k
        f_ref[:, 128:256] = v
        if h_ref is not None:
            h_ref[:, 0:128] = k.astype(BF16)
            h_ref[:, 128:256] = v.astype(BF16)

    lane_w = _iota((tm, SB_W), 1)
    for c0, q_ref, f_ref, h_ref in ((C_SB, qsb_ref, kvsb_ref, kbsb_ref), (C_FOX, qfx_ref, kvfx_ref, kbfx_ref)):
        p = _dot(hb, w_ref[:, c0:c0 + 3 * SB_W])
        q = p[:, 0:SB_W] * SCALE
        for h in range(SB_HEADS):
            q_ref[h] = jnp.where(lane_w // HEAD_DIM == h, q, 0.0).astype(BF16)
        kv = p[:, SB_W:3 * SB_W]
        f_ref[...] = kv
        h_ref[...] = kv.astype(BF16)

    pm = _dot(hb, w_ref[:, C_MISC:C_MISC + LANES])
    sg = jax.nn.sigmoid(pm)
    gate_ref[0] = sg
    gate_ref[1] = pltpu.roll(sg, LANES - GATE_W // 2, 1)
    zf = pm + b_ref[...]
    lf = jnp.minimum(zf, 0.0) - jnp.log1p(jnp.exp(-jnp.abs(zf)))
    logf_ref[...] = lf

    @pl.when(i % tiles_per_seq == 0)
    def _():
        carry_scr[...] = jnp.zeros_like(carry_scr)

    c = _split_dot_l(tri_ref[...], lf, 3) + carry_scr[0:1, :]
    cum_ref[...] = c
    carry_scr[...] = jnp.broadcast_to(c[tm - 1:tm, :], carry_scr.shape)


def _proj(x, gain, w, rope_tab, bias, *, tm, tiles_per_seq):
    m, d = x.shape
    tri = jnp.tril(jnp.ones((tm, tm), F32)).astype(BF16)
    row = lambda wid: pl.BlockSpec((tm, wid), lambda i: (i, 0))
    const = lambda shape: pl.BlockSpec(shape, lambda i: (0,) * len(shape))
    f32o = lambda wid: jax.ShapeDtypeStruct((m, wid), F32)
    b16o = lambda wid: jax.ShapeDtypeStruct((m, wid), BF16)
    out_shape = (
        jax.ShapeDtypeStruct((NSA_GROUPS, NSA_HPG, m, LANES), BF16),
        f32o(KV_W), f32o(KV_W), f32o(KV_W), f32o(2 * SB_W), f32o(2 * SB_W),
        b16o(KV_W), b16o(KV_W), b16o(2 * SB_W), b16o(2 * SB_W),
        jax.ShapeDtypeStruct((SB_HEADS, m, SB_W), BF16), jax.ShapeDtypeStruct((SB_HEADS, m, SB_W), BF16),
        jax.ShapeDtypeStruct((NSA_GROUPS, m, LANES), F32), f32o(LANES), f32o(LANES),
    )
    out_specs = (
        pl.BlockSpec((NSA_GROUPS, NSA_HPG, tm, LANES), lambda i: (0, 0, i, 0)),
        row(KV_W), row(KV_W), row(KV_W), row(2 * SB_W), row(2 * SB_W),
        row(KV_W), row(KV_W), row(2 * SB_W), row(2 * SB_W),
        pl.BlockSpec((SB_HEADS, tm, SB_W), lambda i: (0, i, 0)),
        pl.BlockSpec((SB_HEADS, tm, SB_W), lambda i: (0, i, 0)),
        pl.BlockSpec((NSA_GROUPS, tm, LANES), lambda i: (0, i, 0)),
        row(LANES), row(LANES),
    )
    return pl.pallas_call(
        functools.partial(_proj_kernel, tiles_per_seq=tiles_per_seq),
        grid=(m // tm,),
        in_specs=[row(d), const((1, d)), const((d, N_PROJ)),
                  pl.BlockSpec((tm, 3 * LANES), lambda i: (i % tiles_per_seq, 0)),
                  const((1, LANES)), const((tm, tm))],
        out_specs=out_specs, out_shape=out_shape,
        scratch_shapes=[pltpu.VMEM((tm, d), BF16), pltpu.VMEM((8, LANES), F32)],
        compiler_params=_cp("arbitrary"), name="proj",
    )(x, gain, w, rope_tab, bias, tri)


def _compress_kernel(pt_ref, *refs, pp, nch):
    pages, (w_ref, pe_ref, out_ref, rows_scr, acc_scr) = refs[:pp], refs[pp:]
    j = pl.program_id(1)

    @pl.when(j == 0)
    def _():
        for half in range(KV_W // LANES):
            rows_scr[half, pl.ds(nch * CMP_STRIDE, CMP_STRIDE), :] = jnp.zeros((CMP_STRIDE, LANES), F32)

    for i in range(pp):
        for half in range(KV_W // LANES):
            rows_scr[half, pl.ds(pl.multiple_of((j * pp + i) * PAGE, PAGE), PAGE), :] = (
                pages[i][0, :, half * LANES:(half + 1) * LANES])

    @pl.when(j == pl.num_programs(1) - 1)
    def _():
        acc_scr[...] = jnp.zeros_like(acc_scr)
        for l in range(CMP_BLOCK):
            xl = jnp.concatenate([rows_scr[half, pl.ds(l, nch, stride=CMP_STRIDE), :]
                                  for half in range(KV_W // LANES)], axis=1) + pe_ref[l:l + 1, :]
            acc_scr[...] += _dot(xl.astype(BF16), w_ref[l])
        out_ref[0] = acc_scr[...].astype(BF16)


def _compress(pool, page_table, w_blk, pe4, *, pp):
    b, n_pages = page_table.shape
    nch = n_pages * (PAGE // CMP_STRIDE)
    page_spec = lambda i: pl.BlockSpec((1, PAGE, KV_W), lambda bb, j, pt: (pt[bb, j * pp + i], 0, 0))
    grid_spec = pltpu.PrefetchScalarGridSpec(
        num_scalar_prefetch=1, grid=(b, n_pages // pp),
        in_specs=[page_spec(i) for i in range(pp)] + [
            pl.BlockSpec((CMP_BLOCK, KV_W, KV_W), lambda bb, j, pt: (0, 0, 0)),
            pl.BlockSpec((CMP_BLOCK, KV_W), lambda bb, j, pt: (0, 0))],
        out_specs=pl.BlockSpec((1, nch, KV_W), lambda bb, j, pt: (bb, 0, 0)),
        scratch_shapes=[pltpu.VMEM((KV_W // LANES, nch * CMP_STRIDE + CMP_BLOCK, LANES), F32),
                        pltpu.VMEM((nch, KV_W), F32)])
    return pl.pallas_call(
        functools.partial(_compress_kernel, pp=pp, nch=nch), grid_spec=grid_spec,
        out_shape=jax.ShapeDtypeStruct((b, nch, KV_W), BF16),
        compiler_params=_cp("arbitrary", "arbitrary"), name="compress",
    )(page_table, *([pool] * pp), w_blk, pe4)


def _compress_cache_kernel(pt_ref, *refs, pp, nch):
    pages, (w_ref, pe_ref, out_ref, stage, acc_scr) = refs[:pp], refs[pp:]
    j = pl.program_id(1)
    nc4 = 2 * NSA_GROUPS
    rows = PAGE * nc4

    @pl.when((pl.program_id(0) == 0) & (j == 0))
    def _():
        stage[...] = jnp.zeros_like(stage)

    for i in range(pp):
        stage[pl.ds(pl.multiple_of((j * pp + i) * rows, rows), rows), 0:HEAD_DIM] = pages[i][0]

    @pl.when(j == pl.num_programs(1) - 1)
    def _():
        acc_scr[...] = jnp.zeros_like(acc_scr)
        for l in range(CMP_BLOCK):
            for c in range(nc4):
                xl = stage[pl.ds(nc4 * l + c, nch, stride=nc4 * CMP_STRIDE), :] + pe_ref[l:l + 1, :]
                half = slice(LANES * (c // NSA_GROUPS), LANES * (c // NSA_GROUPS + 1))
                acc_scr[:, half] += _dot(xl.astype(BF16), w_ref[c, l])
        out_ref[0] = acc_scr[...].astype(BF16)


def _compress_cache(pool, page_table, w_pad, pe_pad, *, pp):
    b, n_pages = page_table.shape
    nch = n_pages * (PAGE // CMP_STRIDE)
    nc4 = 2 * NSA_GROUPS
    rows = PAGE * nc4
    grid_spec = pltpu.PrefetchScalarGridSpec(
        num_scalar_prefetch=1, grid=(b, n_pages // pp),
        in_specs=_page_specs(rows, n_pages, pp, False) + [
            pl.BlockSpec(w_pad.shape, lambda bb, j, pt: (0, 0, 0, 0)),
            pl.BlockSpec(pe_pad.shape, lambda bb, j, pt: (0, 0))],
        out_specs=pl.BlockSpec((1, nch, KV_W), lambda bb, j, pt: (bb, 0, 0)),
        scratch_shapes=[pltpu.VMEM((n_pages * rows + nc4 * CMP_STRIDE, LANES), F32), pltpu.VMEM((nch, KV_W), F32)])
    return pl.pallas_call(
        functools.partial(_compress_cache_kernel, pp=pp, nch=nch), grid_spec=grid_spec,
        out_shape=jax.ShapeDtypeStruct((b, nch, KV_W), BF16),
        compiler_params=_cp("arbitrary", "arbitrary"), name="compress_cache",
    )(page_table, *([pool] * pp), w_pad, pe_pad)


def _masked_softmax_rows(s, mask):
    s = jnp.where(mask, s, NEG)
    e = jnp.where(mask, jnp.exp(s - jnp.max(s, axis=1, keepdims=True)), 0.0)
    return e / jnp.maximum(jnp.sum(e, axis=1, keepdims=True), 1e-30)


def _online_step(s, mask, v, m_scr, l_scr, acc_scr):
    if mask is not None:
        s = jnp.where(mask, s, NEG)
    m_prev = m_scr[...]
    m_next = jnp.maximum(m_prev, jnp.max(s, axis=1, keepdims=True))
    alpha = jnp.exp(m_prev - m_next)
    e = jnp.exp(s - m_next)
    if mask is not None:
        e = jnp.where(mask, e, 0.0)
    l_scr[...] = alpha * l_scr[...] + jnp.sum(e, axis=1, keepdims=True)
    reps = acc_scr.shape[1] // LANES
    alpha_w = alpha if reps == 1 else jnp.concatenate([alpha] * reps, axis=1)
    acc_scr[...] = acc_scr[...] * alpha_w + _dot(e.astype(BF16), v)
    m_scr[...] = m_next


def _online_init(m_scr, l_scr, acc_scr):
    m_scr[...] = jnp.full_like(m_scr, NEG)
    l_scr[...] = jnp.zeros_like(l_scr)
    acc_scr[...] = jnp.zeros_like(acc_scr)


def _online_result(l_scr, acc_scr):
    l = jnp.maximum(l_scr[...], 1e-30)
    reps = acc_scr.shape[1] // LANES
    return acc_scr[...] / (l if reps == 1 else jnp.concatenate([l] * reps, axis=1))


def _select_blocks(imp, pos, n_sel):
    blk = _iota(imp.shape, 1)
    cur = pos // SEL_BLOCK
    valid = blk * SEL_BLOCK <= pos
    forced = jnp.where(blk == 0, 1.0, 0.0) + jnp.where(blk == cur, 1.0, 0.0) + jnp.where(blk == cur - 1, 1.0, 0.0)
    score = jnp.where(valid, jnp.where(forced > 0.5, SEL_FORCE_SCORE, imp), -1.0)
    score = jnp.where(blk < n_sel, score, -2.0)
    rank = jnp.zeros(imp.shape, F32)
    for b2 in range(n_sel):
        col = score[:, b2:b2 + 1]
        ge = jnp.where(col >= score, 1.0, 0.0)
        gt = jnp.where(col > score, 1.0, 0.0)
        rank = rank + jnp.where(blk > b2, ge, gt)
    return jnp.where(rank < min(TOP_N, n_sel), 1.0, 0.0)


def _rep(x, n):
    return x if n == 1 else jnp.concatenate([x] * n, axis=1)


def _select_blocks_t(p_grp, amat_t, q0, n_sel):
    t = p_grp.shape[0]
    nb = amat_t.shape[0]
    hi = p_grp.astype(BF16)
    lo = (p_grp - hi.astype(F32)).astype(BF16)
    imp = _dot_nt(amat_t, hi) + _dot_nt(amat_t, lo)
    pos = q0 + _iota((1, t), 1)
    blk = _iota((nb, t), 0)
    cur = pos // SEL_BLOCK
    forced = jnp.where(blk == 0, 1.0, 0.0) + jnp.where(blk == cur, 1.0, 0.0) + jnp.where(blk == cur - 1, 1.0, 0.0)
    score = jnp.where(blk * SEL_BLOCK <= pos, jnp.where(forced > 0.5, SEL_FORCE_SCORE, imp), -1.0)
    score = jnp.where(blk < n_sel, score, -2.0)
    groups = [score[8 * k:8 * (k + 1)] for k in range(nb // 8)]
    ranks = [jnp.zeros((8, t), F32) for _ in groups]
    sub = _iota((8, t), 0)
    for b2 in range(n_sel):
        rowv = score[b2:b2 + 1, :]
        k2, r2 = divmod(b2, 8)
        for k, grp in enumerate(groups):
            if k > k2:
                inc = jnp.where(rowv >= grp, 1.0, 0.0)
            elif k < k2:
                inc = jnp.where(rowv > grp, 1.0, 0.0)
            else:
                inc = jnp.where(sub > r2, jnp.where(rowv >= grp, 1.0, 0.0), jnp.where(rowv > grp, 1.0, 0.0))
            ranks[k] = ranks[k] + inc
    chosen = [jnp.where(rk < min(TOP_N, n_sel), 1.0, 0.0) for rk in ranks]
    pad = [jnp.zeros((LANES - nb, t), F32)] if nb < LANES else []
    return jnp.concatenate(chosen + pad, axis=0)


def _nsa_prompt_kernel(qn_ref, ckv_ref, ks_ref, kw_ref, gate_ref, amat_ref, e_ref, out_ref,
                       m_scr, l_scr, acc_scr, oc_scr, os_scr, *, n_sel):
    g, i = pl.program_id(1), pl.program_id(2)
    t = qn_ref.shape[2]
    q0 = i * t
    rowi, coli = _iota((t, t), 0), _iota((t, t), 1)
    reps = t // LANES

    nc = ckv_ref.shape[1]
    cmask = _iota((t, nc), 1) * CMP_STRIDE + (CMP_BLOCK - 1) <= q0 + _iota((t, 1), 0)
    ck, cv = ckv_ref[0, :, 0:128], ckv_ref[0, :, 128:256]
    p_grp = None
    for h in range(NSA_HPG):
        p = _masked_softmax_rows(_dot_nt(qn_ref[0, h], ck), cmask)
        oc_scr[h] = _dot(p.astype(BF16), cv)
        p_grp = p if p_grp is None else p_grp + p
    sel = _select_blocks_t(p_grp, amat_ref[...], q0, n_sel).T.astype(BF16)

    def online_head(h, s, v):
        m_prev = m_scr[h]
        m_next = jnp.maximum(m_prev, jnp.max(s, axis=1, keepdims=True))
        alpha = jnp.exp(m_prev - m_next)
        e = jnp.exp(s - _rep(m_next, reps))
        l_scr[h] = alpha * l_scr[h] + jnp.sum(e, axis=1, keepdims=True)
        m_scr[h] = m_next
        acc_scr[h] = acc_scr[h] * alpha + _dot(e.astype(BF16), v)

    def reset():
        m_scr[...] = jnp.full_like(m_scr, NEG)
        l_scr[...] = jnp.zeros_like(l_scr)
        acc_scr[...] = jnp.zeros_like(acc_scr)

    def sel_tile(j, diag):
        kt = ks_ref[0, pl.ds(pl.multiple_of(j * t, t), t), :]
        chosen = _dot(sel, e_ref[j])
        if diag:
            chosen = jnp.where(coli <= rowi, chosen, 0.0)
        keep = chosen > 0.5
        for h in range(NSA_HPG):
            online_head(h, jnp.where(keep, _dot_nt(qn_ref[0, h], kt[:, 0:128]), NEG), kt[:, 128:256])

    reset()
    lax.fori_loop(0, i, lambda j, c: (sel_tile(j, False), c)[1], 0)
    sel_tile(i, True)
    for h in range(NSA_HPG):
        os_scr[h] = acc_scr[h] / jnp.maximum(l_scr[h], 1e-30)

    def win_tile(j, c):
        kt = kw_ref[0, pl.ds(pl.multiple_of(j * t, t), t), :]
        d = (i - j) * t + rowi - coli
        keep = jnp.where(d >= 0, jnp.where(d <= WINDOW, 1.0, 0.0), 0.0) > 0.5
        for h in range(NSA_HPG):
            online_head(h, jnp.where(keep, _dot_nt(qn_ref[0, h], kt[:, 0:128]), NEG), kt[:, 128:256])
        return c

    reset()
    lax.fori_loop(jnp.maximum(i - WINDOW // t, 0), i + 1, win_tile, 0)

    gates = gate_ref[0]
    lane = _iota((t, LANES), 1)
    for h in range(NSA_HPG):
        o_w = acc_scr[h] / jnp.maximum(l_scr[h], 1e-30)
        o = (gates[:, 3 * h:3 * h + 1] * oc_scr[h] + gates[:, 3 * h + 1:3 * h + 2] * os_scr[h]
             + gates[:, 3 * h + 2:3 * h + 3] * o_w)
        out_ref[0, :, LANES * h:LANES * (h + 1)] = jnp.where(lane // HEAD_DIM == g, o, 0.0).astype(BF16)


def _imp_matrix(nc, width):
    n = np.arange(nc)[:, None]
    b = np.arange(width)[None, :]
    sub = SEL_BLOCK // CMP_STRIDE
    return jnp.asarray((n // sub == b).astype(np.float32) + ((n + 1) // sub == b), BF16)


def _expand_tiles(n_tiles, n_blk, tk):
    key = np.arange(n_tiles * tk).reshape(n_tiles, 1, tk)
    return jnp.asarray(key // SEL_BLOCK == np.arange(n_blk)[None, :, None], BF16)


def _nsa_prompt(qn, ckv, kb_sel, kb_win, gates, *, batch, seq, tq):
    nq = seq // tq
    nc = ckv.shape[1]
    n_sel = seq // SEL_BLOCK
    assert n_sel <= LANES and WINDOW % tq == 0
    amat = _imp_matrix(nc, -(-n_sel // 8) * 8).T
    etile = _expand_tiles(nq, LANES, tq)
    m = batch * seq
    return pl.pallas_call(
        functools.partial(_nsa_prompt_kernel, n_sel=n_sel),
        grid=(batch, NSA_GROUPS, nq),
        in_specs=[
            pl.BlockSpec((1, NSA_HPG, tq, LANES), lambda b, g, i: (g, 0, b * nq + i, 0)),
            pl.BlockSpec((1, nc, KV_W), lambda b, g, i: (b, 0, 0)),
            pl.BlockSpec((1, seq, KV_W), lambda b, g, i: (b, 0, 0)),
            pl.BlockSpec((1, seq, KV_W), lambda b, g, i: (b, 0, 0)),
            pl.BlockSpec((1, tq, LANES), lambda b, g, i: (g, b * nq + i, 0)),
            pl.BlockSpec(amat.shape, lambda b, g, i: (0, 0)),
            pl.BlockSpec(etile.shape, lambda b, g, i: (0, 0, 0)),
        ],
        out_specs=pl.BlockSpec((1, tq, NSA_W), lambda b, g, i: (g, b * nq + i, 0)),
        out_shape=jax.ShapeDtypeStruct((NSA_GROUPS, m, NSA_W), BF16),
        scratch_shapes=[pltpu.VMEM((NSA_HPG, tq, LANES), F32)] * 5,
        compiler_params=_cp("arbitrary", "arbitrary", "arbitrary"), name="nsa_prompt",
    )(qn, ckv, kb_sel.reshape(batch, seq, KV_W), kb_win.reshape(batch, seq, KV_W), gates, amat, etile)


def _upper_strict(n):
    return jnp.asarray(np.triu(np.ones((n, n), np.float32), 1).T, BF16)


def _stack_heads(v):
    lane = _iota(v.shape, 1)
    return jnp.concatenate([jnp.where(lane // HEAD_DIM == h, v, jnp.zeros_like(v)) for h in range(SB_HEADS)], axis=0)


def _head_lanes(parts):
    lane = _iota(parts[0].shape, 1)
    return jnp.concatenate([jnp.where(lane < HEAD_DIM, parts[0], parts[1]),
                            jnp.where(lane < HEAD_DIM, parts[2], parts[3])], axis=1)


def _sb_prompt_kernel(q_ref, kv_ref, u_ref, out_ref, acc_scr, r_scr):
    i = pl.program_id(1)
    t = q_ref.shape[1]
    rowi, coli = _iota((t, t), 0), _iota((t, t), 1)
    acc_scr[...] = jnp.zeros_like(acc_scr)
    r_scr[...] = jnp.zeros_like(r_scr)

    def tile(j, diag):
        kt = kv_ref[0, pl.ds(pl.multiple_of(j * t, t), t), :]
        ws = []
        for h in range(SB_HEADS):
            z = _dot_nt(q_ref[h], kt[:, 0:SB_W])
            sp = _softplus(z)
            log_keep = -sp
            if diag:
                log_keep = jnp.where(coli < rowi, log_keep, 0.0)
            r = r_scr[h]
            w = jnp.exp(z - sp + _split_dot(log_keep, u_ref[...], 2) + _rep(r, t // LANES))
            if diag:
                w = jnp.where(coli < rowi, w, 0.0)
            ws.append(w.astype(BF16))
            r_scr[h] = r + jnp.sum(log_keep, axis=1, keepdims=True)
        acc_scr[...] += _dot(jnp.concatenate(ws, axis=1), _stack_heads(kt[:, SB_W:2 * SB_W]))

    tile(i, True)
    lax.fori_loop(0, i, lambda n, c: (tile(i - 1 - n, False), c)[1], 0)
    out_ref[...] = acc_scr[...].astype(BF16)


def _sb_prompt(q, kb, *, batch, seq, tq):
    nq = seq // tq
    u = _upper_strict(tq)
    return pl.pallas_call(
        _sb_prompt_kernel, grid=(batch, nq),
        in_specs=[pl.BlockSpec((SB_HEADS, tq, SB_W), lambda b, i: (0, b * nq + i, 0)),
                  pl.BlockSpec((1, seq, 2 * SB_W), lambda b, i: (b, 0, 0)),
                  pl.BlockSpec(u.shape, lambda b, i: (0, 0))],
        out_specs=pl.BlockSpec((tq, SB_W), lambda b, i: (b * nq + i, 0)),
        out_shape=jax.ShapeDtypeStruct((batch * seq, SB_W), BF16),
        scratch_shapes=[pltpu.VMEM((tq, SB_W), F32), pltpu.VMEM((SB_HEADS, tq, LANES), F32)],
        compiler_params=_cp("arbitrary", "arbitrary"), name="sb_prompt",
    )(q, kb.reshape(batch, seq, 2 * SB_W), u)


def _fox_prompt_kernel(q_ref, kv_ref, cq_ref, ck_ref, out_ref, m_scr, l_scr, acc_scr, cq_scr):
    i = pl.program_id(1)
    t = q_ref.shape[1]
    rowi, coli = _iota((t, t), 0), _iota((t, t), 1)
    m_scr[...] = jnp.full_like(m_scr, NEG)
    l_scr[...] = jnp.zeros_like(l_scr)
    acc_scr[...] = jnp.zeros_like(acc_scr)
    for h in range(FOX_HEADS):
        cq_scr[h] = jnp.broadcast_to(cq_ref[h], (t, LANES))

    def tile(j, diag):
        kt = kv_ref[0, pl.ds(pl.multiple_of(j * t, t), t), :]
        es, alphas = [], []
        for h in range(FOX_HEADS):
            s = _dot_nt(q_ref[h], kt[:, 0:SB_W]) + (_rep(cq_scr[h], t // LANES) - ck_ref[0, h, j])
            if diag:
                s = jnp.where(coli <= rowi, s, NEG)
            m_prev = m_scr[h]
            m_next = jnp.maximum(m_prev, jnp.max(s, axis=1, keepdims=True))
            alpha = jnp.exp(m_prev - m_next)
            e = jnp.exp(s - _rep(m_next, t // LANES))
            l_scr[h] = alpha * l_scr[h] + jnp.sum(e, axis=1, keepdims=True)
            m_scr[h] = m_next
            es.append(e.astype(BF16))
            alphas.append(alpha)
        acc_scr[...] = acc_scr[...] * _head_lanes(alphas) + _dot(jnp.concatenate(es, axis=1),
                                                                 _stack_heads(kt[:, SB_W:2 * SB_W]))

    lax.fori_loop(0, i, lambda j, c: (tile(j, False), c)[1], 0)
    tile(i, True)
    out_ref[...] = (acc_scr[...] / _head_lanes([jnp.maximum(l_scr[h], 1e-30) for h in range(FOX_HEADS)])).astype(BF16)


def _fox_prompt(q, kb, cum_q, cum_k, *, batch, seq, tq):
    nq = seq // tq
    nk = seq // tq
    return pl.pallas_call(
        _fox_prompt_kernel, grid=(batch, nq),
        in_specs=[pl.BlockSpec((FOX_HEADS, tq, SB_W), lambda b, i: (0, b * nq + i, 0)),
                  pl.BlockSpec((1, seq, 2 * SB_W), lambda b, i: (b, 0, 0)),
                  pl.BlockSpec((FOX_HEADS, tq, 1), lambda b, i: (0, b * nq + i, 0)),
                  pl.BlockSpec((1, FOX_HEADS, nk, 1, tq), lambda b, i: (b, 0, 0, 0, 0))],
        out_specs=pl.BlockSpec((tq, SB_W), lambda b, i: (b * nq + i, 0)),
        out_shape=jax.ShapeDtypeStruct((batch * seq, SB_W), BF16),
        scratch_shapes=[pltpu.VMEM((FOX_HEADS, tq, LANES), F32), pltpu.VMEM((FOX_HEADS, tq, LANES), F32),
                        pltpu.VMEM((tq, SB_W), F32), pltpu.VMEM((FOX_HEADS, tq, LANES), F32)],
        compiler_params=_cp("arbitrary", "arbitrary"), name="fox_prompt",
    )(q, kb.reshape(batch, seq, 2 * SB_W), cum_q, cum_k)


def _merge_kernel(x_ref, g_ref, oa_ref, ob_ref, oc_ref, wm_ref, wa_ref, wb_ref, wc_ref, wo_ref, out_ref):
    x = x_ref[...]
    d = x.shape[1]
    hb = _rms(x, g_ref[...]).astype(BF16)
    oa = oa_ref[0] + oa_ref[1]
    u = jax.nn.sigmoid(_dot(hb, wm_ref[:, 0:d])) * _dot(oa, wa_ref[...])
    u = u + jax.nn.sigmoid(_dot(hb, wm_ref[:, d:2 * d])) * _dot(ob_ref[...], wb_ref[...])
    u = u + jax.nn.sigmoid(_dot(hb, wm_ref[:, 2 * d:3 * d])) * _dot(oc_ref[...], wc_ref[...])
    out_ref[...] = x + _dot(u.astype(BF16), wo_ref[...])


def _merge(x, gain, o_nsa, o_sb, o_fox, w_merge, w_a, w_b, w_c, w_o, *, tm):
    m, d = x.shape
    row = lambda wid: pl.BlockSpec((tm, wid), lambda i: (i, 0))
    const = lambda a: pl.BlockSpec(a.shape, lambda i: (0,) * a.ndim)
    return pl.pallas_call(
        _merge_kernel, grid=(m // tm,),
        in_specs=[row(d), const(gain), pl.BlockSpec((NSA_GROUPS, tm, NSA_W), lambda i: (0, i, 0)),
                  row(SB_W), row(SB_W), const(w_merge), const(w_a), const(w_b), const(w_c), const(w_o)],
        out_specs=row(d), out_shape=jax.ShapeDtypeStruct((m, d), F32),
        compiler_params=_cp("arbitrary"), name="merge",
    )(x, gain, o_nsa, o_sb, o_fox, w_merge, w_a, w_b, w_c, w_o)


def _ffn_kernel(x_ref, g_ref, wg_ref, wu_ref, wd_ref, out_ref):
    x = x_ref[...]
    hb = _rms(x, g_ref[...]).astype(BF16)
    a = _dot(hb, wg_ref[...])
    act = a * jax.nn.sigmoid(a) * _dot(hb, wu_ref[...])
    out_ref[...] = x + _dot(act.astype(BF16), wd_ref[...])


def _ffn(x, gain, w_gate, w_up, w_down, *, tm):
    m, d = x.shape
    row = pl.BlockSpec((tm, d), lambda i: (i, 0))
    const = lambda a: pl.BlockSpec(a.shape, lambda i: (0,) * a.ndim, pipeline_mode=pl.Buffered(1))
    return pl.pallas_call(
        _ffn_kernel, grid=(m // tm,),
        in_specs=[row, pl.BlockSpec(gain.shape, lambda i: (0, 0)), const(w_gate), const(w_up), const(w_down)],
        out_specs=row, out_shape=jax.ShapeDtypeStruct((m, d), F32),
        compiler_params=_cp("arbitrary"), name="ffn",
    )(x, gain, w_gate, w_up, w_down)


def _norm_kernel(x_ref, g_ref, out_ref):
    out_ref[...] = _rms(x_ref[...], g_ref[...])


def _final_norm(x, gain, *, tm):
    m, d = x.shape
    row = pl.BlockSpec((tm, d), lambda i: (i, 0))
    return pl.pallas_call(
        _norm_kernel, grid=(m // tm,), in_specs=[row, pl.BlockSpec((1, d), lambda i: (0, 0))],
        out_specs=row, out_shape=jax.ShapeDtypeStruct((m, d), F32),
        compiler_params=_cp("arbitrary"), name="final_norm",
    )(x, gain)


def _page_specs(rows, n_pages, pp, descending):
    def spec(i):
        if descending:
            return pl.BlockSpec((1, rows, HEAD_DIM), lambda b, j, pt: (pt[b, n_pages - 1 - (j * pp + i)], 0, 0))
        return pl.BlockSpec((1, rows, HEAD_DIM), lambda b, j, pt: (pt[b, j * pp + i], 0, 0))
    return [spec(i) for i in range(pp)]


def _zero_once(*scratch):
    @pl.when((pl.program_id(0) == 0) & (pl.program_id(1) == 0))
    def _():
        for s in scratch:
            s[...] = jnp.zeros_like(s)


def _stage(pages, stage):
    for i, p in enumerate(pages):
        stage[i, :, 0:HEAD_DIM] = p[0]


def _combo(stage, i, c, n_combo, n_tok=PAGE, first=0):
    return stage[i, pl.ds(first * n_combo + c, n_tok, stride=n_combo), :]


def _pick_rows(y, n):
    row = _iota((y.shape[0], LANES), 0)
    out = y[:, (n - 1) * LANES:n * LANES]
    for h in range(n - 2, -1, -1):
        out = jnp.where(row == h, y[:, h * LANES:(h + 1) * LANES], out)
    return out


def _spread_rows(w, n):
    row = _iota(w.shape, 0)
    return jnp.concatenate([jnp.where(row == h, w, 0.0).astype(BF16) for h in range(n)], axis=1)


def _sb_dec_kernel(pt_ref, q_ref, *refs, pp):
    pages, (u_ref, out_ref, stage, acc_scr, r_scr) = refs[:pp], refs[pp:]
    j = pl.program_id(1)
    nh, nc = SB_HEADS, 2 * SB_HEADS
    _zero_once(stage)

    @pl.when(j == 0)
    def _():
        acc_scr[...] = jnp.zeros_like(acc_scr)
        r_scr[...] = jnp.zeros_like(r_scr)

    _stage(pages, stage)
    q = q_ref[0]
    zs, sps, afters, totals = [], [], [], []
    for i in range(pp):
        keys = jnp.concatenate([_combo(stage, i, h, nc) for h in range(nh)], axis=0).astype(BF16)
        z = _pick_rows(_dot_nt(q, keys), nh)
        sp = _softplus(z)
        zs.append(z)
        sps.append(sp)
        afters.append(_split_dot(-sp, u_ref[...], 2))
        totals.append(jnp.sum(-sp, axis=1, keepdims=True))
    r = r_scr[...]
    ws, vs = [], []
    for i in range(pp):
        ws.append(_spread_rows(jnp.exp(zs[i] - sps[i] + afters[i] + r), nh))
        vs += [_combo(stage, i, nh + h, nc).astype(BF16) for h in range(nh)]
        r = r + totals[i]
    acc_scr[...] += _dot(jnp.concatenate(ws, axis=1), jnp.concatenate(vs, axis=0))
    r_scr[...] = r

    @pl.when(j == pl.num_programs(1) - 1)
    def _():
        out_ref[0] = acc_scr[...]


def _sb_dec(q8, pool, page_table, *, pp):
    bd, n_pages = page_table.shape
    u = _upper_strict(PAGE)
    rows = PAGE * 2 * SB_HEADS
    grid_spec = pltpu.PrefetchScalarGridSpec(
        num_scalar_prefetch=1, grid=(bd, n_pages // pp),
        in_specs=[pl.BlockSpec((1, 8, LANES), lambda b, j, pt: (b, 0, 0))]
        + _page_specs(rows, n_pages, pp, True) + [pl.BlockSpec(u.shape, lambda b, j, pt: (0, 0))],
        out_specs=pl.BlockSpec((1, 8, LANES), lambda b, j, pt: (b, 0, 0)),
        scratch_shapes=[pltpu.VMEM((pp, rows, LANES), F32), pltpu.VMEM((8, LANES), F32), pltpu.VMEM((8, LANES), F32)])
    return pl.pallas_call(
        functools.partial(_sb_dec_kernel, pp=pp), grid_spec=grid_spec,
        out_shape=jax.ShapeDtypeStruct((bd, 8, LANES), F32),
        compiler_params=_cp("arbitrary", "arbitrary"), name="sb_decode",
    )(page_table, q8, *([pool] * pp), u)


def _fox_dec_kernel(pt_ref, lpt_ref, q_ref, kn_ref, vn_ref, lfn_ref, *refs, pp):
    pages, lf_pages = refs[:pp], refs[pp:2 * pp]
    u_ref, out_ref, stage, m_scr, l_scr, acc_scr, r_scr = refs[2 * pp:]
    j = pl.program_id(1)
    nh, nc = FOX_HEADS, 2 * FOX_HEADS
    _zero_once(stage)
    q = q_ref[0]

    @pl.when(j == 0)
    def _():
        kn = kn_ref[0].astype(BF16).astype(F32)
        m_scr[...] = jnp.broadcast_to(jnp.sum(q.astype(F32) * kn, axis=1, keepdims=True), m_scr.shape)
        l_scr[...] = jnp.ones_like(l_scr)
        acc_scr[...] = vn_ref[0].astype(BF16).astype(F32)
        r_scr[...] = lfn_ref[0]

    _stage(pages, stage)
    r = r_scr[...]
    ss, vs = [], []
    for i in range(pp):
        keys = jnp.concatenate([_combo(stage, i, h, nc) for h in range(nh)], axis=0).astype(BF16)
        lf = lf_pages[i][0]
        ss.append(_pick_rows(_dot_nt(q, keys), nh) + _split_dot(lf, u_ref[...], 3) + r)
        vs += [_combo(stage, i, nh + h, nc).astype(BF16) for h in range(nh)]
        r = r + jnp.sum(lf, axis=1, keepdims=True)
    r_scr[...] = r
    m_prev = m_scr[...]
    m_next = m_prev
    for s in ss:
        m_next = jnp.maximum(m_next, jnp.max(s, axis=1, keepdims=True))
    alpha = jnp.exp(m_prev - m_next)
    es = [jnp.exp(s - m_next) for s in ss]
    l_new = alpha * l_scr[...]
    for e in es:
        l_new = l_new + jnp.sum(e, axis=1, keepdims=True)
    l_scr[...] = l_new
    m_scr[...] = m_next
    acc_scr[...] = acc_scr[...] * alpha + _dot(jnp.concatenate([_spread_rows(e, nh) for e in es], axis=1),
                                               jnp.concatenate(vs, axis=0))

    @pl.when(j == pl.num_programs(1) - 1)
    def _():
        out_ref[0] = acc_scr[...] / l_scr[...]


def _fox_dec(q8, k_new, v_new, lf_new, pool, lf_pool, page_table, lf_page_table, *, pp):
    bd, n_pages = page_table.shape
    u = _upper_strict(PAGE)
    rows = PAGE * 2 * FOX_HEADS
    req = pl.BlockSpec((1, 8, LANES), lambda b, j, pt, lpt: (b, 0, 0))
    kv_spec = lambda i: pl.BlockSpec((1, rows, HEAD_DIM),
                                     lambda b, j, pt, lpt: (pt[b, n_pages - 1 - (j * pp + i)], 0, 0))
    lf_spec = lambda i: pl.BlockSpec((1, 8, PAGE), lambda b, j, pt, lpt: (lpt[b, n_pages - 1 - (j * pp + i)], 0, 0))
    grid_spec = pltpu.PrefetchScalarGridSpec(
        num_scalar_prefetch=2, grid=(bd, n_pages // pp),
        in_specs=[req, req, req, req] + [kv_spec(i) for i in range(pp)] + [lf_spec(i) for i in range(pp)]
        + [pl.BlockSpec(u.shape, lambda b, j, pt, lpt: (0, 0))],
        out_specs=req,
        scratch_shapes=[pltpu.VMEM((pp, rows, LANES), F32)] + [pltpu.VMEM((8, LANES), F32)] * 4)
    return pl.pallas_call(
        functools.partial(_fox_dec_kernel, pp=pp), grid_spec=grid_spec,
        out_shape=jax.ShapeDtypeStruct((bd, 8, LANES), F32),
        compiler_params=_cp("arbitrary", "arbitrary"), name="fox_decode",
    )(page_table, lf_page_table, q8, k_new, v_new, lf_new, *([pool] * pp), *([lf_pool] * pp), u)


def _nsa_dec_kernel(pt_ref, q_ref, ckv_ref, gate_ref, win_ref, kvs_new_ref, kvw_new_ref, amat_ref, e_ref, *refs,
                    pp, past_len, n_sel):
    pages, (out_ref, win_out_ref, stage, wstage, m_scr, l_scr, acc_scr, sel_scr, oc_scr, ow_scr) = refs[:pp], refs[pp:]
    j = pl.program_id(1)
    ng, nc4 = NSA_GROUPS, 2 * NSA_GROUPS
    _zero_once(stage, wstage)
    q = q_ref[0]
    rows = q.shape[0]
    row = _iota((rows, LANES), 0)
    row1 = _iota((rows, 1), 0)
    lane = _iota((rows, LANES), 1)
    pos = jnp.full((rows, 1), past_len, I32)

    def to_low_lanes(x):
        return jnp.where(row < NSA_HPG, x, pltpu.roll(x, HEAD_DIM, 1))

    def pick_group(y):
        return jnp.where(row < NSA_HPG, y[:, 0:LANES], y[:, LANES:2 * LANES])

    def spread_group(e):
        return jnp.concatenate([jnp.where(row < NSA_HPG, e, 0.0), jnp.where(row < NSA_HPG, 0.0, e)], axis=1).astype(BF16)

    q_low = to_low_lanes(q.astype(F32)).astype(BF16)

    def new_token(kv_row):
        k8 = to_low_lanes(jnp.broadcast_to(kv_row[:, 0:LANES], (rows, LANES))).astype(BF16).astype(F32)
        v8 = to_low_lanes(jnp.broadcast_to(kv_row[:, LANES:2 * LANES], (rows, LANES))).astype(BF16).astype(F32)
        return jnp.sum(q_low.astype(F32) * k8, axis=1, keepdims=True), v8

    @pl.when(j == 0)
    def _():
        nc = ckv_ref.shape[1]
        cmask = _iota((rows, nc), 1) * CMP_STRIDE + (CMP_BLOCK - 1) <= pos
        p = _masked_softmax_rows(_dot_nt(q, ckv_ref[0, :, 0:128]), cmask)
        oc_scr[...] = to_low_lanes(_dot(p.astype(BF16), ckv_ref[0, :, 128:256]))
        g0 = jnp.sum(p[0:NSA_HPG], axis=0, keepdims=True)
        g1 = jnp.sum(p[NSA_HPG:2 * NSA_HPG], axis=0, keepdims=True)
        p_grp = jnp.where(row1 < NSA_HPG, g0, g1)
        imp = _split_dot(p_grp, amat_ref[...], 2)
        sel_scr[...] = _select_blocks(imp, pos, n_sel)

        n_w = win_ref.shape[1] // nc4
        wstage[0, :, 0:HEAD_DIM] = win_ref[0]
        ss, vs = [], []
        for t in range(n_w // PAGE):
            keys = jnp.concatenate([_combo(wstage, 0, g, nc4, first=t * PAGE) for g in range(ng)], axis=0)
            s = pick_group(_dot_nt(q_low, keys.astype(BF16)))
            kpos = past_len - n_w + t * PAGE + lane
            keep = jnp.where(kpos >= 0, jnp.where(pos - kpos <= WINDOW, 1.0, 0.0), 0.0) > 0.5
            ss.append(jnp.where(keep, s, NEG))
            vs += [_combo(wstage, 0, ng + g, nc4, first=t * PAGE).astype(BF16) for g in range(ng)]
        s_new, v_new = new_token(kvw_new_ref[0])
        m = s_new
        for s in ss:
            m = jnp.maximum(m, jnp.max(s, axis=1, keepdims=True))
        es = [jnp.exp(s - m) for s in ss]
        e_new = jnp.exp(s_new - m)
        l = e_new
        for e in es:
            l = l + jnp.sum(e, axis=1, keepdims=True)
        o_w = _dot(jnp.concatenate([spread_group(e) for e in es], axis=1), jnp.concatenate(vs, axis=0)) + e_new * v_new
        ow_scr[...] = o_w / l

        wn = kvw_new_ref[0]
        win_out_ref[0] = pltpu.roll(wstage[0], (n_w - 1) * nc4, 0)[:, 0:HEAD_DIM]
        win_out_ref[0, (n_w - 1) * nc4:n_w * nc4, :] = jnp.concatenate(
            [wn[:, HEAD_DIM * c:HEAD_DIM * (c + 1)] for c in range(nc4)], axis=0)
        _online_init(m_scr, l_scr, acc_scr)

    _stage(pages, stage)
    sel = sel_scr[...].astype(BF16)
    ss, vs = [], []
    for i in range(pp):
        keys = jnp.concatenate([_combo(stage, i, g, nc4) for g in range(ng)], axis=0).astype(BF16)
        chosen = _dot(sel, e_ref[j * pp + i]) > 0.5
        ss.append(jnp.where(chosen, pick_group(_dot_nt(q_low, keys)), NEG))
        vs += [_combo(stage, i, ng + g, nc4).astype(BF16) for g in range(ng)]
    m_prev = m_scr[...]
    m_next = m_prev
    for s in ss:
        m_next = jnp.maximum(m_next, jnp.max(s, axis=1, keepdims=True))
    alpha = jnp.exp(m_prev - m_next)
    es = [jnp.exp(s - m_next) for s in ss]
    l_new = alpha * l_scr[...]
    for e in es:
        l_new = l_new + jnp.sum(e, axis=1, keepdims=True)
    l_scr[...] = l_new
    m_scr[...] = m_next
    acc_scr[...] = acc_scr[...] * alpha + _dot(jnp.concatenate([spread_group(e) for e in es], axis=1),
                                               jnp.concatenate(vs, axis=0))

    @pl.when(j == pl.num_programs(1) - 1)
    def _():
        s_new, v_new = new_token(kvs_new_ref[0])
        cur = past_len // SEL_BLOCK
        chosen = jnp.sum(jnp.where(_iota(sel_scr.shape, 1) == cur, sel_scr[...], 0.0), axis=1, keepdims=True)
        s_new = jnp.where(chosen > 0.5, s_new, NEG)
        m_prev = m_scr[...]
        m_next = jnp.maximum(m_prev, s_new)
        alpha = jnp.exp(m_prev - m_next)
        e_new = jnp.exp(s_new - m_next)
        o_s = (acc_scr[...] * alpha + e_new * v_new) / jnp.maximum(alpha * l_scr[...] + e_new, 1e-30)
        gates = jnp.where(row < NSA_HPG, gate_ref[0, 0:1, :], gate_ref[0, 1:2, :])
        hh = row % NSA_HPG

        def gate(c):
            return jnp.sum(jnp.where(lane == 3 * hh + c, gates, 0.0), axis=1, keepdims=True)

        out_ref[0] = gate(0) * oc_scr[...] + gate(1) * o_s + gate(2) * ow_scr[...]


def _nsa_dec(q8, ckv, gates, win_state, kvs_new, kvw_new, pool, page_table, *, pp, layer):
    bd, n_pages = page_table.shape
    past_len = n_pages * PAGE
    n_sel = -(-(past_len + 1) // SEL_BLOCK)
    nc = ckv.shape[1]
    wid = -(-n_sel // LANES) * LANES
    amat = _imp_matrix(nc, wid)
    etile = _expand_tiles(n_pages, wid, PAGE)
    rows = PAGE * 2 * NSA_GROUPS
    w_rows = win_state.shape[1]
    req = lambda shape: pl.BlockSpec((1,) + shape, lambda b, j, pt: (b,) + (0,) * len(shape))
    const = lambda a: pl.BlockSpec(a.shape, lambda b, j, pt: (0,) * a.ndim)
    win_spec = pl.BlockSpec((1, w_rows, HEAD_DIM), lambda b, j, pt: (layer * bd + b, 0, 0))
    grid_spec = pltpu.PrefetchScalarGridSpec(
        num_scalar_prefetch=1, grid=(bd, n_pages // pp),
        in_specs=[req((8, LANES)), req((nc, KV_W)), req((NSA_GROUPS, LANES)), win_spec,
                  req((1, KV_W)), req((1, KV_W)), const(amat), const(etile)]
        + _page_specs(rows, n_pages, pp, False),
        out_specs=(req((8, LANES)), req((w_rows, HEAD_DIM))),
        scratch_shapes=[pltpu.VMEM((pp, rows, LANES), F32), pltpu.VMEM((1, w_rows, LANES), F32)]
        + [pltpu.VMEM((8, LANES), F32)] * 3 + [pltpu.VMEM((8, wid), F32)] + [pltpu.VMEM((8, LANES), F32)] * 2)
    return pl.pallas_call(
        functools.partial(_nsa_dec_kernel, pp=pp, past_len=past_len, n_sel=n_sel), grid_spec=grid_spec,
        out_shape=(jax.ShapeDtypeStruct((bd, 8, LANES), F32), jax.ShapeDtypeStruct((bd, w_rows, HEAD_DIM), F32)),
        compiler_params=_cp("arbitrary", "arbitrary"), name="nsa_decode",
    )(page_table, q8, ckv, gates, win_state, kvs_new, kvw_new, amat, etile, *([pool] * pp))


def _rope_table(pos):
    half = ROT_DIM // 2
    inv = ROPE_THETA ** (-jnp.arange(half, dtype=F32) / half)
    ang = pos.astype(F32)[:, None] * inv[None, :]
    cos, sin = jnp.cos(ang), jnp.sin(ang)
    n = pos.shape[0]
    pad = HEAD_DIM - ROT_DIM
    one_head = lambda a, b, fill: jnp.concatenate([a, b, jnp.full((n, pad), fill, F32)], axis=1)
    zero = jnp.zeros_like(sin)
    tabs = [one_head(cos, cos, 1.0), one_head(-sin, zero, 0.0), one_head(zero, sin, 0.0)]
    return jnp.concatenate([jnp.tile(t, (1, LANES // HEAD_DIM)) for t in tabs], axis=1)


_Q_HEAD_ORDER = (0, 4, 1, 5, 2, 6, 3, 7)


def _proj_columns():
    q = np.concatenate([np.arange(HEAD_DIM) + HEAD_DIM * h for h in _Q_HEAD_ORDER])
    kv = np.arange(NSA_W, NSA_W + 3 * KV_W)
    gate0 = NSA_W + 3 * KV_W
    sbfox = np.arange(gate0 + GATE_W, gate0 + GATE_W + 6 * SB_W)
    gate = np.arange(gate0, gate0 + GATE_W)
    pf0 = gate0 + GATE_W + 6 * SB_W
    pf = np.arange(pf0, pf0 + FOX_HEADS)
    return np.concatenate([q, kv, sbfox, gate, pf]), pf0 + FOX_HEADS


def _layer_weights(l, w_in, b_forget, cmp_pos, w_cmp_k, w_cmp_v, w_up_nsa):
    cols, merge0 = _proj_columns()
    d = w_in.shape[1]
    w_proj = jnp.take(w_in[l], jnp.asarray(cols), axis=1)
    w_proj = jnp.concatenate([w_proj, jnp.zeros((d, N_PROJ - w_proj.shape[1]), F32)], axis=1).astype(BF16)
    w_merge = w_in[l][:, merge0:].astype(BF16)
    bias = jnp.zeros((1, LANES), F32).at[0, LOGF_LANE:LOGF_LANE + FOX_HEADS].set(b_forget[l])
    wk = w_cmp_k[l].reshape(CMP_BLOCK, HEAD_DIM, HEAD_DIM)
    wv = w_cmp_v[l].reshape(CMP_BLOCK, HEAD_DIM, HEAD_DIM)
    w_blk = jnp.zeros((CMP_BLOCK, KV_W, KV_W), F32)
    for n, wsrc in enumerate((wk, wk, wv, wv)):
        w_blk = w_blk.at[:, HEAD_DIM * n:HEAD_DIM * (n + 1), HEAD_DIM * n:HEAD_DIM * (n + 1)].set(wsrc)
    pe4 = jnp.tile(cmp_pos[l], (1, KV_W // HEAD_DIM))
    w_pad = jnp.zeros((2 * NSA_GROUPS, CMP_BLOCK, LANES, LANES), F32)
    for c, wsrc in enumerate((wk, wk, wv, wv)):
        g = c % NSA_GROUPS
        w_pad = w_pad.at[c, :, 0:HEAD_DIM, HEAD_DIM * g:HEAD_DIM * (g + 1)].set(wsrc)
    pe_pad = jnp.concatenate([cmp_pos[l], jnp.zeros((CMP_BLOCK, LANES - HEAD_DIM), F32)], axis=1)
    rows = np.concatenate([np.arange(HEAD_DIM) + HEAD_DIM * h for h in _Q_HEAD_ORDER])
    w_a = jnp.take(w_up_nsa[l], jnp.asarray(rows), axis=0).astype(BF16)
    return w_proj, w_merge, bias, w_blk.astype(BF16), pe4, w_pad.astype(BF16), pe_pad, w_a


def _pad_heads(a):
    return jnp.concatenate([a, jnp.zeros((a.shape[0], 8 - a.shape[1], a.shape[2]), a.dtype)], axis=1)


def _head_rows(a, n_heads):
    a = a.reshape(a.shape[0], n_heads, HEAD_DIM)
    a = jnp.concatenate([a, jnp.zeros(a.shape[:2] + (LANES - HEAD_DIM,), a.dtype)], axis=2)
    return _pad_heads(a)


def kernel(x_prompt, x_sample, cache_nsa_cmp_kv, cache_nsa_sel_kv, cache_sb_kv, cache_fox_kv, cache_fox_logf,
           state_nsa_win_kv, page_table, norm_mix_g, norm_ffn_g, norm_final_g, w_in, b_forget, cmp_pos,
           w_cmp_k, w_cmp_v, w_up_nsa, w_up_sb, w_up_fox, w_out, w_ffn_gate, w_ffn_up, w_ffn_down):
    batch, seq, d = x_prompt.shape
    bd = x_sample.shape[0]
    depth = w_in.shape[0]
    n_pool = cache_sb_kv.shape[1]
    n_pages = page_table.shape[1]
    past_len = n_pages * PAGE
    m = batch * seq
    tm = min(256, seq)
    tq = min(256, seq)
    pp = min(PAGES_PER_STEP, n_pages)
    pp_prompt = min(PAGES_PER_STEP, seq // PAGE)

    rope_p = _rope_table(jnp.arange(seq, dtype=I32))
    rope_s = _rope_table(jnp.full((bd,), past_len, I32))
    ident_pt = jnp.arange(batch * (seq // PAGE), dtype=I32).reshape(batch, seq // PAGE)
    pool_cmp = cache_nsa_cmp_kv.reshape(depth * n_pool, PAGE * 2 * NSA_GROUPS, HEAD_DIM)
    pool_sel = cache_nsa_sel_kv.reshape(depth * n_pool, PAGE * 2 * NSA_GROUPS, HEAD_DIM)
    pool_sb = cache_sb_kv.reshape(depth * n_pool, PAGE * 2 * SB_HEADS, HEAD_DIM)
    pool_fox = cache_fox_kv.reshape(depth * n_pool, PAGE * 2 * FOX_HEADS, HEAD_DIM)
    win_state = state_nsa_win_kv.reshape(depth * bd, -1, HEAD_DIM)

    xp = x_prompt.reshape(m, d)
    xs = x_sample.reshape(bd, d)
    p_states, s_states = [], []
    for l in range(depth):
        w_proj, w_merge, bias, w_blk, pe4, w_pad, pe_pad, w_a = _layer_weights(
            l, w_in, b_forget, cmp_pos, w_cmp_k, w_cmp_v, w_up_nsa)
        w_b, w_c, w_o = w_up_sb[l].astype(BF16), w_up_fox[l].astype(BF16), w_out[l].astype(BF16)
        w_g, w_u, w_d = w_ffn_gate[l].astype(BF16), w_ffn_up[l].astype(BF16), w_ffn_down[l].astype(BF16)
        g_mix, g_ffn = norm_mix_g[l].reshape(1, d), norm_ffn_g[l].reshape(1, d)

        (qn, kv_cmp, kv_sel, kv_win, kv_sb, kv_fox, kb_sel, kb_win, kb_sb, kb_fox, q_sb, q_fox, gates, logf, cum
         ) = _proj(xp, g_mix, w_proj, rope_p, bias, tm=tm, tiles_per_seq=seq // tm)
        ckv = _compress(kv_cmp.reshape(m // PAGE, PAGE, KV_W), ident_pt, w_blk, pe4, pp=pp_prompt)
        o_nsa = _nsa_prompt(qn, ckv, kb_sel, kb_win, gates, batch=batch, seq=seq, tq=tq)
        o_sb = _sb_prompt(q_sb, kb_sb, batch=batch, seq=seq, tq=tq)
        cum4 = cum[:, LOGF_LANE:LOGF_LANE + FOX_HEADS]
        cum_q = cum4.T[:, :, None]
        cum_k = cum4.reshape(batch, seq, FOX_HEADS).transpose(0, 2, 1).reshape(batch, FOX_HEADS, seq // tq, 1, tq)
        o_fox = _fox_prompt(q_fox, kb_fox, cum_q, cum_k, batch=batch, seq=seq, tq=tq)
        xp = _merge(xp, g_mix, o_nsa, o_sb, o_fox, w_merge, w_a, w_b, w_c, w_o, tm=tm)
        xp = _ffn(xp, g_ffn, w_g, w_u, w_d, tm=tm)
        win_keep = min(WINDOW, seq)
        p_states.append((
            kv_cmp.reshape(batch, seq, 2, NSA_GROUPS, HEAD_DIM), kv_sel.reshape(batch, seq, 2, NSA_GROUPS, HEAD_DIM),
            kv_sb.reshape(batch, seq, 2, SB_HEADS, HEAD_DIM), kv_fox.reshape(batch, seq, 2, FOX_HEADS, HEAD_DIM),
            logf[:, LOGF_LANE:LOGF_LANE + FOX_HEADS].reshape(batch, seq, FOX_HEADS),
            kv_win.reshape(batch, seq, 2, NSA_GROUPS, HEAD_DIM)[:, seq - win_keep:]))

        (qn, kv_cmp, kv_sel, kv_win, kv_sb, kv_fox, _, _, _, _, q_sb, q_fox, gates, logf, _
         ) = _proj(xs, g_mix, w_proj, rope_s, bias, tm=bd, tiles_per_seq=1)
        pt_l = page_table + l * n_pool
        ckv = _compress_cache(pool_cmp, pt_l, w_pad, pe_pad, pp=pp)
        q8 = qn.transpose(2, 0, 1, 3).reshape(bd, NSA_HEADS, LANES)
        o_nsa8, win_new = _nsa_dec(q8, ckv, gates.transpose(1, 0, 2), win_state, kv_sel.reshape(bd, 1, KV_W),
                                   kv_win.reshape(bd, 1, KV_W), pool_sel, pt_l, pp=pp, layer=l)
        o_sb8 = _sb_dec(_head_rows(q_sb.astype(F32).sum(0).astype(BF16), SB_HEADS), pool_sb, pt_l, pp=pp)
        lf4 = logf[:, LOGF_LANE:LOGF_LANE + FOX_HEADS]
        lf_new = _pad_heads(jnp.broadcast_to(lf4[:, :, None], (bd, FOX_HEADS, LANES)))
        lf_pool = _pad_heads(cache_fox_logf[l].astype(F32).transpose(0, 2, 1))
        o_fox8 = _fox_dec(_head_rows(q_fox.astype(F32).sum(0).astype(BF16), FOX_HEADS),
                          _head_rows(kv_fox[:, 0:SB_W], FOX_HEADS), _head_rows(kv_fox[:, SB_W:2 * SB_W], FOX_HEADS),
                          lf_new, pool_fox, lf_pool, pt_l, page_table, pp=pp)
        o_a = o_nsa8[:, :, 0:HEAD_DIM].reshape(bd, NSA_GROUPS, NSA_HPG, HEAD_DIM).transpose(0, 2, 1, 3).reshape(bd, NSA_W)
        o_a = jnp.stack([o_a, jnp.zeros_like(o_a)], axis=0).astype(BF16)
        o_b = o_sb8[:, 0:SB_HEADS, 0:HEAD_DIM].reshape(bd, SB_W).astype(BF16)
        o_c = o_fox8[:, 0:FOX_HEADS, 0:HEAD_DIM].reshape(bd, SB_W).astype(BF16)
        xs = _merge(xs, g_mix, o_a, o_b, o_c, w_merge, w_a, w_b, w_c, w_o, tm=bd)
        xs = _ffn(xs, g_ffn, w_g, w_u, w_d, tm=bd)
        s_states.append((
            kv_cmp.reshape(bd, 1, 2, NSA_GROUPS, HEAD_DIM), kv_sel.reshape(bd, 1, 2, NSA_GROUPS, HEAD_DIM),
            kv_sb.reshape(bd, 1, 2, SB_HEADS, HEAD_DIM), kv_fox.reshape(bd, 1, 2, FOX_HEADS, HEAD_DIM),
            lf4.reshape(bd, 1, FOX_HEADS),
            win_new.reshape(bd, -1, 2, NSA_GROUPS, HEAD_DIM)))

    g_fin = norm_final_g.reshape(1, d)
    y_prompt = _final_norm(xp, g_fin, tm=tm).reshape(batch, seq, d)
    y_sample = _final_norm(xs, g_fin, tm=bd).reshape(bd, 1, d)
    stk = lambda sts, i: jnp.stack([st[i] for st in sts], axis=0)
    return (y_prompt, y_sample) + tuple(stk(p_states, i) for i in range(6)) + tuple(stk(s_states, i) for i in range(6))
```

```python
import functools

import numpy as np
import jax
import jax.numpy as jnp
from jax import lax
from jax.experimental import pallas as pl
from jax.experimental.pallas import tpu as pltpu

F32, BF16, I32 = jnp.float32, jnp.bfloat16, jnp.int32

HEAD_DIM = 64
ROT_DIM = HEAD_DIM // 4
ROPE_THETA = 500000.0
NSA_HEADS = 8
NSA_GROUPS = 2
NSA_HPG = NSA_HEADS // NSA_GROUPS
CMP_STRIDE = 16
CMP_BLOCK = 32
SEL_BLOCK = 64
TOP_N = 16
WINDOW = 512
SB_HEADS = 4
FOX_HEADS = 4
PAGE = 128
SEL_FORCE_SCORE = 1.0e4
NEG = -1.0e30
RMS_EPS = 1e-6
SCALE = HEAD_DIM ** -0.5

LANES = 128
NSA_W = NSA_HEADS * HEAD_DIM
KV_W = 2 * NSA_GROUPS * HEAD_DIM
SB_W = SB_HEADS * HEAD_DIM
GATE_W = 3 * NSA_HEADS
C_Q, C_CMP, C_SEL, C_WIN, C_SB, C_FOX, C_MISC = 0, 512, 768, 1024, 1280, 2048, 2816
N_PROJ = C_MISC + LANES
LOGF_LANE = GATE_W
VMEM_LIMIT = 56 * 1024 * 1024
PAGES_PER_STEP = 8


def _cp(*sem):
    return pltpu.CompilerParams(dimension_semantics=sem, vmem_limit_bytes=VMEM_LIMIT)


def _dot(a, b):
    return jnp.dot(a, b, preferred_element_type=F32)


def _dot_nt(a, b):
    return lax.dot_general(a, b, (((1,), (1,)), ((), ())), preferred_element_type=F32)


def _split_dot(a, r, parts):
    out, rem = None, a
    for _ in range(parts):
        hi = rem.astype(BF16)
        d = _dot(hi, r)
        out = d if out is None else out + d
        rem = rem - hi.astype(F32)
    return out


def _split_dot_l(l, a, parts):
    out, rem = None, a
    for _ in range(parts):
        hi = rem.astype(BF16)
        d = _dot(l, hi)
        out = d if out is None else out + d
        rem = rem - hi.astype(F32)
    return out


def _rms(x, g):
    return x * lax.rsqrt(jnp.mean(x * x, axis=-1, keepdims=True) + RMS_EPS) * g


def _softplus(z):
    return jnp.maximum(z, 0.0) + jnp.log(1.0 + jnp.exp(-jnp.abs(z)))


def _iota(shape, axis):
    return lax.broadcasted_iota(I32, shape, axis)


def _proj_kernel(x_ref, g_ref, w_ref, rope_ref, b_ref, tri_ref,
                 qn_ref, kvc_ref, kvs_ref, kvw_ref, kvsb_ref, kvfx_ref,
                 kbs_ref, kbw_ref, kbsb_ref, kbfx_ref, qsb_ref, qfx_ref,
                 gate_ref, logf_ref, cum_ref, h_scr, carry_scr, *, tiles_per_seq):
    i = pl.program_id(0)
    tm = x_ref.shape[0]
    h_scr[...] = _rms(x_ref[...], g_ref[...]).astype(BF16)
    hb = h_scr[...]
    cos, sin_lo, sin_hi = rope_ref[:, 0:128], rope_ref[:, 128:256], rope_ref[:, 256:384]

    def rope(seg):
        return seg * cos + pltpu.roll(seg, LANES - 8, 1) * sin_lo + pltpu.roll(seg, 8, 1) * sin_hi

    lane = _iota((tm, LANES), 1)
    pq = _dot(hb, w_ref[:, C_Q:C_Q + NSA_W])
    for s in range(NSA_HPG):
        seg = rope(pq[:, LANES * s:LANES * (s + 1)]) * SCALE
        for g in range(NSA_GROUPS):
            qn_ref[g, s] = jnp.where(lane // HEAD_DIM == g, seg, 0.0).astype(BF16)

    for c0, f_ref, h_ref in ((C_CMP, kvc_ref, None), (C_SEL, kvs_ref, kbs_ref), (C_WIN, kvw_ref, kbw_ref)):
        p = _dot(hb, w_ref[:, c0:c0 + KV_W])
        k, v = rope(p[:, 0:128]), p[:, 128:256]
        f_ref[:, 0:128] = k
        f_ref[:, 128:256] = v
        if h_ref is not None:
            h_ref[:, 0:128] = k.astype(BF16)
            h_ref[:, 128:256] = v.astype(BF16)

    lane_w = _iota((tm, SB_W), 1)
    for c0, q_ref, f_ref, h_ref in ((C_SB, qsb_ref, kvsb_ref, kbsb_ref), (C_FOX, qfx_ref, kvfx_ref, kbfx_ref)):
        p = _dot(hb, w_ref[:, c0:c0 + 3 * SB_W])
        q = p[:, 0:SB_W] * SCALE
        for h in range(SB_HEADS):
            q_ref[h] = jnp.where(lane_w // HEAD_DIM == h, q, 0.0).astype(BF16)
        kv = p[:, SB_W:3 * SB_W]
        f_ref[...] = kv
        h_ref[...] = kv.astype(BF16)

    pm = _dot(hb, w_ref[:, C_MISC:C_MISC + LANES])
    sg = jax.nn.sigmoid(pm)
    gate_ref[0] = sg
    gate_ref[1] = pltpu.roll(sg, LANES - GATE_W // 2, 1)
    zf = pm + b_ref[...]
    lf = jnp.minimum(zf, 0.0) - jnp.log1p(jnp.exp(-jnp.abs(zf)))
    logf_ref[...] = lf

    @pl.when(i % tiles_per_seq == 0)
    def _():
        carry_scr[...] = jnp.zeros_like(carry_scr)

    c = _split_dot_l(tri_ref[...], lf, 3) + carry_scr[0:1, :]
    cum_ref[...] = c
    carry_scr[...] = jnp.broadcast_to(c[tm - 1:tm, :], carry_scr.shape)


def _proj(x, gain, w, rope_tab, bias, *, tm, tiles_per_seq):
    m, d = x.shape
    tri = jnp.tril(jnp.ones((tm, tm), F32)).astype(BF16)
    row = lambda wid: pl.BlockSpec((tm, wid), lambda i: (i, 0))
    const = lambda shape: pl.BlockSpec(shape, lambda i: (0,) * len(shape))
    f32o = lambda wid: jax.ShapeDtypeStruct((m, wid), F32)
    b16o = lambda wid: jax.ShapeDtypeStruct((m, wid), BF16)
    out_shape = (
        jax.ShapeDtypeStruct((NSA_GROUPS, NSA_HPG, m, LANES), BF16),
        f32o(KV_W), f32o(KV_W), f32o(KV_W), f32o(2 * SB_W), f32o(2 * SB_W),
        b16o(KV_W), b16o(KV_W), b16o(2 * SB_W), b16o(2 * SB_W),
        jax.ShapeDtypeStruct((SB_HEADS, m, SB_W), BF16), jax.ShapeDtypeStruct((SB_HEADS, m, SB_W), BF16),
        jax.ShapeDtypeStruct((NSA_GROUPS, m, LANES), F32), f32o(LANES), f32o(LANES),
    )
    out_specs = (
        pl.BlockSpec((NSA_GROUPS, NSA_HPG, tm, LANES), lambda i: (0, 0, i, 0)),
        row(KV_W), row(KV_W), row(KV_W), row(2 * SB_W), row(2 * SB_W),
        row(KV_W), row(KV_W), row(2 * SB_W), row(2 * SB_W),
        pl.BlockSpec((SB_HEADS, tm, SB_W), lambda i: (0, i, 0)),
        pl.BlockSpec((SB_HEADS, tm, SB_W), lambda i: (0, i, 0)),
        pl.BlockSpec((NSA_GROUPS, tm, LANES), lambda i: (0, i, 0)),
        row(LANES), row(LANES),
    )
    return pl.pallas_call(
        functools.partial(_proj_kernel, tiles_per_seq=tiles_per_seq),
        grid=(m // tm,),
        in_specs=[row(d), const((1, d)), const((d, N_PROJ)),
                  pl.BlockSpec((tm, 3 * LANES), lambda i: (i % tiles_per_seq, 0)),
                  const((1, LANES)), const((tm, tm))],
        out_specs=out_specs, out_shape=out_shape,
        scratch_shapes=[pltpu.VMEM((tm, d), BF16), pltpu.VMEM((8, LANES), F32)],
        compiler_params=_cp("arbitrary"), name="proj",
    )(x, gain, w, rope_tab, bias, tri)


def _compress_kernel(pt_ref, *refs, pp, nch, token_lanes):
    pages, (w_ref, pe_ref, out_ref, rows_scr, acc_scr) = refs[:pp], refs[pp:]
    j = pl.program_id(1)

    @pl.when(j == 0)
    def _():
        for half in range(KV_W // LANES):
            rows_scr[half, pl.ds(nch * CMP_STRIDE, CMP_STRIDE), :] = jnp.zeros((CMP_STRIDE, LANES), F32)

    for i in range(pp):
        for half in range(KV_W // LANES):
            cols = slice(half * LANES, (half + 1) * LANES)
            tile = pages[i][0, cols, :].T if token_lanes else pages[i][0, :, cols]
            rows_scr[half, pl.ds(pl.multiple_of((j * pp + i) * PAGE, PAGE), PAGE), :] = tile

    @pl.when(j == pl.num_programs(1) - 1)
    def _():
        acc_scr[...] = jnp.zeros_like(acc_scr)
        for l in range(CMP_BLOCK):
            xl = jnp.concatenate([rows_scr[half, pl.ds(l, nch, stride=CMP_STRIDE), :]
                                  for half in range(KV_W // LANES)], axis=1) + pe_ref[l:l + 1, :]
            acc_scr[...] += _dot(xl.astype(BF16), w_ref[l])
        out_ref[0] = acc_scr[...].astype(BF16)


def _compress(pool, page_table, w_blk, pe4, *, pp, token_lanes):
    b, n_pages = page_table.shape
    nch = n_pages * (PAGE // CMP_STRIDE)
    page_shape = (1, KV_W, PAGE) if token_lanes else (1, PAGE, KV_W)
    page_spec = lambda i: pl.BlockSpec(page_shape, lambda bb, j, pt: (pt[bb, j * pp + i], 0, 0))
    grid_spec = pltpu.PrefetchScalarGridSpec(
        num_scalar_prefetch=1, grid=(b, n_pages // pp),
        in_specs=[page_spec(i) for i in range(pp)] + [
            pl.BlockSpec((CMP_BLOCK, KV_W, KV_W), lambda bb, j, pt: (0, 0, 0)),
            pl.BlockSpec((CMP_BLOCK, KV_W), lambda bb, j, pt: (0, 0))],
        out_specs=pl.BlockSpec((1, nch, KV_W), lambda bb, j, pt: (bb, 0, 0)),
        scratch_shapes=[pltpu.VMEM((KV_W // LANES, nch * CMP_STRIDE + CMP_BLOCK, LANES), F32),
                        pltpu.VMEM((nch, KV_W), F32)])
    return pl.pallas_call(
        functools.partial(_compress_kernel, pp=pp, nch=nch, token_lanes=token_lanes), grid_spec=grid_spec,
        out_shape=jax.ShapeDtypeStruct((b, nch, KV_W), BF16),
        compiler_params=_cp("arbitrary", "arbitrary"), name="compress",
    )(page_table, *([pool] * pp), w_blk, pe4)


def _masked_softmax_rows(s, mask):
    s = jnp.where(mask, s, NEG)
    e = jnp.where(mask, jnp.exp(s - jnp.max(s, axis=1, keepdims=True)), 0.0)
    return e / jnp.maximum(jnp.sum(e, axis=1, keepdims=True), 1e-30)


def _online_step(s, mask, v, m_scr, l_scr, acc_scr):
    if mask is not None:
        s = jnp.where(mask, s, NEG)
    m_prev = m_scr[...]
    m_next = jnp.maximum(m_prev, jnp.max(s, axis=1, keepdims=True))
    alpha = jnp.exp(m_prev - m_next)
    e = jnp.exp(s - m_next)
    if mask is not None:
        e = jnp.where(mask, e, 0.0)
    l_scr[...] = alpha * l_scr[...] + jnp.sum(e, axis=1, keepdims=True)
    reps = acc_scr.shape[1] // LANES
    alpha_w = alpha if reps == 1 else jnp.concatenate([alpha] * reps, axis=1)
    acc_scr[...] = acc_scr[...] * alpha_w + _dot(e.astype(BF16), v)
    m_scr[...] = m_next


def _online_init(m_scr, l_scr, acc_scr):
    m_scr[...] = jnp.full_like(m_scr, NEG)
    l_scr[...] = jnp.zeros_like(l_scr)
    acc_scr[...] = jnp.zeros_like(acc_scr)


def _online_result(l_scr, acc_scr):
    l = jnp.maximum(l_scr[...], 1e-30)
    reps = acc_scr.shape[1] // LANES
    return acc_scr[...] / (l if reps == 1 else jnp.concatenate([l] * reps, axis=1))


def _select_blocks(imp, pos, n_sel):
    blk = _iota(imp.shape, 1)
    cur = pos // SEL_BLOCK
    valid = blk * SEL_BLOCK <= pos
    forced = jnp.where(blk == 0, 1.0, 0.0) + jnp.where(blk == cur, 1.0, 0.0) + jnp.where(blk == cur - 1, 1.0, 0.0)
    score = jnp.where(valid, jnp.where(forced > 0.5, SEL_FORCE_SCORE, imp), -1.0)
    score = jnp.where(blk < n_sel, score, -2.0)
    rank = jnp.zeros(imp.shape, F32)
    for b2 in range(n_sel):
        col = score[:, b2:b2 + 1]
        ge = jnp.where(col >= score, 1.0, 0.0)
        gt = jnp.where(col > score, 1.0, 0.0)
        rank = rank + jnp.where(blk > b2, ge, gt)
    return jnp.where(rank < min(TOP_N, n_sel), 1.0, 0.0)


def _rep(x, n):
    return x if n == 1 else jnp.concatenate([x] * n, axis=1)


def _select_blocks_t(p_grp, amat_t, q0, n_sel):
    t = p_grp.shape[0]
    nb = amat_t.shape[0]
    hi = p_grp.astype(BF16)
    lo = (p_grp - hi.astype(F32)).astype(BF16)
    imp = _dot_nt(amat_t, hi) + _dot_nt(amat_t, lo)
    pos = q0 + _iota((1, t), 1)
    blk = _iota((nb, t), 0)
    cur = pos // SEL_BLOCK
    forced = jnp.where(blk == 0, 1.0, 0.0) + jnp.where(blk == cur, 1.0, 0.0) + jnp.where(blk == cur - 1, 1.0, 0.0)
    score = jnp.where(blk * SEL_BLOCK <= pos, jnp.where(forced > 0.5, SEL_FORCE_SCORE, imp), -1.0)
    score = jnp.where(blk < n_sel, score, -2.0)
    groups = [score[8 * k:8 * (k + 1)] for k in range(nb // 8)]
    ranks = [jnp.zeros((8, t), F32) for _ in groups]
    sub = _iota((8, t), 0)
    for b2 in range(n_sel):
        rowv = score[b2:b2 + 1, :]
        k2, r2 = divmod(b2, 8)
        for k, grp in enumerate(groups):
            if k > k2:
                inc = jnp.where(rowv >= grp, 1.0, 0.0)
            elif k < k2:
                inc = jnp.where(rowv > grp, 1.0, 0.0)
            else:
                inc = jnp.where(sub > r2, jnp.where(rowv >= grp, 1.0, 0.0), jnp.where(rowv > grp, 1.0, 0.0))
            ranks[k] = ranks[k] + inc
    chosen = [jnp.where(rk < min(TOP_N, n_sel), 1.0, 0.0) for rk in ranks]
    pad = [jnp.zeros((LANES - nb, t), F32)] if nb < LANES else []
    return jnp.concatenate(chosen + pad, axis=0)


def _nsa_prompt_kernel(qn_ref, ckv_ref, ks_ref, kw_ref, gate_ref, amat_ref, e_ref, out_ref,
                       m_scr, l_scr, acc_scr, oc_scr, os_scr, *, n_sel):
    g, i = pl.program_id(1), pl.program_id(2)
    t = qn_ref.shape[2]
    q0 = i * t
    rowi, coli = _iota((t, t), 0), _iota((t, t), 1)
    reps = t // LANES

    nc = ckv_ref.shape[1]
    cmask = _iota((t, nc), 1) * CMP_STRIDE + (CMP_BLOCK - 1) <= q0 + _iota((t, 1), 0)
    ck, cv = ckv_ref[0, :, 0:128], ckv_ref[0, :, 128:256]
    p_grp = None
    for h in range(NSA_HPG):
        p = _masked_softmax_rows(_dot_nt(qn_ref[0, h], ck), cmask)
        oc_scr[h] = _dot(p.astype(BF16), cv)
        p_grp = p if p_grp is None else p_grp + p
    sel = _select_blocks_t(p_grp, amat_ref[...], q0, n_sel).T.astype(BF16)

    def online_head(h, s, v):
        m_prev = m_scr[h]
        m_next = jnp.maximum(m_prev, jnp.max(s, axis=1, keepdims=True))
        alpha = jnp.exp(m_prev - m_next)
        e = jnp.exp(s - _rep(m_next, reps))
        l_scr[h] = alpha * l_scr[h] + jnp.sum(e, axis=1, keepdims=True)
        m_scr[h] = m_next
        acc_scr[h] = acc_scr[h] * alpha + _dot(e.astype(BF16), v)

    def reset():
        m_scr[...] = jnp.full_like(m_scr, NEG)
        l_scr[...] = jnp.zeros_like(l_scr)
        acc_scr[...] = jnp.zeros_like(acc_scr)

    def sel_tile(j, diag):
        kt = ks_ref[0, pl.ds(pl.multiple_of(j * t, t), t), :]
        chosen = _dot(sel, e_ref[j])
        if diag:
            chosen = jnp.where(coli <= rowi, chosen, 0.0)
        keep = chosen > 0.5
        for h in range(NSA_HPG):
            online_head(h, jnp.where(keep, _dot_nt(qn_ref[0, h], kt[:, 0:128]), NEG), kt[:, 128:256])

    reset()
    lax.fori_loop(0, i, lambda j, c: (sel_tile(j, False), c)[1], 0)
    sel_tile(i, True)
    for h in range(NSA_HPG):
        os_scr[h] = acc_scr[h] / jnp.maximum(l_scr[h], 1e-30)

    def win_tile(j, c):
        kt = kw_ref[0, pl.ds(pl.multiple_of(j * t, t), t), :]
        d = (i - j) * t + rowi - coli
        keep = jnp.where(d >= 0, jnp.where(d <= WINDOW, 1.0, 0.0), 0.0) > 0.5
        for h in range(NSA_HPG):
            online_head(h, jnp.where(keep, _dot_nt(qn_ref[0, h], kt[:, 0:128]), NEG), kt[:, 128:256])
        return c

    reset()
    lax.fori_loop(jnp.maximum(i - WINDOW // t, 0), i + 1, win_tile, 0)

    gates = gate_ref[0]
    lane = _iota((t, LANES), 1)
    for h in range(NSA_HPG):
        o_w = acc_scr[h] / jnp.maximum(l_scr[h], 1e-30)
        o = (gates[:, 3 * h:3 * h + 1] * oc_scr[h] + gates[:, 3 * h + 1:3 * h + 2] * os_scr[h]
             + gates[:, 3 * h + 2:3 * h + 3] * o_w)
        out_ref[0, :, LANES * h:LANES * (h + 1)] = jnp.where(lane // HEAD_DIM == g, o, 0.0).astype(BF16)


def _imp_matrix(nc, width):
    n = np.arange(nc)[:, None]
    b = np.arange(width)[None, :]
    sub = SEL_BLOCK // CMP_STRIDE
    return jnp.asarray((n // sub == b).astype(np.float32) + ((n + 1) // sub == b), BF16)


def _expand_tiles(n_tiles, n_blk, tk):
    key = np.arange(n_tiles * tk).reshape(n_tiles, 1, tk)
    return jnp.asarray(key // SEL_BLOCK == np.arange(n_blk)[None, :, None], BF16)


def _nsa_prompt(qn, ckv, kb_sel, kb_win, gates, *, batch, seq, tq):
    nq = seq // tq
    nc = ckv.shape[1]
    n_sel = seq // SEL_BLOCK
    assert n_sel <= LANES and WINDOW % tq == 0
    amat = _imp_matrix(nc, -(-n_sel // 8) * 8).T
    etile = _expand_tiles(nq, LANES, tq)
    m = batch * seq
    return pl.pallas_call(
        functools.partial(_nsa_prompt_kernel, n_sel=n_sel),
        grid=(batch, NSA_GROUPS, nq),
        in_specs=[
            pl.BlockSpec((1, NSA_HPG, tq, LANES), lambda b, g, i: (g, 0, b * nq + i, 0)),
            pl.BlockSpec((1, nc, KV_W), lambda b, g, i: (b, 0, 0)),
            pl.BlockSpec((1, seq, KV_W), lambda b, g, i: (b, 0, 0)),
            pl.BlockSpec((1, seq, KV_W), lambda b, g, i: (b, 0, 0)),
            pl.BlockSpec((1, tq, LANES), lambda b, g, i: (g, b * nq + i, 0)),
            pl.BlockSpec(amat.shape, lambda b, g, i: (0, 0)),
            pl.BlockSpec(etile.shape, lambda b, g, i: (0, 0, 0)),
        ],
        out_specs=pl.BlockSpec((1, tq, NSA_W), lambda b, g, i: (g, b * nq + i, 0)),
        out_shape=jax.ShapeDtypeStruct((NSA_GROUPS, m, NSA_W), BF16),
        scratch_shapes=[pltpu.VMEM((NSA_HPG, tq, LANES), F32)] * 5,
        compiler_params=_cp("arbitrary", "arbitrary", "arbitrary"), name="nsa_prompt",
    )(qn, ckv, kb_sel.reshape(batch, seq, KV_W), kb_win.reshape(batch, seq, KV_W), gates, amat, etile)


def _upper_strict(n):
    return jnp.asarray(np.triu(np.ones((n, n), np.float32), 1).T, BF16)


def _stack_heads(v):
    lane = _iota(v.shape, 1)
    return jnp.concatenate([jnp.where(lane // HEAD_DIM == h, v, jnp.zeros_like(v)) for h in range(SB_HEADS)], axis=0)


def _head_lanes(parts):
    lane = _iota(parts[0].shape, 1)
    return jnp.concatenate([jnp.where(lane < HEAD_DIM, parts[0], parts[1]),
                            jnp.where(lane < HEAD_DIM, parts[2], parts[3])], axis=1)


def _sb_prompt_kernel(q_ref, kv_ref, u_ref, out_ref, acc_scr, r_scr):
    i = pl.program_id(1)
    t = q_ref.shape[1]
    rowi, coli = _iota((t, t), 0), _iota((t, t), 1)
    acc_scr[...] = jnp.zeros_like(acc_scr)
    r_scr[...] = jnp.zeros_like(r_scr)

    def tile(j, diag):
        kt = kv_ref[0, pl.ds(pl.multiple_of(j * t, t), t), :]
        ws = []
        for h in range(SB_HEADS):
            z = _dot_nt(q_ref[h], kt[:, 0:SB_W])
            sp = _softplus(z)
            log_keep = -sp
            if diag:
                log_keep = jnp.where(coli < rowi, log_keep, 0.0)
            r = r_scr[h]
            w = jnp.exp(z - sp + _split_dot(log_keep, u_ref[...], 2) + _rep(r, t // LANES))
            if diag:
                w = jnp.where(coli < rowi, w, 0.0)
            ws.append(w.astype(BF16))
            r_scr[h] = r + jnp.sum(log_keep, axis=1, keepdims=True)
        acc_scr[...] += _dot(jnp.concatenate(ws, axis=1), _stack_heads(kt[:, SB_W:2 * SB_W]))

    tile(i, True)
    lax.fori_loop(0, i, lambda n, c: (tile(i - 1 - n, False), c)[1], 0)
    out_ref[...] = acc_scr[...].astype(BF16)


def _sb_prompt(q, kb, *, batch, seq, tq):
    nq = seq // tq
    u = _upper_strict(tq)
    return pl.pallas_call(
        _sb_prompt_kernel, grid=(batch, nq),
        in_specs=[pl.BlockSpec((SB_HEADS, tq, SB_W), lambda b, i: (0, b * nq + i, 0)),
                  pl.BlockSpec((1, seq, 2 * SB_W), lambda b, i: (b, 0, 0)),
                  pl.BlockSpec(u.shape, lambda b, i: (0, 0))],
        out_specs=pl.BlockSpec((tq, SB_W), lambda b, i: (b * nq + i, 0)),
        out_shape=jax.ShapeDtypeStruct((batch * seq, SB_W), BF16),
        scratch_shapes=[pltpu.VMEM((tq, SB_W), F32), pltpu.VMEM((SB_HEADS, tq, LANES), F32)],
        compiler_params=_cp("arbitrary", "arbitrary"), name="sb_prompt",
    )(q, kb.reshape(batch, seq, 2 * SB_W), u)


def _fox_prompt_kernel(q_ref, kv_ref, cq_ref, ck_ref, out_ref, m_scr, l_scr, acc_scr, cq_scr):
    i = pl.program_id(1)
    t = q_ref.shape[1]
    rowi, coli = _iota((t, t), 0), _iota((t, t), 1)
    m_scr[...] = jnp.full_like(m_scr, NEG)
    l_scr[...] = jnp.zeros_like(l_scr)
    acc_scr[...] = jnp.zeros_like(acc_scr)
    for h in range(FOX_HEADS):
        cq_scr[h] = jnp.broadcast_to(cq_ref[h], (t, LANES))

    def tile(j, diag):
        kt = kv_ref[0, pl.ds(pl.multiple_of(j * t, t), t), :]
        es, alphas = [], []
        for h in range(FOX_HEADS):
            s = _dot_nt(q_ref[h], kt[:, 0:SB_W]) + (_rep(cq_scr[h], t // LANES) - ck_ref[0, h, j])
            if diag:
                s = jnp.where(coli <= rowi, s, NEG)
            m_prev = m_scr[h]
            m_next = jnp.maximum(m_prev, jnp.max(s, axis=1, keepdims=True))
            alpha = jnp.exp(m_prev - m_next)
            e = jnp.exp(s - _rep(m_next, t // LANES))
            l_scr[h] = alpha * l_scr[h] + jnp.sum(e, axis=1, keepdims=True)
            m_scr[h] = m_next
            es.append(e.astype(BF16))
            alphas.append(alpha)
        acc_scr[...] = acc_scr[...] * _head_lanes(alphas) + _dot(jnp.concatenate(es, axis=1),
                                                                 _stack_heads(kt[:, SB_W:2 * SB_W]))

    lax.fori_loop(0, i, lambda j, c: (tile(j, False), c)[1], 0)
    tile(i, True)
    out_ref[...] = (acc_scr[...] / _head_lanes([jnp.maximum(l_scr[h], 1e-30) for h in range(FOX_HEADS)])).astype(BF16)


def _fox_prompt(q, kb, cum_q, cum_k, *, batch, seq, tq):
    nq = seq // tq
    nk = seq // tq
    return pl.pallas_call(
        _fox_prompt_kernel, grid=(batch, nq),
        in_specs=[pl.BlockSpec((FOX_HEADS, tq, SB_W), lambda b, i: (0, b * nq + i, 0)),
                  pl.BlockSpec((1, seq, 2 * SB_W), lambda b, i: (b, 0, 0)),
                  pl.BlockSpec((FOX_HEADS, tq, 1), lambda b, i: (0, b * nq + i, 0)),
                  pl.BlockSpec((1, FOX_HEADS, nk, 1, tq), lambda b, i: (b, 0, 0, 0, 0))],
        out_specs=pl.BlockSpec((tq, SB_W), lambda b, i: (b * nq + i, 0)),
        out_shape=jax.ShapeDtypeStruct((batch * seq, SB_W), BF16),
        scratch_shapes=[pltpu.VMEM((FOX_HEADS, tq, LANES), F32), pltpu.VMEM((FOX_HEADS, tq, LANES), F32),
                        pltpu.VMEM((tq, SB_W), F32), pltpu.VMEM((FOX_HEADS, tq, LANES), F32)],
        compiler_params=_cp("arbitrary", "arbitrary"), name="fox_prompt",
    )(q, kb.reshape(batch, seq, 2 * SB_W), cum_q, cum_k)


def _merge_kernel(x_ref, g_ref, oa_ref, ob_ref, oc_ref, wm_ref, wa_ref, wb_ref, wc_ref, wo_ref, out_ref):
    x = x_ref[...]
    d = x.shape[1]
    hb = _rms(x, g_ref[...]).astype(BF16)
    oa = oa_ref[0] + oa_ref[1]
    u = jax.nn.sigmoid(_dot(hb, wm_ref[:, 0:d])) * _dot(oa, wa_ref[...])
    u = u + jax.nn.sigmoid(_dot(hb, wm_ref[:, d:2 * d])) * _dot(ob_ref[...], wb_ref[...])
    u = u + jax.nn.sigmoid(_dot(hb, wm_ref[:, 2 * d:3 * d])) * _dot(oc_ref[...], wc_ref[...])
    out_ref[...] = x + _dot(u.astype(BF16), wo_ref[...])


def _merge(x, gain, o_nsa, o_sb, o_fox, w_merge, w_a, w_b, w_c, w_o, *, tm):
    m, d = x.shape
    row = lambda wid: pl.BlockSpec((tm, wid), lambda i: (i, 0))
    const = lambda a: pl.BlockSpec(a.shape, lambda i: (0,) * a.ndim)
    return pl.pallas_call(
        _merge_kernel, grid=(m // tm,),
        in_specs=[row(d), const(gain), pl.BlockSpec((NSA_GROUPS, tm, NSA_W), lambda i: (0, i, 0)),
                  row(SB_W), row(SB_W), const(w_merge), const(w_a), const(w_b), const(w_c), const(w_o)],
        out_specs=row(d), out_shape=jax.ShapeDtypeStruct((m, d), F32),
        compiler_params=_cp("arbitrary"), name="merge",
    )(x, gain, o_nsa, o_sb, o_fox, w_merge, w_a, w_b, w_c, w_o)


def _ffn_kernel(x_ref, g_ref, wg_ref, wu_ref, wd_ref, out_ref):
    x = x_ref[...]
    hb = _rms(x, g_ref[...]).astype(BF16)
    a = _dot(hb, wg_ref[...])
    act = a * jax.nn.sigmoid(a) * _dot(hb, wu_ref[...])
    out_ref[...] = x + _dot(act.astype(BF16), wd_ref[...])


def _ffn(x, gain, w_gate, w_up, w_down, *, tm):
    m, d = x.shape
    row = pl.BlockSpec((tm, d), lambda i: (i, 0))
    const = lambda a: pl.BlockSpec(a.shape, lambda i: (0,) * a.ndim, pipeline_mode=pl.Buffered(1))
    return pl.pallas_call(
        _ffn_kernel, grid=(m // tm,),
        in_specs=[row, pl.BlockSpec(gain.shape, lambda i: (0, 0)), const(w_gate), const(w_up), const(w_down)],
        out_specs=row, out_shape=jax.ShapeDtypeStruct((m, d), F32),
        compiler_params=_cp("arbitrary"), name="ffn",
    )(x, gain, w_gate, w_up, w_down)


def _norm_kernel(x_ref, g_ref, out_ref):
    out_ref[...] = _rms(x_ref[...], g_ref[...])


def _final_norm(x, gain, *, tm):
    m, d = x.shape
    row = pl.BlockSpec((tm, d), lambda i: (i, 0))
    return pl.pallas_call(
        _norm_kernel, grid=(m // tm,), in_specs=[row, pl.BlockSpec((1, d), lambda i: (0, 0))],
        out_specs=row, out_shape=jax.ShapeDtypeStruct((m, d), F32),
        compiler_params=_cp("arbitrary"), name="final_norm",
    )(x, gain)


def _page_specs(rows, n_pages, pp, descending):
    def spec(i):
        if descending:
            return pl.BlockSpec((1, rows, PAGE), lambda b, j, pt: (pt[b, n_pages - 1 - (j * pp + i)], 0, 0))
        return pl.BlockSpec((1, rows, PAGE), lambda b, j, pt: (pt[b, j * pp + i], 0, 0))
    return [spec(i) for i in range(pp)]


def _lanes(parts):
    return jnp.concatenate(parts, axis=1)


def _sb_dec_kernel(pt_ref, q_ref, *refs, pp):
    pages, (u_ref, out_ref, acc_scr, r_scr) = refs[:pp], refs[pp:]
    j = pl.program_id(1)

    @pl.when(j == 0)
    def _():
        acc_scr[...] = jnp.zeros_like(acc_scr)
        r_scr[...] = jnp.zeros_like(r_scr)

    q = q_ref[0]
    z_all = _dot(q, _lanes([p[0, 0:SB_W, :].astype(BF16) for p in pages]))
    zs = [z_all[:, i * PAGE:(i + 1) * PAGE] for i in range(pp)]
    sps = [_softplus(z) for z in zs]
    afters = [_split_dot(-sp, u_ref[...], 2) for sp in sps]
    r = r_scr[...]
    ws = []
    for i in range(pp):
        ws.append(jnp.exp(zs[i] - sps[i] + afters[i] + r).astype(BF16))
        r = r + jnp.sum(-sps[i], axis=1, keepdims=True)
    acc_scr[...] += _dot_nt(_lanes(ws), _lanes([p[0, SB_W:2 * SB_W, :].astype(BF16) for p in pages]))
    r_scr[...] = r

    @pl.when(j == pl.num_programs(1) - 1)
    def _():
        out_ref[0] = acc_scr[...]


def _sb_dec(q8, pool, page_table, *, pp):
    bd, n_pages = page_table.shape
    u = _upper_strict(PAGE)
    grid_spec = pltpu.PrefetchScalarGridSpec(
        num_scalar_prefetch=1, grid=(bd, n_pages // pp),
        in_specs=[pl.BlockSpec((1, 8, SB_W), lambda b, j, pt: (b, 0, 0))]
        + _page_specs(2 * SB_W, n_pages, pp, True) + [pl.BlockSpec(u.shape, lambda b, j, pt: (0, 0))],
        out_specs=pl.BlockSpec((1, 8, SB_W), lambda b, j, pt: (b, 0, 0)),
        scratch_shapes=[pltpu.VMEM((8, SB_W), F32), pltpu.VMEM((8, LANES), F32)])
    return pl.pallas_call(
        functools.partial(_sb_dec_kernel, pp=pp), grid_spec=grid_spec,
        out_shape=jax.ShapeDtypeStruct((bd, 8, SB_W), F32),
        compiler_params=_cp("arbitrary", "arbitrary"), name="sb_decode",
    )(page_table, q8, *([pool] * pp), u)


def _fox_dec_kernel(pt_ref, q_ref, kn_ref, vn_ref, lfn_ref, *refs, pp):
    pages, lf_pages = refs[:pp], refs[pp:2 * pp]
    u_ref, out_ref, m_scr, l_scr, acc_scr, r_scr = refs[2 * pp:]
    j = pl.program_id(1)
    q = q_ref[0]
    reps = SB_W // LANES

    @pl.when(j == 0)
    def _():
        kn = kn_ref[0].astype(BF16).astype(F32)
        m_scr[...] = jnp.broadcast_to(jnp.sum(q.astype(F32) * kn, axis=1, keepdims=True), m_scr.shape)
        l_scr[...] = jnp.ones_like(l_scr)
        acc_scr[...] = jnp.broadcast_to(vn_ref[0].astype(BF16).astype(F32), acc_scr.shape)
        r_scr[...] = lfn_ref[0]

    z_all = _dot(q, _lanes([p[0, 0:SB_W, :].astype(BF16) for p in pages]))
    pad = jnp.zeros((8 - FOX_HEADS, PAGE), F32)
    r = r_scr[...]
    ss = []
    for i in range(pp):
        lf = jnp.concatenate([lf_pages[i][0], pad], axis=0)
        ss.append(z_all[:, i * PAGE:(i + 1) * PAGE] + _split_dot(lf, u_ref[...], 3) + r)
        r = r + jnp.sum(lf, axis=1, keepdims=True)
    r_scr[...] = r
    m_prev = m_scr[...]
    m_next = m_prev
    for s in ss:
        m_next = jnp.maximum(m_next, jnp.max(s, axis=1, keepdims=True))
    alpha = jnp.exp(m_prev - m_next)
    es = [jnp.exp(s - m_next) for s in ss]
    l_new = alpha * l_scr[...]
    for e in es:
        l_new = l_new + jnp.sum(e, axis=1, keepdims=True)
    l_scr[...] = l_new
    m_scr[...] = m_next
    acc_scr[...] = acc_scr[...] * _rep(alpha, reps) + _dot_nt(
        _lanes([e.astype(BF16) for e in es]), _lanes([p[0, SB_W:2 * SB_W, :].astype(BF16) for p in pages]))

    @pl.when(j == pl.num_programs(1) - 1)
    def _():
        out_ref[0] = acc_scr[...] / _rep(l_scr[...], reps)


def _fox_dec(q8, k_new, v_new, lf_new, pool, lf_pool, page_table, *, pp):
    bd, n_pages = page_table.shape
    u = _upper_strict(PAGE)
    req = lambda shape: pl.BlockSpec((1,) + shape, lambda b, j, pt: (b,) + (0,) * len(shape))
    lf_spec = lambda i: pl.BlockSpec((1, FOX_HEADS, PAGE), lambda b, j, pt: (pt[b, n_pages - 1 - (j * pp + i)], 0, 0))
    grid_spec = pltpu.PrefetchScalarGridSpec(
        num_scalar_prefetch=1, grid=(bd, n_pages // pp),
        in_specs=[req((8, SB_W)), req((1, SB_W)), req((1, SB_W)), req((8, LANES))]
        + _page_specs(2 * SB_W, n_pages, pp, True) + [lf_spec(i) for i in range(pp)]
        + [pl.BlockSpec(u.shape, lambda b, j, pt: (0, 0))],
        out_specs=req((8, SB_W)),
        scratch_shapes=[pltpu.VMEM((8, LANES), F32), pltpu.VMEM((8, LANES), F32),
                        pltpu.VMEM((8, SB_W), F32), pltpu.VMEM((8, LANES), F32)])
    return pl.pallas_call(
        functools.partial(_fox_dec_kernel, pp=pp), grid_spec=grid_spec,
        out_shape=jax.ShapeDtypeStruct((bd, 8, SB_W), F32),
        compiler_params=_cp("arbitrary", "arbitrary"), name="fox_decode",
    )(page_table, q8, k_new, v_new, lf_new, *([pool] * pp), *([lf_pool] * pp), u)


def _nsa_dec_kernel(pt_ref, q_ref, ckv_ref, gate_ref, win_ref, kvs_new_ref, kvw_new_ref, kvw_col_ref, amat_ref, e_ref,
                    *refs, pp, past_len, n_sel):
    pages, (out_ref, win_out_ref, m_scr, l_scr, acc_scr, sel_scr, oc_scr, ow_scr) = refs[:pp], refs[pp:]
    j = pl.program_id(1)
    q = q_ref[0]
    rows = q.shape[0]
    row = _iota((rows, LANES), 0)
    row1 = _iota((rows, 1), 0)
    lane = _iota((rows, LANES), 1)
    pos = jnp.full((rows, 1), past_len, I32)

    def new_token(kv_row):
        k = kv_row[:, 0:LANES].astype(BF16).astype(F32)
        return jnp.sum(q.astype(F32) * k, axis=1, keepdims=True), kv_row[:, LANES:2 * LANES].astype(BF16).astype(F32)

    @pl.when(j == 0)
    def _():
        nc = ckv_ref.shape[1]
        cmask = _iota((rows, nc), 1) * CMP_STRIDE + (CMP_BLOCK - 1) <= pos
        p = _masked_softmax_rows(_dot_nt(q, ckv_ref[0, :, 0:128]), cmask)
        oc_scr[...] = _dot(p.astype(BF16), ckv_ref[0, :, 128:256])
        g0 = jnp.sum(p[0:NSA_HPG], axis=0, keepdims=True)
        g1 = jnp.sum(p[NSA_HPG:2 * NSA_HPG], axis=0, keepdims=True)
        p_grp = jnp.where(row1 < NSA_HPG, g0, g1)
        imp = _split_dot(p_grp, amat_ref[...], 2)
        sel_scr[...] = _select_blocks(imp, pos, n_sel)

        wk = win_ref[0]
        n_w = wk.shape[1]
        s = _dot(q, wk[0:LANES, :].astype(BF16))
        kpos = past_len - n_w + _iota((rows, n_w), 1)
        keep = jnp.where(kpos >= 0, jnp.where(pos - kpos <= WINDOW, 1.0, 0.0), 0.0) > 0.5
        s = jnp.where(keep, s, NEG)
        s_new, v_new = new_token(kvw_new_ref[0])
        m = jnp.maximum(s_new, jnp.max(s, axis=1, keepdims=True))
        e = jnp.exp(s - m)
        e_new = jnp.exp(s_new - m)
        o_w = _dot_nt(e.astype(BF16), wk[LANES:2 * LANES, :].astype(BF16)) + e_new * v_new
        ow_scr[...] = o_w / (jnp.sum(e, axis=1, keepdims=True) + e_new)

        shifted = pltpu.roll(wk, n_w - 1, 1)
        win_out_ref[0] = jnp.where(_iota(wk.shape, 1) == n_w - 1, kvw_col_ref[0], shifted)
        _online_init(m_scr, l_scr, acc_scr)

    sel = sel_scr[...].astype(BF16)
    z_all = _dot(q, _lanes([p[0, 0:LANES, :].astype(BF16) for p in pages]))
    ss = []
    for i in range(pp):
        chosen = _dot(sel, e_ref[j * pp + i]) > 0.5
        ss.append(jnp.where(chosen, z_all[:, i * PAGE:(i + 1) * PAGE], NEG))
    m_prev = m_scr[...]
    m_next = m_prev
    for s in ss:
        m_next = jnp.maximum(m_next, jnp.max(s, axis=1, keepdims=True))
    alpha = jnp.exp(m_prev - m_next)
    es = [jnp.exp(s - m_next) for s in ss]
    l_new = alpha * l_scr[...]
    for e in es:
        l_new = l_new + jnp.sum(e, axis=1, keepdims=True)
    l_scr[...] = l_new
    m_scr[...] = m_next
    acc_scr[...] = acc_scr[...] * alpha + _dot_nt(_lanes([e.astype(BF16) for e in es]),
                                                  _lanes([p[0, LANES:2 * LANES, :].astype(BF16) for p in pages]))

    @pl.when(j == pl.num_programs(1) - 1)
    def _():
        s_new, v_new = new_token(kvs_new_ref[0])
        cur = past_len // SEL_BLOCK
        chosen = jnp.sum(jnp.where(_iota(sel_scr.shape, 1) == cur, sel_scr[...], 0.0), axis=1, keepdims=True)
        s_new = jnp.where(chosen > 0.5, s_new, NEG)
        m_prev = m_scr[...]
        m_next = jnp.maximum(m_prev, s_new)
        alpha = jnp.exp(m_prev - m_next)
        e_new = jnp.exp(s_new - m_next)
        o_s = (acc_scr[...] * alpha + e_new * v_new) / jnp.maximum(alpha * l_scr[...] + e_new, 1e-30)
        gates = jnp.where(row < NSA_HPG, gate_ref[0, 0:1, :], gate_ref[0, 1:2, :])
        hh = row % NSA_HPG

        def gate(c):
            return jnp.sum(jnp.where(lane == 3 * hh + c, gates, 0.0), axis=1, keepdims=True)

        out_ref[0] = gate(0) * oc_scr[...] + gate(1) * o_s + gate(2) * ow_scr[...]


def _nsa_dec(q8, ckv, gates, win_state, kvs_new, kvw_new, pool, page_table, *, pp, layer):
    bd, n_pages = page_table.shape
    past_len = n_pages * PAGE
    n_sel = -(-(past_len + 1) // SEL_BLOCK)
    nc = ckv.shape[1]
    wid = -(-n_sel // LANES) * LANES
    amat = _imp_matrix(nc, wid)
    etile = _expand_tiles(n_pages, wid, PAGE)
    n_w = win_state.shape[2]
    req = lambda shape: pl.BlockSpec((1,) + shape, lambda b, j, pt: (b,) + (0,) * len(shape))
    const = lambda a: pl.BlockSpec(a.shape, lambda b, j, pt: (0,) * a.ndim)
    win_spec = pl.BlockSpec((1, KV_W, n_w), lambda b, j, pt: (layer * bd + b, 0, 0))
    grid_spec = pltpu.PrefetchScalarGridSpec(
        num_scalar_prefetch=1, grid=(bd, n_pages // pp),
        in_specs=[req((8, LANES)), req((nc, KV_W)), req((NSA_GROUPS, LANES)), win_spec,
                  req((1, KV_W)), req((1, KV_W)), req((KV_W, 1)), const(amat), const(etile)]
        + _page_specs(KV_W, n_pages, pp, False),
        out_specs=(req((8, LANES)), req((KV_W, n_w))),
        scratch_shapes=[pltpu.VMEM((8, LANES), F32)] * 3 + [pltpu.VMEM((8, wid), F32)] + [pltpu.VMEM((8, LANES), F32)] * 2)
    return pl.pallas_call(
        functools.partial(_nsa_dec_kernel, pp=pp, past_len=past_len, n_sel=n_sel), grid_spec=grid_spec,
        out_shape=(jax.ShapeDtypeStruct((bd, 8, LANES), F32), jax.ShapeDtypeStruct((bd, KV_W, n_w), F32)),
        compiler_params=_cp("arbitrary", "arbitrary"), name="nsa_decode",
    )(page_table, q8, ckv, gates, win_state, kvs_new, kvw_new, kvw_new.reshape(bd, KV_W, 1), amat, etile,
      *([pool] * pp))


def _rope_table(pos):
    half = ROT_DIM // 2
    inv = ROPE_THETA ** (-jnp.arange(half, dtype=F32) / half)
    ang = pos.astype(F32)[:, None] * inv[None, :]
    cos, sin = jnp.cos(ang), jnp.sin(ang)
    n = pos.shape[0]
    pad = HEAD_DIM - ROT_DIM
    one_head = lambda a, b, fill: jnp.concatenate([a, b, jnp.full((n, pad), fill, F32)], axis=1)
    zero = jnp.zeros_like(sin)
    tabs = [one_head(cos, cos, 1.0), one_head(-sin, zero, 0.0), one_head(zero, sin, 0.0)]
    return jnp.concatenate([jnp.tile(t, (1, LANES // HEAD_DIM)) for t in tabs], axis=1)


_Q_HEAD_ORDER = (0, 4, 1, 5, 2, 6, 3, 7)


def _proj_columns():
    q = np.concatenate([np.arange(HEAD_DIM) + HEAD_DIM * h for h in _Q_HEAD_ORDER])
    kv = np.arange(NSA_W, NSA_W + 3 * KV_W)
    gate0 = NSA_W + 3 * KV_W
    sbfox = np.arange(gate0 + GATE_W, gate0 + GATE_W + 6 * SB_W)
    gate = np.arange(gate0, gate0 + GATE_W)
    pf0 = gate0 + GATE_W + 6 * SB_W
    pf = np.arange(pf0, pf0 + FOX_HEADS)
    return np.concatenate([q, kv, sbfox, gate, pf]), pf0 + FOX_HEADS


def _layer_weights(l, w_in, b_forget, cmp_pos, w_cmp_k, w_cmp_v, w_up_nsa):
    cols, merge0 = _proj_columns()
    d = w_in.shape[1]
    w_proj = jnp.take(w_in[l], jnp.asarray(cols), axis=1)
    w_proj = jnp.concatenate([w_proj, jnp.zeros((d, N_PROJ - w_proj.shape[1]), F32)], axis=1).astype(BF16)
    w_merge = w_in[l][:, merge0:].astype(BF16)
    bias = jnp.zeros((1, LANES), F32).at[0, LOGF_LANE:LOGF_LANE + FOX_HEADS].set(b_forget[l])
    wk = w_cmp_k[l].reshape(CMP_BLOCK, HEAD_DIM, HEAD_DIM)
    wv = w_cmp_v[l].reshape(CMP_BLOCK, HEAD_DIM, HEAD_DIM)
    w_blk = jnp.zeros((CMP_BLOCK, KV_W, KV_W), F32)
    for n, wsrc in enumerate((wk, wk, wv, wv)):
        w_blk = w_blk.at[:, HEAD_DIM * n:HEAD_DIM * (n + 1), HEAD_DIM * n:HEAD_DIM * (n + 1)].set(wsrc)
    pe4 = jnp.tile(cmp_pos[l], (1, KV_W // HEAD_DIM))
    rows = np.concatenate([np.arange(HEAD_DIM) + HEAD_DIM * h for h in _Q_HEAD_ORDER])
    w_a = jnp.take(w_up_nsa[l], jnp.asarray(rows), axis=0).astype(BF16)
    return w_proj, w_merge, bias, w_blk.astype(BF16), pe4, w_a


def _pad_heads(a):
    return jnp.concatenate([a, jnp.zeros((a.shape[0], 8 - a.shape[1], a.shape[2]), a.dtype)], axis=1)


def _token_lanes(cache):
    l, p, t = cache.shape[:3]
    return cache.transpose(0, 1, 3, 4, 5, 2).reshape(l * p, -1, t)


def _own_lanes(o8, n_heads):
    n = o8.shape[0]
    o = o8[:, 0:n_heads].reshape(n, n_heads, n_heads, HEAD_DIM)
    return jnp.stack([o[:, h, h] for h in range(n_heads)], axis=1).reshape(n, n_heads * HEAD_DIM)


def kernel(x_prompt, x_sample, cache_nsa_cmp_kv, cache_nsa_sel_kv, cache_sb_kv, cache_fox_kv, cache_fox_logf,
           state_nsa_win_kv, page_table, norm_mix_g, norm_ffn_g, norm_final_g, w_in, b_forget, cmp_pos,
           w_cmp_k, w_cmp_v, w_up_nsa, w_up_sb, w_up_fox, w_out, w_ffn_gate, w_ffn_up, w_ffn_down):
    batch, seq, d = x_prompt.shape
    bd = x_sample.shape[0]
    depth = w_in.shape[0]
    n_pool = cache_sb_kv.shape[1]
    n_pages = page_table.shape[1]
    past_len = n_pages * PAGE
    m = batch * seq
    tm = min(256, seq)
    tq = min(256, seq)
    pp = min(PAGES_PER_STEP, n_pages)
    pp_prompt = min(PAGES_PER_STEP, seq // PAGE)

    rope_p = _rope_table(jnp.arange(seq, dtype=I32))
    rope_s = _rope_table(jnp.full((bd,), past_len, I32))
    ident_pt = jnp.arange(batch * (seq // PAGE), dtype=I32).reshape(batch, seq // PAGE)
    pool_cmp, pool_sel = _token_lanes(cache_nsa_cmp_kv), _token_lanes(cache_nsa_sel_kv)
    pool_sb, pool_fox = _token_lanes(cache_sb_kv), _token_lanes(cache_fox_kv)
    pool_lf = cache_fox_logf.astype(F32).transpose(0, 1, 3, 2).reshape(depth * n_pool, FOX_HEADS, PAGE)
    win_state = _token_lanes(state_nsa_win_kv)

    xp = x_prompt.reshape(m, d)
    xs = x_sample.reshape(bd, d)
    p_states, s_states = [], []
    for l in range(depth):
        w_proj, w_merge, bias, w_blk, pe4, w_a = _layer_weights(l, w_in, b_forget, cmp_pos, w_cmp_k, w_cmp_v, w_up_nsa)
        w_b, w_c, w_o = w_up_sb[l].astype(BF16), w_up_fox[l].astype(BF16), w_out[l].astype(BF16)
        w_g, w_u, w_d = w_ffn_gate[l].astype(BF16), w_ffn_up[l].astype(BF16), w_ffn_down[l].astype(BF16)
        g_mix, g_ffn = norm_mix_g[l].reshape(1, d), norm_ffn_g[l].reshape(1, d)

        (qn, kv_cmp, kv_sel, kv_win, kv_sb, kv_fox, kb_sel, kb_win, kb_sb, kb_fox, q_sb, q_fox, gates, logf, cum
         ) = _proj(xp, g_mix, w_proj, rope_p, bias, tm=tm, tiles_per_seq=seq // tm)
        ckv = _compress(kv_cmp.reshape(m // PAGE, PAGE, KV_W), ident_pt, w_blk, pe4, pp=pp_prompt, token_lanes=False)
        o_nsa = _nsa_prompt(qn, ckv, kb_sel, kb_win, gates, batch=batch, seq=seq, tq=tq)
        o_sb = _sb_prompt(q_sb, kb_sb, batch=batch, seq=seq, tq=tq)
        cum4 = cum[:, LOGF_LANE:LOGF_LANE + FOX_HEADS]
        cum_q = cum4.T[:, :, None]
        cum_k = cum4.reshape(batch, seq, FOX_HEADS).transpose(0, 2, 1).reshape(batch, FOX_HEADS, seq // tq, 1, tq)
        o_fox = _fox_prompt(q_fox, kb_fox, cum_q, cum_k, batch=batch, seq=seq, tq=tq)
        xp = _merge(xp, g_mix, o_nsa, o_sb, o_fox, w_merge, w_a, w_b, w_c, w_o, tm=tm)
        xp = _ffn(xp, g_ffn, w_g, w_u, w_d, tm=tm)
        win_keep = min(WINDOW, seq)
        p_states.append((
            kv_cmp.reshape(batch, seq, 2, NSA_GROUPS, HEAD_DIM), kv_sel.reshape(batch, seq, 2, NSA_GROUPS, HEAD_DIM),
            kv_sb.reshape(batch, seq, 2, SB_HEADS, HEAD_DIM), kv_fox.reshape(batch, seq, 2, FOX_HEADS, HEAD_DIM),
            logf[:, LOGF_LANE:LOGF_LANE + FOX_HEADS].reshape(batch, seq, FOX_HEADS),
            kv_win.reshape(batch, seq, 2, NSA_GROUPS, HEAD_DIM)[:, seq - win_keep:]))

        (qn, kv_cmp, kv_sel, kv_win, kv_sb, kv_fox, _, _, _, _, q_sb, q_fox, gates, logf, _
         ) = _proj(xs, g_mix, w_proj, rope_s, bias, tm=bd, tiles_per_seq=1)
        pt_l = page_table + l * n_pool
        ckv = _compress(pool_cmp, pt_l, w_blk, pe4, pp=pp, token_lanes=True)
        q8 = qn.transpose(2, 0, 1, 3).reshape(bd, NSA_HEADS, LANES)
        o_nsa8, win_new = _nsa_dec(q8, ckv, gates.transpose(1, 0, 2), win_state, kv_sel.reshape(bd, 1, KV_W),
                                   kv_win.reshape(bd, 1, KV_W), pool_sel, pt_l, pp=pp, layer=l)
        o_sb8 = _sb_dec(_pad_heads(q_sb.transpose(1, 0, 2)), pool_sb, pt_l, pp=pp)
        lf4 = logf[:, LOGF_LANE:LOGF_LANE + FOX_HEADS]
        lf_new = _pad_heads(jnp.broadcast_to(lf4[:, :, None], (bd, FOX_HEADS, LANES)))
        o_fox8 = _fox_dec(_pad_heads(q_fox.transpose(1, 0, 2)), kv_fox[:, 0:SB_W].reshape(bd, 1, SB_W),
                          kv_fox[:, SB_W:2 * SB_W].reshape(bd, 1, SB_W), lf_new, pool_fox, pool_lf, pt_l, pp=pp)
        o_a = o_nsa8.reshape(bd, NSA_GROUPS, NSA_HPG, NSA_GROUPS, HEAD_DIM)
        o_a = jnp.stack([o_a[:, g, :, g] for g in range(NSA_GROUPS)], axis=2).reshape(bd, NSA_W)
        o_a = jnp.stack([o_a, jnp.zeros_like(o_a)], axis=0).astype(BF16)
        o_b = _own_lanes(o_sb8, SB_HEADS).astype(BF16)
        o_c = _own_lanes(o_fox8, FOX_HEADS).astype(BF16)
        xs = _merge(xs, g_mix, o_a, o_b, o_c, w_merge, w_a, w_b, w_c, w_o, tm=bd)
        xs = _ffn(xs, g_ffn, w_g, w_u, w_d, tm=bd)
        s_states.append((
            kv_cmp.reshape(bd, 1, 2, NSA_GROUPS, HEAD_DIM), kv_sel.reshape(bd, 1, 2, NSA_GROUPS, HEAD_DIM),
            kv_sb.reshape(bd, 1, 2, SB_HEADS, HEAD_DIM), kv_fox.reshape(bd, 1, 2, FOX_HEADS, HEAD_DIM),
            lf4.reshape(bd, 1, FOX_HEADS),
            win_new.reshape(bd, 2, NSA_GROUPS, HEAD_DIM, -1).transpose(0, 4, 1, 2, 3)))

    g_fin = norm_final_g.reshape(1, d)
    y_prompt = _final_norm(xp, g_fin, tm=tm).reshape(batch, seq, d)
    y_sample = _final_norm(xs, g_fin, tm=bd).reshape(bd, 1, d)
    stk = lambda sts, i: jnp.stack([st[i] for st in sts], axis=0)
    return (y_prompt, y_sample) + tuple(stk(p_states, i) for i in range(6)) + tuple(stk(s_states, i) for i in range(6))
```

```python
import functools

import numpy as np
import jax
import jax.numpy as jnp
from jax import lax
from jax.experimental import pallas as pl
from jax.experimental.pallas import tpu as pltpu

F32, BF16, I32 = jnp.float32, jnp.bfloat16, jnp.int32

HEAD_DIM = 64
ROT_DIM = HEAD_DIM // 4
ROPE_THETA = 500000.0
NSA_HEADS = 8
NSA_GROUPS = 2
NSA_HPG = NSA_HEADS // NSA_GROUPS
CMP_STRIDE = 16
CMP_BLOCK = 32
SEL_BLOCK = 64
TOP_N = 16
WINDOW = 512
SB_HEADS = 4
FOX_HEADS = 4
PAGE = 128
SEL_FORCE_SCORE = 1.0e4
NEG = -1.0e30
RMS_EPS = 1e-6
SCALE = HEAD_DIM ** -0.5
LOG2E = 1.4426950408889634

LANES = 128
NSA_W = NSA_HEADS * HEAD_DIM
KV_W = 2 * NSA_GROUPS * HEAD_DIM
SB_W = SB_HEADS * HEAD_DIM
GATE_W = 3 * NSA_HEADS
C_Q, C_CMP, C_SEL, C_WIN, C_SB, C_FOX, C_MISC = 0, 512, 768, 1024, 1280, 2048, 2816
N_PROJ = C_MISC + LANES
LOGF_LANE = GATE_W
VMEM_LIMIT = 56 * 1024 * 1024
PAGES_PER_STEP = 8


def _cp(*sem):
    return pltpu.CompilerParams(dimension_semantics=sem, vmem_limit_bytes=VMEM_LIMIT)


def _dot(a, b):
    return jnp.dot(a, b, preferred_element_type=F32)


def _dot_nt(a, b):
    return lax.dot_general(a, b, (((1,), (1,)), ((), ())), preferred_element_type=F32)


def _split_dot(a, r, parts):
    out, rem = None, a
    for _ in range(parts):
        hi = rem.astype(BF16)
        d = _dot(hi, r)
        out = d if out is None else out + d
        rem = rem - hi.astype(F32)
    return out


def _split_dot_l(l, a, parts):
    out, rem = None, a
    for _ in range(parts):
        hi = rem.astype(BF16)
        d = _dot(l, hi)
        out = d if out is None else out + d
        rem = rem - hi.astype(F32)
    return out


def _rms(x, g):
    return x * lax.rsqrt(jnp.mean(x * x, axis=-1, keepdims=True) + RMS_EPS) * g


def _softplus(z):
    return jnp.maximum(z, 0.0) + jnp.log(1.0 + jnp.exp(-jnp.abs(z)))


def _iota(shape, axis):
    return lax.broadcasted_iota(I32, shape, axis)


def _proj_kernel(x_ref, g_ref, w_ref, rope_ref, b_ref, tri_ref,
                 qn_ref, kvc_ref, kvs_ref, kvw_ref, kvsb_ref, kvfx_ref,
                 kbs_ref, kbw_ref, kbsb_ref, kbfx_ref, qsb_ref, qfx_ref,
                 gate_ref, logf_ref, cum_ref, h_scr, carry_scr, *, tiles_per_seq):
    i = pl.program_id(0)
    tm = x_ref.shape[0]
    h_scr[...] = _rms(x_ref[...], g_ref[...]).astype(BF16)
    hb = h_scr[...]
    cos, sin_lo, sin_hi = rope_ref[:, 0:128], rope_ref[:, 128:256], rope_ref[:, 256:384]

    def rope(seg):
        return seg * cos + pltpu.roll(seg, LANES - 8, 1) * sin_lo + pltpu.roll(seg, 8, 1) * sin_hi

    lane = _iota((tm, LANES), 1)
    pq = _dot(hb, w_ref[:, C_Q:C_Q + NSA_W])
    for s in range(NSA_HPG):
        seg = rope(pq[:, LANES * s:LANES * (s + 1)]) * (SCALE * LOG2E)
        for g in range(NSA_GROUPS):
            qn_ref[g, s] = jnp.where(lane // HEAD_DIM == g, seg, 0.0).astype(BF16)

    for c0, f_ref, h_ref in ((C_CMP, kvc_ref, None), (C_SEL, kvs_ref, kbs_ref), (C_WIN, kvw_ref, kbw_ref)):
        p = _dot(hb, w_ref[:, c0:c0 + KV_W])
        k, v = rope(p[:, 0:128]), p[:, 128:256]
        f_ref[:, 0:128] = k
        f_ref[:, 128:256] = v
        if h_ref is not None:
            h_ref[:, 0:128] = k.astype(BF16)
            h_ref[:, 128:256] = v.astype(BF16)

    lane_w = _iota((tm, SB_W), 1)
    for c0, q_ref, f_ref, h_ref, q_scale in ((C_SB, qsb_ref, kvsb_ref, kbsb_ref, SCALE),
                                             (C_FOX, qfx_ref, kvfx_ref, kbfx_ref, SCALE * LOG2E)):
        p = _dot(hb, w_ref[:, c0:c0 + 3 * SB_W])
        q = p[:, 0:SB_W] * q_scale
        for h in range(SB_HEADS):
            q_ref[h] = jnp.where(lane_w // HEAD_DIM == h, q, 0.0).astype(BF16)
        kv = p[:, SB_W:3 * SB_W]
        f_ref[...] = kv
        h_ref[...] = kv.astype(BF16)

    pm = _dot(hb, w_ref[:, C_MISC:C_MISC + LANES])
    sg = jax.nn.sigmoid(pm)
    gate_ref[0] = sg
    gate_ref[1] = pltpu.roll(sg, LANES - GATE_W // 2, 1)
    zf = pm + b_ref[...]
    lf = jnp.minimum(zf, 0.0) - jnp.log1p(jnp.exp(-jnp.abs(zf)))
    logf_ref[...] = lf

    @pl.when(i % tiles_per_seq == 0)
    def _():
        carry_scr[...] = jnp.zeros_like(carry_scr)

    c = _split_dot_l(tri_ref[...], lf, 3) + carry_scr[0:1, :]
    cum_ref[...] = c
    carry_scr[...] = jnp.broadcast_to(c[tm - 1:tm, :], carry_scr.shape)


def _proj(x, gain, w, rope_tab, bias, *, tm, tiles_per_seq):
    m, d = x.shape
    tri = jnp.tril(jnp.ones((tm, tm), F32)).astype(BF16)
    row = lambda wid: pl.BlockSpec((tm, wid), lambda i: (i, 0))
    const = lambda shape: pl.BlockSpec(shape, lambda i: (0,) * len(shape))
    f32o = lambda wid: jax.ShapeDtypeStruct((m, wid), F32)
    b16o = lambda wid: jax.ShapeDtypeStruct((m, wid), BF16)
    out_shape = (
        jax.ShapeDtypeStruct((NSA_GROUPS, NSA_HPG, m, LANES), BF16),
        f32o(KV_W), f32o(KV_W), f32o(KV_W), f32o(2 * SB_W), f32o(2 * SB_W),
        b16o(KV_W), b16o(KV_W), b16o(2 * SB_W), b16o(2 * SB_W),
        jax.ShapeDtypeStruct((SB_HEADS, m, SB_W), BF16), jax.ShapeDtypeStruct((SB_HEADS, m, SB_W), BF16),
        jax.ShapeDtypeStruct((NSA_GROUPS, m, LANES), F32), f32o(LANES), f32o(LANES),
    )
    out_specs = (
        pl.BlockSpec((NSA_GROUPS, NSA_HPG, tm, LANES), lambda i: (0, 0, i, 0)),
        row(KV_W), row(KV_W), row(KV_W), row(2 * SB_W), row(2 * SB_W),
        row(KV_W), row(KV_W), row(2 * SB_W), row(2 * SB_W),
        pl.BlockSpec((SB_HEADS, tm, SB_W), lambda i: (0, i, 0)),
        pl.BlockSpec((SB_HEADS, tm, SB_W), lambda i: (0, i, 0)),
        pl.BlockSpec((NSA_GROUPS, tm, LANES), lambda i: (0, i, 0)),
        row(LANES), row(LANES),
    )
    return pl.pallas_call(
        functools.partial(_proj_kernel, tiles_per_seq=tiles_per_seq),
        grid=(m // tm,),
        in_specs=[row(d), const((1, d)), const((d, N_PROJ)),
                  pl.BlockSpec((tm, 3 * LANES), lambda i: (i % tiles_per_seq, 0)),
                  const((1, LANES)), const((tm, tm))],
        out_specs=out_specs, out_shape=out_shape,
        scratch_shapes=[pltpu.VMEM((tm, d), BF16), pltpu.VMEM((8, LANES), F32)],
        compiler_params=_cp("arbitrary"), name="proj",
    )(x, gain, w, rope_tab, bias, tri)


def _compress_kernel(pt_ref, *refs, pp, nch, token_lanes):
    pages, (perm_ref, w_ref, pe_ref, out_ref, x_scr, acc_scr, pos_scr) = refs[:pp], refs[pp:]
    j = pl.program_id(1)
    half = CMP_BLOCK // 2
    chunks = PAGE // CMP_STRIDE

    @pl.when((pl.program_id(0) == 0) & (j == 0))
    def _():
        c = jnp.zeros((8, KV_W), F32)
        for l in range(half):
            for off, cols in ((0, slice(0, KV_W)), (half, slice(KV_W, 2 * KV_W))):
                row = jnp.broadcast_to(pe_ref[off + l:off + l + 1, :], (8, KV_W))
                c = c + _split_dot(row, w_ref[l, :, cols], 2)
        pos_scr[...] = c

    for i in range(pp):
        pg = pages[i][0].astype(BF16)
        gt = _dot_nt(perm_ref[...], pg) if token_lanes else _dot(perm_ref[...], pg)
        base = pl.multiple_of((j * pp + i) * chunks, chunks)
        for l in range(half):
            x_scr[l, pl.ds(base, chunks), :] = gt[chunks * l:chunks * (l + 1), :]

    @pl.when(j == pl.num_programs(1) - 1)
    def _():
        acc_scr[...] = jnp.zeros_like(acc_scr)
        for l in range(half):
            acc_scr[...] += _dot(x_scr[l].astype(BF16), w_ref[l])
        nxt = pltpu.roll(acc_scr[:, KV_W:2 * KV_W], nch - 1, 0)
        nxt = jnp.where(_iota((nch, KV_W), 0) < nch - 1, nxt, 0.0)
        out_ref[0] = (acc_scr[:, 0:KV_W] + nxt + pos_scr[0:1, :]).astype(BF16)


def _compress(pool, page_table, w_cat, pe4, *, pp, token_lanes):
    b, n_pages = page_table.shape
    chunks = PAGE // CMP_STRIDE
    nch = n_pages * chunks
    half = CMP_BLOCK // 2
    tok = np.arange(PAGE)
    perm = jnp.asarray(np.arange(PAGE)[:, None] == ((tok % CMP_STRIDE) * chunks + tok // CMP_STRIDE)[None, :], BF16)
    page_shape = (1, KV_W, PAGE) if token_lanes else (1, PAGE, KV_W)
    page_spec = lambda i: pl.BlockSpec(page_shape, lambda bb, j, pt: (pt[bb, j * pp + i], 0, 0))
    grid_spec = pltpu.PrefetchScalarGridSpec(
        num_scalar_prefetch=1, grid=(b, n_pages // pp),
        in_specs=[page_spec(i) for i in range(pp)] + [
            pl.BlockSpec((PAGE, PAGE), lambda bb, j, pt: (0, 0)),
            pl.BlockSpec((half, KV_W, 2 * KV_W), lambda bb, j, pt: (0, 0, 0)),
            pl.BlockSpec((CMP_BLOCK, KV_W), lambda bb, j, pt: (0, 0))],
        out_specs=pl.BlockSpec((1, nch, KV_W), lambda bb, j, pt: (bb, 0, 0)),
        scratch_shapes=[pltpu.VMEM((half, nch, KV_W), F32), pltpu.VMEM((nch, 2 * KV_W), F32),
                        pltpu.VMEM((8, KV_W), F32)])
    return pl.pallas_call(
        functools.partial(_compress_kernel, pp=pp, nch=nch, token_lanes=token_lanes), grid_spec=grid_spec,
        out_shape=jax.ShapeDtypeStruct((b, nch, KV_W), BF16),
        compiler_params=_cp("arbitrary", "arbitrary"), name="compress",
    )(page_table, *([pool] * pp), perm, w_cat, pe4)


def _masked_softmax_rows(s, mask):
    s = jnp.where(mask, s, NEG)
    e = jnp.where(mask, jnp.exp2(s - jnp.max(s, axis=1, keepdims=True)), 0.0)
    return e / jnp.maximum(jnp.sum(e, axis=1, keepdims=True), 1e-30)


def _online_init(m_scr, l_scr, acc_scr):
    m_scr[...] = jnp.full_like(m_scr, NEG)
    l_scr[...] = jnp.zeros_like(l_scr)
    acc_scr[...] = jnp.zeros_like(acc_scr)


def _select_blocks(imp, pos, n_sel):
    blk = _iota(imp.shape, 1)
    cur = pos // SEL_BLOCK
    valid = blk * SEL_BLOCK <= pos
    forced = jnp.where(blk == 0, 1.0, 0.0) + jnp.where(blk == cur, 1.0, 0.0) + jnp.where(blk == cur - 1, 1.0, 0.0)
    score = jnp.where(valid, jnp.where(forced > 0.5, SEL_FORCE_SCORE, imp), -1.0)
    score = jnp.where(blk < n_sel, score, -2.0)
    rank = jnp.zeros(imp.shape, F32)
    for b2 in range(n_sel):
        col = score[:, b2:b2 + 1]
        ge = jnp.where(col >= score, 1.0, 0.0)
        gt = jnp.where(col > score, 1.0, 0.0)
        rank = rank + jnp.where(blk > b2, ge, gt)
    return jnp.where(rank < min(TOP_N, n_sel), 1.0, 0.0)


def _rep(x, n):
    return x if n == 1 else jnp.concatenate([x] * n, axis=1)


def _select_blocks_t(p_grp, amat_t, q0, n_sel):
    t = p_grp.shape[0]
    nb = amat_t.shape[0]
    hi = p_grp.astype(BF16)
    lo = (p_grp - hi.astype(F32)).astype(BF16)
    imp = _dot_nt(amat_t, hi) + _dot_nt(amat_t, lo)
    pos = q0 + _iota((1, t), 1)
    blk = _iota((nb, t), 0)
    cur = pos // SEL_BLOCK
    forced = jnp.where(blk == 0, 1.0, 0.0) + jnp.where(blk == cur, 1.0, 0.0) + jnp.where(blk == cur - 1, 1.0, 0.0)
    score = jnp.where(blk * SEL_BLOCK <= pos, jnp.where(forced > 0.5, SEL_FORCE_SCORE, imp), -1.0)
    score = jnp.where(blk < n_sel, score, -2.0)
    groups = [score[8 * k:8 * (k + 1)] for k in range(nb // 8)]
    ranks = [jnp.zeros((8, t), F32) for _ in groups]
    sub = _iota((8, t), 0)
    for b2 in range(n_sel):
        rowv = score[b2:b2 + 1, :]
        k2, r2 = divmod(b2, 8)
        for k, grp in enumerate(groups):
            if k > k2:
                inc = jnp.where(rowv >= grp, 1.0, 0.0)
            elif k < k2:
                inc = jnp.where(rowv > grp, 1.0, 0.0)
            else:
                inc = jnp.where(sub > r2, jnp.where(rowv >= grp, 1.0, 0.0), jnp.where(rowv > grp, 1.0, 0.0))
            ranks[k] = ranks[k] + inc
    chosen = [jnp.where(rk < min(TOP_N, n_sel), 1.0, 0.0) for rk in ranks]
    pad = [jnp.zeros((LANES - nb, t), F32)] if nb < LANES else []
    return jnp.concatenate(chosen + pad, axis=0)


def _nsa_prompt_kernel(qn_ref, ckv_ref, ks_ref, kw_ref, gate_ref, amat_ref, e_ref, out_ref,
                       m_scr, l_scr, acc_scr, oc_scr, os_scr, *, n_sel):
    g, i = pl.program_id(1), pl.program_id(2)
    t = qn_ref.shape[2]
    q0 = i * t
    rowi, coli = _iota((t, t), 0), _iota((t, t), 1)
    reps = t // LANES

    nc = ckv_ref.shape[1]
    cmask = _iota((t, nc), 1) * CMP_STRIDE + (CMP_BLOCK - 1) <= q0 + _iota((t, 1), 0)
    ck, cv = ckv_ref[0, :, 0:128], ckv_ref[0, :, 128:256]
    p_grp = None
    for h in range(NSA_HPG):
        p = _masked_softmax_rows(_dot_nt(qn_ref[0, h], ck), cmask)
        oc_scr[h] = _dot(p.astype(BF16), cv)
        p_grp = p if p_grp is None else p_grp + p
    sel = _select_blocks_t(p_grp, amat_ref[...], q0, n_sel).T.astype(BF16)

    def online_head(h, s, v):
        m_prev = m_scr[h]
        m_next = jnp.maximum(m_prev, jnp.max(s, axis=1, keepdims=True))
        alpha = jnp.exp2(m_prev - m_next)
        e = jnp.exp2(s - _rep(m_next, reps))
        l_scr[h] = alpha * l_scr[h] + jnp.sum(e, axis=1, keepdims=True)
        m_scr[h] = m_next
        acc_scr[h] = acc_scr[h] * alpha + _dot(e.astype(BF16), v)

    def reset():
        m_scr[...] = jnp.full_like(m_scr, NEG)
        l_scr[...] = jnp.zeros_like(l_scr)
        acc_scr[...] = jnp.zeros_like(acc_scr)

    def sel_tile(j, diag):
        kt = ks_ref[0, pl.ds(pl.multiple_of(j * t, t), t), :]
        chosen = _dot(sel, e_ref[j])
        if diag:
            chosen = jnp.where(coli <= rowi, chosen, 0.0)
        keep = chosen > 0.5
        for h in range(NSA_HPG):
            online_head(h, jnp.where(keep, _dot_nt(qn_ref[0, h], kt[:, 0:128]), NEG), kt[:, 128:256])

    reset()
    lax.fori_loop(0, i, lambda j, c: (sel_tile(j, False), c)[1], 0)
    sel_tile(i, True)
    for h in range(NSA_HPG):
        os_scr[h] = acc_scr[h] / jnp.maximum(l_scr[h], 1e-30)

    w_len = WINDOW + t
    k0 = jnp.maximum(i - WINDOW // t, 0) * t
    kw = kw_ref[0, pl.ds(pl.multiple_of(k0, t), w_len), :]
    d = (q0 - k0) + _iota((t, w_len), 0) - _iota((t, w_len), 1)
    keep_w = jnp.where(d >= 0, jnp.where(d <= WINDOW, 1.0, 0.0), 0.0) > 0.5

    gates = gate_ref[0]
    lane = _iota((t, LANES), 1)
    for h in range(NSA_HPG):
        s = jnp.where(keep_w, _dot_nt(qn_ref[0, h], kw[:, 0:128]), NEG)
        e = jnp.exp2(s - jnp.max(s, axis=1, keepdims=True))
        o_w = _dot(e.astype(BF16), kw[:, 128:256]) / jnp.sum(e, axis=1, keepdims=True)
        o = (gates[:, 3 * h:3 * h + 1] * oc_scr[h] + gates[:, 3 * h + 1:3 * h + 2] * os_scr[h]
             + gates[:, 3 * h + 2:3 * h + 3] * o_w)
        out_ref[0, :, LANES * h:LANES * (h + 1)] = jnp.where(lane // HEAD_DIM == g, o, 0.0).astype(BF16)


def _imp_matrix(nc, width):
    n = np.arange(nc)[:, None]
    b = np.arange(width)[None, :]
    sub = SEL_BLOCK // CMP_STRIDE
    return jnp.asarray((n // sub == b).astype(np.float32) + ((n + 1) // sub == b), BF16)


def _expand_tiles(n_tiles, n_blk, tk):
    key = np.arange(n_tiles * tk).reshape(n_tiles, 1, tk)
    return jnp.asarray(key // SEL_BLOCK == np.arange(n_blk)[None, :, None], BF16)


def _nsa_prompt(qn, ckv, kb_sel, kb_win, gates, *, batch, seq, tq):
    nq = seq // tq
    nc = ckv.shape[1]
    n_sel = seq // SEL_BLOCK
    assert n_sel <= LANES and WINDOW % tq == 0 and seq >= WINDOW + tq
    amat = _imp_matrix(nc, -(-n_sel // 8) * 8).T
    etile = _expand_tiles(nq, LANES, tq)
    m = batch * seq
    return pl.pallas_call(
        functools.partial(_nsa_prompt_kernel, n_sel=n_sel),
        grid=(batch, NSA_GROUPS, nq),
        in_specs=[
            pl.BlockSpec((1, NSA_HPG, tq, LANES), lambda b, g, i: (g, 0, b * nq + i, 0)),
            pl.BlockSpec((1, nc, KV_W), lambda b, g, i: (b, 0, 0)),
            pl.BlockSpec((1, seq, KV_W), lambda b, g, i: (b, 0, 0)),
            pl.BlockSpec((1, seq, KV_W), lambda b, g, i: (b, 0, 0)),
            pl.BlockSpec((1, tq, LANES), lambda b, g, i: (g, b * nq + i, 0)),
            pl.BlockSpec(amat.shape, lambda b, g, i: (0, 0)),
            pl.BlockSpec(etile.shape, lambda b, g, i: (0, 0, 0)),
        ],
        out_specs=pl.BlockSpec((1, tq, NSA_W), lambda b, g, i: (g, b * nq + i, 0)),
        out_shape=jax.ShapeDtypeStruct((NSA_GROUPS, m, NSA_W), BF16),
        scratch_shapes=[pltpu.VMEM((NSA_HPG, tq, LANES), F32)] * 5,
        compiler_params=_cp("arbitrary", "arbitrary", "arbitrary"), name="nsa_prompt",
    )(qn, ckv, kb_sel.reshape(batch, seq, KV_W), kb_win.reshape(batch, seq, KV_W), gates, amat, etile)


def _upper_strict(n):
    return jnp.asarray(np.triu(np.ones((n, n), np.float32), 1).T, BF16)


def _stack_heads(v):
    lane = _iota(v.shape, 1)
    return jnp.concatenate([jnp.where(lane // HEAD_DIM == h, v, jnp.zeros_like(v)) for h in range(SB_HEADS)], axis=0)


def _head_lanes(parts):
    lane = _iota(parts[0].shape, 1)
    return jnp.concatenate([jnp.where(lane < HEAD_DIM, parts[0], parts[1]),
                            jnp.where(lane < HEAD_DIM, parts[2], parts[3])], axis=1)


def _sb_prompt_kernel(q_ref, kv_ref, u_ref, out_ref, acc_scr, r_scr):
    i = pl.program_id(1)
    t = q_ref.shape[1]
    rowi, coli = _iota((t, t), 0), _iota((t, t), 1)
    acc_scr[...] = jnp.zeros_like(acc_scr)
    r_scr[...] = jnp.zeros_like(r_scr)

    def tile(j, diag):
        kt = kv_ref[0, pl.ds(pl.multiple_of(j * t, t), t), :]
        ws = []
        for h in range(SB_HEADS):
            z = _dot_nt(q_ref[h], kt[:, 0:SB_W])
            sp = _softplus(z)
            log_keep = -sp
            if diag:
                log_keep = jnp.where(coli < rowi, log_keep, 0.0)
            r = r_scr[h]
            w = jnp.exp(z - sp + _split_dot(log_keep, u_ref[...], 2) + _rep(r, t // LANES))
            if diag:
                w = jnp.where(coli < rowi, w, 0.0)
            ws.append(w.astype(BF16))
            r_scr[h] = r + jnp.sum(log_keep, axis=1, keepdims=True)
        acc_scr[...] += _dot(jnp.concatenate(ws, axis=1), _stack_heads(kt[:, SB_W:2 * SB_W]))

    tile(i, True)
    lax.fori_loop(0, i, lambda n, c: (tile(i - 1 - n, False), c)[1], 0)
    out_ref[...] = acc_scr[...].astype(BF16)


def _sb_prompt(q, kb, *, batch, seq, tq):
    nq = seq // tq
    u = _upper_strict(tq)
    return pl.pallas_call(
        _sb_prompt_kernel, grid=(batch, nq),
        in_specs=[pl.BlockSpec((SB_HEADS, tq, SB_W), lambda b, i: (0, b * nq + i, 0)),
                  pl.BlockSpec((1, seq, 2 * SB_W), lambda b, i: (b, 0, 0)),
                  pl.BlockSpec(u.shape, lambda b, i: (0, 0))],
        out_specs=pl.BlockSpec((tq, SB_W), lambda b, i: (b * nq + i, 0)),
        out_shape=jax.ShapeDtypeStruct((batch * seq, SB_W), BF16),
        scratch_shapes=[pltpu.VMEM((tq, SB_W), F32), pltpu.VMEM((SB_HEADS, tq, LANES), F32)],
        compiler_params=_cp("arbitrary", "arbitrary"), name="sb_prompt",
    )(q, kb.reshape(batch, seq, 2 * SB_W), u)


def _fox_prompt_kernel(q_ref, kv_ref, cq_ref, ck_ref, out_ref, m_scr, l_scr, acc_scr, cq_scr):
    i = pl.program_id(1)
    t = q_ref.shape[1]
    rowi, coli = _iota((t, t), 0), _iota((t, t), 1)
    m_scr[...] = jnp.full_like(m_scr, NEG)
    l_scr[...] = jnp.zeros_like(l_scr)
    acc_scr[...] = jnp.zeros_like(acc_scr)
    for h in range(FOX_HEADS):
        cq_scr[h] = jnp.broadcast_to(cq_ref[h] * LOG2E, (t, LANES))

    def tile(j, diag):
        kt = kv_ref[0, pl.ds(pl.multiple_of(j * t, t), t), :]
        es, alphas = [], []
        for h in range(FOX_HEADS):
            s = _dot_nt(q_ref[h], kt[:, 0:SB_W]) + (_rep(cq_scr[h], t // LANES) - ck_ref[0, h, j] * LOG2E)
            if diag:
                s = jnp.where(coli <= rowi, s, NEG)
            m_prev = m_scr[h]
            m_next = jnp.maximum(m_prev, jnp.max(s, axis=1, keepdims=True))
            alpha = jnp.exp2(m_prev - m_next)
            e = jnp.exp2(s - _rep(m_next, t // LANES))
            l_scr[h] = alpha * l_scr[h] + jnp.sum(e, axis=1, keepdims=True)
            m_scr[h] = m_next
            es.append(e.astype(BF16))
            alphas.append(alpha)
        acc_scr[...] = acc_scr[...] * _head_lanes(alphas) + _dot(jnp.concatenate(es, axis=1),
                                                                 _stack_heads(kt[:, SB_W:2 * SB_W]))

    lax.fori_loop(0, i, lambda j, c: (tile(j, False), c)[1], 0)
    tile(i, True)
    out_ref[...] = (acc_scr[...] / _head_lanes([jnp.maximum(l_scr[h], 1e-30) for h in range(FOX_HEADS)])).astype(BF16)


def _fox_prompt(q, kb, cum_q, cum_k, *, batch, seq, tq):
    nq = seq // tq
    nk = seq // tq
    return pl.pallas_call(
        _fox_prompt_kernel, grid=(batch, nq),
        in_specs=[pl.BlockSpec((FOX_HEADS, tq, SB_W), lambda b, i: (0, b * nq + i, 0)),
                  pl.BlockSpec((1, seq, 2 * SB_W), lambda b, i: (b, 0, 0)),
                  pl.BlockSpec((FOX_HEADS, tq, 1), lambda b, i: (0, b * nq + i, 0)),
                  pl.BlockSpec((1, FOX_HEADS, nk, 1, tq), lambda b, i: (b, 0, 0, 0, 0))],
        out_specs=pl.BlockSpec((tq, SB_W), lambda b, i: (b * nq + i, 0)),
        out_shape=jax.ShapeDtypeStruct((batch * seq, SB_W), BF16),
        scratch_shapes=[pltpu.VMEM((FOX_HEADS, tq, LANES), F32), pltpu.VMEM((FOX_HEADS, tq, LANES), F32),
                        pltpu.VMEM((tq, SB_W), F32), pltpu.VMEM((FOX_HEADS, tq, LANES), F32)],
        compiler_params=_cp("arbitrary", "arbitrary"), name="fox_prompt",
    )(q, kb.reshape(batch, seq, 2 * SB_W), cum_q, cum_k)


def _merge_kernel(x_ref, g_ref, oa_ref, ob_ref, oc_ref, wm_ref, wa_ref, wb_ref, wc_ref, wo_ref, out_ref):
    x = x_ref[...]
    d = x.shape[1]
    hb = _rms(x, g_ref[...]).astype(BF16)
    oa = oa_ref[0] + oa_ref[1]
    u = jax.nn.sigmoid(_dot(hb, wm_ref[:, 0:d])) * _dot(oa, wa_ref[...])
    u = u + jax.nn.sigmoid(_dot(hb, wm_ref[:, d:2 * d])) * _dot(ob_ref[...], wb_ref[...])
    u = u + jax.nn.sigmoid(_dot(hb, wm_ref[:, 2 * d:3 * d])) * _dot(oc_ref[...], wc_ref[...])
    out_ref[...] = x + _dot(u.astype(BF16), wo_ref[...])


def _merge(x, gain, o_nsa, o_sb, o_fox, w_merge, w_a, w_b, w_c, w_o, *, tm):
    m, d = x.shape
    row = lambda wid: pl.BlockSpec((tm, wid), lambda i: (i, 0))
    const = lambda a: pl.BlockSpec(a.shape, lambda i: (0,) * a.ndim)
    return pl.pallas_call(
        _merge_kernel, grid=(m // tm,),
        in_specs=[row(d), const(gain), pl.BlockSpec((NSA_GROUPS, tm, NSA_W), lambda i: (0, i, 0)),
                  row(SB_W), row(SB_W), const(w_merge), const(w_a), const(w_b), const(w_c), const(w_o)],
        out_specs=row(d), out_shape=jax.ShapeDtypeStruct((m, d), F32),
        compiler_params=_cp("arbitrary"), name="merge",
    )(x, gain, o_nsa, o_sb, o_fox, w_merge, w_a, w_b, w_c, w_o)


def _ffn_kernel(x_ref, g_ref, wg_ref, wu_ref, wd_ref, out_ref):
    x = x_ref[...]
    hb = _rms(x, g_ref[...]).astype(BF16)
    a = _dot(hb, wg_ref[...])
    act = a * jax.nn.sigmoid(a) * _dot(hb, wu_ref[...])
    out_ref[...] = x + _dot(act.astype(BF16), wd_ref[...])


def _ffn(x, gain, w_gate, w_up, w_down, *, tm):
    m, d = x.shape
    row = pl.BlockSpec((tm, d), lambda i: (i, 0))
    const = lambda a: pl.BlockSpec(a.shape, lambda i: (0,) * a.ndim, pipeline_mode=pl.Buffered(1))
    return pl.pallas_call(
        _ffn_kernel, grid=(m // tm,),
        in_specs=[row, pl.BlockSpec(gain.shape, lambda i: (0, 0)), const(w_gate), const(w_up), const(w_down)],
        out_specs=row, out_shape=jax.ShapeDtypeStruct((m, d), F32),
        compiler_params=_cp("arbitrary"), name="ffn",
    )(x, gain, w_gate, w_up, w_down)


def _norm_kernel(x_ref, g_ref, out_ref):
    out_ref[...] = _rms(x_ref[...], g_ref[...])


def _final_norm(x, gain, *, tm):
    m, d = x.shape
    row = pl.BlockSpec((tm, d), lambda i: (i, 0))
    return pl.pallas_call(
        _norm_kernel, grid=(m // tm,), in_specs=[row, pl.BlockSpec((1, d), lambda i: (0, 0))],
        out_specs=row, out_shape=jax.ShapeDtypeStruct((m, d), F32),
        compiler_params=_cp("arbitrary"), name="final_norm",
    )(x, gain)


def _page_specs(rows, n_pages, pp, descending):
    def spec(i):
        if descending:
            return pl.BlockSpec((1, rows, PAGE), lambda b, j, pt: (pt[b, n_pages - 1 - (j * pp + i)], 0, 0))
        return pl.BlockSpec((1, rows, PAGE), lambda b, j, pt: (pt[b, j * pp + i], 0, 0))
    return [spec(i) for i in range(pp)]


def _lanes(parts):
    return jnp.concatenate(parts, axis=1)


def _sb_dec_kernel(pt_ref, q_ref, *refs, pp):
    pages, (u_ref, out_ref, acc_scr, r_scr) = refs[:pp], refs[pp:]
    j = pl.program_id(1)

    @pl.when(j == 0)
    def _():
        acc_scr[...] = jnp.zeros_like(acc_scr)
        r_scr[...] = jnp.zeros_like(r_scr)

    q = q_ref[0]
    z_all = _dot(q, _lanes([p[0, 0:SB_W, :].astype(BF16) for p in pages]))
    zs = [z_all[:, i * PAGE:(i + 1) * PAGE] for i in range(pp)]
    sps = [_softplus(z) for z in zs]
    afters = [_split_dot(-sp, u_ref[...], 2) for sp in sps]
    r = r_scr[...]
    ws = []
    for i in range(pp):
        ws.append(jnp.exp(zs[i] - sps[i] + afters[i] + r).astype(BF16))
        r = r + jnp.sum(-sps[i], axis=1, keepdims=True)
    acc_scr[...] += _dot_nt(_lanes(ws), _lanes([p[0, SB_W:2 * SB_W, :].astype(BF16) for p in pages]))
    r_scr[...] = r

    @pl.when(j == pl.num_programs(1) - 1)
    def _():
        out_ref[0] = acc_scr[...]


def _sb_dec(q8, pool, page_table, *, pp):
    bd, n_pages = page_table.shape
    u = _upper_strict(PAGE)
    grid_spec = pltpu.PrefetchScalarGridSpec(
        num_scalar_prefetch=1, grid=(bd, n_pages // pp),
        in_specs=[pl.BlockSpec((1, 8, SB_W), lambda b, j, pt: (b, 0, 0))]
        + _page_specs(2 * SB_W, n_pages, pp, True) + [pl.BlockSpec(u.shape, lambda b, j, pt: (0, 0))],
        out_specs=pl.BlockSpec((1, 8, SB_W), lambda b, j, pt: (b, 0, 0)),
        scratch_shapes=[pltpu.VMEM((8, SB_W), F32), pltpu.VMEM((8, LANES), F32)])
    return pl.pallas_call(
        functools.partial(_sb_dec_kernel, pp=pp), grid_spec=grid_spec,
        out_shape=jax.ShapeDtypeStruct((bd, 8, SB_W), F32),
        compiler_params=_cp("arbitrary", "arbitrary"), name="sb_decode",
    )(page_table, q8, *([pool] * pp), u)


def _fox_dec_kernel(pt_ref, q_ref, kn_ref, vn_ref, lfn_ref, *refs, pp):
    pages, lf_pages = refs[:pp], refs[pp:2 * pp]
    u_ref, out_ref, m_scr, l_scr, acc_scr, r_scr = refs[2 * pp:]
    j = pl.program_id(1)
    q = q_ref[0]
    reps = SB_W // LANES

    @pl.when(j == 0)
    def _():
        kn = kn_ref[0].astype(BF16).astype(F32)
        m_scr[...] = jnp.broadcast_to(jnp.sum(q.astype(F32) * kn, axis=1, keepdims=True), m_scr.shape)
        l_scr[...] = jnp.ones_like(l_scr)
        acc_scr[...] = jnp.broadcast_to(vn_ref[0].astype(BF16).astype(F32), acc_scr.shape)
        r_scr[...] = lfn_ref[0]

    z_all = _dot(q, _lanes([p[0, 0:SB_W, :].astype(BF16) for p in pages]))
    pad = jnp.zeros((8 - FOX_HEADS, PAGE), F32)
    r = r_scr[...]
    ss = []
    for i in range(pp):
        lf = jnp.concatenate([lf_pages[i][0], pad], axis=0)
        ss.append(z_all[:, i * PAGE:(i + 1) * PAGE] + LOG2E * (_split_dot(lf, u_ref[...], 3) + r))
        r = r + jnp.sum(lf, axis=1, keepdims=True)
    r_scr[...] = r
    m_prev = m_scr[...]
    m_next = m_prev
    for s in ss:
        m_next = jnp.maximum(m_next, jnp.max(s, axis=1, keepdims=True))
    alpha = jnp.exp2(m_prev - m_next)
    es = [jnp.exp2(s - m_next) for s in ss]
    l_new = alpha * l_scr[...]
    for e in es:
        l_new = l_new + jnp.sum(e, axis=1, keepdims=True)
    l_scr[...] = l_new
    m_scr[...] = m_next
    acc_scr[...] = acc_scr[...] * _rep(alpha, reps) + _dot_nt(
        _lanes([e.astype(BF16) for e in es]), _lanes([p[0, SB_W:2 * SB_W, :].astype(BF16) for p in pages]))

    @pl.when(j == pl.num_programs(1) - 1)
    def _():
        out_ref[0] = acc_scr[...] / _rep(l_scr[...], reps)


def _fox_dec(q8, k_new, v_new, lf_new, pool, lf_pool, page_table, *, pp):
    bd, n_pages = page_table.shape
    u = _upper_strict(PAGE)
    req = lambda shape: pl.BlockSpec((1,) + shape, lambda b, j, pt: (b,) + (0,) * len(shape))
    lf_spec = lambda i: pl.BlockSpec((1, FOX_HEADS, PAGE), lambda b, j, pt: (pt[b, n_pages - 1 - (j * pp + i)], 0, 0))
    grid_spec = pltpu.PrefetchScalarGridSpec(
        num_scalar_prefetch=1, grid=(bd, n_pages // pp),
        in_specs=[req((8, SB_W)), req((1, SB_W)), req((1, SB_W)), req((8, LANES))]
        + _page_specs(2 * SB_W, n_pages, pp, True) + [lf_spec(i) for i in range(pp)]
        + [pl.BlockSpec(u.shape, lambda b, j, pt: (0, 0))],
        out_specs=req((8, SB_W)),
        scratch_shapes=[pltpu.VMEM((8, LANES), F32), pltpu.VMEM((8, LANES), F32),
                        pltpu.VMEM((8, SB_W), F32), pltpu.VMEM((8, LANES), F32)])
    return pl.pallas_call(
        functools.partial(_fox_dec_kernel, pp=pp), grid_spec=grid_spec,
        out_shape=jax.ShapeDtypeStruct((bd, 8, SB_W), F32),
        compiler_params=_cp("arbitrary", "arbitrary"), name="fox_decode",
    )(page_table, q8, k_new, v_new, lf_new, *([pool] * pp), *([lf_pool] * pp), u)


def _nsa_dec_kernel(pt_ref, q_ref, ckv_ref, gate_ref, win_ref, kvs_new_ref, kvw_new_ref, kvw_col_ref, amat_ref, e_ref,
                    *refs, pp, past_len, n_sel):
    pages, (out_ref, win_out_ref, m_scr, l_scr, acc_scr, sel_scr, oc_scr, ow_scr) = refs[:pp], refs[pp:]
    j = pl.program_id(1)
    q = q_ref[0]
    rows = q.shape[0]
    row = _iota((rows, LANES), 0)
    row1 = _iota((rows, 1), 0)
    lane = _iota((rows, LANES), 1)
    pos = jnp.full((rows, 1), past_len, I32)

    def new_token(kv_row):
        k = kv_row[:, 0:LANES].astype(BF16).astype(F32)
        return jnp.sum(q.astype(F32) * k, axis=1, keepdims=True), kv_row[:, LANES:2 * LANES].astype(BF16).astype(F32)

    @pl.when(j == 0)
    def _():
        nc = ckv_ref.shape[1]
        cmask = _iota((rows, nc), 1) * CMP_STRIDE + (CMP_BLOCK - 1) <= pos
        p = _masked_softmax_rows(_dot_nt(q, ckv_ref[0, :, 0:128]), cmask)
        oc_scr[...] = _dot(p.astype(BF16), ckv_ref[0, :, 128:256])
        g0 = jnp.sum(p[0:NSA_HPG], axis=0, keepdims=True)
        g1 = jnp.sum(p[NSA_HPG:2 * NSA_HPG], axis=0, keepdims=True)
        p_grp = jnp.where(row1 < NSA_HPG, g0, g1)
        imp = _split_dot(p_grp, amat_ref[...], 2)
        sel_scr[...] = _select_blocks(imp, pos, n_sel)

        wk = win_ref[0]
        n_w = wk.shape[1]
        s = _dot(q, wk[0:LANES, :].astype(BF16))
        kpos = past_len - n_w + _iota((rows, n_w), 1)
        keep = jnp.where(kpos >= 0, jnp.where(pos - kpos <= WINDOW, 1.0, 0.0), 0.0) > 0.5
        s = jnp.where(keep, s, NEG)
        s_new, v_new = new_token(kvw_new_ref[0])
        m = jnp.maximum(s_new, jnp.max(s, axis=1, keepdims=True))
        e = jnp.exp2(s - m)
        e_new = jnp.exp2(s_new - m)
        o_w = _dot_nt(e.astype(BF16), wk[LANES:2 * LANES, :].astype(BF16)) + e_new * v_new
        ow_scr[...] = o_w / (jnp.sum(e, axis=1, keepdims=True) + e_new)

        shifted = pltpu.roll(wk, n_w - 1, 1)
        win_out_ref[0] = jnp.where(_iota(wk.shape, 1) == n_w - 1, kvw_col_ref[0], shifted)
        _online_init(m_scr, l_scr, acc_scr)

    sel = sel_scr[...].astype(BF16)
    z_all = _dot(q, _lanes([p[0, 0:LANES, :].astype(BF16) for p in pages]))
    ss = []
    for i in range(pp):
        chosen = _dot(sel, e_ref[j * pp + i]) > 0.5
        ss.append(jnp.where(chosen, z_all[:, i * PAGE:(i + 1) * PAGE], NEG))
    m_prev = m_scr[...]
    m_next = m_prev
    for s in ss:
        m_next = jnp.maximum(m_next, jnp.max(s, axis=1, keepdims=True))
    alpha = jnp.exp2(m_prev - m_next)
    es = [jnp.exp2(s - m_next) for s in ss]
    l_new = alpha * l_scr[...]
    for e in es:
        l_new = l_new + jnp.sum(e, axis=1, keepdims=True)
    l_scr[...] = l_new
    m_scr[...] = m_next
    acc_scr[...] = acc_scr[...] * alpha + _dot_nt(_lanes([e.astype(BF16) for e in es]),
                                                  _lanes([p[0, LANES:2 * LANES, :].astype(BF16) for p in pages]))

    @pl.when(j == pl.num_programs(1) - 1)
    def _():
        s_new, v_new = new_token(kvs_new_ref[0])
        cur = past_len // SEL_BLOCK
        chosen = jnp.sum(jnp.where(_iota(sel_scr.shape, 1) == cur, sel_scr[...], 0.0), axis=1, keepdims=True)
        s_new = jnp.where(chosen > 0.5, s_new, NEG)
        m_prev = m_scr[...]
        m_next = jnp.maximum(m_prev, s_new)
        alpha = jnp.exp2(m_prev - m_next)
        e_new = jnp.exp2(s_new - m_next)
        o_s = (acc_scr[...] * alpha + e_new * v_new) / jnp.maximum(alpha * l_scr[...] + e_new, 1e-30)
        gates = jnp.where(row < NSA_HPG, gate_ref[0, 0:1, :], gate_ref[0, 1:2, :])
        hh = row % NSA_HPG

        def gate(c):
            return jnp.sum(jnp.where(lane == 3 * hh + c, gates, 0.0), axis=1, keepdims=True)

        out_ref[0] = gate(0) * oc_scr[...] + gate(1) * o_s + gate(2) * ow_scr[...]


def _nsa_dec(q8, ckv, gates, win_state, kvs_new, kvw_new, pool, page_table, *, pp, layer):
    bd, n_pages = page_table.shape
    past_len = n_pages * PAGE
    n_sel = -(-(past_len + 1) // SEL_BLOCK)
    nc = ckv.shape[1]
    wid = -(-n_sel // LANES) * LANES
    amat = _imp_matrix(nc, wid)
    etile = _expand_tiles(n_pages, wid, PAGE)
    n_w = win_state.shape[2]
    req = lambda shape: pl.BlockSpec((1,) + shape, lambda b, j, pt: (b,) + (0,) * len(shape))
    const = lambda a: pl.BlockSpec(a.shape, lambda b, j, pt: (0,) * a.ndim)
    win_spec = pl.BlockSpec((1, KV_W, n_w), lambda b, j, pt: (layer * bd + b, 0, 0))
    grid_spec = pltpu.PrefetchScalarGridSpec(
        num_scalar_prefetch=1, grid=(bd, n_pages // pp),
        in_specs=[req((8, LANES)), req((nc, KV_W)), req((NSA_GROUPS, LANES)), win_spec,
                  req((1, KV_W)), req((1, KV_W)), req((KV_W, 1)), const(amat), const(etile)]
        + _page_specs(KV_W, n_pages, pp, False),
        out_specs=(req((8, LANES)), req((KV_W, n_w))),
        scratch_shapes=[pltpu.VMEM((8, LANES), F32)] * 3 + [pltpu.VMEM((8, wid), F32)] + [pltpu.VMEM((8, LANES), F32)] * 2)
    return pl.pallas_call(
        functools.partial(_nsa_dec_kernel, pp=pp, past_len=past_len, n_sel=n_sel), grid_spec=grid_spec,
        out_shape=(jax.ShapeDtypeStruct((bd, 8, LANES), F32), jax.ShapeDtypeStruct((bd, KV_W, n_w), F32)),
        compiler_params=_cp("arbitrary", "arbitrary"), name="nsa_decode",
    )(page_table, q8, ckv, gates, win_state, kvs_new, kvw_new, kvw_new.reshape(bd, KV_W, 1), amat, etile,
      *([pool] * pp))


def _rope_table(pos):
    half = ROT_DIM // 2
    inv = ROPE_THETA ** (-jnp.arange(half, dtype=F32) / half)
    ang = pos.astype(F32)[:, None] * inv[None, :]
    cos, sin = jnp.cos(ang), jnp.sin(ang)
    n = pos.shape[0]
    pad = HEAD_DIM - ROT_DIM
    one_head = lambda a, b, fill: jnp.concatenate([a, b, jnp.full((n, pad), fill, F32)], axis=1)
    zero = jnp.zeros_like(sin)
    tabs = [one_head(cos, cos, 1.0), one_head(-sin, zero, 0.0), one_head(zero, sin, 0.0)]
    return jnp.concatenate([jnp.tile(t, (1, LANES // HEAD_DIM)) for t in tabs], axis=1)


_Q_HEAD_ORDER = (0, 4, 1, 5, 2, 6, 3, 7)


def _proj_columns():
    q = np.concatenate([np.arange(HEAD_DIM) + HEAD_DIM * h for h in _Q_HEAD_ORDER])
    kv = np.arange(NSA_W, NSA_W + 3 * KV_W)
    gate0 = NSA_W + 3 * KV_W
    sbfox = np.arange(gate0 + GATE_W, gate0 + GATE_W + 6 * SB_W)
    gate = np.arange(gate0, gate0 + GATE_W)
    pf0 = gate0 + GATE_W + 6 * SB_W
    pf = np.arange(pf0, pf0 + FOX_HEADS)
    return np.concatenate([q, kv, sbfox, gate, pf]), pf0 + FOX_HEADS


def _layer_weights(l, w_in, b_forget, cmp_pos, w_cmp_k, w_cmp_v, w_up_nsa):
    cols, merge0 = _proj_columns()
    d = w_in.shape[1]
    w_proj = jnp.take(w_in[l], jnp.asarray(cols), axis=1)
    w_proj = jnp.concatenate([w_proj, jnp.zeros((d, N_PROJ - w_proj.shape[1]), F32)], axis=1).astype(BF16)
    w_merge = w_in[l][:, merge0:].astype(BF16)
    bias = jnp.zeros((1, LANES), F32).at[0, LOGF_LANE:LOGF_LANE + FOX_HEADS].set(b_forget[l])
    wk = w_cmp_k[l].reshape(CMP_BLOCK, HEAD_DIM, HEAD_DIM)
    wv = w_cmp_v[l].reshape(CMP_BLOCK, HEAD_DIM, HEAD_DIM)
    w_blk = jnp.zeros((CMP_BLOCK, KV_W, KV_W), F32)
    for n, wsrc in enumerate((wk, wk, wv, wv)):
        w_blk = w_blk.at[:, HEAD_DIM * n:HEAD_DIM * (n + 1), HEAD_DIM * n:HEAD_DIM * (n + 1)].set(wsrc)
    pe4 = jnp.tile(cmp_pos[l], (1, KV_W // HEAD_DIM))
    rows = np.concatenate([np.arange(HEAD_DIM) + HEAD_DIM * h for h in _Q_HEAD_ORDER])
    w_a = jnp.take(w_up_nsa[l], jnp.asarray(rows), axis=0).astype(BF16)
    w_cat = jnp.concatenate([w_blk[0:CMP_BLOCK // 2], w_blk[CMP_BLOCK // 2:]], axis=2).astype(BF16)
    return w_proj, w_merge, bias, w_cat, pe4, w_a


def _pad_heads(a):
    return jnp.concatenate([a, jnp.zeros((a.shape[0], 8 - a.shape[1], a.shape[2]), a.dtype)], axis=1)


def _token_lanes(cache):
    l, p, t = cache.shape[:3]
    return cache.transpose(0, 1, 3, 4, 5, 2).reshape(l * p, -1, t)


def _own_lanes(o8, n_heads):
    n = o8.shape[0]
    o = o8[:, 0:n_heads].reshape(n, n_heads, n_heads, HEAD_DIM)
    return jnp.stack([o[:, h, h] for h in range(n_heads)], axis=1).reshape(n, n_heads * HEAD_DIM)


def kernel(x_prompt, x_sample, cache_nsa_cmp_kv, cache_nsa_sel_kv, cache_sb_kv, cache_fox_kv, cache_fox_logf,
           state_nsa_win_kv, page_table, norm_mix_g, norm_ffn_g, norm_final_g, w_in, b_forget, cmp_pos,
           w_cmp_k, w_cmp_v, w_up_nsa, w_up_sb, w_up_fox, w_out, w_ffn_gate, w_ffn_up, w_ffn_down):
    batch, seq, d = x_prompt.shape
    bd = x_sample.shape[0]
    depth = w_in.shape[0]
    n_pool = cache_sb_kv.shape[1]
    n_pages = page_table.shape[1]
    past_len = n_pages * PAGE
    m = batch * seq
    tm = min(256, seq)
    tq = min(256, seq)
    pp = min(PAGES_PER_STEP, n_pages)
    pp_prompt = min(PAGES_PER_STEP, seq // PAGE)

    rope_p = _rope_table(jnp.arange(seq, dtype=I32))
    rope_s = _rope_table(jnp.full((bd,), past_len, I32))
    ident_pt = jnp.arange(batch * (seq // PAGE), dtype=I32).reshape(batch, seq // PAGE)
    pool_cmp, pool_sel = _token_lanes(cache_nsa_cmp_kv), _token_lanes(cache_nsa_sel_kv)
    pool_sb, pool_fox = _token_lanes(cache_sb_kv), _token_lanes(cache_fox_kv)
    pool_lf = cache_fox_logf.astype(F32).transpose(0, 1, 3, 2).reshape(depth * n_pool, FOX_HEADS, PAGE)
    win_state = _token_lanes(state_nsa_win_kv)

    xp = x_prompt.reshape(m, d)
    xs = x_sample.reshape(bd, d)
    p_states, s_states = [], []
    for l in range(depth):
        w_proj, w_merge, bias, w_blk, pe4, w_a = _layer_weights(l, w_in, b_forget, cmp_pos, w_cmp_k, w_cmp_v, w_up_nsa)
        w_b, w_c, w_o = w_up_sb[l].astype(BF16), w_up_fox[l].astype(BF16), w_out[l].astype(BF16)
        w_g, w_u, w_d = w_ffn_gate[l].astype(BF16), w_ffn_up[l].astype(BF16), w_ffn_down[l].astype(BF16)
        g_mix, g_ffn = norm_mix_g[l].reshape(1, d), norm_ffn_g[l].reshape(1, d)

        (qn, kv_cmp, kv_sel, kv_win, kv_sb, kv_fox, kb_sel, kb_win, kb_sb, kb_fox, q_sb, q_fox, gates, logf, cum
         ) = _proj(xp, g_mix, w_proj, rope_p, bias, tm=tm, tiles_per_seq=seq // tm)
        ckv = _compress(kv_cmp.reshape(m // PAGE, PAGE, KV_W), ident_pt, w_blk, pe4, pp=pp_prompt, token_lanes=False)
        o_nsa = _nsa_prompt(qn, ckv, kb_sel, kb_win, gates, batch=batch, seq=seq, tq=tq)
        o_sb = _sb_prompt(q_sb, kb_sb, batch=batch, seq=seq, tq=tq)
        cum4 = cum[:, LOGF_LANE:LOGF_LANE + FOX_HEADS]
        cum_q = cum4.T[:, :, None]
        cum_k = cum4.reshape(batch, seq, FOX_HEADS).transpose(0, 2, 1).reshape(batch, FOX_HEADS, seq // tq, 1, tq)
        o_fox = _fox_prompt(q_fox, kb_fox, cum_q, cum_k, batch=batch, seq=seq, tq=tq)
        xp = _merge(xp, g_mix, o_nsa, o_sb, o_fox, w_merge, w_a, w_b, w_c, w_o, tm=tm)
        xp = _ffn(xp, g_ffn, w_g, w_u, w_d, tm=tm)
        win_keep = min(WINDOW, seq)
        p_states.append((
            kv_cmp.reshape(batch, seq, 2, NSA_GROUPS, HEAD_DIM), kv_sel.reshape(batch, seq, 2, NSA_GROUPS, HEAD_DIM),
            kv_sb.reshape(batch, seq, 2, SB_HEADS, HEAD_DIM), kv_fox.reshape(batch, seq, 2, FOX_HEADS, HEAD_DIM),
            logf[:, LOGF_LANE:LOGF_LANE + FOX_HEADS].reshape(batch, seq, FOX_HEADS),
            kv_win.reshape(batch, seq, 2, NSA_GROUPS, HEAD_DIM)[:, seq - win_keep:]))

        (qn, kv_cmp, kv_sel, kv_win, kv_sb, kv_fox, _, _, _, _, q_sb, q_fox, gates, logf, _
         ) = _proj(xs, g_mix, w_proj, rope_s, bias, tm=bd, tiles_per_seq=1)
        pt_l = page_table + l * n_pool
        ckv = _compress(pool_cmp, pt_l, w_blk, pe4, pp=pp, token_lanes=True)
        q8 = qn.transpose(2, 0, 1, 3).reshape(bd, NSA_HEADS, LANES)
        o_nsa8, win_new = _nsa_dec(q8, ckv, gates.transpose(1, 0, 2), win_state, kv_sel.reshape(bd, 1, KV_W),
                                   kv_win.reshape(bd, 1, KV_W), pool_sel, pt_l, pp=pp, layer=l)
        o_sb8 = _sb_dec(_pad_heads(q_sb.transpose(1, 0, 2)), pool_sb, pt_l, pp=pp)
        lf4 = logf[:, LOGF_LANE:LOGF_LANE + FOX_HEADS]
        lf_new = _pad_heads(jnp.broadcast_to(lf4[:, :, None], (bd, FOX_HEADS, LANES)))
        o_fox8 = _fox_dec(_pad_heads(q_fox.transpose(1, 0, 2)), kv_fox[:, 0:SB_W].reshape(bd, 1, SB_W),
                          kv_fox[:, SB_W:2 * SB_W].reshape(bd, 1, SB_W), lf_new, pool_fox, pool_lf, pt_l, pp=pp)
        o_a = o_nsa8.reshape(bd, NSA_GROUPS, NSA_HPG, NSA_GROUPS, HEAD_DIM)
        o_a = jnp.stack([o_a[:, g, :, g] for g in range(NSA_GROUPS)], axis=2).reshape(bd, NSA_W)
        o_a = jnp.stack([o_a, jnp.zeros_like(o_a)], axis=0).astype(BF16)
        o_b = _own_lanes(o_sb8, SB_HEADS).astype(BF16)
        o_c = _own_lanes(o_fox8, FOX_HEADS).astype(BF16)
        xs = _merge(xs, g_mix, o_a, o_b, o_c, w_merge, w_a, w_b, w_c, w_o, tm=bd)
        xs = _ffn(xs, g_ffn, w_g, w_u, w_d, tm=bd)
        s_states.append((
            kv_cmp.reshape(bd, 1, 2, NSA_GROUPS, HEAD_DIM), kv_sel.reshape(bd, 1, 2, NSA_GROUPS, HEAD_DIM),
            kv_sb.reshape(bd, 1, 2, SB_HEADS, HEAD_DIM), kv_fox.reshape(bd, 1, 2, FOX_HEADS, HEAD_DIM),
            lf4.reshape(bd, 1, FOX_HEADS),
            win_new.reshape(bd, 2, NSA_GROUPS, HEAD_DIM, -1).transpose(0, 4, 1, 2, 3)))

    g_fin = norm_final_g.reshape(1, d)
    y_prompt = _final_norm(xp, g_fin, tm=tm).reshape(batch, seq, d)
    y_sample = _final_norm(xs, g_fin, tm=bd).reshape(bd, 1, d)
    stk = lambda sts, i: jnp.stack([st[i] for st in sts], axis=0)
    return (y_prompt, y_sample) + tuple(stk(p_states, i) for i in range(6)) + tuple(stk(s_states, i) for i in range(6))
```

```python
import functools

import numpy as np
import jax
import jax.numpy as jnp
from jax import lax
from jax.experimental import pallas as pl
from jax.experimental.pallas import tpu as pltpu

F32, BF16, I32 = jnp.float32, jnp.bfloat16, jnp.int32

HEAD_DIM = 64
ROT_DIM = HEAD_DIM // 4
ROPE_THETA = 500000.0
NSA_HEADS = 8
NSA_GROUPS = 2
NSA_HPG = NSA_HEADS // NSA_GROUPS
CMP_STRIDE = 16
CMP_BLOCK = 32
SEL_BLOCK = 64
TOP_N = 16
WINDOW = 512
SB_HEADS = 4
FOX_HEADS = 4
PAGE = 128
SEL_FORCE_SCORE = 1.0e4
NEG = -1.0e30
RMS_EPS = 1e-6
SCALE = HEAD_DIM ** -0.5
LOG2E = 1.4426950408889634

LANES = 128
NSA_W = NSA_HEADS * HEAD_DIM
KV_W = 2 * NSA_GROUPS * HEAD_DIM
SB_W = SB_HEADS * HEAD_DIM
GATE_W = 3 * NSA_HEADS
C_Q, C_CMP, C_SEL, C_WIN, C_SB, C_FOX, C_MISC = 0, 512, 768, 1024, 1280, 2048, 2816
N_PROJ = C_MISC + LANES
LOGF_LANE = GATE_W
VMEM_LIMIT = 56 * 1024 * 1024
PAGES_PER_STEP = 8


def _cp(*sem):
    return pltpu.CompilerParams(dimension_semantics=sem, vmem_limit_bytes=VMEM_LIMIT)


def _dot(a, b):
    return jnp.dot(a, b, preferred_element_type=F32)


def _dot_nt(a, b):
    return lax.dot_general(a, b, (((1,), (1,)), ((), ())), preferred_element_type=F32)


def _split_dot(a, r, parts):
    out, rem = None, a
    for _ in range(parts):
        hi = rem.astype(BF16)
        d = _dot(hi, r)
        out = d if out is None else out + d
        rem = rem - hi.astype(F32)
    return out


def _split_dot_l(l, a, parts):
    out, rem = None, a
    for _ in range(parts):
        hi = rem.astype(BF16)
        d = _dot(l, hi)
        out = d if out is None else out + d
        rem = rem - hi.astype(F32)
    return out


def _rms(x, g):
    return x * lax.rsqrt(jnp.mean(x * x, axis=-1, keepdims=True) + RMS_EPS) * g


def _softplus2(z2):
    return jnp.maximum(z2, 0.0) + jnp.log2(1.0 + jnp.exp2(-jnp.abs(z2)))


def _iota(shape, axis):
    return lax.broadcasted_iota(I32, shape, axis)


def _proj_kernel(x_ref, g_ref, w_ref, rope_ref, b_ref, tri_ref,
                 qn_ref, kvc_ref, kvs_ref, kvw_ref, kvsb_ref, kvfx_ref,
                 kbs_ref, kbw_ref, kbsb_ref, kbfx_ref, qsb_ref, qfx_ref,
                 gate_ref, logf_ref, cum_ref, h_scr, carry_scr, *, tiles_per_seq, feature_states):
    i = pl.program_id(0)
    tm = x_ref.shape[0]
    h_scr[...] = _rms(x_ref[...], g_ref[...]).astype(BF16)
    hb = h_scr[...]
    cos, sin_lo, sin_hi = rope_ref[:, 0:128], rope_ref[:, 128:256], rope_ref[:, 256:384]

    def rope(seg):
        return seg * cos + pltpu.roll(seg, LANES - 8, 1) * sin_lo + pltpu.roll(seg, 8, 1) * sin_hi

    def put_state(f_ref, blocks):
        for n, blk in enumerate(blocks):
            if feature_states:
                f_ref[0, LANES * n:LANES * (n + 1), :] = blk.T
            else:
                f_ref[:, LANES * n:LANES * (n + 1)] = blk

    lane = _iota((tm, LANES), 1)
    pq = _dot(hb, w_ref[:, C_Q:C_Q + NSA_W])
    for s in range(NSA_HPG):
        seg = rope(pq[:, LANES * s:LANES * (s + 1)]) * (SCALE * LOG2E)
        for g in range(NSA_GROUPS):
            qn_ref[g, s] = jnp.where(lane // HEAD_DIM == g, seg, 0.0).astype(BF16)

    for c0, f_ref, h_ref in ((C_CMP, kvc_ref, None), (C_SEL, kvs_ref, kbs_ref), (C_WIN, kvw_ref, kbw_ref)):
        p = _dot(hb, w_ref[:, c0:c0 + KV_W])
        k, v = rope(p[:, 0:128]), p[:, 128:256]
        put_state(f_ref, [k, v])
        if h_ref is not None:
            h_ref[:, 0:128] = k.astype(BF16)
            h_ref[:, 128:256] = v.astype(BF16)

    lane_w = _iota((tm, SB_W), 1)
    for c0, q_ref, f_ref, h_ref in ((C_SB, qsb_ref, kvsb_ref, kbsb_ref), (C_FOX, qfx_ref, kvfx_ref, kbfx_ref)):
        p = _dot(hb, w_ref[:, c0:c0 + 3 * SB_W])
        q = p[:, 0:SB_W] * (SCALE * LOG2E)
        for h in range(SB_HEADS):
            q_ref[h] = jnp.where(lane_w // HEAD_DIM == h, q, 0.0).astype(BF16)
        kv = p[:, SB_W:3 * SB_W]
        put_state(f_ref, [kv[:, LANES * n:LANES * (n + 1)] for n in range(2 * SB_W // LANES)])
        h_ref[...] = kv.astype(BF16)

    pm = _dot(hb, w_ref[:, C_MISC:C_MISC + LANES])
    sg = jax.nn.sigmoid(pm)
    gate_ref[0] = sg
    gate_ref[1] = pltpu.roll(sg, LANES - GATE_W // 2, 1)
    zf = pm + b_ref[...]
    lf = jnp.minimum(zf, 0.0) - jnp.log1p(jnp.exp(-jnp.abs(zf)))
    if feature_states:
        logf_ref[0] = lf.T[LOGF_LANE:LOGF_LANE + FOX_HEADS, :]
    else:
        logf_ref[...] = lf

    @pl.when(i % tiles_per_seq == 0)
    def _():
        carry_scr[...] = jnp.zeros_like(carry_scr)

    c = _split_dot_l(tri_ref[...], lf, 3) + carry_scr[0:1, :]
    cum_ref[...] = c
    carry_scr[...] = jnp.broadcast_to(c[tm - 1:tm, :], carry_scr.shape)


def _proj(x, gain, w, rope_tab, bias, *, tm, tiles_per_seq, feature_states):
    m, d = x.shape
    tri = jnp.tril(jnp.ones((tm, tm), F32)).astype(BF16)
    row = lambda wid: pl.BlockSpec((tm, wid), lambda i: (i, 0))
    const = lambda shape: pl.BlockSpec(shape, lambda i: (0,) * len(shape))
    f32o = lambda wid: jax.ShapeDtypeStruct((m, wid), F32)
    b16o = lambda wid: jax.ShapeDtypeStruct((m, wid), BF16)
    if feature_states:
        nb, seq = m // (tm * tiles_per_seq), tm * tiles_per_seq
        st_shape = lambda wid: jax.ShapeDtypeStruct((nb, wid, seq), F32)
        st_spec = lambda wid: pl.BlockSpec((1, wid, tm), lambda i: (i // tiles_per_seq, 0, i % tiles_per_seq))
        lf_shape, lf_spec = st_shape(FOX_HEADS), st_spec(FOX_HEADS)
    else:
        st_shape, st_spec, lf_shape, lf_spec = f32o, row, f32o(LANES), row(LANES)
    out_shape = (
        jax.ShapeDtypeStruct((NSA_GROUPS, NSA_HPG, m, LANES), BF16),
        st_shape(KV_W), st_shape(KV_W), st_shape(KV_W), st_shape(2 * SB_W), st_shape(2 * SB_W),
        b16o(KV_W), b16o(KV_W), b16o(2 * SB_W), b16o(2 * SB_W),
        jax.ShapeDtypeStruct((SB_HEADS, m, SB_W), BF16), jax.ShapeDtypeStruct((SB_HEADS, m, SB_W), BF16),
        jax.ShapeDtypeStruct((NSA_GROUPS, m, LANES), F32), lf_shape, f32o(LANES),
    )
    out_specs = (
        pl.BlockSpec((NSA_GROUPS, NSA_HPG, tm, LANES), lambda i: (0, 0, i, 0)),
        st_spec(KV_W), st_spec(KV_W), st_spec(KV_W), st_spec(2 * SB_W), st_spec(2 * SB_W),
        row(KV_W), row(KV_W), row(2 * SB_W), row(2 * SB_W),
        pl.BlockSpec((SB_HEADS, tm, SB_W), lambda i: (0, i, 0)),
        pl.BlockSpec((SB_HEADS, tm, SB_W), lambda i: (0, i, 0)),
        pl.BlockSpec((NSA_GROUPS, tm, LANES), lambda i: (0, i, 0)),
        lf_spec, row(LANES),
    )
    return pl.pallas_call(
        functools.partial(_proj_kernel, tiles_per_seq=tiles_per_seq, feature_states=feature_states),
        grid=(m // tm,),
        in_specs=[row(d), const((1, d)), const((d, N_PROJ)),
                  pl.BlockSpec((tm, 3 * LANES), lambda i: (i % tiles_per_seq, 0)),
                  const((1, LANES)), const((tm, tm))],
        out_specs=out_specs, out_shape=out_shape,
        scratch_shapes=[pltpu.VMEM((tm, d), BF16), pltpu.VMEM((8, LANES), F32)],
        compiler_params=_cp("arbitrary"), name="proj",
    )(x, gain, w, rope_tab, bias, tri)


def _compress_kernel(pt_ref, *refs, pp, nch):
    pages, (perm_ref, w_ref, pe_ref, out_ref, x_scr, acc_scr, pos_scr) = refs[:pp], refs[pp:]
    j = pl.program_id(1)
    half = CMP_BLOCK // 2
    chunks = PAGE // CMP_STRIDE

    @pl.when((pl.program_id(0) == 0) & (j == 0))
    def _():
        c = jnp.zeros((8, KV_W), F32)
        for l in range(half):
            for off, cols in ((0, slice(0, KV_W)), (half, slice(KV_W, 2 * KV_W))):
                row = jnp.broadcast_to(pe_ref[off + l:off + l + 1, :], (8, KV_W))
                c = c + _split_dot(row, w_ref[l, :, cols], 2)
        pos_scr[...] = c

    for i in range(pp):
        gt = _dot_nt(perm_ref[...], pages[i][0].astype(BF16))
        base = pl.multiple_of((j * pp + i) * chunks, chunks)
        for l in range(half):
            x_scr[l, pl.ds(base, chunks), :] = gt[chunks * l:chunks * (l + 1), :]

    @pl.when(j == pl.num_programs(1) - 1)
    def _():
        acc_scr[...] = jnp.zeros_like(acc_scr)
        for l in range(half):
            acc_scr[...] += _dot(x_scr[l].astype(BF16), w_ref[l])
        nxt = pltpu.roll(acc_scr[:, KV_W:2 * KV_W], nch - 1, 0)
        nxt = jnp.where(_iota((nch, KV_W), 0) < nch - 1, nxt, 0.0)
        out_ref[0] = (acc_scr[:, 0:KV_W] + nxt + pos_scr[0:1, :]).astype(BF16)


def _compress(pool, page_table, w_cat, pe4, *, pp, paged):
    b, n_pages = page_table.shape
    chunks = PAGE // CMP_STRIDE
    nch = n_pages * chunks
    half = CMP_BLOCK // 2
    tok = np.arange(PAGE)
    perm = jnp.asarray(np.arange(PAGE)[:, None] == ((tok % CMP_STRIDE) * chunks + tok // CMP_STRIDE)[None, :], BF16)
    if paged:
        page_spec = lambda i: pl.BlockSpec((1, KV_W, PAGE), lambda bb, j, pt: (pt[bb, j * pp + i], 0, 0))
    else:
        page_spec = lambda i: pl.BlockSpec((1, KV_W, PAGE), lambda bb, j, pt: (bb, 0, j * pp + i))
    grid_spec = pltpu.PrefetchScalarGridSpec(
        num_scalar_prefetch=1, grid=(b, n_pages // pp),
        in_specs=[page_spec(i) for i in range(pp)] + [
            pl.BlockSpec((PAGE, PAGE), lambda bb, j, pt: (0, 0)),
            pl.BlockSpec((half, KV_W, 2 * KV_W), lambda bb, j, pt: (0, 0, 0)),
            pl.BlockSpec((CMP_BLOCK, KV_W), lambda bb, j, pt: (0, 0))],
        out_specs=pl.BlockSpec((1, nch, KV_W), lambda bb, j, pt: (bb, 0, 0)),
        scratch_shapes=[pltpu.VMEM((half, nch, KV_W), F32), pltpu.VMEM((nch, 2 * KV_W), F32),
                        pltpu.VMEM((8, KV_W), F32)])
    return pl.pallas_call(
        functools.partial(_compress_kernel, pp=pp, nch=nch), grid_spec=grid_spec,
        out_shape=jax.ShapeDtypeStruct((b, nch, KV_W), BF16),
        compiler_params=_cp("arbitrary", "arbitrary"), name="compress",
    )(page_table, *([pool] * pp), perm, w_cat, pe4)


def _masked_softmax_rows(s, mask):
    s = jnp.where(mask, s, NEG)
    e = jnp.where(mask, jnp.exp2(s - jnp.max(s, axis=1, keepdims=True)), 0.0)
    return e / jnp.maximum(jnp.sum(e, axis=1, keepdims=True), 1e-30)


def _online_init(m_scr, l_scr, acc_scr):
    m_scr[...] = jnp.full_like(m_scr, NEG)
    l_scr[...] = jnp.zeros_like(l_scr)
    acc_scr[...] = jnp.zeros_like(acc_scr)


def _select_blocks(imp, pos, n_sel):
    blk = _iota(imp.shape, 1)
    cur = pos // SEL_BLOCK
    valid = blk * SEL_BLOCK <= pos
    forced = jnp.where(blk == 0, 1.0, 0.0) + jnp.where(blk == cur, 1.0, 0.0) + jnp.where(blk == cur - 1, 1.0, 0.0)
    score = jnp.where(valid, jnp.where(forced > 0.5, SEL_FORCE_SCORE, imp), -1.0)
    score = jnp.where(blk < n_sel, score, -2.0)
    rank = jnp.zeros(imp.shape, F32)
    for b2 in range(n_sel):
        col = score[:, b2:b2 + 1]
        ge = jnp.where(col >= score, 1.0, 0.0)
        gt = jnp.where(col > score, 1.0, 0.0)
        rank = rank + jnp.where(blk > b2, ge, gt)
    return jnp.where(rank < min(TOP_N, n_sel), 1.0, 0.0)


def _rep(x, n):
    return x if n == 1 else jnp.concatenate([x] * n, axis=1)


def _select_blocks_t(p_grp, amat_t, q0, n_sel):
    t = p_grp.shape[0]
    nb = amat_t.shape[0]
    hi = p_grp.astype(BF16)
    lo = (p_grp - hi.astype(F32)).astype(BF16)
    imp = _dot_nt(amat_t, hi) + _dot_nt(amat_t, lo)
    pos = q0 + _iota((1, t), 1)
    blk = _iota((nb, t), 0)
    cur = pos // SEL_BLOCK
    forced = jnp.where(blk == 0, 1.0, 0.0) + jnp.where(blk == cur, 1.0, 0.0) + jnp.where(blk == cur - 1, 1.0, 0.0)
    score = jnp.where(blk * SEL_BLOCK <= pos, jnp.where(forced > 0.5, SEL_FORCE_SCORE, imp), -1.0)
    score = jnp.where(blk < n_sel, score, -2.0)
    groups = [score[8 * k:8 * (k + 1)] for k in range(nb // 8)]
    ranks = [jnp.zeros((8, t), F32) for _ in groups]
    sub = _iota((8, t), 0)
    for b2 in range(n_sel):
        rowv = score[b2:b2 + 1, :]
        k2, r2 = divmod(b2, 8)
        for k, grp in enumerate(groups):
            if k > k2:
                inc = jnp.where(rowv >= grp, 1.0, 0.0)
            elif k < k2:
                inc = jnp.where(rowv > grp, 1.0, 0.0)
            else:
                inc = jnp.where(sub > r2, jnp.where(rowv >= grp, 1.0, 0.0), jnp.where(rowv > grp, 1.0, 0.0))
            ranks[k] = ranks[k] + inc
    chosen = [jnp.where(rk < min(TOP_N, n_sel), 1.0, 0.0) for rk in ranks]
    pad = [jnp.zeros((LANES - nb, t), F32)] if nb < LANES else []
    return jnp.concatenate(chosen + pad, axis=0)


def _nsa_prompt_kernel(qn_ref, ckv_ref, ks_ref, kw_ref, gate_ref, amat_ref, e_ref, out_ref,
                       m_scr, l_scr, acc_scr, oc_scr, os_scr, *, n_sel):
    g, i = pl.program_id(1), pl.program_id(2)
    t = qn_ref.shape[2]
    q0 = i * t
    rowi, coli = _iota((t, t), 0), _iota((t, t), 1)
    reps = t // LANES

    nc = ckv_ref.shape[1]
    cmask = _iota((t, nc), 1) * CMP_STRIDE + (CMP_BLOCK - 1) <= q0 + _iota((t, 1), 0)
    ck, cv = ckv_ref[0, :, 0:128], ckv_ref[0, :, 128:256]
    p_grp = None
    for h in range(NSA_HPG):
        p = _masked_softmax_rows(_dot_nt(qn_ref[0, h], ck), cmask)
        oc_scr[h] = _dot(p.astype(BF16), cv)
        p_grp = p if p_grp is None else p_grp + p
    sel = _select_blocks_t(p_grp, amat_ref[...], q0, n_sel).T.astype(BF16)

    def online_head(h, s, v):
        m_prev = m_scr[h]
        m_next = jnp.maximum(m_prev, jnp.max(s, axis=1, keepdims=True))
        alpha = jnp.exp2(m_prev - m_next)
        e = jnp.exp2(s - _rep(m_next, reps))
        l_scr[h] = alpha * l_scr[h] + jnp.sum(e, axis=1, keepdims=True)
        m_scr[h] = m_next
        acc_scr[h] = acc_scr[h] * alpha + _dot(e.astype(BF16), v)

    def reset():
        m_scr[...] = jnp.full_like(m_scr, NEG)
        l_scr[...] = jnp.zeros_like(l_scr)
        acc_scr[...] = jnp.zeros_like(acc_scr)

    def sel_tile(j, diag):
        kt = ks_ref[0, pl.ds(pl.multiple_of(j * t, t), t), :]
        chosen = _dot(sel, e_ref[j])
        if diag:
            chosen = jnp.where(coli <= rowi, chosen, 0.0)
        keep = chosen > 0.5
        for h in range(NSA_HPG):
            online_head(h, jnp.where(keep, _dot_nt(qn_ref[0, h], kt[:, 0:128]), NEG), kt[:, 128:256])

    reset()
    lax.fori_loop(0, i, lambda j, c: (sel_tile(j, False), c)[1], 0)
    sel_tile(i, True)
    for h in range(NSA_HPG):
        os_scr[h] = acc_scr[h] / jnp.maximum(l_scr[h], 1e-30)

    w_len = WINDOW + t
    k0 = jnp.maximum(i - WINDOW // t, 0) * t
    kw = kw_ref[0, pl.ds(pl.multiple_of(k0, t), w_len), :]
    d = (q0 - k0) + _iota((t, w_len), 0) - _iota((t, w_len), 1)
    keep_w = jnp.where(d >= 0, jnp.where(d <= WINDOW, 1.0, 0.0), 0.0) > 0.5

    gates = gate_ref[0]
    lane = _iota((t, LANES), 1)
    for h in range(NSA_HPG):
        s = jnp.where(keep_w, _dot_nt(qn_ref[0, h], kw[:, 0:128]), NEG)
        e = jnp.exp2(s - jnp.max(s, axis=1, keepdims=True))
        o_w = _dot(e.astype(BF16), kw[:, 128:256]) / jnp.sum(e, axis=1, keepdims=True)
        o = (gates[:, 3 * h:3 * h + 1] * oc_scr[h] + gates[:, 3 * h + 1:3 * h + 2] * os_scr[h]
             + gates[:, 3 * h + 2:3 * h + 3] * o_w)
        out_ref[0, :, LANES * h:LANES * (h + 1)] = jnp.where(lane // HEAD_DIM == g, o, 0.0).astype(BF16)


def _imp_matrix(nc, width):
    n = np.arange(nc)[:, None]
    b = np.arange(width)[None, :]
    sub = SEL_BLOCK // CMP_STRIDE
    return jnp.asarray((n // sub == b).astype(np.float32) + ((n + 1) // sub == b), BF16)


def _expand_tiles(n_tiles, n_blk, tk):
    key = np.arange(n_tiles * tk).reshape(n_tiles, 1, tk)
    return jnp.asarray(key // SEL_BLOCK == np.arange(n_blk)[None, :, None], BF16)


def _nsa_prompt(qn, ckv, kb_sel, kb_win, gates, *, batch, seq, tq):
    nq = seq // tq
    nc = ckv.shape[1]
    n_sel = seq // SEL_BLOCK
    assert n_sel <= LANES and WINDOW % tq == 0 and seq >= WINDOW + tq
    amat = _imp_matrix(nc, -(-n_sel // 8) * 8).T
    etile = _expand_tiles(nq, LANES, tq)
    m = batch * seq
    return pl.pallas_call(
        functools.partial(_nsa_prompt_kernel, n_sel=n_sel),
        grid=(batch, NSA_GROUPS, nq),
        in_specs=[
            pl.BlockSpec((1, NSA_HPG, tq, LANES), lambda b, g, i: (g, 0, b * nq + i, 0)),
            pl.BlockSpec((1, nc, KV_W), lambda b, g, i: (b, 0, 0)),
            pl.BlockSpec((1, seq, KV_W), lambda b, g, i: (b, 0, 0)),
            pl.BlockSpec((1, seq, KV_W), lambda b, g, i: (b, 0, 0)),
            pl.BlockSpec((1, tq, LANES), lambda b, g, i: (g, b * nq + i, 0)),
            pl.BlockSpec(amat.shape, lambda b, g, i: (0, 0)),
            pl.BlockSpec(etile.shape, lambda b, g, i: (0, 0, 0)),
        ],
        out_specs=pl.BlockSpec((1, tq, NSA_W), lambda b, g, i: (g, b * nq + i, 0)),
        out_shape=jax.ShapeDtypeStruct((NSA_GROUPS, m, NSA_W), BF16),
        scratch_shapes=[pltpu.VMEM((NSA_HPG, tq, LANES), F32)] * 5,
        compiler_params=_cp("arbitrary", "arbitrary", "arbitrary"), name="nsa_prompt",
    )(qn, ckv, kb_sel.reshape(batch, seq, KV_W), kb_win.reshape(batch, seq, KV_W), gates, amat, etile)


def _upper_strict(n):
    return jnp.asarray(np.triu(np.ones((n, n), np.float32), 1).T, BF16)


def _stack_heads(v):
    lane = _iota(v.shape, 1)
    return jnp.concatenate([jnp.where(lane // HEAD_DIM == h, v, jnp.zeros_like(v)) for h in range(SB_HEADS)], axis=0)


def _head_lanes(parts):
    lane = _iota(parts[0].shape, 1)
    return jnp.concatenate([jnp.where(lane < HEAD_DIM, parts[0], parts[1]),
                            jnp.where(lane < HEAD_DIM, parts[2], parts[3])], axis=1)


def _sb_prompt_kernel(q_ref, kv_ref, u_ref, out_ref, acc_scr, r_scr):
    i = pl.program_id(1)
    t = q_ref.shape[1]
    rowi, coli = _iota((t, t), 0), _iota((t, t), 1)
    acc_scr[...] = jnp.zeros_like(acc_scr)
    r_scr[...] = jnp.zeros_like(r_scr)

    def tile(j, diag):
        kt = kv_ref[0, pl.ds(pl.multiple_of(j * t, t), t), :]
        ws = []
        for h in range(SB_HEADS):
            z = _dot_nt(q_ref[h], kt[:, 0:SB_W])
            sp = _softplus2(z)
            cost = jnp.where(coli < rowi, sp, 0.0) if diag else sp
            r = r_scr[h]
            w = jnp.exp2(z - sp - _split_dot(cost, u_ref[...], 2) - _rep(r, t // LANES))
            if diag:
                w = jnp.where(coli < rowi, w, 0.0)
            ws.append(w.astype(BF16))
            r_scr[h] = r + jnp.sum(cost, axis=1, keepdims=True)
        acc_scr[...] += _dot(jnp.concatenate(ws, axis=1), _stack_heads(kt[:, SB_W:2 * SB_W]))

    tile(i, True)
    lax.fori_loop(0, i, lambda n, c: (tile(i - 1 - n, False), c)[1], 0)
    out_ref[...] = acc_scr[...].astype(BF16)


def _sb_prompt(q, kb, *, batch, seq, tq):
    nq = seq // tq
    u = _upper_strict(tq)
    return pl.pallas_call(
        _sb_prompt_kernel, grid=(batch, nq),
        in_specs=[pl.BlockSpec((SB_HEADS, tq, SB_W), lambda b, i: (0, b * nq + i, 0)),
                  pl.BlockSpec((1, seq, 2 * SB_W), lambda b, i: (b, 0, 0)),
                  pl.BlockSpec(u.shape, lambda b, i: (0, 0))],
        out_specs=pl.BlockSpec((tq, SB_W), lambda b, i: (b * nq + i, 0)),
        out_shape=jax.ShapeDtypeStruct((batch * seq, SB_W), BF16),
        scratch_shapes=[pltpu.VMEM((tq, SB_W), F32), pltpu.VMEM((SB_HEADS, tq, LANES), F32)],
        compiler_params=_cp("arbitrary", "arbitrary"), name="sb_prompt",
    )(q, kb.reshape(batch, seq, 2 * SB_W), u)


def _fox_prompt_kernel(q_ref, kv_ref, cq_ref, ck_ref, out_ref, m_scr, l_scr, acc_scr, cq_scr):
    i = pl.program_id(1)
    t = q_ref.shape[1]
    rowi, coli = _iota((t, t), 0), _iota((t, t), 1)
    m_scr[...] = jnp.full_like(m_scr, NEG)
    l_scr[...] = jnp.zeros_like(l_scr)
    acc_scr[...] = jnp.zeros_like(acc_scr)
    for h in range(FOX_HEADS):
        cq_scr[h] = jnp.broadcast_to(cq_ref[h] * LOG2E, (t, LANES))

    def tile(j, diag):
        kt = kv_ref[0, pl.ds(pl.multiple_of(j * t, t), t), :]
        es, alphas = [], []
        for h in range(FOX_HEADS):
            s = _dot_nt(q_ref[h], kt[:, 0:SB_W]) + (_rep(cq_scr[h], t // LANES) - ck_ref[0, h, j] * LOG2E)
            if diag:
                s = jnp.where(coli <= rowi, s, NEG)
            m_prev = m_scr[h]
            m_next = jnp.maximum(m_prev, jnp.max(s, axis=1, keepdims=True))
            alpha = jnp.exp2(m_prev - m_next)
            e = jnp.exp2(s - _rep(m_next, t // LANES))
            l_scr[h] = alpha * l_scr[h] + jnp.sum(e, axis=1, keepdims=True)
            m_scr[h] = m_next
            es.append(e.astype(BF16))
            alphas.append(alpha)
        acc_scr[...] = acc_scr[...] * _head_lanes(alphas) + _dot(jnp.concatenate(es, axis=1),
                                                                 _stack_heads(kt[:, SB_W:2 * SB_W]))

    lax.fori_loop(0, i, lambda j, c: (tile(j, False), c)[1], 0)
    tile(i, True)
    out_ref[...] = (acc_scr[...] / _head_lanes([jnp.maximum(l_scr[h], 1e-30) for h in range(FOX_HEADS)])).astype(BF16)


def _fox_prompt(q, kb, cum_q, cum_k, *, batch, seq, tq):
    nq = seq // tq
    nk = seq // tq
    return pl.pallas_call(
        _fox_prompt_kernel, grid=(batch, nq),
        in_specs=[pl.BlockSpec((FOX_HEADS, tq, SB_W), lambda b, i: (0, b * nq + i, 0)),
                  pl.BlockSpec((1, seq, 2 * SB_W), lambda b, i: (b, 0, 0)),
                  pl.BlockSpec((FOX_HEADS, tq, 1), lambda b, i: (0, b * nq + i, 0)),
                  pl.BlockSpec((1, FOX_HEADS, nk, 1, tq), lambda b, i: (b, 0, 0, 0, 0))],
        out_specs=pl.BlockSpec((tq, SB_W), lambda b, i: (b * nq + i, 0)),
        out_shape=jax.ShapeDtypeStruct((batch * seq, SB_W), BF16),
        scratch_shapes=[pltpu.VMEM((FOX_HEADS, tq, LANES), F32), pltpu.VMEM((FOX_HEADS, tq, LANES), F32),
                        pltpu.VMEM((tq, SB_W), F32), pltpu.VMEM((FOX_HEADS, tq, LANES), F32)],
        compiler_params=_cp("arbitrary", "arbitrary"), name="fox_prompt",
    )(q, kb.reshape(batch, seq, 2 * SB_W), cum_q, cum_k)


def _merge_kernel(x_ref, g_ref, oa_ref, ob_ref, oc_ref, wm_ref, wa_ref, wb_ref, wc_ref, wo_ref, out_ref):
    x = x_ref[...]
    d = x.shape[1]
    hb = _rms(x, g_ref[...]).astype(BF16)
    oa = oa_ref[0] + oa_ref[1]
    u = jax.nn.sigmoid(_dot(hb, wm_ref[:, 0:d])) * _dot(oa, wa_ref[...])
    u = u + jax.nn.sigmoid(_dot(hb, wm_ref[:, d:2 * d])) * _dot(ob_ref[...], wb_ref[...])
    u = u + jax.nn.sigmoid(_dot(hb, wm_ref[:, 2 * d:3 * d])) * _dot(oc_ref[...], wc_ref[...])
    out_ref[...] = x + _dot(u.astype(BF16), wo_ref[...])


def _merge(x, gain, o_nsa, o_sb, o_fox, w_merge, w_a, w_b, w_c, w_o, *, tm):
    m, d = x.shape
    row = lambda wid: pl.BlockSpec((tm, wid), lambda i: (i, 0))
    const = lambda a: pl.BlockSpec(a.shape, lambda i: (0,) * a.ndim)
    return pl.pallas_call(
        _merge_kernel, grid=(m // tm,),
        in_specs=[row(d), const(gain), pl.BlockSpec((NSA_GROUPS, tm, NSA_W), lambda i: (0, i, 0)),
                  row(SB_W), row(SB_W), const(w_merge), const(w_a), const(w_b), const(w_c), const(w_o)],
        out_specs=row(d), out_shape=jax.ShapeDtypeStruct((m, d), F32),
        compiler_params=_cp("arbitrary"), name="merge",
    )(x, gain, o_nsa, o_sb, o_fox, w_merge, w_a, w_b, w_c, w_o)


def _ffn_kernel(x_ref, g_ref, wg_ref, wu_ref, wd_ref, out_ref):
    x = x_ref[...]
    hb = _rms(x, g_ref[...]).astype(BF16)
    a = _dot(hb, wg_ref[...])
    act = a * jax.nn.sigmoid(a) * _dot(hb, wu_ref[...])
    out_ref[...] = x + _dot(act.astype(BF16), wd_ref[...])


def _ffn(x, gain, w_gate, w_up, w_down, *, tm):
    m, d = x.shape
    row = pl.BlockSpec((tm, d), lambda i: (i, 0))
    const = lambda a: pl.BlockSpec(a.shape, lambda i: (0,) * a.ndim, pipeline_mode=pl.Buffered(1))
    return pl.pallas_call(
        _ffn_kernel, grid=(m // tm,),
        in_specs=[row, pl.BlockSpec(gain.shape, lambda i: (0, 0)), const(w_gate), const(w_up), const(w_down)],
        out_specs=row, out_shape=jax.ShapeDtypeStruct((m, d), F32),
        compiler_params=_cp("arbitrary"), name="ffn",
    )(x, gain, w_gate, w_up, w_down)


def _norm_kernel(x_ref, g_ref, out_ref):
    out_ref[...] = _rms(x_ref[...], g_ref[...])


def _final_norm(x, gain, *, tm):
    m, d = x.shape
    row = pl.BlockSpec((tm, d), lambda i: (i, 0))
    return pl.pallas_call(
        _norm_kernel, grid=(m // tm,), in_specs=[row, pl.BlockSpec((1, d), lambda i: (0, 0))],
        out_specs=row, out_shape=jax.ShapeDtypeStruct((m, d), F32),
        compiler_params=_cp("arbitrary"), name="final_norm",
    )(x, gain)


def _page_specs(rows, n_pages, pp, descending):
    def spec(i):
        if descending:
            return pl.BlockSpec((1, rows, PAGE), lambda b, j, pt: (pt[b, n_pages - 1 - (j * pp + i)], 0, 0))
        return pl.BlockSpec((1, rows, PAGE), lambda b, j, pt: (pt[b, j * pp + i], 0, 0))
    return [spec(i) for i in range(pp)]


def _lanes(parts):
    return jnp.concatenate(parts, axis=1)


def _sb_dec_kernel(pt_ref, q_ref, *refs, pp):
    pages, (u_ref, out_ref, acc_scr, r_scr) = refs[:pp], refs[pp:]
    j = pl.program_id(1)

    @pl.when(j == 0)
    def _():
        acc_scr[...] = jnp.zeros_like(acc_scr)
        r_scr[...] = jnp.zeros_like(r_scr)

    q = q_ref[0]
    z_all = _dot(q, _lanes([p[0, 0:SB_W, :].astype(BF16) for p in pages]))
    zs = [z_all[:, i * PAGE:(i + 1) * PAGE] for i in range(pp)]
    sps = [_softplus2(z) for z in zs]
    afters = [_split_dot(sp, u_ref[...], 2) for sp in sps]
    r = r_scr[...]
    ws = []
    for i in range(pp):
        ws.append(jnp.exp2(zs[i] - sps[i] - afters[i] - r).astype(BF16))
        r = r + jnp.sum(sps[i], axis=1, keepdims=True)
    acc_scr[...] += _dot_nt(_lanes(ws), _lanes([p[0, SB_W:2 * SB_W, :].astype(BF16) for p in pages]))
    r_scr[...] = r

    @pl.when(j == pl.num_programs(1) - 1)
    def _():
        out_ref[0] = acc_scr[...]


def _sb_dec(q8, pool, page_table, *, pp):
    bd, n_pages = page_table.shape
    u = _upper_strict(PAGE)
    grid_spec = pltpu.PrefetchScalarGridSpec(
        num_scalar_prefetch=1, grid=(bd, n_pages // pp),
        in_specs=[pl.BlockSpec((1, 8, SB_W), lambda b, j, pt: (b, 0, 0))]
        + _page_specs(2 * SB_W, n_pages, pp, True) + [pl.BlockSpec(u.shape, lambda b, j, pt: (0, 0))],
        out_specs=pl.BlockSpec((1, 8, SB_W), lambda b, j, pt: (b, 0, 0)),
        scratch_shapes=[pltpu.VMEM((8, SB_W), F32), pltpu.VMEM((8, LANES), F32)])
    return pl.pallas_call(
        functools.partial(_sb_dec_kernel, pp=pp), grid_spec=grid_spec,
        out_shape=jax.ShapeDtypeStruct((bd, 8, SB_W), F32),
        compiler_params=_cp("arbitrary", "arbitrary"), name="sb_decode",
    )(page_table, q8, *([pool] * pp), u)


def _fox_dec_kernel(pt_ref, q_ref, kn_ref, vn_ref, lfn_ref, *refs, pp):
    pages, lf_pages = refs[:pp], refs[pp:2 * pp]
    u_ref, out_ref, m_scr, l_scr, acc_scr, r_scr = refs[2 * pp:]
    j = pl.program_id(1)
    q = q_ref[0]
    reps = SB_W // LANES

    @pl.when(j == 0)
    def _():
        kn = kn_ref[0].astype(BF16).astype(F32)
        m_scr[...] = jnp.broadcast_to(jnp.sum(q.astype(F32) * kn, axis=1, keepdims=True), m_scr.shape)
        l_scr[...] = jnp.ones_like(l_scr)
        acc_scr[...] = jnp.broadcast_to(vn_ref[0].astype(BF16).astype(F32), acc_scr.shape)
        r_scr[...] = lfn_ref[0]

    z_all = _dot(q, _lanes([p[0, 0:SB_W, :].astype(BF16) for p in pages]))
    pad = jnp.zeros((8 - FOX_HEADS, PAGE), F32)
    r = r_scr[...]
    ss = []
    for i in range(pp):
        lf = jnp.concatenate([lf_pages[i][0], pad], axis=0)
        ss.append(z_all[:, i * PAGE:(i + 1) * PAGE] + LOG2E * (_split_dot(lf, u_ref[...], 3) + r))
        r = r + jnp.sum(lf, axis=1, keepdims=True)
    r_scr[...] = r
    m_prev = m_scr[...]
    m_next = m_prev
    for s in ss:
        m_next = jnp.maximum(m_next, jnp.max(s, axis=1, keepdims=True))
    alpha = jnp.exp2(m_prev - m_next)
    es = [jnp.exp2(s - m_next) for s in ss]
    l_new = alpha * l_scr[...]
    for e in es:
        l_new = l_new + jnp.sum(e, axis=1, keepdims=True)
    l_scr[...] = l_new
    m_scr[...] = m_next
    acc_scr[...] = acc_scr[...] * _rep(alpha, reps) + _dot_nt(
        _lanes([e.astype(BF16) for e in es]), _lanes([p[0, SB_W:2 * SB_W, :].astype(BF16) for p in pages]))

    @pl.when(j == pl.num_programs(1) - 1)
    def _():
        out_ref[0] = acc_scr[...] / _rep(l_scr[...], reps)


def _fox_dec(q8, k_new, v_new, lf_new, pool, lf_pool, page_table, *, pp):
    bd, n_pages = page_table.shape
    u = _upper_strict(PAGE)
    req = lambda shape: pl.BlockSpec((1,) + shape, lambda b, j, pt: (b,) + (0,) * len(shape))
    lf_spec = lambda i: pl.BlockSpec((1, FOX_HEADS, PAGE), lambda b, j, pt: (pt[b, n_pages - 1 - (j * pp + i)], 0, 0))
    grid_spec = pltpu.PrefetchScalarGridSpec(
        num_scalar_prefetch=1, grid=(bd, n_pages // pp),
        in_specs=[req((8, SB_W)), req((1, SB_W)), req((1, SB_W)), req((8, LANES))]
        + _page_specs(2 * SB_W, n_pages, pp, True) + [lf_spec(i) for i in range(pp)]
        + [pl.BlockSpec(u.shape, lambda b, j, pt: (0, 0))],
        out_specs=req((8, SB_W)),
        scratch_shapes=[pltpu.VMEM((8, LANES), F32), pltpu.VMEM((8, LANES), F32),
                        pltpu.VMEM((8, SB_W), F32), pltpu.VMEM((8, LANES), F32)])
    return pl.pallas_call(
        functools.partial(_fox_dec_kernel, pp=pp), grid_spec=grid_spec,
        out_shape=jax.ShapeDtypeStruct((bd, 8, SB_W), F32),
        compiler_params=_cp("arbitrary", "arbitrary"), name="fox_decode",
    )(page_table, q8, k_new, v_new, lf_new, *([pool] * pp), *([lf_pool] * pp), u)


def _nsa_dec_kernel(pt_ref, q_ref, ckv_ref, gate_ref, win_ref, kvs_new_ref, kvw_new_ref, kvw_col_ref, amat_ref, e_ref,
                    *refs, pp, past_len, n_sel):
    pages, (out_ref, win_out_ref, m_scr, l_scr, acc_scr, sel_scr, oc_scr, ow_scr) = refs[:pp], refs[pp:]
    j = pl.program_id(1)
    q = q_ref[0]
    rows = q.shape[0]
    row = _iota((rows, LANES), 0)
    row1 = _iota((rows, 1), 0)
    lane = _iota((rows, LANES), 1)
    pos = jnp.full((rows, 1), past_len, I32)

    def new_token(kv_row):
        k = kv_row[:, 0:LANES].astype(BF16).astype(F32)
        return jnp.sum(q.astype(F32) * k, axis=1, keepdims=True), kv_row[:, LANES:2 * LANES].astype(BF16).astype(F32)

    @pl.when(j == 0)
    def _():
        nc = ckv_ref.shape[1]
        cmask = _iota((rows, nc), 1) * CMP_STRIDE + (CMP_BLOCK - 1) <= pos
        p = _masked_softmax_rows(_dot_nt(q, ckv_ref[0, :, 0:128]), cmask)
        oc_scr[...] = _dot(p.astype(BF16), ckv_ref[0, :, 128:256])
        g0 = jnp.sum(p[0:NSA_HPG], axis=0, keepdims=True)
        g1 = jnp.sum(p[NSA_HPG:2 * NSA_HPG], axis=0, keepdims=True)
        p_grp = jnp.where(row1 < NSA_HPG, g0, g1)
        imp = _split_dot(p_grp, amat_ref[...], 2)
        sel_scr[...] = _select_blocks(imp, pos, n_sel)

        wk = win_ref[0]
        n_w = wk.shape[1]
        s = _dot(q, wk[0:LANES, :].astype(BF16))
        kpos = past_len - n_w + _iota((rows, n_w), 1)
        keep = jnp.where(kpos >= 0, jnp.where(pos - kpos <= WINDOW, 1.0, 0.0), 0.0) > 0.5
        s = jnp.where(keep, s, NEG)
        s_new, v_new = new_token(kvw_new_ref[0])
        m = jnp.maximum(s_new, jnp.max(s, axis=1, keepdims=True))
        e = jnp.exp2(s - m)
        e_new = jnp.exp2(s_new - m)
        o_w = _dot_nt(e.astype(BF16), wk[LANES:2 * LANES, :].astype(BF16)) + e_new * v_new
        ow_scr[...] = o_w / (jnp.sum(e, axis=1, keepdims=True) + e_new)

        shifted = pltpu.roll(wk, n_w - 1, 1)
        win_out_ref[0] = jnp.where(_iota(wk.shape, 1) == n_w - 1, kvw_col_ref[0], shifted)
        _online_init(m_scr, l_scr, acc_scr)

    sel = sel_scr[...].astype(BF16)
    z_all = _dot(q, _lanes([p[0, 0:LANES, :].astype(BF16) for p in pages]))
    ss = []
    for i in range(pp):
        chosen = _dot(sel, e_ref[j * pp + i]) > 0.5
        ss.append(jnp.where(chosen, z_all[:, i * PAGE:(i + 1) * PAGE], NEG))
    m_prev = m_scr[...]
    m_next = m_prev
    for s in ss:
        m_next = jnp.maximum(m_next, jnp.max(s, axis=1, keepdims=True))
    alpha = jnp.exp2(m_prev - m_next)
    es = [jnp.exp2(s - m_next) for s in ss]
    l_new = alpha * l_scr[...]
    for e in es:
        l_new = l_new + jnp.sum(e, axis=1, keepdims=True)
    l_scr[...] = l_new
    m_scr[...] = m_next
    acc_scr[...] = acc_scr[...] * alpha + _dot_nt(_lanes([e.astype(BF16) for e in es]),
                                                  _lanes([p[0, LANES:2 * LANES, :].astype(BF16) for p in pages]))

    @pl.when(j == pl.num_programs(1) - 1)
    def _():
        s_new, v_new = new_token(kvs_new_ref[0])
        cur = past_len // SEL_BLOCK
        chosen = jnp.sum(jnp.where(_iota(sel_scr.shape, 1) == cur, sel_scr[...], 0.0), axis=1, keepdims=True)
        s_new = jnp.where(chosen > 0.5, s_new, NEG)
        m_prev = m_scr[...]
        m_next = jnp.maximum(m_prev, s_new)
        alpha = jnp.exp2(m_prev - m_next)
        e_new = jnp.exp2(s_new - m_next)
        o_s = (acc_scr[...] * alpha + e_new * v_new) / jnp.maximum(alpha * l_scr[...] + e_new, 1e-30)
        gates = jnp.where(row < NSA_HPG, gate_ref[0, 0:1, :], gate_ref[0, 1:2, :])
        hh = row % NSA_HPG

        def gate(c):
            return jnp.sum(jnp.where(lane == 3 * hh + c, gates, 0.0), axis=1, keepdims=True)

        out_ref[0] = gate(0) * oc_scr[...] + gate(1) * o_s + gate(2) * ow_scr[...]


def _nsa_dec(q8, ckv, gates, win_state, kvs_new, kvw_new, pool, page_table, *, pp, layer):
    bd, n_pages = page_table.shape
    past_len = n_pages * PAGE
    n_sel = -(-(past_len + 1) // SEL_BLOCK)
    nc = ckv.shape[1]
    wid = -(-n_sel // LANES) * LANES
    amat = _imp_matrix(nc, wid)
    etile = _expand_tiles(n_pages, wid, PAGE)
    n_w = win_state.shape[2]
    req = lambda shape: pl.BlockSpec((1,) + shape, lambda b, j, pt: (b,) + (0,) * len(shape))
    const = lambda a: pl.BlockSpec(a.shape, lambda b, j, pt: (0,) * a.ndim)
    win_spec = pl.BlockSpec((1, KV_W, n_w), lambda b, j, pt: (layer * bd + b, 0, 0))
    grid_spec = pltpu.PrefetchScalarGridSpec(
        num_scalar_prefetch=1, grid=(bd, n_pages // pp),
        in_specs=[req((8, LANES)), req((nc, KV_W)), req((NSA_GROUPS, LANES)), win_spec,
                  req((1, KV_W)), req((1, KV_W)), req((KV_W, 1)), const(amat), const(etile)]
        + _page_specs(KV_W, n_pages, pp, False),
        out_specs=(req((8, LANES)), req((KV_W, n_w))),
        scratch_shapes=[pltpu.VMEM((8, LANES), F32)] * 3 + [pltpu.VMEM((8, wid), F32)] + [pltpu.VMEM((8, LANES), F32)] * 2)
    return pl.pallas_call(
        functools.partial(_nsa_dec_kernel, pp=pp, past_len=past_len, n_sel=n_sel), grid_spec=grid_spec,
        out_shape=(jax.ShapeDtypeStruct((bd, 8, LANES), F32), jax.ShapeDtypeStruct((bd, KV_W, n_w), F32)),
        compiler_params=_cp("arbitrary", "arbitrary"), name="nsa_decode",
    )(page_table, q8, ckv, gates, win_state, kvs_new, kvw_new, kvw_new.reshape(bd, KV_W, 1), amat, etile,
      *([pool] * pp))


def _rope_table(pos):
    half = ROT_DIM // 2
    inv = ROPE_THETA ** (-jnp.arange(half, dtype=F32) / half)
    ang = pos.astype(F32)[:, None] * inv[None, :]
    cos, sin = jnp.cos(ang), jnp.sin(ang)
    n = pos.shape[0]
    pad = HEAD_DIM - ROT_DIM
    one_head = lambda a, b, fill: jnp.concatenate([a, b, jnp.full((n, pad), fill, F32)], axis=1)
    zero = jnp.zeros_like(sin)
    tabs = [one_head(cos, cos, 1.0), one_head(-sin, zero, 0.0), one_head(zero, sin, 0.0)]
    return jnp.concatenate([jnp.tile(t, (1, LANES // HEAD_DIM)) for t in tabs], axis=1)


_Q_HEAD_ORDER = (0, 4, 1, 5, 2, 6, 3, 7)


def _proj_columns():
    q = np.concatenate([np.arange(HEAD_DIM) + HEAD_DIM * h for h in _Q_HEAD_ORDER])
    kv = np.arange(NSA_W, NSA_W + 3 * KV_W)
    gate0 = NSA_W + 3 * KV_W
    sbfox = np.arange(gate0 + GATE_W, gate0 + GATE_W + 6 * SB_W)
    gate = np.arange(gate0, gate0 + GATE_W)
    pf0 = gate0 + GATE_W + 6 * SB_W
    pf = np.arange(pf0, pf0 + FOX_HEADS)
    return np.concatenate([q, kv, sbfox, gate, pf]), pf0 + FOX_HEADS


def _layer_weights(l, w_in, b_forget, cmp_pos, w_cmp_k, w_cmp_v, w_up_nsa):
    cols, merge0 = _proj_columns()
    d = w_in.shape[1]
    w_proj = jnp.take(w_in[l], jnp.asarray(cols), axis=1)
    w_proj = jnp.concatenate([w_proj, jnp.zeros((d, N_PROJ - w_proj.shape[1]), F32)], axis=1).astype(BF16)
    w_merge = w_in[l][:, merge0:].astype(BF16)
    bias = jnp.zeros((1, LANES), F32).at[0, LOGF_LANE:LOGF_LANE + FOX_HEADS].set(b_forget[l])
    wk = w_cmp_k[l].reshape(CMP_BLOCK, HEAD_DIM, HEAD_DIM)
    wv = w_cmp_v[l].reshape(CMP_BLOCK, HEAD_DIM, HEAD_DIM)
    w_blk = jnp.zeros((CMP_BLOCK, KV_W, KV_W), F32)
    for n, wsrc in enumerate((wk, wk, wv, wv)):
        w_blk = w_blk.at[:, HEAD_DIM * n:HEAD_DIM * (n + 1), HEAD_DIM * n:HEAD_DIM * (n + 1)].set(wsrc)
    pe4 = jnp.tile(cmp_pos[l], (1, KV_W // HEAD_DIM))
    rows = np.concatenate([np.arange(HEAD_DIM) + HEAD_DIM * h for h in _Q_HEAD_ORDER])
    w_a = jnp.take(w_up_nsa[l], jnp.asarray(rows), axis=0).astype(BF16)
    w_cat = jnp.concatenate([w_blk[0:CMP_BLOCK // 2], w_blk[CMP_BLOCK // 2:]], axis=2).astype(BF16)
    return w_proj, w_merge, bias, w_cat, pe4, w_a


def _pad_heads(a):
    return jnp.concatenate([a, jnp.zeros((a.shape[0], 8 - a.shape[1], a.shape[2]), a.dtype)], axis=1)


def _token_lanes(cache):
    l, p, t = cache.shape[:3]
    return cache.transpose(0, 1, 3, 4, 5, 2).reshape(l * p, -1, t)


def _own_lanes(o8, n_heads):
    n = o8.shape[0]
    o = o8[:, 0:n_heads].reshape(n, n_heads, n_heads, HEAD_DIM)
    return jnp.stack([o[:, h, h] for h in range(n_heads)], axis=1).reshape(n, n_heads * HEAD_DIM)


def kernel(x_prompt, x_sample, cache_nsa_cmp_kv, cache_nsa_sel_kv, cache_sb_kv, cache_fox_kv, cache_fox_logf,
           state_nsa_win_kv, page_table, norm_mix_g, norm_ffn_g, norm_final_g, w_in, b_forget, cmp_pos,
           w_cmp_k, w_cmp_v, w_up_nsa, w_up_sb, w_up_fox, w_out, w_ffn_gate, w_ffn_up, w_ffn_down):
    batch, seq, d = x_prompt.shape
    bd = x_sample.shape[0]
    depth = w_in.shape[0]
    n_pool = cache_sb_kv.shape[1]
    n_pages = page_table.shape[1]
    past_len = n_pages * PAGE
    m = batch * seq
    tm = min(256, seq)
    tq = min(256, seq)
    pp = min(PAGES_PER_STEP, n_pages)
    pp_prompt = min(PAGES_PER_STEP, seq // PAGE)

    rope_p = _rope_table(jnp.arange(seq, dtype=I32))
    rope_s = _rope_table(jnp.full((bd,), past_len, I32))
    ident_pt = jnp.arange(batch * (seq // PAGE), dtype=I32).reshape(batch, seq // PAGE)
    pool_cmp, pool_sel = _token_lanes(cache_nsa_cmp_kv), _token_lanes(cache_nsa_sel_kv)
    pool_sb, pool_fox = _token_lanes(cache_sb_kv), _token_lanes(cache_fox_kv)
    pool_lf = cache_fox_logf.astype(F32).transpose(0, 1, 3, 2).reshape(depth * n_pool, FOX_HEADS, PAGE)
    win_state = _token_lanes(state_nsa_win_kv)

    xp = x_prompt.reshape(m, d)
    xs = x_sample.reshape(bd, d)
    p_states, s_states = [], []
    for l in range(depth):
        w_proj, w_merge, bias, w_blk, pe4, w_a = _layer_weights(l, w_in, b_forget, cmp_pos, w_cmp_k, w_cmp_v, w_up_nsa)
        w_b, w_c, w_o = w_up_sb[l].astype(BF16), w_up_fox[l].astype(BF16), w_out[l].astype(BF16)
        w_g, w_u, w_d = w_ffn_gate[l].astype(BF16), w_ffn_up[l].astype(BF16), w_ffn_down[l].astype(BF16)
        g_mix, g_ffn = norm_mix_g[l].reshape(1, d), norm_ffn_g[l].reshape(1, d)

        (qn, kv_cmp, kv_sel, kv_win, kv_sb, kv_fox, kb_sel, kb_win, kb_sb, kb_fox, q_sb, q_fox, gates, logf, cum
         ) = _proj(xp, g_mix, w_proj, rope_p, bias, tm=tm, tiles_per_seq=seq // tm, feature_states=True)
        ckv = _compress(kv_cmp, ident_pt, w_blk, pe4, pp=pp_prompt, paged=False)
        o_nsa = _nsa_prompt(qn, ckv, kb_sel, kb_win, gates, batch=batch, seq=seq, tq=tq)
        o_sb = _sb_prompt(q_sb, kb_sb, batch=batch, seq=seq, tq=tq)
        cum4 = cum[:, LOGF_LANE:LOGF_LANE + FOX_HEADS]
        cum_q = cum4.T[:, :, None]
        cum_k = cum4.reshape(batch, seq, FOX_HEADS).transpose(0, 2, 1).reshape(batch, FOX_HEADS, seq // tq, 1, tq)
        o_fox = _fox_prompt(q_fox, kb_fox, cum_q, cum_k, batch=batch, seq=seq, tq=tq)
        xp = _merge(xp, g_mix, o_nsa, o_sb, o_fox, w_merge, w_a, w_b, w_c, w_o, tm=tm)
        xp = _ffn(xp, g_ffn, w_g, w_u, w_d, tm=tm)
        win_keep = min(WINDOW, seq)
        p_states.append((kv_cmp, kv_sel, kv_sb, kv_fox, logf, kv_win[:, :, seq - win_keep:]))

        (qn, kv_cmp, kv_sel, kv_win, kv_sb, kv_fox, _, _, _, _, q_sb, q_fox, gates, logf, _
         ) = _proj(xs, g_mix, w_proj, rope_s, bias, tm=bd, tiles_per_seq=1, feature_states=False)
        pt_l = page_table + l * n_pool
        ckv = _compress(pool_cmp, pt_l, w_blk, pe4, pp=pp, paged=True)
        q8 = qn.transpose(2, 0, 1, 3).reshape(bd, NSA_HEADS, LANES)
        o_nsa8, win_new = _nsa_dec(q8, ckv, gates.transpose(1, 0, 2), win_state, kv_sel.reshape(bd, 1, KV_W),
                                   kv_win.reshape(bd, 1, KV_W), pool_sel, pt_l, pp=pp, layer=l)
        o_sb8 = _sb_dec(_pad_heads(q_sb.transpose(1, 0, 2)), pool_sb, pt_l, pp=pp)
        lf4 = logf[:, LOGF_LANE:LOGF_LANE + FOX_HEADS]
        lf_new = _pad_heads(jnp.broadcast_to(lf4[:, :, None], (bd, FOX_HEADS, LANES)))
        o_fox8 = _fox_dec(_pad_heads(q_fox.transpose(1, 0, 2)), kv_fox[:, 0:SB_W].reshape(bd, 1, SB_W),
                          kv_fox[:, SB_W:2 * SB_W].reshape(bd, 1, SB_W), lf_new, pool_fox, pool_lf, pt_l, pp=pp)
        o_a = o_nsa8.reshape(bd, NSA_GROUPS, NSA_HPG, NSA_GROUPS, HEAD_DIM)
        o_a = jnp.stack([o_a[:, g, :, g] for g in range(NSA_GROUPS)], axis=2).reshape(bd, NSA_W)
        o_a = jnp.stack([o_a, jnp.zeros_like(o_a)], axis=0).astype(BF16)
        o_b = _own_lanes(o_sb8, SB_HEADS).astype(BF16)
        o_c = _own_lanes(o_fox8, FOX_HEADS).astype(BF16)
        xs = _merge(xs, g_mix, o_a, o_b, o_c, w_merge, w_a, w_b, w_c, w_o, tm=bd)
        xs = _ffn(xs, g_ffn, w_g, w_u, w_d, tm=bd)
        s_states.append((
            kv_cmp.reshape(bd, 1, 2, NSA_GROUPS, HEAD_DIM), kv_sel.reshape(bd, 1, 2, NSA_GROUPS, HEAD_DIM),
            kv_sb.reshape(bd, 1, 2, SB_HEADS, HEAD_DIM), kv_fox.reshape(bd, 1, 2, FOX_HEADS, HEAD_DIM),
            lf4.reshape(bd, 1, FOX_HEADS),
            win_new.reshape(bd, 2, NSA_GROUPS, HEAD_DIM, -1).transpose(0, 4, 1, 2, 3)))

    g_fin = norm_final_g.reshape(1, d)
    y_prompt = _final_norm(xp, g_fin, tm=tm).reshape(batch, seq, d)
    y_sample = _final_norm(xs, g_fin, tm=bd).reshape(bd, 1, d)
    stk = lambda sts, i: jnp.stack([st[i] for st in sts], axis=0)

    def kv_state(i, heads):
        a = stk(p_states, i)
        return a.reshape(depth, batch, 2, heads, HEAD_DIM, a.shape[-1]).transpose(0, 1, 5, 2, 3, 4)

    p_out = (kv_state(0, NSA_GROUPS), kv_state(1, NSA_GROUPS), kv_state(2, SB_HEADS), kv_state(3, FOX_HEADS),
             stk(p_states, 4).transpose(0, 1, 3, 2), kv_state(5, NSA_GROUPS))
    return (y_prompt, y_sample) + p_out + tuple(stk(s_states, i) for i in range(6))
```

```python
import functools

import numpy as np
import jax
import jax.numpy as jnp
from jax import lax
from jax.experimental import pallas as pl
from jax.experimental.pallas import tpu as pltpu

F32, BF16, I32 = jnp.float32, jnp.bfloat16, jnp.int32

HEAD_DIM = 64
ROT_DIM = HEAD_DIM // 4
ROPE_THETA = 500000.0
NSA_HEADS = 8
NSA_GROUPS = 2
NSA_HPG = NSA_HEADS // NSA_GROUPS
CMP_STRIDE = 16
CMP_BLOCK = 32
SEL_BLOCK = 64
TOP_N = 16
WINDOW = 512
SB_HEADS = 4
FOX_HEADS = 4
PAGE = 128
SEL_FORCE_SCORE = 1.0e4
NEG = -1.0e30
RMS_EPS = 1e-6
SCALE = HEAD_DIM ** -0.5
LOG2E = 1.4426950408889634

LANES = 128
NSA_W = NSA_HEADS * HEAD_DIM
KV_W = 2 * NSA_GROUPS * HEAD_DIM
SB_W = SB_HEADS * HEAD_DIM
GATE_W = 3 * NSA_HEADS
C_Q, C_CMP, C_SEL, C_WIN, C_SB, C_FOX, C_MISC = 0, 512, 768, 1024, 1280, 2048, 2816
N_PROJ = C_MISC + LANES
LOGF_LANE = GATE_W
VMEM_LIMIT = 56 * 1024 * 1024
PAGES_PER_STEP = 8


def _cp(*sem):
    return pltpu.CompilerParams(dimension_semantics=sem, vmem_limit_bytes=VMEM_LIMIT)


def _dot(a, b):
    return jnp.dot(a, b, preferred_element_type=F32)


def _dot_nt(a, b):
    return lax.dot_general(a, b, (((1,), (1,)), ((), ())), preferred_element_type=F32)


def _split_dot(a, r, parts):
    out, rem = None, a
    for _ in range(parts):
        hi = rem.astype(BF16)
        d = _dot(hi, r)
        out = d if out is None else out + d
        rem = rem - hi.astype(F32)
    return out


def _split_dot_l(l, a, parts):
    out, rem = None, a
    for _ in range(parts):
        hi = rem.astype(BF16)
        d = _dot(l, hi)
        out = d if out is None else out + d
        rem = rem - hi.astype(F32)
    return out


def _rms(x, g):
    return x * lax.rsqrt(jnp.mean(x * x, axis=-1, keepdims=True) + RMS_EPS) * g


def _softplus2(z2):
    return jnp.maximum(z2, 0.0) + jnp.log2(1.0 + jnp.exp2(-jnp.abs(z2)))


def _iota(shape, axis):
    return lax.broadcasted_iota(I32, shape, axis)


def _proj_kernel(x_ref, g_ref, w_ref, rope_ref, b_ref, tri_ref,
                 qn_ref, kvc_ref, kvs_ref, kvw_ref, kvsb_ref, kvfx_ref,
                 kbs_ref, kbw_ref, kbsb_ref, kbfx_ref, qsb_ref, qfx_ref,
                 gate_ref, logf_ref, cum_ref, h_scr, carry_scr, *, tiles_per_seq, feature_states):
    i = pl.program_id(0)
    tm = x_ref.shape[0]
    h_scr[...] = _rms(x_ref[...], g_ref[...]).astype(BF16)
    hb = h_scr[...]
    cos, sin_lo, sin_hi = rope_ref[:, 0:128], rope_ref[:, 128:256], rope_ref[:, 256:384]

    def rope(seg):
        return seg * cos + pltpu.roll(seg, LANES - 8, 1) * sin_lo + pltpu.roll(seg, 8, 1) * sin_hi

    def put_state(f_ref, blocks):
        for n, blk in enumerate(blocks):
            if feature_states:
                f_ref[0, LANES * n:LANES * (n + 1), :] = blk.T
            else:
                f_ref[:, LANES * n:LANES * (n + 1)] = blk

    lane = _iota((tm, LANES), 1)
    pq = _dot(hb, w_ref[:, C_Q:C_Q + NSA_W])
    for s in range(NSA_HPG):
        seg = rope(pq[:, LANES * s:LANES * (s + 1)]) * (SCALE * LOG2E)
        for g in range(NSA_GROUPS):
            qn_ref[g, s] = jnp.where(lane // HEAD_DIM == g, seg, 0.0).astype(BF16)

    for c0, f_ref, h_ref in ((C_CMP, kvc_ref, None), (C_SEL, kvs_ref, kbs_ref), (C_WIN, kvw_ref, kbw_ref)):
        p = _dot(hb, w_ref[:, c0:c0 + KV_W])
        k, v = rope(p[:, 0:128]), p[:, 128:256]
        put_state(f_ref, [k, v])
        if h_ref is not None:
            h_ref[:, 0:128] = k.astype(BF16)
            h_ref[:, 128:256] = v.astype(BF16)

    lane_w = _iota((tm, SB_W), 1)
    for c0, q_ref, f_ref, h_ref in ((C_SB, qsb_ref, kvsb_ref, kbsb_ref), (C_FOX, qfx_ref, kvfx_ref, kbfx_ref)):
        p = _dot(hb, w_ref[:, c0:c0 + 3 * SB_W])
        q = p[:, 0:SB_W] * (SCALE * LOG2E)
        for h in range(SB_HEADS):
            q_ref[h] = jnp.where(lane_w // HEAD_DIM == h, q, 0.0).astype(BF16)
        kv = p[:, SB_W:3 * SB_W]
        put_state(f_ref, [kv[:, LANES * n:LANES * (n + 1)] for n in range(2 * SB_W // LANES)])
        h_ref[...] = kv.astype(BF16)

    pm = _dot(hb, w_ref[:, C_MISC:C_MISC + LANES])
    sg = jax.nn.sigmoid(pm)
    gate_ref[0] = sg
    gate_ref[1] = pltpu.roll(sg, LANES - GATE_W // 2, 1)
    zf = pm + b_ref[...]
    lf = jnp.minimum(zf, 0.0) - jnp.log1p(jnp.exp(-jnp.abs(zf)))
    if feature_states:
        logf_ref[0] = lf.T[LOGF_LANE:LOGF_LANE + FOX_HEADS, :]
    else:
        logf_ref[...] = lf

    @pl.when(i % tiles_per_seq == 0)
    def _():
        carry_scr[...] = jnp.zeros_like(carry_scr)

    c = _split_dot_l(tri_ref[...], lf, 3) + carry_scr[0:1, :]
    cum_ref[...] = c
    carry_scr[...] = jnp.broadcast_to(c[tm - 1:tm, :], carry_scr.shape)


def _proj(x, gain, w, rope_tab, bias, *, tm, tiles_per_seq, feature_states):
    m, d = x.shape
    tri = jnp.tril(jnp.ones((tm, tm), F32)).astype(BF16)
    row = lambda wid: pl.BlockSpec((tm, wid), lambda i: (i, 0))
    const = lambda shape: pl.BlockSpec(shape, lambda i: (0,) * len(shape))
    f32o = lambda wid: jax.ShapeDtypeStruct((m, wid), F32)
    b16o = lambda wid: jax.ShapeDtypeStruct((m, wid), BF16)
    if feature_states:
        nb, seq = m // (tm * tiles_per_seq), tm * tiles_per_seq
        st_shape = lambda wid: jax.ShapeDtypeStruct((nb, wid, seq), F32)
        st_spec = lambda wid: pl.BlockSpec((1, wid, tm), lambda i: (i // tiles_per_seq, 0, i % tiles_per_seq))
        lf_shape, lf_spec = st_shape(FOX_HEADS), st_spec(FOX_HEADS)
    else:
        st_shape, st_spec, lf_shape, lf_spec = f32o, row, f32o(LANES), row(LANES)
    out_shape = (
        jax.ShapeDtypeStruct((NSA_GROUPS, NSA_HPG, m, LANES), BF16),
        st_shape(KV_W), st_shape(KV_W), st_shape(KV_W), st_shape(2 * SB_W), st_shape(2 * SB_W),
        b16o(KV_W), b16o(KV_W), b16o(2 * SB_W), b16o(2 * SB_W),
        jax.ShapeDtypeStruct((SB_HEADS, m, SB_W), BF16), jax.ShapeDtypeStruct((SB_HEADS, m, SB_W), BF16),
        jax.ShapeDtypeStruct((NSA_GROUPS, m, LANES), F32), lf_shape, f32o(LANES),
    )
    out_specs = (
        pl.BlockSpec((NSA_GROUPS, NSA_HPG, tm, LANES), lambda i: (0, 0, i, 0)),
        st_spec(KV_W), st_spec(KV_W), st_spec(KV_W), st_spec(2 * SB_W), st_spec(2 * SB_W),
        row(KV_W), row(KV_W), row(2 * SB_W), row(2 * SB_W),
        pl.BlockSpec((SB_HEADS, tm, SB_W), lambda i: (0, i, 0)),
        pl.BlockSpec((SB_HEADS, tm, SB_W), lambda i: (0, i, 0)),
        pl.BlockSpec((NSA_GROUPS, tm, LANES), lambda i: (0, i, 0)),
        lf_spec, row(LANES),
    )
    return pl.pallas_call(
        functools.partial(_proj_kernel, tiles_per_seq=tiles_per_seq, feature_states=feature_states),
        grid=(m // tm,),
        in_specs=[row(d), const((1, d)), const((d, N_PROJ)),
                  pl.BlockSpec((tm, 3 * LANES), lambda i: (i % tiles_per_seq, 0)),
                  const((1, LANES)), const((tm, tm))],
        out_specs=out_specs, out_shape=out_shape,
        scratch_shapes=[pltpu.VMEM((tm, d), BF16), pltpu.VMEM((8, LANES), F32)],
        compiler_params=_cp("arbitrary"), name="proj",
    )(x, gain, w, rope_tab, bias, tri)


def _compress_kernel(pt_ref, *refs, pp, nch):
    pages, (perm_ref, w_ref, pe_ref, out_ref, x_scr, acc_scr, pos_scr) = refs[:pp], refs[pp:]
    j = pl.program_id(1)
    half = CMP_BLOCK // 2
    chunks = PAGE // CMP_STRIDE

    @pl.when((pl.program_id(0) == 0) & (j == 0))
    def _():
        c = jnp.zeros((8, KV_W), F32)
        for l in range(half):
            for off, cols in ((0, slice(0, KV_W)), (half, slice(KV_W, 2 * KV_W))):
                row = jnp.broadcast_to(pe_ref[off + l:off + l + 1, :], (8, KV_W))
                c = c + _split_dot(row, w_ref[l, :, cols], 2)
        pos_scr[...] = c

    for i in range(pp):
        gt = _dot_nt(perm_ref[...], pages[i][0].astype(BF16))
        base = pl.multiple_of((j * pp + i) * chunks, chunks)
        for l in range(half):
            x_scr[l, pl.ds(base, chunks), :] = gt[chunks * l:chunks * (l + 1), :]

    @pl.when(j == pl.num_programs(1) - 1)
    def _():
        acc_scr[...] = jnp.zeros_like(acc_scr)
        for l in range(half):
            acc_scr[...] += _dot(x_scr[l].astype(BF16), w_ref[l])
        nxt = pltpu.roll(acc_scr[:, KV_W:2 * KV_W], nch - 1, 0)
        nxt = jnp.where(_iota((nch, KV_W), 0) < nch - 1, nxt, 0.0)
        out_ref[0] = (acc_scr[:, 0:KV_W] + nxt + pos_scr[0:1, :]).astype(BF16)


def _compress(pool, page_table, w_cat, pe4, *, pp, paged):
    b, n_pages = page_table.shape
    chunks = PAGE // CMP_STRIDE
    nch = n_pages * chunks
    half = CMP_BLOCK // 2
    tok = np.arange(PAGE)
    perm = jnp.asarray(np.arange(PAGE)[:, None] == ((tok % CMP_STRIDE) * chunks + tok // CMP_STRIDE)[None, :], BF16)
    if paged:
        page_spec = lambda i: pl.BlockSpec((1, KV_W, PAGE), lambda bb, j, pt: (pt[bb, j * pp + i], 0, 0))
    else:
        page_spec = lambda i: pl.BlockSpec((1, KV_W, PAGE), lambda bb, j, pt: (bb, 0, j * pp + i))
    grid_spec = pltpu.PrefetchScalarGridSpec(
        num_scalar_prefetch=1, grid=(b, n_pages // pp),
        in_specs=[page_spec(i) for i in range(pp)] + [
            pl.BlockSpec((PAGE, PAGE), lambda bb, j, pt: (0, 0)),
            pl.BlockSpec((half, KV_W, 2 * KV_W), lambda bb, j, pt: (0, 0, 0)),
            pl.BlockSpec((CMP_BLOCK, KV_W), lambda bb, j, pt: (0, 0))],
        out_specs=pl.BlockSpec((1, nch, KV_W), lambda bb, j, pt: (bb, 0, 0)),
        scratch_shapes=[pltpu.VMEM((half, nch, KV_W), F32), pltpu.VMEM((nch, 2 * KV_W), F32),
                        pltpu.VMEM((8, KV_W), F32)])
    return pl.pallas_call(
        functools.partial(_compress_kernel, pp=pp, nch=nch), grid_spec=grid_spec,
        out_shape=jax.ShapeDtypeStruct((b, nch, KV_W), BF16),
        compiler_params=_cp("arbitrary", "arbitrary"), name="compress",
    )(page_table, *([pool] * pp), perm, w_cat, pe4)


def _masked_softmax_rows(s, mask):
    s = jnp.where(mask, s, NEG)
    e = jnp.where(mask, jnp.exp2(s - jnp.max(s, axis=1, keepdims=True)), 0.0)
    return e / jnp.maximum(jnp.sum(e, axis=1, keepdims=True), 1e-30)


def _online_init(m_scr, l_scr, acc_scr):
    m_scr[...] = jnp.full_like(m_scr, NEG)
    l_scr[...] = jnp.zeros_like(l_scr)
    acc_scr[...] = jnp.zeros_like(acc_scr)


def _select_blocks(imp, pos, n_sel):
    blk = _iota(imp.shape, 1)
    cur = pos // SEL_BLOCK
    valid = blk * SEL_BLOCK <= pos
    forced = jnp.where(blk == 0, 1.0, 0.0) + jnp.where(blk == cur, 1.0, 0.0) + jnp.where(blk == cur - 1, 1.0, 0.0)
    score = jnp.where(valid, jnp.where(forced > 0.5, SEL_FORCE_SCORE, imp), -1.0)
    score = jnp.where(blk < n_sel, score, -2.0)
    rank = jnp.zeros(imp.shape, F32)
    for b2 in range(n_sel):
        col = score[:, b2:b2 + 1]
        ge = jnp.where(col >= score, 1.0, 0.0)
        gt = jnp.where(col > score, 1.0, 0.0)
        rank = rank + jnp.where(blk > b2, ge, gt)
    return jnp.where(rank < min(TOP_N, n_sel), 1.0, 0.0)


def _rep(x, n):
    return x if n == 1 else jnp.concatenate([x] * n, axis=1)


def _select_blocks_t(p_grp, amat_t, q0, n_sel):
    t = p_grp.shape[0]
    nb = amat_t.shape[0]
    hi = p_grp.astype(BF16)
    lo = (p_grp - hi.astype(F32)).astype(BF16)
    imp = _dot_nt(amat_t, hi) + _dot_nt(amat_t, lo)
    pos = q0 + _iota((1, t), 1)
    blk = _iota((nb, t), 0)
    cur = pos // SEL_BLOCK
    forced = jnp.where(blk == 0, 1.0, 0.0) + jnp.where(blk == cur, 1.0, 0.0) + jnp.where(blk == cur - 1, 1.0, 0.0)
    score = jnp.where(blk * SEL_BLOCK <= pos, jnp.where(forced > 0.5, SEL_FORCE_SCORE, imp), -1.0)
    score = jnp.where(blk < n_sel, score, -2.0)
    groups = [score[8 * k:8 * (k + 1)] for k in range(nb // 8)]
    ranks = [jnp.zeros((8, t), F32) for _ in groups]
    sub = _iota((8, t), 0)
    for b2 in range(n_sel):
        rowv = score[b2:b2 + 1, :]
        k2, r2 = divmod(b2, 8)
        for k, grp in enumerate(groups):
            if k > k2:
                inc = jnp.where(rowv >= grp, 1.0, 0.0)
            elif k < k2:
                inc = jnp.where(rowv > grp, 1.0, 0.0)
            else:
                inc = jnp.where(sub > r2, jnp.where(rowv >= grp, 1.0, 0.0), jnp.where(rowv > grp, 1.0, 0.0))
            ranks[k] = ranks[k] + inc
    chosen = [jnp.where(rk < min(TOP_N, n_sel), 1.0, 0.0) for rk in ranks]
    pad = [jnp.zeros((LANES - nb, t), F32)] if nb < LANES else []
    return jnp.concatenate(chosen + pad, axis=0)


def _nsa_prompt_kernel(qn_ref, ckv_ref, ks_ref, kw_ref, gate_ref, amat_ref, e_ref, out_ref,
                       m_scr, l_scr, acc_scr, oc_scr, os_scr, *, n_sel):
    g, i = pl.program_id(1), pl.program_id(2)
    t = qn_ref.shape[2]
    q0 = i * t
    rowi, coli = _iota((t, t), 0), _iota((t, t), 1)
    reps = t // LANES

    nc = ckv_ref.shape[1]
    cmask = _iota((t, nc), 1) * CMP_STRIDE + (CMP_BLOCK - 1) <= q0 + _iota((t, 1), 0)
    ck, cv = ckv_ref[0, :, 0:128], ckv_ref[0, :, 128:256]
    p_grp = None
    for h in range(NSA_HPG):
        p = _masked_softmax_rows(_dot_nt(qn_ref[0, h], ck), cmask)
        oc_scr[h] = _dot(p.astype(BF16), cv)
        p_grp = p if p_grp is None else p_grp + p
    sel = _select_blocks_t(p_grp, amat_ref[...], q0, n_sel).T.astype(BF16)

    def online_head(h, s, v):
        m_prev = m_scr[h]
        m_next = jnp.maximum(m_prev, jnp.max(s, axis=1, keepdims=True))
        alpha = jnp.exp2(m_prev - m_next)
        e = jnp.exp2(s - _rep(m_next, reps))
        l_scr[h] = alpha * l_scr[h] + jnp.sum(e, axis=1, keepdims=True)
        m_scr[h] = m_next
        acc_scr[h] = acc_scr[h] * alpha + _dot(e.astype(BF16), v)

    def reset():
        m_scr[...] = jnp.full_like(m_scr, NEG)
        l_scr[...] = jnp.zeros_like(l_scr)
        acc_scr[...] = jnp.zeros_like(acc_scr)

    def sel_tile(j, diag):
        kt = ks_ref[0, pl.ds(pl.multiple_of(j * t, t), t), :]
        chosen = _dot(sel, e_ref[j])
        if diag:
            chosen = jnp.where(coli <= rowi, chosen, 0.0)
        keep = chosen > 0.5
        for h in range(NSA_HPG):
            online_head(h, jnp.where(keep, _dot_nt(qn_ref[0, h], kt[:, 0:128]), NEG), kt[:, 128:256])

    reset()
    lax.fori_loop(0, i, lambda j, c: (sel_tile(j, False), c)[1], 0)
    sel_tile(i, True)
    for h in range(NSA_HPG):
        os_scr[h] = acc_scr[h] / jnp.maximum(l_scr[h], 1e-30)

    w_len = WINDOW + t
    k0 = jnp.maximum(i - WINDOW // t, 0) * t
    kw = kw_ref[0, pl.ds(pl.multiple_of(k0, t), w_len), :]
    d = (q0 - k0) + _iota((t, w_len), 0) - _iota((t, w_len), 1)
    keep_w = jnp.where(d >= 0, jnp.where(d <= WINDOW, 1.0, 0.0), 0.0) > 0.5

    gates = gate_ref[0]
    lane = _iota((t, LANES), 1)
    for h in range(NSA_HPG):
        s = jnp.where(keep_w, _dot_nt(qn_ref[0, h], kw[:, 0:128]), NEG)
        e = jnp.exp2(s - jnp.max(s, axis=1, keepdims=True))
        o_w = _dot(e.astype(BF16), kw[:, 128:256]) / jnp.sum(e, axis=1, keepdims=True)
        o = (gates[:, 3 * h:3 * h + 1] * oc_scr[h] + gates[:, 3 * h + 1:3 * h + 2] * os_scr[h]
             + gates[:, 3 * h + 2:3 * h + 3] * o_w)
        out_ref[0, :, LANES * h:LANES * (h + 1)] = jnp.where(lane // HEAD_DIM == g, o, 0.0).astype(BF16)


def _imp_matrix(nc, width):
    n = np.arange(nc)[:, None]
    b = np.arange(width)[None, :]
    sub = SEL_BLOCK // CMP_STRIDE
    return jnp.asarray((n // sub == b).astype(np.float32) + ((n + 1) // sub == b), BF16)


def _expand_tiles(n_tiles, n_blk, tk):
    key = np.arange(n_tiles * tk).reshape(n_tiles, 1, tk)
    return jnp.asarray(key // SEL_BLOCK == np.arange(n_blk)[None, :, None], BF16)


def _nsa_prompt(qn, ckv, kb_sel, kb_win, gates, *, batch, seq, tq):
    nq = seq // tq
    nc = ckv.shape[1]
    n_sel = seq // SEL_BLOCK
    assert n_sel <= LANES and WINDOW % tq == 0 and seq >= WINDOW + tq
    amat = _imp_matrix(nc, -(-n_sel // 8) * 8).T
    etile = _expand_tiles(nq, LANES, tq)
    m = batch * seq
    return pl.pallas_call(
        functools.partial(_nsa_prompt_kernel, n_sel=n_sel),
        grid=(batch, NSA_GROUPS, nq),
        in_specs=[
            pl.BlockSpec((1, NSA_HPG, tq, LANES), lambda b, g, i: (g, 0, b * nq + i, 0)),
            pl.BlockSpec((1, nc, KV_W), lambda b, g, i: (b, 0, 0)),
            pl.BlockSpec((1, seq, KV_W), lambda b, g, i: (b, 0, 0)),
            pl.BlockSpec((1, seq, KV_W), lambda b, g, i: (b, 0, 0)),
            pl.BlockSpec((1, tq, LANES), lambda b, g, i: (g, b * nq + i, 0)),
            pl.BlockSpec(amat.shape, lambda b, g, i: (0, 0)),
            pl.BlockSpec(etile.shape, lambda b, g, i: (0, 0, 0)),
        ],
        out_specs=pl.BlockSpec((1, tq, NSA_W), lambda b, g, i: (g, b * nq + i, 0)),
        out_shape=jax.ShapeDtypeStruct((NSA_GROUPS, m, NSA_W), BF16),
        scratch_shapes=[pltpu.VMEM((NSA_HPG, tq, LANES), F32)] * 5,
        compiler_params=_cp("arbitrary", "arbitrary", "arbitrary"), name="nsa_prompt",
    )(qn, ckv, kb_sel.reshape(batch, seq, KV_W), kb_win.reshape(batch, seq, KV_W), gates, amat, etile)


def _upper_strict(n):
    return jnp.asarray(np.triu(np.ones((n, n), np.float32), 1).T, BF16)


def _stack_heads(v):
    lane = _iota(v.shape, 1)
    return jnp.concatenate([jnp.where(lane // HEAD_DIM == h, v, jnp.zeros_like(v)) for h in range(SB_HEADS)], axis=0)


def _head_lanes(parts):
    lane = _iota(parts[0].shape, 1)
    return jnp.concatenate([jnp.where(lane < HEAD_DIM, parts[0], parts[1]),
                            jnp.where(lane < HEAD_DIM, parts[2], parts[3])], axis=1)


def _sb_prompt_kernel(q_ref, kv_ref, u_ref, out_ref, acc_scr, r_scr):
    i = pl.program_id(1)
    t = q_ref.shape[1]
    rowi, coli = _iota((t, t), 0), _iota((t, t), 1)
    acc_scr[...] = jnp.zeros_like(acc_scr)
    r_scr[...] = jnp.zeros_like(r_scr)

    def tile(j, diag):
        kt = kv_ref[0, pl.ds(pl.multiple_of(j * t, t), t), :]
        ws = []
        for h in range(SB_HEADS):
            z = _dot_nt(q_ref[h], kt[:, 0:SB_W])
            sp = _softplus2(z)
            cost = jnp.where(coli < rowi, sp, 0.0) if diag else sp
            r = r_scr[h]
            w = jnp.exp2(z - sp - _split_dot(cost, u_ref[...], 1) - _rep(r, t // LANES))
            if diag:
                w = jnp.where(coli < rowi, w, 0.0)
            ws.append(w.astype(BF16))
            r_scr[h] = r + jnp.sum(cost, axis=1, keepdims=True)
        acc_scr[...] += _dot(jnp.concatenate(ws, axis=1), _stack_heads(kt[:, SB_W:2 * SB_W]))

    tile(i, True)
    lax.fori_loop(0, i, lambda n, c: (tile(i - 1 - n, False), c)[1], 0)
    out_ref[...] = acc_scr[...].astype(BF16)


def _sb_prompt(q, kb, *, batch, seq, tq):
    nq = seq // tq
    u = _upper_strict(tq)
    return pl.pallas_call(
        _sb_prompt_kernel, grid=(batch, nq),
        in_specs=[pl.BlockSpec((SB_HEADS, tq, SB_W), lambda b, i: (0, b * nq + i, 0)),
                  pl.BlockSpec((1, seq, 2 * SB_W), lambda b, i: (b, 0, 0)),
                  pl.BlockSpec(u.shape, lambda b, i: (0, 0))],
        out_specs=pl.BlockSpec((tq, SB_W), lambda b, i: (b * nq + i, 0)),
        out_shape=jax.ShapeDtypeStruct((batch * seq, SB_W), BF16),
        scratch_shapes=[pltpu.VMEM((tq, SB_W), F32), pltpu.VMEM((SB_HEADS, tq, LANES), F32)],
        compiler_params=_cp("arbitrary", "arbitrary"), name="sb_prompt",
    )(q, kb.reshape(batch, seq, 2 * SB_W), u)


def _fox_prompt_kernel(q_ref, kv_ref, cq_ref, ck_ref, out_ref, m_scr, l_scr, acc_scr, cq_scr):
    i = pl.program_id(1)
    t = q_ref.shape[1]
    rowi, coli = _iota((t, t), 0), _iota((t, t), 1)
    m_scr[...] = jnp.full_like(m_scr, NEG)
    l_scr[...] = jnp.zeros_like(l_scr)
    acc_scr[...] = jnp.zeros_like(acc_scr)
    for h in range(FOX_HEADS):
        cq_scr[h] = jnp.broadcast_to(cq_ref[h] * LOG2E, (t, LANES))

    def tile(j, diag):
        kt = kv_ref[0, pl.ds(pl.multiple_of(j * t, t), t), :]
        es, alphas = [], []
        for h in range(FOX_HEADS):
            s = _dot_nt(q_ref[h], kt[:, 0:SB_W]) + (_rep(cq_scr[h], t // LANES) - ck_ref[0, h, j] * LOG2E)
            if diag:
                s = jnp.where(coli <= rowi, s, NEG)
            m_prev = m_scr[h]
            m_next = jnp.maximum(m_prev, jnp.max(s, axis=1, keepdims=True))
            alpha = jnp.exp2(m_prev - m_next)
            e = jnp.exp2(s - _rep(m_next, t // LANES))
            l_scr[h] = alpha * l_scr[h] + jnp.sum(e, axis=1, keepdims=True)
            m_scr[h] = m_next
            es.append(e.astype(BF16))
            alphas.append(alpha)
        acc_scr[...] = acc_scr[...] * _head_lanes(alphas) + _dot(jnp.concatenate(es, axis=1),
                                                                 _stack_heads(kt[:, SB_W:2 * SB_W]))

    lax.fori_loop(0, i, lambda j, c: (tile(j, False), c)[1], 0)
    tile(i, True)
    out_ref[...] = (acc_scr[...] / _head_lanes([jnp.maximum(l_scr[h], 1e-30) for h in range(FOX_HEADS)])).astype(BF16)


def _fox_prompt(q, kb, cum_q, cum_k, *, batch, seq, tq):
    nq = seq // tq
    nk = seq // tq
    return pl.pallas_call(
        _fox_prompt_kernel, grid=(batch, nq),
        in_specs=[pl.BlockSpec((FOX_HEADS, tq, SB_W), lambda b, i: (0, b * nq + i, 0)),
                  pl.BlockSpec((1, seq, 2 * SB_W), lambda b, i: (b, 0, 0)),
                  pl.BlockSpec((FOX_HEADS, tq, 1), lambda b, i: (0, b * nq + i, 0)),
                  pl.BlockSpec((1, FOX_HEADS, nk, 1, tq), lambda b, i: (b, 0, 0, 0, 0))],
        out_specs=pl.BlockSpec((tq, SB_W), lambda b, i: (b * nq + i, 0)),
        out_shape=jax.ShapeDtypeStruct((batch * seq, SB_W), BF16),
        scratch_shapes=[pltpu.VMEM((FOX_HEADS, tq, LANES), F32), pltpu.VMEM((FOX_HEADS, tq, LANES), F32),
                        pltpu.VMEM((tq, SB_W), F32), pltpu.VMEM((FOX_HEADS, tq, LANES), F32)],
        compiler_params=_cp("arbitrary", "arbitrary"), name="fox_prompt",
    )(q, kb.reshape(batch, seq, 2 * SB_W), cum_q, cum_k)


def _merge_kernel(x_ref, g_ref, oa_ref, ob_ref, oc_ref, wm_ref, wa_ref, wb_ref, wc_ref, wo_ref, out_ref):
    x = x_ref[...]
    d = x.shape[1]
    hb = _rms(x, g_ref[...]).astype(BF16)
    oa = oa_ref[0] + oa_ref[1]
    u = jax.nn.sigmoid(_dot(hb, wm_ref[:, 0:d])) * _dot(oa, wa_ref[...])
    u = u + jax.nn.sigmoid(_dot(hb, wm_ref[:, d:2 * d])) * _dot(ob_ref[...], wb_ref[...])
    u = u + jax.nn.sigmoid(_dot(hb, wm_ref[:, 2 * d:3 * d])) * _dot(oc_ref[...], wc_ref[...])
    out_ref[...] = x + _dot(u.astype(BF16), wo_ref[...])


def _merge(x, gain, o_nsa, o_sb, o_fox, w_merge, w_a, w_b, w_c, w_o, *, tm):
    m, d = x.shape
    row = lambda wid: pl.BlockSpec((tm, wid), lambda i: (i, 0))
    const = lambda a: pl.BlockSpec(a.shape, lambda i: (0,) * a.ndim)
    return pl.pallas_call(
        _merge_kernel, grid=(m // tm,),
        in_specs=[row(d), const(gain), pl.BlockSpec((NSA_GROUPS, tm, NSA_W), lambda i: (0, i, 0)),
                  row(SB_W), row(SB_W), const(w_merge), const(w_a), const(w_b), const(w_c), const(w_o)],
        out_specs=row(d), out_shape=jax.ShapeDtypeStruct((m, d), F32),
        compiler_params=_cp("arbitrary"), name="merge",
    )(x, gain, o_nsa, o_sb, o_fox, w_merge, w_a, w_b, w_c, w_o)


def _ffn_kernel(x_ref, g_ref, wg_ref, wu_ref, wd_ref, gf_ref, out_ref, *, final_norm):
    x = x_ref[...]
    hb = _rms(x, g_ref[...]).astype(BF16)
    a = _dot(hb, wg_ref[...])
    act = a * jax.nn.sigmoid(a) * _dot(hb, wu_ref[...])
    y = x + _dot(act.astype(BF16), wd_ref[...])
    out_ref[...] = _rms(y, gf_ref[...]) if final_norm else y


def _ffn(x, gain, w_gate, w_up, w_down, final_gain, *, tm, final_norm):
    m, d = x.shape
    row = pl.BlockSpec((tm, d), lambda i: (i, 0))
    vec = pl.BlockSpec((1, d), lambda i: (0, 0))
    const = lambda a: pl.BlockSpec(a.shape, lambda i: (0,) * a.ndim, pipeline_mode=pl.Buffered(1))
    return pl.pallas_call(
        functools.partial(_ffn_kernel, final_norm=final_norm), grid=(m // tm,),
        in_specs=[row, vec, const(w_gate), const(w_up), const(w_down), vec],
        out_specs=row, out_shape=jax.ShapeDtypeStruct((m, d), F32),
        compiler_params=_cp("arbitrary"), name="ffn",
    )(x, gain, w_gate, w_up, w_down, final_gain)


def _page_specs(rows, n_pages, pp, descending):
    def spec(i):
        if descending:
            return pl.BlockSpec((1, rows, PAGE), lambda b, j, pt: (pt[b, n_pages - 1 - (j * pp + i)], 0, 0))
        return pl.BlockSpec((1, rows, PAGE), lambda b, j, pt: (pt[b, j * pp + i], 0, 0))
    return [spec(i) for i in range(pp)]


def _lanes(parts):
    return jnp.concatenate(parts, axis=1)


def _sb_dec_kernel(pt_ref, q_ref, *refs, pp):
    pages, (u_ref, out_ref, acc_scr, r_scr) = refs[:pp], refs[pp:]
    j = pl.program_id(1)

    @pl.when(j == 0)
    def _():
        acc_scr[...] = jnp.zeros_like(acc_scr)
        r_scr[...] = jnp.zeros_like(r_scr)

    q = q_ref[0]
    z_all = _dot(q, _lanes([p[0, 0:SB_W, :].astype(BF16) for p in pages]))
    zs = [z_all[:, i * PAGE:(i + 1) * PAGE] for i in range(pp)]
    sps = [_softplus2(z) for z in zs]
    afters = [_split_dot(sp, u_ref[...], 2) for sp in sps]
    r = r_scr[...]
    ws = []
    for i in range(pp):
        ws.append(jnp.exp2(zs[i] - sps[i] - afters[i] - r).astype(BF16))
        r = r + jnp.sum(sps[i], axis=1, keepdims=True)
    acc_scr[...] += _dot_nt(_lanes(ws), _lanes([p[0, SB_W:2 * SB_W, :].astype(BF16) for p in pages]))
    r_scr[...] = r

    @pl.when(j == pl.num_programs(1) - 1)
    def _():
        out_ref[0] = acc_scr[...]


def _sb_dec(q8, pool, page_table, *, pp):
    bd, n_pages = page_table.shape
    u = _upper_strict(PAGE)
    grid_spec = pltpu.PrefetchScalarGridSpec(
        num_scalar_prefetch=1, grid=(bd, n_pages // pp),
        in_specs=[pl.BlockSpec((1, 8, SB_W), lambda b, j, pt: (b, 0, 0))]
        + _page_specs(2 * SB_W, n_pages, pp, True) + [pl.BlockSpec(u.shape, lambda b, j, pt: (0, 0))],
        out_specs=pl.BlockSpec((1, 8, SB_W), lambda b, j, pt: (b, 0, 0)),
        scratch_shapes=[pltpu.VMEM((8, SB_W), F32), pltpu.VMEM((8, LANES), F32)])
    return pl.pallas_call(
        functools.partial(_sb_dec_kernel, pp=pp), grid_spec=grid_spec,
        out_shape=jax.ShapeDtypeStruct((bd, 8, SB_W), F32),
        compiler_params=_cp("arbitrary", "arbitrary"), name="sb_decode",
    )(page_table, q8, *([pool] * pp), u)


def _fox_dec_kernel(pt_ref, q_ref, kn_ref, vn_ref, lfn_ref, *refs, pp):
    pages, lf_pages = refs[:pp], refs[pp:2 * pp]
    u_ref, out_ref, m_scr, l_scr, acc_scr, r_scr = refs[2 * pp:]
    j = pl.program_id(1)
    q = q_ref[0]
    reps = SB_W // LANES

    @pl.when(j == 0)
    def _():
        kn = kn_ref[0].astype(BF16).astype(F32)
        m_scr[...] = jnp.broadcast_to(jnp.sum(q.astype(F32) * kn, axis=1, keepdims=True), m_scr.shape)
        l_scr[...] = jnp.ones_like(l_scr)
        acc_scr[...] = jnp.broadcast_to(vn_ref[0].astype(BF16).astype(F32), acc_scr.shape)
        r_scr[...] = lfn_ref[0]

    z_all = _dot(q, _lanes([p[0, 0:SB_W, :].astype(BF16) for p in pages]))
    pad = jnp.zeros((8 - FOX_HEADS, PAGE), F32)
    r = r_scr[...]
    ss = []
    for i in range(pp):
        lf = jnp.concatenate([lf_pages[i][0], pad], axis=0)
        ss.append(z_all[:, i * PAGE:(i + 1) * PAGE] + LOG2E * (_split_dot(lf, u_ref[...], 3) + r))
        r = r + jnp.sum(lf, axis=1, keepdims=True)
    r_scr[...] = r
    m_prev = m_scr[...]
    m_next = m_prev
    for s in ss:
        m_next = jnp.maximum(m_next, jnp.max(s, axis=1, keepdims=True))
    alpha = jnp.exp2(m_prev - m_next)
    es = [jnp.exp2(s - m_next) for s in ss]
    l_new = alpha * l_scr[...]
    for e in es:
        l_new = l_new + jnp.sum(e, axis=1, keepdims=True)
    l_scr[...] = l_new
    m_scr[...] = m_next
    acc_scr[...] = acc_scr[...] * _rep(alpha, reps) + _dot_nt(
        _lanes([e.astype(BF16) for e in es]), _lanes([p[0, SB_W:2 * SB_W, :].astype(BF16) for p in pages]))

    @pl.when(j == pl.num_programs(1) - 1)
    def _():
        out_ref[0] = acc_scr[...] / _rep(l_scr[...], reps)


def _fox_dec(q8, k_new, v_new, lf_new, pool, lf_pool, page_table, *, pp):
    bd, n_pages = page_table.shape
    u = _upper_strict(PAGE)
    req = lambda shape: pl.BlockSpec((1,) + shape, lambda b, j, pt: (b,) + (0,) * len(shape))
    lf_spec = lambda i: pl.BlockSpec((1, FOX_HEADS, PAGE), lambda b, j, pt: (pt[b, n_pages - 1 - (j * pp + i)], 0, 0))
    grid_spec = pltpu.PrefetchScalarGridSpec(
        num_scalar_prefetch=1, grid=(bd, n_pages // pp),
        in_specs=[req((8, SB_W)), req((1, SB_W)), req((1, SB_W)), req((8, LANES))]
        + _page_specs(2 * SB_W, n_pages, pp, True) + [lf_spec(i) for i in range(pp)]
        + [pl.BlockSpec(u.shape, lambda b, j, pt: (0, 0))],
        out_specs=req((8, SB_W)),
        scratch_shapes=[pltpu.VMEM((8, LANES), F32), pltpu.VMEM((8, LANES), F32),
                        pltpu.VMEM((8, SB_W), F32), pltpu.VMEM((8, LANES), F32)])
    return pl.pallas_call(
        functools.partial(_fox_dec_kernel, pp=pp), grid_spec=grid_spec,
        out_shape=jax.ShapeDtypeStruct((bd, 8, SB_W), F32),
        compiler_params=_cp("arbitrary", "arbitrary"), name="fox_decode",
    )(page_table, q8, k_new, v_new, lf_new, *([pool] * pp), *([lf_pool] * pp), u)


def _nsa_dec_kernel(pt_ref, q_ref, ckv_ref, gate_ref, win_ref, kvs_new_ref, kvw_new_ref, kvw_col_ref, amat_ref, e_ref,
                    *refs, pp, past_len, n_sel):
    pages, (out_ref, win_out_ref, m_scr, l_scr, acc_scr, sel_scr, oc_scr, ow_scr) = refs[:pp], refs[pp:]
    j = pl.program_id(1)
    q = q_ref[0]
    rows = q.shape[0]
    row = _iota((rows, LANES), 0)
    row1 = _iota((rows, 1), 0)
    lane = _iota((rows, LANES), 1)
    pos = jnp.full((rows, 1), past_len, I32)

    def new_token(kv_row):
        k = kv_row[:, 0:LANES].astype(BF16).astype(F32)
        return jnp.sum(q.astype(F32) * k, axis=1, keepdims=True), kv_row[:, LANES:2 * LANES].astype(BF16).astype(F32)

    @pl.when(j == 0)
    def _():
        nc = ckv_ref.shape[1]
        cmask = _iota((rows, nc), 1) * CMP_STRIDE + (CMP_BLOCK - 1) <= pos
        p = _masked_softmax_rows(_dot_nt(q, ckv_ref[0, :, 0:128]), cmask)
        oc_scr[...] = _dot(p.astype(BF16), ckv_ref[0, :, 128:256])
        g0 = jnp.sum(p[0:NSA_HPG], axis=0, keepdims=True)
        g1 = jnp.sum(p[NSA_HPG:2 * NSA_HPG], axis=0, keepdims=True)
        p_grp = jnp.where(row1 < NSA_HPG, g0, g1)
        imp = _split_dot(p_grp, amat_ref[...], 2)
        sel_scr[...] = _select_blocks(imp, pos, n_sel)

        wk = win_ref[0]
        n_w = wk.shape[1]
        s = _dot(q, wk[0:LANES, :].astype(BF16))
        kpos = past_len - n_w + _iota((rows, n_w), 1)
        keep = jnp.where(kpos >= 0, jnp.where(pos - kpos <= WINDOW, 1.0, 0.0), 0.0) > 0.5
        s = jnp.where(keep, s, NEG)
        s_new, v_new = new_token(kvw_new_ref[0])
        m = jnp.maximum(s_new, jnp.max(s, axis=1, keepdims=True))
        e = jnp.exp2(s - m)
        e_new = jnp.exp2(s_new - m)
        o_w = _dot_nt(e.astype(BF16), wk[LANES:2 * LANES, :].astype(BF16)) + e_new * v_new
        ow_scr[...] = o_w / (jnp.sum(e, axis=1, keepdims=True) + e_new)

        shifted = pltpu.roll(wk, n_w - 1, 1)
        win_out_ref[0] = jnp.where(_iota(wk.shape, 1) == n_w - 1, kvw_col_ref[0], shifted)
        _online_init(m_scr, l_scr, acc_scr)

    sel = sel_scr[...].astype(BF16)
    z_all = _dot(q, _lanes([p[0, 0:LANES, :].astype(BF16) for p in pages]))
    ss = []
    for i in range(pp):
        chosen = _dot(sel, e_ref[j * pp + i]) > 0.5
        ss.append(jnp.where(chosen, z_all[:, i * PAGE:(i + 1) * PAGE], NEG))
    m_prev = m_scr[...]
    m_next = m_prev
    for s in ss:
        m_next = jnp.maximum(m_next, jnp.max(s, axis=1, keepdims=True))
    alpha = jnp.exp2(m_prev - m_next)
    es = [jnp.exp2(s - m_next) for s in ss]
    l_new = alpha * l_scr[...]
    for e in es:
        l_new = l_new + jnp.sum(e, axis=1, keepdims=True)
    l_scr[...] = l_new
    m_scr[...] = m_next
    acc_scr[...] = acc_scr[...] * alpha + _dot_nt(_lanes([e.astype(BF16) for e in es]),
                                                  _lanes([p[0, LANES:2 * LANES, :].astype(BF16) for p in pages]))

    @pl.when(j == pl.num_programs(1) - 1)
    def _():
        s_new, v_new = new_token(kvs_new_ref[0])
        cur = past_len // SEL_BLOCK
        chosen = jnp.sum(jnp.where(_iota(sel_scr.shape, 1) == cur, sel_scr[...], 0.0), axis=1, keepdims=True)
        s_new = jnp.where(chosen > 0.5, s_new, NEG)
        m_prev = m_scr[...]
        m_next = jnp.maximum(m_prev, s_new)
        alpha = jnp.exp2(m_prev - m_next)
        e_new = jnp.exp2(s_new - m_next)
        o_s = (acc_scr[...] * alpha + e_new * v_new) / jnp.maximum(alpha * l_scr[...] + e_new, 1e-30)
        gates = jnp.where(row < NSA_HPG, gate_ref[0, 0:1, :], gate_ref[0, 1:2, :])
        hh = row % NSA_HPG

        def gate(c):
            return jnp.sum(jnp.where(lane == 3 * hh + c, gates, 0.0), axis=1, keepdims=True)

        out_ref[0] = gate(0) * oc_scr[...] + gate(1) * o_s + gate(2) * ow_scr[...]


def _nsa_dec(q8, ckv, gates, win_state, kvs_new, kvw_new, pool, page_table, *, pp, layer):
    bd, n_pages = page_table.shape
    past_len = n_pages * PAGE
    n_sel = -(-(past_len + 1) // SEL_BLOCK)
    nc = ckv.shape[1]
    wid = -(-n_sel // LANES) * LANES
    amat = _imp_matrix(nc, wid)
    etile = _expand_tiles(n_pages, wid, PAGE)
    n_w = win_state.shape[2]
    req = lambda shape: pl.BlockSpec((1,) + shape, lambda b, j, pt: (b,) + (0,) * len(shape))
    const = lambda a: pl.BlockSpec(a.shape, lambda b, j, pt: (0,) * a.ndim)
    win_spec = pl.BlockSpec((1, KV_W, n_w), lambda b, j, pt: (layer * bd + b, 0, 0))
    grid_spec = pltpu.PrefetchScalarGridSpec(
        num_scalar_prefetch=1, grid=(bd, n_pages // pp),
        in_specs=[req((8, LANES)), req((nc, KV_W)), req((NSA_GROUPS, LANES)), win_spec,
                  req((1, KV_W)), req((1, KV_W)), req((KV_W, 1)), const(amat), const(etile)]
        + _page_specs(KV_W, n_pages, pp, False),
        out_specs=(req((8, LANES)), req((KV_W, n_w))),
        scratch_shapes=[pltpu.VMEM((8, LANES), F32)] * 3 + [pltpu.VMEM((8, wid), F32)] + [pltpu.VMEM((8, LANES), F32)] * 2)
    return pl.pallas_call(
        functools.partial(_nsa_dec_kernel, pp=pp, past_len=past_len, n_sel=n_sel), grid_spec=grid_spec,
        out_shape=(jax.ShapeDtypeStruct((bd, 8, LANES), F32), jax.ShapeDtypeStruct((bd, KV_W, n_w), F32)),
        compiler_params=_cp("arbitrary", "arbitrary"), name="nsa_decode",
    )(page_table, q8, ckv, gates, win_state, kvs_new, kvw_new, kvw_new.reshape(bd, KV_W, 1), amat, etile,
      *([pool] * pp))


def _rope_table(pos):
    half = ROT_DIM // 2
    inv = ROPE_THETA ** (-jnp.arange(half, dtype=F32) / half)
    ang = pos.astype(F32)[:, None] * inv[None, :]
    cos, sin = jnp.cos(ang), jnp.sin(ang)
    n = pos.shape[0]
    pad = HEAD_DIM - ROT_DIM
    one_head = lambda a, b, fill: jnp.concatenate([a, b, jnp.full((n, pad), fill, F32)], axis=1)
    zero = jnp.zeros_like(sin)
    tabs = [one_head(cos, cos, 1.0), one_head(-sin, zero, 0.0), one_head(zero, sin, 0.0)]
    return jnp.concatenate([jnp.tile(t, (1, LANES // HEAD_DIM)) for t in tabs], axis=1)


_Q_HEAD_ORDER = (0, 4, 1, 5, 2, 6, 3, 7)


def _proj_columns():
    q = np.concatenate([np.arange(HEAD_DIM) + HEAD_DIM * h for h in _Q_HEAD_ORDER])
    kv = np.arange(NSA_W, NSA_W + 3 * KV_W)
    gate0 = NSA_W + 3 * KV_W
    sbfox = np.arange(gate0 + GATE_W, gate0 + GATE_W + 6 * SB_W)
    gate = np.arange(gate0, gate0 + GATE_W)
    pf0 = gate0 + GATE_W + 6 * SB_W
    pf = np.arange(pf0, pf0 + FOX_HEADS)
    return np.concatenate([q, kv, sbfox, gate, pf]), pf0 + FOX_HEADS


def _layer_weights(l, w_in, b_forget, cmp_pos, w_cmp_k, w_cmp_v, w_up_nsa):
    cols, merge0 = _proj_columns()
    d = w_in.shape[1]
    w_proj = jnp.take(w_in[l], jnp.asarray(cols), axis=1)
    w_proj = jnp.concatenate([w_proj, jnp.zeros((d, N_PROJ - w_proj.shape[1]), F32)], axis=1).astype(BF16)
    w_merge = w_in[l][:, merge0:].astype(BF16)
    bias = jnp.zeros((1, LANES), F32).at[0, LOGF_LANE:LOGF_LANE + FOX_HEADS].set(b_forget[l])
    wk = w_cmp_k[l].reshape(CMP_BLOCK, HEAD_DIM, HEAD_DIM)
    wv = w_cmp_v[l].reshape(CMP_BLOCK, HEAD_DIM, HEAD_DIM)
    w_blk = jnp.zeros((CMP_BLOCK, KV_W, KV_W), F32)
    for n, wsrc in enumerate((wk, wk, wv, wv)):
        w_blk = w_blk.at[:, HEAD_DIM * n:HEAD_DIM * (n + 1), HEAD_DIM * n:HEAD_DIM * (n + 1)].set(wsrc)
    pe4 = jnp.tile(cmp_pos[l], (1, KV_W // HEAD_DIM))
    rows = np.concatenate([np.arange(HEAD_DIM) + HEAD_DIM * h for h in _Q_HEAD_ORDER])
    w_a = jnp.take(w_up_nsa[l], jnp.asarray(rows), axis=0).astype(BF16)
    w_cat = jnp.concatenate([w_blk[0:CMP_BLOCK // 2], w_blk[CMP_BLOCK // 2:]], axis=2).astype(BF16)
    return w_proj, w_merge, bias, w_cat, pe4, w_a


def _pad_heads(a):
    return jnp.concatenate([a, jnp.zeros((a.shape[0], 8 - a.shape[1], a.shape[2]), a.dtype)], axis=1)


def _token_lanes(cache):
    l, p, t = cache.shape[:3]
    return cache.transpose(0, 1, 3, 4, 5, 2).reshape(l * p, -1, t)


def _own_lanes(o8, n_heads):
    n = o8.shape[0]
    o = o8[:, 0:n_heads].reshape(n, n_heads, n_heads, HEAD_DIM)
    return jnp.stack([o[:, h, h] for h in range(n_heads)], axis=1).reshape(n, n_heads * HEAD_DIM)


def kernel(x_prompt, x_sample, cache_nsa_cmp_kv, cache_nsa_sel_kv, cache_sb_kv, cache_fox_kv, cache_fox_logf,
           state_nsa_win_kv, page_table, norm_mix_g, norm_ffn_g, norm_final_g, w_in, b_forget, cmp_pos,
           w_cmp_k, w_cmp_v, w_up_nsa, w_up_sb, w_up_fox, w_out, w_ffn_gate, w_ffn_up, w_ffn_down):
    batch, seq, d = x_prompt.shape
    bd = x_sample.shape[0]
    depth = w_in.shape[0]
    n_pool = cache_sb_kv.shape[1]
    n_pages = page_table.shape[1]
    past_len = n_pages * PAGE
    m = batch * seq
    tm = min(256, seq)
    tq = min(256, seq)
    tq_soft = min(512, seq)
    pp = min(PAGES_PER_STEP, n_pages)
    pp_prompt = min(PAGES_PER_STEP, seq // PAGE)

    rope_p = _rope_table(jnp.arange(seq, dtype=I32))
    rope_s = _rope_table(jnp.full((bd,), past_len, I32))
    ident_pt = jnp.arange(batch * (seq // PAGE), dtype=I32).reshape(batch, seq // PAGE)
    pool_cmp, pool_sel = _token_lanes(cache_nsa_cmp_kv), _token_lanes(cache_nsa_sel_kv)
    pool_sb, pool_fox = _token_lanes(cache_sb_kv), _token_lanes(cache_fox_kv)
    pool_lf = cache_fox_logf.astype(F32).transpose(0, 1, 3, 2).reshape(depth * n_pool, FOX_HEADS, PAGE)
    win_state = _token_lanes(state_nsa_win_kv)

    xp = x_prompt.reshape(m, d)
    xs = x_sample.reshape(bd, d)
    g_fin = norm_final_g.reshape(1, d)
    p_states, s_states = [], []
    for l in range(depth):
        w_proj, w_merge, bias, w_blk, pe4, w_a = _layer_weights(l, w_in, b_forget, cmp_pos, w_cmp_k, w_cmp_v, w_up_nsa)
        w_b, w_c, w_o = w_up_sb[l].astype(BF16), w_up_fox[l].astype(BF16), w_out[l].astype(BF16)
        w_g, w_u, w_d = w_ffn_gate[l].astype(BF16), w_ffn_up[l].astype(BF16), w_ffn_down[l].astype(BF16)
        g_mix, g_ffn = norm_mix_g[l].reshape(1, d), norm_ffn_g[l].reshape(1, d)

        (qn, kv_cmp, kv_sel, kv_win, kv_sb, kv_fox, kb_sel, kb_win, kb_sb, kb_fox, q_sb, q_fox, gates, logf, cum
         ) = _proj(xp, g_mix, w_proj, rope_p, bias, tm=tm, tiles_per_seq=seq // tm, feature_states=True)
        ckv = _compress(kv_cmp, ident_pt, w_blk, pe4, pp=pp_prompt, paged=False)
        o_nsa = _nsa_prompt(qn, ckv, kb_sel, kb_win, gates, batch=batch, seq=seq, tq=tq_soft)
        o_sb = _sb_prompt(q_sb, kb_sb, batch=batch, seq=seq, tq=tq)
        cum4 = cum[:, LOGF_LANE:LOGF_LANE + FOX_HEADS]
        cum_q = cum4.T[:, :, None]
        cum_k = cum4.reshape(batch, seq, FOX_HEADS).transpose(0, 2, 1).reshape(
            batch, FOX_HEADS, seq // tq_soft, 1, tq_soft)
        o_fox = _fox_prompt(q_fox, kb_fox, cum_q, cum_k, batch=batch, seq=seq, tq=tq_soft)
        xp = _merge(xp, g_mix, o_nsa, o_sb, o_fox, w_merge, w_a, w_b, w_c, w_o, tm=tm)
        xp = _ffn(xp, g_ffn, w_g, w_u, w_d, g_fin, tm=tm, final_norm=(l == depth - 1))
        win_keep = min(WINDOW, seq)
        p_states.append((kv_cmp, kv_sel, kv_sb, kv_fox, logf, kv_win[:, :, seq - win_keep:]))

        (qn, kv_cmp, kv_sel, kv_win, kv_sb, kv_fox, _, _, _, _, q_sb, q_fox, gates, logf, _
         ) = _proj(xs, g_mix, w_proj, rope_s, bias, tm=bd, tiles_per_seq=1, feature_states=False)
        pt_l = page_table + l * n_pool
        ckv = _compress(pool_cmp, pt_l, w_blk, pe4, pp=pp, paged=True)
        q8 = qn.transpose(2, 0, 1, 3).reshape(bd, NSA_HEADS, LANES)
        o_nsa8, win_new = _nsa_dec(q8, ckv, gates.transpose(1, 0, 2), win_state, kv_sel.reshape(bd, 1, KV_W),
                                   kv_win.reshape(bd, 1, KV_W), pool_sel, pt_l, pp=pp, layer=l)
        o_sb8 = _sb_dec(_pad_heads(q_sb.transpose(1, 0, 2)), pool_sb, pt_l, pp=pp)
        lf4 = logf[:, LOGF_LANE:LOGF_LANE + FOX_HEADS]
        lf_new = _pad_heads(jnp.broadcast_to(lf4[:, :, None], (bd, FOX_HEADS, LANES)))
        o_fox8 = _fox_dec(_pad_heads(q_fox.transpose(1, 0, 2)), kv_fox[:, 0:SB_W].reshape(bd, 1, SB_W),
                          kv_fox[:, SB_W:2 * SB_W].reshape(bd, 1, SB_W), lf_new, pool_fox, pool_lf, pt_l, pp=pp)
        o_a = o_nsa8.reshape(bd, NSA_GROUPS, NSA_HPG, NSA_GROUPS, HEAD_DIM)
        o_a = jnp.stack([o_a[:, g, :, g] for g in range(NSA_GROUPS)], axis=2).reshape(bd, NSA_W)
        o_a = jnp.stack([o_a, jnp.zeros_like(o_a)], axis=0).astype(BF16)
        o_b = _own_lanes(o_sb8, SB_HEADS).astype(BF16)
        o_c = _own_lanes(o_fox8, FOX_HEADS).astype(BF16)
        xs = _merge(xs, g_mix, o_a, o_b, o_c, w_merge, w_a, w_b, w_c, w_o, tm=bd)
        xs = _ffn(xs, g_ffn, w_g, w_u, w_d, g_fin, tm=bd, final_norm=(l == depth - 1))
        s_states.append((
            kv_cmp.reshape(bd, 1, 2, NSA_GROUPS, HEAD_DIM), kv_sel.reshape(bd, 1, 2, NSA_GROUPS, HEAD_DIM),
            kv_sb.reshape(bd, 1, 2, SB_HEADS, HEAD_DIM), kv_fox.reshape(bd, 1, 2, FOX_HEADS, HEAD_DIM),
            lf4.reshape(bd, 1, FOX_HEADS),
            win_new.reshape(bd, 2, NSA_GROUPS, HEAD_DIM, -1).transpose(0, 4, 1, 2, 3)))

    y_prompt = xp.reshape(batch, seq, d)
    y_sample = xs.reshape(bd, 1, d)
    stk = lambda sts, i: jnp.stack([st[i] for st in sts], axis=0)

    def kv_state(i, heads):
        a = stk(p_states, i)
        return a.reshape(depth, batch, 2, heads, HEAD_DIM, a.shape[-1]).transpose(0, 1, 5, 2, 3, 4)

    p_out = (kv_state(0, NSA_GROUPS), kv_state(1, NSA_GROUPS), kv_state(2, SB_HEADS), kv_state(3, FOX_HEADS),
             stk(p_states, 4).transpose(0, 1, 3, 2), kv_state(5, NSA_GROUPS))
    return (y_prompt, y_sample) + p_out + tuple(stk(s_states, i) for i in range(6))
```

```python
import functools

import numpy as np
import jax
import jax.numpy as jnp
from jax import lax
from jax.experimental import pallas as pl
from jax.experimental.pallas import tpu as pltpu

F32, BF16, I32 = jnp.float32, jnp.bfloat16, jnp.int32

HEAD_DIM = 64
ROT_DIM = HEAD_DIM // 4
ROPE_THETA = 500000.0
NSA_HEADS = 8
NSA_GROUPS = 2
NSA_HPG = NSA_HEADS // NSA_GROUPS
CMP_STRIDE = 16
CMP_BLOCK = 32
SEL_BLOCK = 64
TOP_N = 16
WINDOW = 512
SB_HEADS = 4
FOX_HEADS = 4
PAGE = 128
SEL_FORCE_SCORE = 1.0e4
NEG = -1.0e30
RMS_EPS = 1e-6
SCALE = HEAD_DIM ** -0.5
LOG2E = 1.4426950408889634

LANES = 128
NSA_W = NSA_HEADS * HEAD_DIM
KV_W = 2 * NSA_GROUPS * HEAD_DIM
SB_W = SB_HEADS * HEAD_DIM
GATE_W = 3 * NSA_HEADS
C_Q, C_CMP, C_SEL, C_WIN, C_SB, C_FOX, C_MISC = 0, 512, 768, 1024, 1280, 2048, 2816
N_PROJ = C_MISC + LANES
LOGF_LANE = GATE_W
VMEM_LIMIT = 56 * 1024 * 1024
PAGES_PER_STEP = 32


def _cp(*sem):
    return pltpu.CompilerParams(dimension_semantics=sem, vmem_limit_bytes=VMEM_LIMIT)


def _dot(a, b):
    return jnp.dot(a, b, preferred_element_type=F32)


def _dot_nt(a, b):
    return lax.dot_general(a, b, (((1,), (1,)), ((), ())), preferred_element_type=F32)


def _split_dot(a, r, parts):
    out, rem = None, a
    for _ in range(parts):
        hi = rem.astype(BF16)
        d = _dot(hi, r)
        out = d if out is None else out + d
        rem = rem - hi.astype(F32)
    return out


def _split_dot_l(l, a, parts):
    out, rem = None, a
    for _ in range(parts):
        hi = rem.astype(BF16)
        d = _dot(l, hi)
        out = d if out is None else out + d
        rem = rem - hi.astype(F32)
    return out


def _rms(x, g):
    return x * lax.rsqrt(jnp.mean(x * x, axis=-1, keepdims=True) + RMS_EPS) * g


def _softplus2(z2):
    return jnp.maximum(z2, 0.0) + jnp.log2(1.0 + jnp.exp2(-jnp.abs(z2)))


def _iota(shape, axis):
    return lax.broadcasted_iota(I32, shape, axis)


def _proj_kernel(x_ref, g_ref, w_ref, rope_ref, b_ref, tri_ref,
                 qn_ref, kvc_ref, kvs_ref, kvw_ref, kvsb_ref, kvfx_ref,
                 kbs_ref, kbw_ref, kbsb_ref, kbfx_ref, qsb_ref, qfx_ref,
                 gate_ref, logf_ref, cum_ref, h_scr, carry_scr, *, tiles_per_seq, feature_states):
    i = pl.program_id(0)
    tm = x_ref.shape[0]
    h_scr[...] = _rms(x_ref[...], g_ref[...]).astype(BF16)
    hb = h_scr[...]
    cos, sin_lo, sin_hi = rope_ref[:, 0:128], rope_ref[:, 128:256], rope_ref[:, 256:384]

    def rope(seg):
        return seg * cos + pltpu.roll(seg, LANES - 8, 1) * sin_lo + pltpu.roll(seg, 8, 1) * sin_hi

    def put_state(f_ref, blocks):
        for n, blk in enumerate(blocks):
            if feature_states:
                f_ref[0, LANES * n:LANES * (n + 1), :] = blk.T
            else:
                f_ref[:, LANES * n:LANES * (n + 1)] = blk

    lane = _iota((tm, LANES), 1)
    pq = _dot(hb, w_ref[:, C_Q:C_Q + NSA_W])
    for s in range(NSA_HPG):
        seg = rope(pq[:, LANES * s:LANES * (s + 1)]) * (SCALE * LOG2E)
        for g in range(NSA_GROUPS):
            qn_ref[g, s] = jnp.where(lane // HEAD_DIM == g, seg, 0.0).astype(BF16)

    for c0, f_ref, h_ref in ((C_CMP, kvc_ref, None), (C_SEL, kvs_ref, kbs_ref), (C_WIN, kvw_ref, kbw_ref)):
        p = _dot(hb, w_ref[:, c0:c0 + KV_W])
        k, v = rope(p[:, 0:128]), p[:, 128:256]
        put_state(f_ref, [k, v])
        if h_ref is not None:
            h_ref[:, 0:128] = k.astype(BF16)
            h_ref[:, 128:256] = v.astype(BF16)

    lane_w = _iota((tm, SB_W), 1)
    for c0, q_ref, f_ref, h_ref in ((C_SB, qsb_ref, kvsb_ref, kbsb_ref), (C_FOX, qfx_ref, kvfx_ref, kbfx_ref)):
        p = _dot(hb, w_ref[:, c0:c0 + 3 * SB_W])
        q = p[:, 0:SB_W] * (SCALE * LOG2E)
        for h in range(SB_HEADS):
            q_ref[h] = jnp.where(lane_w // HEAD_DIM == h, q, 0.0).astype(BF16)
        kv = p[:, SB_W:3 * SB_W]
        put_state(f_ref, [kv[:, LANES * n:LANES * (n + 1)] for n in range(2 * SB_W // LANES)])
        h_ref[...] = kv.astype(BF16)

    pm = _dot(hb, w_ref[:, C_MISC:C_MISC + LANES])
    sg = jax.nn.sigmoid(pm)
    gate_ref[0] = sg
    gate_ref[1] = pltpu.roll(sg, LANES - GATE_W // 2, 1)
    zf = pm + b_ref[...]
    lf = jnp.minimum(zf, 0.0) - jnp.log1p(jnp.exp(-jnp.abs(zf)))
    if feature_states:
        logf_ref[0] = lf.T[LOGF_LANE:LOGF_LANE + FOX_HEADS, :]
    else:
        logf_ref[...] = lf

    @pl.when(i % tiles_per_seq == 0)
    def _():
        carry_scr[...] = jnp.zeros_like(carry_scr)

    c = _split_dot_l(tri_ref[...], lf, 3) + carry_scr[0:1, :]
    cum_ref[...] = c
    carry_scr[...] = jnp.broadcast_to(c[tm - 1:tm, :], carry_scr.shape)


def _proj(x, gain, w, rope_tab, bias, *, tm, tiles_per_seq, feature_states):
    m, d = x.shape
    tri = jnp.tril(jnp.ones((tm, tm), F32)).astype(BF16)
    row = lambda wid: pl.BlockSpec((tm, wid), lambda i: (i, 0))
    const = lambda shape: pl.BlockSpec(shape, lambda i: (0,) * len(shape))
    f32o = lambda wid: jax.ShapeDtypeStruct((m, wid), F32)
    b16o = lambda wid: jax.ShapeDtypeStruct((m, wid), BF16)
    if feature_states:
        nb, seq = m // (tm * tiles_per_seq), tm * tiles_per_seq
        st_shape = lambda wid: jax.ShapeDtypeStruct((nb, wid, seq), F32)
        st_spec = lambda wid: pl.BlockSpec((1, wid, tm), lambda i: (i // tiles_per_seq, 0, i % tiles_per_seq))
        lf_shape, lf_spec = st_shape(FOX_HEADS), st_spec(FOX_HEADS)
    else:
        st_shape, st_spec, lf_shape, lf_spec = f32o, row, f32o(LANES), row(LANES)
    out_shape = (
        jax.ShapeDtypeStruct((NSA_GROUPS, NSA_HPG, m, LANES), BF16),
        st_shape(KV_W), st_shape(KV_W), st_shape(KV_W), st_shape(2 * SB_W), st_shape(2 * SB_W),
        b16o(KV_W), b16o(KV_W), b16o(2 * SB_W), b16o(2 * SB_W),
        jax.ShapeDtypeStruct((SB_HEADS, m, SB_W), BF16), jax.ShapeDtypeStruct((SB_HEADS, m, SB_W), BF16),
        jax.ShapeDtypeStruct((NSA_GROUPS, m, LANES), F32), lf_shape, f32o(LANES),
    )
    out_specs = (
        pl.BlockSpec((NSA_GROUPS, NSA_HPG, tm, LANES), lambda i: (0, 0, i, 0)),
        st_spec(KV_W), st_spec(KV_W), st_spec(KV_W), st_spec(2 * SB_W), st_spec(2 * SB_W),
        row(KV_W), row(KV_W), row(2 * SB_W), row(2 * SB_W),
        pl.BlockSpec((SB_HEADS, tm, SB_W), lambda i: (0, i, 0)),
        pl.BlockSpec((SB_HEADS, tm, SB_W), lambda i: (0, i, 0)),
        pl.BlockSpec((NSA_GROUPS, tm, LANES), lambda i: (0, i, 0)),
        lf_spec, row(LANES),
    )
    return pl.pallas_call(
        functools.partial(_proj_kernel, tiles_per_seq=tiles_per_seq, feature_states=feature_states),
        grid=(m // tm,),
        in_specs=[row(d), const((1, d)), const((d, N_PROJ)),
                  pl.BlockSpec((tm, 3 * LANES), lambda i: (i % tiles_per_seq, 0)),
                  const((1, LANES)), const((tm, tm))],
        out_specs=out_specs, out_shape=out_shape,
        scratch_shapes=[pltpu.VMEM((tm, d), BF16), pltpu.VMEM((8, LANES), F32)],
        compiler_params=_cp("arbitrary"), name="proj",
    )(x, gain, w, rope_tab, bias, tri)


def _compress_kernel(pt_ref, *refs, pp, nch):
    pages, (perm_ref, w_ref, pe_ref, out_ref, x_scr, acc_scr, pos_scr) = refs[:pp], refs[pp:]
    j = pl.program_id(1)
    half = CMP_BLOCK // 2
    chunks = PAGE // CMP_STRIDE

    @pl.when((pl.program_id(0) == 0) & (j == 0))
    def _():
        c = jnp.zeros((8, KV_W), F32)
        for l in range(half):
            for off, cols in ((0, slice(0, KV_W)), (half, slice(KV_W, 2 * KV_W))):
                row = jnp.broadcast_to(pe_ref[off + l:off + l + 1, :], (8, KV_W))
                c = c + _split_dot(row, w_ref[l, :, cols], 2)
        pos_scr[...] = c

    for i in range(pp):
        gt = _dot_nt(perm_ref[...], pages[i][0].astype(BF16))
        base = pl.multiple_of((j * pp + i) * chunks, chunks)
        for l in range(half):
            x_scr[l, pl.ds(base, chunks), :] = gt[chunks * l:chunks * (l + 1), :]

    @pl.when(j == pl.num_programs(1) - 1)
    def _():
        acc_scr[...] = jnp.zeros_like(acc_scr)
        for l in range(half):
            acc_scr[...] += _dot(x_scr[l].astype(BF16), w_ref[l])
        nxt = pltpu.roll(acc_scr[:, KV_W:2 * KV_W], nch - 1, 0)
        nxt = jnp.where(_iota((nch, KV_W), 0) < nch - 1, nxt, 0.0)
        out_ref[0] = (acc_scr[:, 0:KV_W] + nxt + pos_scr[0:1, :]).astype(BF16)


def _compress(pool, page_table, w_cat, pe4, *, pp, paged):
    b, n_pages = page_table.shape
    chunks = PAGE // CMP_STRIDE
    nch = n_pages * chunks
    half = CMP_BLOCK // 2
    tok = np.arange(PAGE)
    perm = jnp.asarray(np.arange(PAGE)[:, None] == ((tok % CMP_STRIDE) * chunks + tok // CMP_STRIDE)[None, :], BF16)
    if paged:
        page_spec = lambda i: pl.BlockSpec((1, KV_W, PAGE), lambda bb, j, pt: (pt[bb, j * pp + i], 0, 0))
    else:
        page_spec = lambda i: pl.BlockSpec((1, KV_W, PAGE), lambda bb, j, pt: (bb, 0, j * pp + i))
    grid_spec = pltpu.PrefetchScalarGridSpec(
        num_scalar_prefetch=1, grid=(b, n_pages // pp),
        in_specs=[page_spec(i) for i in range(pp)] + [
            pl.BlockSpec((PAGE, PAGE), lambda bb, j, pt: (0, 0)),
            pl.BlockSpec((half, KV_W, 2 * KV_W), lambda bb, j, pt: (0, 0, 0)),
            pl.BlockSpec((CMP_BLOCK, KV_W), lambda bb, j, pt: (0, 0))],
        out_specs=pl.BlockSpec((1, nch, KV_W), lambda bb, j, pt: (bb, 0, 0)),
        scratch_shapes=[pltpu.VMEM((half, nch, KV_W), F32), pltpu.VMEM((nch, 2 * KV_W), F32),
                        pltpu.VMEM((8, KV_W), F32)])
    return pl.pallas_call(
        functools.partial(_compress_kernel, pp=pp, nch=nch), grid_spec=grid_spec,
        out_shape=jax.ShapeDtypeStruct((b, nch, KV_W), BF16),
        compiler_params=_cp("arbitrary", "arbitrary"), name="compress",
    )(page_table, *([pool] * pp), perm, w_cat, pe4)


def _masked_softmax_rows(s, mask):
    s = jnp.where(mask, s, NEG)
    e = jnp.where(mask, jnp.exp2(s - jnp.max(s, axis=1, keepdims=True)), 0.0)
    return e / jnp.maximum(jnp.sum(e, axis=1, keepdims=True), 1e-30)


def _online_init(m_scr, l_scr, acc_scr):
    m_scr[...] = jnp.full_like(m_scr, NEG)
    l_scr[...] = jnp.zeros_like(l_scr)
    acc_scr[...] = jnp.zeros_like(acc_scr)


def _select_blocks(imp, pos, n_sel):
    blk = _iota(imp.shape, 1)
    cur = pos // SEL_BLOCK
    valid = blk * SEL_BLOCK <= pos
    forced = jnp.where(blk == 0, 1.0, 0.0) + jnp.where(blk == cur, 1.0, 0.0) + jnp.where(blk == cur - 1, 1.0, 0.0)
    score = jnp.where(valid, jnp.where(forced > 0.5, SEL_FORCE_SCORE, imp), -1.0)
    score = jnp.where(blk < n_sel, score, -2.0)
    rank = jnp.zeros(imp.shape, F32)
    for b2 in range(n_sel):
        col = score[:, b2:b2 + 1]
        ge = jnp.where(col >= score, 1.0, 0.0)
        gt = jnp.where(col > score, 1.0, 0.0)
        rank = rank + jnp.where(blk > b2, ge, gt)
    return jnp.where(rank < min(TOP_N, n_sel), 1.0, 0.0)


def _rep(x, n):
    return x if n == 1 else jnp.concatenate([x] * n, axis=1)


def _select_blocks_t(p_grp, amat_t, q0, n_sel):
    t = p_grp.shape[0]
    nb = amat_t.shape[0]
    hi = p_grp.astype(BF16)
    lo = (p_grp - hi.astype(F32)).astype(BF16)
    imp = _dot_nt(amat_t, hi) + _dot_nt(amat_t, lo)
    pos = q0 + _iota((1, t), 1)
    blk = _iota((nb, t), 0)
    cur = pos // SEL_BLOCK
    forced = jnp.where(blk == 0, 1.0, 0.0) + jnp.where(blk == cur, 1.0, 0.0) + jnp.where(blk == cur - 1, 1.0, 0.0)
    score = jnp.where(blk * SEL_BLOCK <= pos, jnp.where(forced > 0.5, SEL_FORCE_SCORE, imp), -1.0)
    score = jnp.where(blk < n_sel, score, -2.0)
    groups = [score[8 * k:8 * (k + 1)] for k in range(nb // 8)]
    ranks = [jnp.zeros((8, t), F32) for _ in groups]
    sub = _iota((8, t), 0)
    for b2 in range(n_sel):
        rowv = score[b2:b2 + 1, :]
        k2, r2 = divmod(b2, 8)
        for k, grp in enumerate(groups):
            if k > k2:
                inc = jnp.where(rowv >= grp, 1.0, 0.0)
            elif k < k2:
                inc = jnp.where(rowv > grp, 1.0, 0.0)
            else:
                inc = jnp.where(sub > r2, jnp.where(rowv >= grp, 1.0, 0.0), jnp.where(rowv > grp, 1.0, 0.0))
            ranks[k] = ranks[k] + inc
    chosen = [jnp.where(rk < min(TOP_N, n_sel), 1.0, 0.0) for rk in ranks]
    pad = [jnp.zeros((LANES - nb, t), F32)] if nb < LANES else []
    return jnp.concatenate(chosen + pad, axis=0)


def _nsa_prompt_kernel(qn_ref, ckv_ref, ks_ref, kw_ref, gate_ref, amat_ref, e_ref, out_ref,
                       m_scr, l_scr, acc_scr, oc_scr, os_scr, *, n_sel):
    g, i = pl.program_id(1), pl.program_id(2)
    t = qn_ref.shape[2]
    q0 = i * t
    rowi, coli = _iota((t, t), 0), _iota((t, t), 1)
    reps = t // LANES

    nc = ckv_ref.shape[1]
    cmask = _iota((t, nc), 1) * CMP_STRIDE + (CMP_BLOCK - 1) <= q0 + _iota((t, 1), 0)
    ck, cv = ckv_ref[0, :, 0:128], ckv_ref[0, :, 128:256]
    p_grp = None
    for h in range(NSA_HPG):
        p = _masked_softmax_rows(_dot_nt(qn_ref[0, h], ck), cmask)
        oc_scr[h] = _dot(p.astype(BF16), cv)
        p_grp = p if p_grp is None else p_grp + p
    sel = _select_blocks_t(p_grp, amat_ref[...], q0, n_sel).T.astype(BF16)

    def online_head(h, s, v):
        m_prev = m_scr[h]
        m_next = jnp.maximum(m_prev, jnp.max(s, axis=1, keepdims=True))
        alpha = jnp.exp2(m_prev - m_next)
        e = jnp.exp2(s - _rep(m_next, reps))
        l_scr[h] = alpha * l_scr[h] + jnp.sum(e, axis=1, keepdims=True)
        m_scr[h] = m_next
        acc_scr[h] = acc_scr[h] * alpha + _dot(e.astype(BF16), v)

    def reset():
        m_scr[...] = jnp.full_like(m_scr, NEG)
        l_scr[...] = jnp.zeros_like(l_scr)
        acc_scr[...] = jnp.zeros_like(acc_scr)

    def sel_tile(j, diag):
        kt = ks_ref[0, pl.ds(pl.multiple_of(j * t, t), t), :]
        chosen = _dot(sel, e_ref[j])
        if diag:
            chosen = jnp.where(coli <= rowi, chosen, 0.0)
        keep = chosen > 0.5
        for h in range(NSA_HPG):
            online_head(h, jnp.where(keep, _dot_nt(qn_ref[0, h], kt[:, 0:128]), NEG), kt[:, 128:256])

    reset()
    lax.fori_loop(0, i, lambda j, c: (sel_tile(j, False), c)[1], 0)
    sel_tile(i, True)
    for h in range(NSA_HPG):
        os_scr[h] = acc_scr[h] / jnp.maximum(l_scr[h], 1e-30)

    w_len = WINDOW + t
    k0 = jnp.maximum(i - WINDOW // t, 0) * t
    kw = kw_ref[0, pl.ds(pl.multiple_of(k0, t), w_len), :]
    d = (q0 - k0) + _iota((t, w_len), 0) - _iota((t, w_len), 1)
    keep_w = jnp.where(d >= 0, jnp.where(d <= WINDOW, 1.0, 0.0), 0.0) > 0.5

    gates = gate_ref[0]
    lane = _iota((t, LANES), 1)
    for h in range(NSA_HPG):
        s = jnp.where(keep_w, _dot_nt(qn_ref[0, h], kw[:, 0:128]), NEG)
        e = jnp.exp2(s - jnp.max(s, axis=1, keepdims=True))
        o_w = _dot(e.astype(BF16), kw[:, 128:256]) / jnp.sum(e, axis=1, keepdims=True)
        o = (gates[:, 3 * h:3 * h + 1] * oc_scr[h] + gates[:, 3 * h + 1:3 * h + 2] * os_scr[h]
             + gates[:, 3 * h + 2:3 * h + 3] * o_w)
        out_ref[0, :, LANES * h:LANES * (h + 1)] = jnp.where(lane // HEAD_DIM == g, o, 0.0).astype(BF16)


def _imp_matrix(nc, width):
    n = np.arange(nc)[:, None]
    b = np.arange(width)[None, :]
    sub = SEL_BLOCK // CMP_STRIDE
    return jnp.asarray((n // sub == b).astype(np.float32) + ((n + 1) // sub == b), BF16)


def _expand_tiles(n_tiles, n_blk, tk):
    key = np.arange(n_tiles * tk).reshape(n_tiles, 1, tk)
    return jnp.asarray(key // SEL_BLOCK == np.arange(n_blk)[None, :, None], BF16)


def _nsa_prompt(qn, ckv, kb_sel, kb_win, gates, *, batch, seq, tq):
    nq = seq // tq
    nc = ckv.shape[1]
    n_sel = seq // SEL_BLOCK
    assert n_sel <= LANES and WINDOW % tq == 0 and seq >= WINDOW + tq
    amat = _imp_matrix(nc, -(-n_sel // 8) * 8).T
    etile = _expand_tiles(nq, LANES, tq)
    m = batch * seq
    return pl.pallas_call(
        functools.partial(_nsa_prompt_kernel, n_sel=n_sel),
        grid=(batch, NSA_GROUPS, nq),
        in_specs=[
            pl.BlockSpec((1, NSA_HPG, tq, LANES), lambda b, g, i: (g, 0, b * nq + i, 0)),
            pl.BlockSpec((1, nc, KV_W), lambda b, g, i: (b, 0, 0)),
            pl.BlockSpec((1, seq, KV_W), lambda b, g, i: (b, 0, 0)),
            pl.BlockSpec((1, seq, KV_W), lambda b, g, i: (b, 0, 0)),
            pl.BlockSpec((1, tq, LANES), lambda b, g, i: (g, b * nq + i, 0)),
            pl.BlockSpec(amat.shape, lambda b, g, i: (0, 0)),
            pl.BlockSpec(etile.shape, lambda b, g, i: (0, 0, 0)),
        ],
        out_specs=pl.BlockSpec((1, tq, NSA_W), lambda b, g, i: (g, b * nq + i, 0)),
        out_shape=jax.ShapeDtypeStruct((NSA_GROUPS, m, NSA_W), BF16),
        scratch_shapes=[pltpu.VMEM((NSA_HPG, tq, LANES), F32)] * 5,
        compiler_params=_cp("arbitrary", "arbitrary", "arbitrary"), name="nsa_prompt",
    )(qn, ckv, kb_sel.reshape(batch, seq, KV_W), kb_win.reshape(batch, seq, KV_W), gates, amat, etile)


def _upper_strict(n):
    return jnp.asarray(np.triu(np.ones((n, n), np.float32), 1).T, BF16)


def _stack_heads(v):
    lane = _iota(v.shape, 1)
    return jnp.concatenate([jnp.where(lane // HEAD_DIM == h, v, jnp.zeros_like(v)) for h in range(SB_HEADS)], axis=0)


def _head_lanes(parts):
    lane = _iota(parts[0].shape, 1)
    return jnp.concatenate([jnp.where(lane < HEAD_DIM, parts[0], parts[1]),
                            jnp.where(lane < HEAD_DIM, parts[2], parts[3])], axis=1)


def _sb_prompt_kernel(q_ref, kv_ref, u_ref, out_ref, acc_scr, r_scr):
    i = pl.program_id(1)
    t = q_ref.shape[1]
    rowi, coli = _iota((t, t), 0), _iota((t, t), 1)
    acc_scr[...] = jnp.zeros_like(acc_scr)
    r_scr[...] = jnp.zeros_like(r_scr)

    def tile(j, diag):
        kt = kv_ref[0, pl.ds(pl.multiple_of(j * t, t), t), :]
        ws = []
        for h in range(SB_HEADS):
            z = _dot_nt(q_ref[h], kt[:, 0:SB_W])
            sp = _softplus2(z)
            cost = jnp.where(coli < rowi, sp, 0.0) if diag else sp
            r = r_scr[h]
            w = jnp.exp2(z - sp - _split_dot(cost, u_ref[...], 1) - _rep(r, t // LANES))
            if diag:
                w = jnp.where(coli < rowi, w, 0.0)
            ws.append(w.astype(BF16))
            r_scr[h] = r + jnp.sum(cost, axis=1, keepdims=True)
        acc_scr[...] += _dot(jnp.concatenate(ws, axis=1), _stack_heads(kt[:, SB_W:2 * SB_W]))

    tile(i, True)
    lax.fori_loop(0, i, lambda n, c: (tile(i - 1 - n, False), c)[1], 0)
    out_ref[...] = acc_scr[...].astype(BF16)


def _sb_prompt(q, kb, *, batch, seq, tq):
    nq = seq // tq
    u = _upper_strict(tq)
    return pl.pallas_call(
        _sb_prompt_kernel, grid=(batch, nq),
        in_specs=[pl.BlockSpec((SB_HEADS, tq, SB_W), lambda b, i: (0, b * nq + i, 0)),
                  pl.BlockSpec((1, seq, 2 * SB_W), lambda b, i: (b, 0, 0)),
                  pl.BlockSpec(u.shape, lambda b, i: (0, 0))],
        out_specs=pl.BlockSpec((tq, SB_W), lambda b, i: (b * nq + i, 0)),
        out_shape=jax.ShapeDtypeStruct((batch * seq, SB_W), BF16),
        scratch_shapes=[pltpu.VMEM((tq, SB_W), F32), pltpu.VMEM((SB_HEADS, tq, LANES), F32)],
        compiler_params=_cp("arbitrary", "arbitrary"), name="sb_prompt",
    )(q, kb.reshape(batch, seq, 2 * SB_W), u)


def _fox_prompt_kernel(q_ref, kv_ref, cq_ref, ck_ref, out_ref, m_scr, l_scr, acc_scr, cq_scr):
    i = pl.program_id(1)
    t = q_ref.shape[1]
    rowi, coli = _iota((t, t), 0), _iota((t, t), 1)
    m_scr[...] = jnp.full_like(m_scr, NEG)
    l_scr[...] = jnp.zeros_like(l_scr)
    acc_scr[...] = jnp.zeros_like(acc_scr)
    for h in range(FOX_HEADS):
        cq_scr[h] = jnp.broadcast_to(cq_ref[h] * LOG2E, (t, LANES))

    def tile(j, diag):
        kt = kv_ref[0, pl.ds(pl.multiple_of(j * t, t), t), :]
        es, alphas = [], []
        for h in range(FOX_HEADS):
            s = _dot_nt(q_ref[h], kt[:, 0:SB_W]) + (_rep(cq_scr[h], t // LANES) - ck_ref[0, h, j] * LOG2E)
            if diag:
                s = jnp.where(coli <= rowi, s, NEG)
            m_prev = m_scr[h]
            m_next = jnp.maximum(m_prev, jnp.max(s, axis=1, keepdims=True))
            alpha = jnp.exp2(m_prev - m_next)
            e = jnp.exp2(s - _rep(m_next, t // LANES))
            l_scr[h] = alpha * l_scr[h] + jnp.sum(e, axis=1, keepdims=True)
            m_scr[h] = m_next
            es.append(e.astype(BF16))
            alphas.append(alpha)
        acc_scr[...] = acc_scr[...] * _head_lanes(alphas) + _dot(jnp.concatenate(es, axis=1),
                                                                 _stack_heads(kt[:, SB_W:2 * SB_W]))

    lax.fori_loop(0, i, lambda j, c: (tile(j, False), c)[1], 0)
    tile(i, True)
    out_ref[...] = (acc_scr[...] / _head_lanes([jnp.maximum(l_scr[h], 1e-30) for h in range(FOX_HEADS)])).astype(BF16)


def _fox_prompt(q, kb, cum_q, cum_k, *, batch, seq, tq):
    nq = seq // tq
    nk = seq // tq
    return pl.pallas_call(
        _fox_prompt_kernel, grid=(batch, nq),
        in_specs=[pl.BlockSpec((FOX_HEADS, tq, SB_W), lambda b, i: (0, b * nq + i, 0)),
                  pl.BlockSpec((1, seq, 2 * SB_W), lambda b, i: (b, 0, 0)),
                  pl.BlockSpec((FOX_HEADS, tq, 1), lambda b, i: (0, b * nq + i, 0)),
                  pl.BlockSpec((1, FOX_HEADS, nk, 1, tq), lambda b, i: (b, 0, 0, 0, 0))],
        out_specs=pl.BlockSpec((tq, SB_W), lambda b, i: (b * nq + i, 0)),
        out_shape=jax.ShapeDtypeStruct((batch * seq, SB_W), BF16),
        scratch_shapes=[pltpu.VMEM((FOX_HEADS, tq, LANES), F32), pltpu.VMEM((FOX_HEADS, tq, LANES), F32),
                        pltpu.VMEM((tq, SB_W), F32), pltpu.VMEM((FOX_HEADS, tq, LANES), F32)],
        compiler_params=_cp("arbitrary", "arbitrary"), name="fox_prompt",
    )(q, kb.reshape(batch, seq, 2 * SB_W), cum_q, cum_k)


def _merge_kernel(x_ref, g_ref, oa_ref, ob_ref, oc_ref, wm_ref, wa_ref, wb_ref, wc_ref, wo_ref, out_ref):
    x = x_ref[...]
    d = x.shape[1]
    hb = _rms(x, g_ref[...]).astype(BF16)
    oa = oa_ref[0] + oa_ref[1]
    u = jax.nn.sigmoid(_dot(hb, wm_ref[:, 0:d])) * _dot(oa, wa_ref[...])
    u = u + jax.nn.sigmoid(_dot(hb, wm_ref[:, d:2 * d])) * _dot(ob_ref[...], wb_ref[...])
    u = u + jax.nn.sigmoid(_dot(hb, wm_ref[:, 2 * d:3 * d])) * _dot(oc_ref[...], wc_ref[...])
    out_ref[...] = x + _dot(u.astype(BF16), wo_ref[...])


def _merge(x, gain, o_nsa, o_sb, o_fox, w_merge, w_a, w_b, w_c, w_o, *, tm):
    m, d = x.shape
    row = lambda wid: pl.BlockSpec((tm, wid), lambda i: (i, 0))
    const = lambda a: pl.BlockSpec(a.shape, lambda i: (0,) * a.ndim)
    return pl.pallas_call(
        _merge_kernel, grid=(m // tm,),
        in_specs=[row(d), const(gain), pl.BlockSpec((NSA_GROUPS, tm, NSA_W), lambda i: (0, i, 0)),
                  row(SB_W), row(SB_W), const(w_merge), const(w_a), const(w_b), const(w_c), const(w_o)],
        out_specs=row(d), out_shape=jax.ShapeDtypeStruct((m, d), F32),
        compiler_params=_cp("arbitrary"), name="merge",
    )(x, gain, o_nsa, o_sb, o_fox, w_merge, w_a, w_b, w_c, w_o)


def _ffn_kernel(x_ref, g_ref, wg_ref, wu_ref, wd_ref, gf_ref, out_ref, *, final_norm):
    x = x_ref[...]
    hb = _rms(x, g_ref[...]).astype(BF16)
    a = _dot(hb, wg_ref[...])
    act = a * jax.nn.sigmoid(a) * _dot(hb, wu_ref[...])
    y = x + _dot(act.astype(BF16), wd_ref[...])
    out_ref[...] = _rms(y, gf_ref[...]) if final_norm else y


def _ffn(x, gain, w_gate, w_up, w_down, final_gain, *, tm, final_norm):
    m, d = x.shape
    row = pl.BlockSpec((tm, d), lambda i: (i, 0))
    vec = pl.BlockSpec((1, d), lambda i: (0, 0))
    const = lambda a: pl.BlockSpec(a.shape, lambda i: (0,) * a.ndim, pipeline_mode=pl.Buffered(1))
    return pl.pallas_call(
        functools.partial(_ffn_kernel, final_norm=final_norm), grid=(m // tm,),
        in_specs=[row, vec, const(w_gate), const(w_up), const(w_down), vec],
        out_specs=row, out_shape=jax.ShapeDtypeStruct((m, d), F32),
        compiler_params=_cp("arbitrary"), name="ffn",
    )(x, gain, w_gate, w_up, w_down, final_gain)


def _page_specs(rows, n_pages, pp, descending):
    def spec(i):
        if descending:
            return pl.BlockSpec((1, rows, PAGE), lambda b, j, pt: (pt[b, n_pages - 1 - (j * pp + i)], 0, 0))
        return pl.BlockSpec((1, rows, PAGE), lambda b, j, pt: (pt[b, j * pp + i], 0, 0))
    return [spec(i) for i in range(pp)]


def _lanes(parts):
    return jnp.concatenate(parts, axis=1)


def _sb_dec_kernel(pt_ref, q_ref, *refs, pp):
    pages, (u_ref, out_ref, acc_scr, r_scr) = refs[:pp], refs[pp:]
    j = pl.program_id(1)

    @pl.when(j == 0)
    def _():
        acc_scr[...] = jnp.zeros_like(acc_scr)
        r_scr[...] = jnp.zeros_like(r_scr)

    q = q_ref[0]
    z_all = _dot(q, _lanes([p[0, 0:SB_W, :].astype(BF16) for p in pages]))
    zs = [z_all[:, i * PAGE:(i + 1) * PAGE] for i in range(pp)]
    sps = [_softplus2(z) for z in zs]
    afters = [_split_dot(sp, u_ref[...], 2) for sp in sps]
    r = r_scr[...]
    ws = []
    for i in range(pp):
        ws.append(jnp.exp2(zs[i] - sps[i] - afters[i] - r).astype(BF16))
        r = r + jnp.sum(sps[i], axis=1, keepdims=True)
    acc_scr[...] += _dot_nt(_lanes(ws), _lanes([p[0, SB_W:2 * SB_W, :].astype(BF16) for p in pages]))
    r_scr[...] = r

    @pl.when(j == pl.num_programs(1) - 1)
    def _():
        out_ref[0] = acc_scr[...]


def _sb_dec(q8, pool, page_table, *, pp):
    bd, n_pages = page_table.shape
    u = _upper_strict(PAGE)
    grid_spec = pltpu.PrefetchScalarGridSpec(
        num_scalar_prefetch=1, grid=(bd, n_pages // pp),
        in_specs=[pl.BlockSpec((1, 8, SB_W), lambda b, j, pt: (b, 0, 0))]
        + _page_specs(2 * SB_W, n_pages, pp, True) + [pl.BlockSpec(u.shape, lambda b, j, pt: (0, 0))],
        out_specs=pl.BlockSpec((1, 8, SB_W), lambda b, j, pt: (b, 0, 0)),
        scratch_shapes=[pltpu.VMEM((8, SB_W), F32), pltpu.VMEM((8, LANES), F32)])
    return pl.pallas_call(
        functools.partial(_sb_dec_kernel, pp=pp), grid_spec=grid_spec,
        out_shape=jax.ShapeDtypeStruct((bd, 8, SB_W), F32),
        compiler_params=_cp("arbitrary", "arbitrary"), name="sb_decode",
    )(page_table, q8, *([pool] * pp), u)


def _fox_dec_kernel(pt_ref, q_ref, kn_ref, vn_ref, lfn_ref, *refs, pp):
    pages, lf_pages = refs[:pp], refs[pp:2 * pp]
    u_ref, out_ref, m_scr, l_scr, acc_scr, r_scr = refs[2 * pp:]
    j = pl.program_id(1)
    q = q_ref[0]
    reps = SB_W // LANES

    @pl.when(j == 0)
    def _():
        kn = kn_ref[0].astype(BF16).astype(F32)
        m_scr[...] = jnp.broadcast_to(jnp.sum(q.astype(F32) * kn, axis=1, keepdims=True), m_scr.shape)
        l_scr[...] = jnp.ones_like(l_scr)
        acc_scr[...] = jnp.broadcast_to(vn_ref[0].astype(BF16).astype(F32), acc_scr.shape)
        r_scr[...] = lfn_ref[0]

    z_all = _dot(q, _lanes([p[0, 0:SB_W, :].astype(BF16) for p in pages]))
    pad = jnp.zeros((8 - FOX_HEADS, PAGE), F32)
    r = r_scr[...]
    ss = []
    for i in range(pp):
        lf = jnp.concatenate([lf_pages[i][0], pad], axis=0)
        ss.append(z_all[:, i * PAGE:(i + 1) * PAGE] + LOG2E * (_split_dot(lf, u_ref[...], 3) + r))
        r = r + jnp.sum(lf, axis=1, keepdims=True)
    r_scr[...] = r
    m_prev = m_scr[...]
    m_next = m_prev
    for s in ss:
        m_next = jnp.maximum(m_next, jnp.max(s, axis=1, keepdims=True))
    alpha = jnp.exp2(m_prev - m_next)
    es = [jnp.exp2(s - m_next) for s in ss]
    l_new = alpha * l_scr[...]
    for e in es:
        l_new = l_new + jnp.sum(e, axis=1, keepdims=True)
    l_scr[...] = l_new
    m_scr[...] = m_next
    acc_scr[...] = acc_scr[...] * _rep(alpha, reps) + _dot_nt(
        _lanes([e.astype(BF16) for e in es]), _lanes([p[0, SB_W:2 * SB_W, :].astype(BF16) for p in pages]))

    @pl.when(j == pl.num_programs(1) - 1)
    def _():
        out_ref[0] = acc_scr[...] / _rep(l_scr[...], reps)


def _fox_dec(q8, k_new, v_new, lf_new, pool, lf_pool, page_table, *, pp):
    bd, n_pages = page_table.shape
    u = _upper_strict(PAGE)
    req = lambda shape: pl.BlockSpec((1,) + shape, lambda b, j, pt: (b,) + (0,) * len(shape))
    lf_spec = lambda i: pl.BlockSpec((1, FOX_HEADS, PAGE), lambda b, j, pt: (pt[b, n_pages - 1 - (j * pp + i)], 0, 0))
    grid_spec = pltpu.PrefetchScalarGridSpec(
        num_scalar_prefetch=1, grid=(bd, n_pages // pp),
        in_specs=[req((8, SB_W)), req((1, SB_W)), req((1, SB_W)), req((8, LANES))]
        + _page_specs(2 * SB_W, n_pages, pp, True) + [lf_spec(i) for i in range(pp)]
        + [pl.BlockSpec(u.shape, lambda b, j, pt: (0, 0))],
        out_specs=req((8, SB_W)),
        scratch_shapes=[pltpu.VMEM((8, LANES), F32), pltpu.VMEM((8, LANES), F32),
                        pltpu.VMEM((8, SB_W), F32), pltpu.VMEM((8, LANES), F32)])
    return pl.pallas_call(
        functools.partial(_fox_dec_kernel, pp=pp), grid_spec=grid_spec,
        out_shape=jax.ShapeDtypeStruct((bd, 8, SB_W), F32),
        compiler_params=_cp("arbitrary", "arbitrary"), name="fox_decode",
    )(page_table, q8, k_new, v_new, lf_new, *([pool] * pp), *([lf_pool] * pp), u)


def _nsa_dec_kernel(pt_ref, q_ref, ckv_ref, gate_ref, win_ref, kvs_new_ref, kvw_new_ref, kvw_col_ref, amat_ref, e_ref,
                    *refs, pp, past_len, n_sel):
    pages, (out_ref, win_out_ref, m_scr, l_scr, acc_scr, sel_scr, oc_scr, ow_scr) = refs[:pp], refs[pp:]
    j = pl.program_id(1)
    q = q_ref[0]
    rows = q.shape[0]
    row = _iota((rows, LANES), 0)
    row1 = _iota((rows, 1), 0)
    lane = _iota((rows, LANES), 1)
    pos = jnp.full((rows, 1), past_len, I32)

    def new_token(kv_row):
        k = kv_row[:, 0:LANES].astype(BF16).astype(F32)
        return jnp.sum(q.astype(F32) * k, axis=1, keepdims=True), kv_row[:, LANES:2 * LANES].astype(BF16).astype(F32)

    @pl.when(j == 0)
    def _():
        nc = ckv_ref.shape[1]
        cmask = _iota((rows, nc), 1) * CMP_STRIDE + (CMP_BLOCK - 1) <= pos
        p = _masked_softmax_rows(_dot_nt(q, ckv_ref[0, :, 0:128]), cmask)
        oc_scr[...] = _dot(p.astype(BF16), ckv_ref[0, :, 128:256])
        g0 = jnp.sum(p[0:NSA_HPG], axis=0, keepdims=True)
        g1 = jnp.sum(p[NSA_HPG:2 * NSA_HPG], axis=0, keepdims=True)
        p_grp = jnp.where(row1 < NSA_HPG, g0, g1)
        imp = _split_dot(p_grp, amat_ref[...], 2)
        sel_scr[...] = _select_blocks(imp, pos, n_sel)

        wk = win_ref[0]
        n_w = wk.shape[1]
        s = _dot(q, wk[0:LANES, :].astype(BF16))
        kpos = past_len - n_w + _iota((rows, n_w), 1)
        keep = jnp.where(kpos >= 0, jnp.where(pos - kpos <= WINDOW, 1.0, 0.0), 0.0) > 0.5
        s = jnp.where(keep, s, NEG)
        s_new, v_new = new_token(kvw_new_ref[0])
        m = jnp.maximum(s_new, jnp.max(s, axis=1, keepdims=True))
        e = jnp.exp2(s - m)
        e_new = jnp.exp2(s_new - m)
        o_w = _dot_nt(e.astype(BF16), wk[LANES:2 * LANES, :].astype(BF16)) + e_new * v_new
        ow_scr[...] = o_w / (jnp.sum(e, axis=1, keepdims=True) + e_new)

        shifted = pltpu.roll(wk, n_w - 1, 1)
        win_out_ref[0] = jnp.where(_iota(wk.shape, 1) == n_w - 1, kvw_col_ref[0], shifted)
        _online_init(m_scr, l_scr, acc_scr)

    sel = sel_scr[...].astype(BF16)
    z_all = _dot(q, _lanes([p[0, 0:LANES, :].astype(BF16) for p in pages]))
    ss = []
    for i in range(pp):
        chosen = _dot(sel, e_ref[j * pp + i]) > 0.5
        ss.append(jnp.where(chosen, z_all[:, i * PAGE:(i + 1) * PAGE], NEG))
    m_prev = m_scr[...]
    m_next = m_prev
    for s in ss:
        m_next = jnp.maximum(m_next, jnp.max(s, axis=1, keepdims=True))
    alpha = jnp.exp2(m_prev - m_next)
    es = [jnp.exp2(s - m_next) for s in ss]
    l_new = alpha * l_scr[...]
    for e in es:
        l_new = l_new + jnp.sum(e, axis=1, keepdims=True)
    l_scr[...] = l_new
    m_scr[...] = m_next
    acc_scr[...] = acc_scr[...] * alpha + _dot_nt(_lanes([e.astype(BF16) for e in es]),
                                                  _lanes([p[0, LANES:2 * LANES, :].astype(BF16) for p in pages]))

    @pl.when(j == pl.num_programs(1) - 1)
    def _():
        s_new, v_new = new_token(kvs_new_ref[0])
        cur = past_len // SEL_BLOCK
        chosen = jnp.sum(jnp.where(_iota(sel_scr.shape, 1) == cur, sel_scr[...], 0.0), axis=1, keepdims=True)
        s_new = jnp.where(chosen > 0.5, s_new, NEG)
        m_prev = m_scr[...]
        m_next = jnp.maximum(m_prev, s_new)
        alpha = jnp.exp2(m_prev - m_next)
        e_new = jnp.exp2(s_new - m_next)
        o_s = (acc_scr[...] * alpha + e_new * v_new) / jnp.maximum(alpha * l_scr[...] + e_new, 1e-30)
        gates = jnp.where(row < NSA_HPG, gate_ref[0, 0:1, :], gate_ref[0, 1:2, :])
        hh = row % NSA_HPG

        def gate(c):
            return jnp.sum(jnp.where(lane == 3 * hh + c, gates, 0.0), axis=1, keepdims=True)

        out_ref[0] = gate(0) * oc_scr[...] + gate(1) * o_s + gate(2) * ow_scr[...]


def _nsa_dec(q8, ckv, gates, win_state, kvs_new, kvw_new, pool, page_table, *, pp, layer):
    bd, n_pages = page_table.shape
    past_len = n_pages * PAGE
    n_sel = -(-(past_len + 1) // SEL_BLOCK)
    nc = ckv.shape[1]
    wid = -(-n_sel // LANES) * LANES
    amat = _imp_matrix(nc, wid)
    etile = _expand_tiles(n_pages, wid, PAGE)
    n_w = win_state.shape[2]
    req = lambda shape: pl.BlockSpec((1,) + shape, lambda b, j, pt: (b,) + (0,) * len(shape))
    const = lambda a: pl.BlockSpec(a.shape, lambda b, j, pt: (0,) * a.ndim)
    win_spec = pl.BlockSpec((1, KV_W, n_w), lambda b, j, pt: (layer * bd + b, 0, 0))
    grid_spec = pltpu.PrefetchScalarGridSpec(
        num_scalar_prefetch=1, grid=(bd, n_pages // pp),
        in_specs=[req((8, LANES)), req((nc, KV_W)), req((NSA_GROUPS, LANES)), win_spec,
                  req((1, KV_W)), req((1, KV_W)), req((KV_W, 1)), const(amat), const(etile)]
        + _page_specs(KV_W, n_pages, pp, False),
        out_specs=(req((8, LANES)), req((KV_W, n_w))),
        scratch_shapes=[pltpu.VMEM((8, LANES), F32)] * 3 + [pltpu.VMEM((8, wid), F32)] + [pltpu.VMEM((8, LANES), F32)] * 2)
    return pl.pallas_call(
        functools.partial(_nsa_dec_kernel, pp=pp, past_len=past_len, n_sel=n_sel), grid_spec=grid_spec,
        out_shape=(jax.ShapeDtypeStruct((bd, 8, LANES), F32), jax.ShapeDtypeStruct((bd, KV_W, n_w), F32)),
        compiler_params=_cp("arbitrary", "arbitrary"), name="nsa_decode",
    )(page_table, q8, ckv, gates, win_state, kvs_new, kvw_new, kvw_new.reshape(bd, KV_W, 1), amat, etile,
      *([pool] * pp))


def _rope_table(pos):
    half = ROT_DIM // 2
    inv = ROPE_THETA ** (-jnp.arange(half, dtype=F32) / half)
    ang = pos.astype(F32)[:, None] * inv[None, :]
    cos, sin = jnp.cos(ang), jnp.sin(ang)
    n = pos.shape[0]
    pad = HEAD_DIM - ROT_DIM
    one_head = lambda a, b, fill: jnp.concatenate([a, b, jnp.full((n, pad), fill, F32)], axis=1)
    zero = jnp.zeros_like(sin)
    tabs = [one_head(cos, cos, 1.0), one_head(-sin, zero, 0.0), one_head(zero, sin, 0.0)]
    return jnp.concatenate([jnp.tile(t, (1, LANES // HEAD_DIM)) for t in tabs], axis=1)


_Q_HEAD_ORDER = (0, 4, 1, 5, 2, 6, 3, 7)


def _layer_weights(l, w_in, b_forget, cmp_pos, w_cmp_k, w_cmp_v, w_up_nsa):
    w = w_in[l]
    d = w.shape[0]
    gate0 = NSA_W + 3 * KV_W
    pf0 = gate0 + GATE_W + 6 * SB_W
    merge0 = pf0 + FOX_HEADS
    cols = [w[:, HEAD_DIM * h:HEAD_DIM * (h + 1)] for h in _Q_HEAD_ORDER]
    cols += [w[:, NSA_W:gate0], w[:, gate0 + GATE_W:pf0], w[:, gate0:gate0 + GATE_W], w[:, pf0:merge0]]
    cols.append(jnp.zeros((d, N_PROJ - (merge0 - 0)), F32))
    w_proj = jnp.concatenate(cols, axis=1).astype(BF16)
    w_merge = w[:, merge0:].astype(BF16)
    zpad = lambda n: jnp.zeros((1, n), F32)
    bias = jnp.concatenate([zpad(LOGF_LANE), b_forget[l].reshape(1, FOX_HEADS), zpad(LANES - LOGF_LANE - FOX_HEADS)], axis=1)
    wk = w_cmp_k[l].reshape(CMP_BLOCK, HEAD_DIM, HEAD_DIM)
    wv = w_cmp_v[l].reshape(CMP_BLOCK, HEAD_DIM, HEAD_DIM)
    zero = jnp.zeros_like(wk)
    w_blk = jnp.concatenate([jnp.concatenate([wsrc if c == n else zero for c in range(4)], axis=2)
                             for n, wsrc in enumerate((wk, wk, wv, wv))], axis=1)
    w_cat = jnp.concatenate([w_blk[0:CMP_BLOCK // 2], w_blk[CMP_BLOCK // 2:]], axis=2).astype(BF16)
    pe4 = jnp.tile(cmp_pos[l], (1, KV_W // HEAD_DIM))
    w_a = jnp.concatenate([w_up_nsa[l][HEAD_DIM * h:HEAD_DIM * (h + 1)] for h in _Q_HEAD_ORDER], axis=0).astype(BF16)
    return w_proj, w_merge, bias, w_cat, pe4, w_a


def _pad_heads(a):
    return jnp.concatenate([a, jnp.zeros((a.shape[0], 8 - a.shape[1], a.shape[2]), a.dtype)], axis=1)


def _token_lanes(cache):
    l, p, t = cache.shape[:3]
    return cache.transpose(0, 1, 3, 4, 5, 2).reshape(l * p, -1, t)


def _own_lanes(o8, n_heads):
    n = o8.shape[0]
    o = o8[:, 0:n_heads].reshape(n, n_heads, n_heads, HEAD_DIM)
    return jnp.stack([o[:, h, h] for h in range(n_heads)], axis=1).reshape(n, n_heads * HEAD_DIM)


def kernel(x_prompt, x_sample, cache_nsa_cmp_kv, cache_nsa_sel_kv, cache_sb_kv, cache_fox_kv, cache_fox_logf,
           state_nsa_win_kv, page_table, norm_mix_g, norm_ffn_g, norm_final_g, w_in, b_forget, cmp_pos,
           w_cmp_k, w_cmp_v, w_up_nsa, w_up_sb, w_up_fox, w_out, w_ffn_gate, w_ffn_up, w_ffn_down):
    batch, seq, d = x_prompt.shape
    bd = x_sample.shape[0]
    depth = w_in.shape[0]
    n_pool = cache_sb_kv.shape[1]
    n_pages = page_table.shape[1]
    past_len = n_pages * PAGE
    m = batch * seq
    tm = min(256, seq)
    tq = min(256, seq)
    tq_soft = min(512, seq)
    pp = min(PAGES_PER_STEP, n_pages)
    pp_prompt = min(PAGES_PER_STEP, seq // PAGE)

    rope_p = _rope_table(jnp.arange(seq, dtype=I32))
    rope_s = _rope_table(jnp.full((bd,), past_len, I32))
    ident_pt = jnp.arange(batch * (seq // PAGE), dtype=I32).reshape(batch, seq // PAGE)
    pool_cmp, pool_sel = _token_lanes(cache_nsa_cmp_kv), _token_lanes(cache_nsa_sel_kv)
    pool_sb, pool_fox = _token_lanes(cache_sb_kv), _token_lanes(cache_fox_kv)
    pool_lf = cache_fox_logf.astype(F32).transpose(0, 1, 3, 2).reshape(depth * n_pool, FOX_HEADS, PAGE)
    win_state = _token_lanes(state_nsa_win_kv)

    xp = x_prompt.reshape(m, d)
    xs = x_sample.reshape(bd, d)
    g_fin = norm_final_g.reshape(1, d)
    p_states, s_states = [], []
    for l in range(depth):
        w_proj, w_merge, bias, w_blk, pe4, w_a = _layer_weights(l, w_in, b_forget, cmp_pos, w_cmp_k, w_cmp_v, w_up_nsa)
        w_b, w_c, w_o = w_up_sb[l].astype(BF16), w_up_fox[l].astype(BF16), w_out[l].astype(BF16)
        w_g, w_u, w_d = w_ffn_gate[l].astype(BF16), w_ffn_up[l].astype(BF16), w_ffn_down[l].astype(BF16)
        g_mix, g_ffn = norm_mix_g[l].reshape(1, d), norm_ffn_g[l].reshape(1, d)

        (qn, kv_cmp, kv_sel, kv_win, kv_sb, kv_fox, kb_sel, kb_win, kb_sb, kb_fox, q_sb, q_fox, gates, logf, cum
         ) = _proj(xp, g_mix, w_proj, rope_p, bias, tm=tm, tiles_per_seq=seq // tm, feature_states=True)
        ckv = _compress(kv_cmp, ident_pt, w_blk, pe4, pp=pp_prompt, paged=False)
        o_nsa = _nsa_prompt(qn, ckv, kb_sel, kb_win, gates, batch=batch, seq=seq, tq=tq_soft)
        o_sb = _sb_prompt(q_sb, kb_sb, batch=batch, seq=seq, tq=tq)
        cum4 = cum[:, LOGF_LANE:LOGF_LANE + FOX_HEADS]
        cum_q = cum4.T[:, :, None]
        cum_k = cum4.reshape(batch, seq, FOX_HEADS).transpose(0, 2, 1).reshape(
            batch, FOX_HEADS, seq // tq_soft, 1, tq_soft)
        o_fox = _fox_prompt(q_fox, kb_fox, cum_q, cum_k, batch=batch, seq=seq, tq=tq_soft)
        xp = _merge(xp, g_mix, o_nsa, o_sb, o_fox, w_merge, w_a, w_b, w_c, w_o, tm=tm)
        xp = _ffn(xp, g_ffn, w_g, w_u, w_d, g_fin, tm=tm, final_norm=(l == depth - 1))
        win_keep = min(WINDOW, seq)
        p_states.append((kv_cmp, kv_sel, kv_sb, kv_fox, logf, kv_win[:, :, seq - win_keep:]))

        (qn, kv_cmp, kv_sel, kv_win, kv_sb, kv_fox, _, _, _, _, q_sb, q_fox, gates, logf, _
         ) = _proj(xs, g_mix, w_proj, rope_s, bias, tm=bd, tiles_per_seq=1, feature_states=False)
        pt_l = page_table + l * n_pool
        ckv = _compress(pool_cmp, pt_l, w_blk, pe4, pp=pp, paged=True)
        q8 = qn.transpose(2, 0, 1, 3).reshape(bd, NSA_HEADS, LANES)
        o_nsa8, win_new = _nsa_dec(q8, ckv, gates.transpose(1, 0, 2), win_state, kv_sel.reshape(bd, 1, KV_W),
                                   kv_win.reshape(bd, 1, KV_W), pool_sel, pt_l, pp=pp, layer=l)
        o_sb8 = _sb_dec(_pad_heads(q_sb.transpose(1, 0, 2)), pool_sb, pt_l, pp=pp)
        lf4 = logf[:, LOGF_LANE:LOGF_LANE + FOX_HEADS]
        lf_new = _pad_heads(jnp.broadcast_to(lf4[:, :, None], (bd, FOX_HEADS, LANES)))
        o_fox8 = _fox_dec(_pad_heads(q_fox.transpose(1, 0, 2)), kv_fox[:, 0:SB_W].reshape(bd, 1, SB_W),
                          kv_fox[:, SB_W:2 * SB_W].reshape(bd, 1, SB_W), lf_new, pool_fox, pool_lf, pt_l, pp=pp)
        o_a = o_nsa8.reshape(bd, NSA_GROUPS, NSA_HPG, NSA_GROUPS, HEAD_DIM)
        o_a = jnp.stack([o_a[:, g, :, g] for g in range(NSA_GROUPS)], axis=2).reshape(bd, NSA_W)
        o_a = jnp.stack([o_a, jnp.zeros_like(o_a)], axis=0).astype(BF16)
        o_b = _own_lanes(o_sb8, SB_HEADS).astype(BF16)
        o_c = _own_lanes(o_fox8, FOX_HEADS).astype(BF16)
        xs = _merge(xs, g_mix, o_a, o_b, o_c, w_merge, w_a, w_b, w_c, w_o, tm=bd)
        xs = _ffn(xs, g_ffn, w_g, w_u, w_d, g_fin, tm=bd, final_norm=(l == depth - 1))
        s_states.append((
            kv_cmp.reshape(bd, 1, 2, NSA_GROUPS, HEAD_DIM), kv_sel.reshape(bd, 1, 2, NSA_GROUPS, HEAD_DIM),
            kv_sb.reshape(bd, 1, 2, SB_HEADS, HEAD_DIM), kv_fox.reshape(bd, 1, 2, FOX_HEADS, HEAD_DIM),
            lf4.reshape(bd, 1, FOX_HEADS),
            win_new.reshape(bd, 2, NSA_GROUPS, HEAD_DIM, -1).transpose(0, 4, 1, 2, 3)))

    y_prompt = xp.reshape(batch, seq, d)
    y_sample = xs.reshape(bd, 1, d)
    stk = lambda sts, i: jnp.stack([st[i] for st in sts], axis=0)

    def kv_state(i, heads):
        a = stk(p_states, i)
        return a.reshape(depth, batch, 2, heads, HEAD_DIM, a.shape[-1]).transpose(0, 1, 5, 2, 3, 4)

    p_out = (kv_state(0, NSA_GROUPS), kv_state(1, NSA_GROUPS), kv_state(2, SB_HEADS), kv_state(3, FOX_HEADS),
             stk(p_states, 4).transpose(0, 1, 3, 2), kv_state(5, NSA_GROUPS))
    return (y_prompt, y_sample) + p_out + tuple(stk(s_states, i) for i in range(6))
```

```python
import functools

import numpy as np
import jax
import jax.numpy as jnp
from jax import lax
from jax.experimental import pallas as pl
from jax.experimental.pallas import tpu as pltpu

F32, BF16, I32 = jnp.float32, jnp.bfloat16, jnp.int32

HEAD_DIM = 64
ROT_DIM = HEAD_DIM // 4
ROPE_THETA = 500000.0
NSA_HEADS = 8
NSA_GROUPS = 2
NSA_HPG = NSA_HEADS // NSA_GROUPS
CMP_STRIDE = 16
CMP_BLOCK = 32
SEL_BLOCK = 64
TOP_N = 16
WINDOW = 512
SB_HEADS = 4
FOX_HEADS = 4
PAGE = 128
SEL_FORCE_SCORE = 1.0e4
NEG = -1.0e30
RMS_EPS = 1e-6
SCALE = HEAD_DIM ** -0.5
LOG2E = 1.4426950408889634

LANES = 128
NSA_W = NSA_HEADS * HEAD_DIM
KV_W = 2 * NSA_GROUPS * HEAD_DIM
SB_W = SB_HEADS * HEAD_DIM
GATE_W = 3 * NSA_HEADS
C_Q, C_CMP, C_SEL, C_WIN, C_SB, C_FOX, C_MISC = 0, 512, 768, 1024, 1280, 2048, 2816
N_PROJ = C_MISC + LANES
LOGF_LANE = GATE_W
VMEM_LIMIT = 56 * 1024 * 1024
PAGES_PER_STEP = 32


def _cp(*sem):
    return pltpu.CompilerParams(dimension_semantics=sem, vmem_limit_bytes=VMEM_LIMIT)


def _dot(a, b):
    return jnp.dot(a, b, preferred_element_type=F32)


def _dot_nt(a, b):
    return lax.dot_general(a, b, (((1,), (1,)), ((), ())), preferred_element_type=F32)


def _split_dot(a, r, parts):
    out, rem = None, a
    for _ in range(parts):
        hi = rem.astype(BF16)
        d = _dot(hi, r)
        out = d if out is None else out + d
        rem = rem - hi.astype(F32)
    return out


def _split_dot_l(l, a, parts):
    out, rem = None, a
    for _ in range(parts):
        hi = rem.astype(BF16)
        d = _dot(l, hi)
        out = d if out is None else out + d
        rem = rem - hi.astype(F32)
    return out


def _rms(x, g):
    return x * lax.rsqrt(jnp.mean(x * x, axis=-1, keepdims=True) + RMS_EPS) * g


def _softplus2(z2):
    return jnp.maximum(z2, 0.0) + jnp.log2(1.0 + jnp.exp2(-jnp.abs(z2)))


def _iota(shape, axis):
    return lax.broadcasted_iota(I32, shape, axis)


def _proj_kernel(x_ref, g_ref, w_ref, rope_ref, b_ref, tri_ref,
                 qn_ref, kvc_ref, kvs_ref, kvw_ref, kvsb_ref, kvfx_ref,
                 kbs_ref, kbw_ref, kbsb_ref, kbfx_ref, qsb_ref, qfx_ref,
                 gate_ref, logf_ref, cum_ref, h_scr, carry_scr, *, tiles_per_seq, feature_states):
    i = pl.program_id(0)
    tm = x_ref.shape[0]
    h_scr[...] = _rms(x_ref[...], g_ref[...]).astype(BF16)
    hb = h_scr[...]
    cos, sin_lo, sin_hi = rope_ref[:, 0:128], rope_ref[:, 128:256], rope_ref[:, 256:384]

    def rope(seg):
        return seg * cos + pltpu.roll(seg, LANES - 8, 1) * sin_lo + pltpu.roll(seg, 8, 1) * sin_hi

    def put_state(f_ref, blocks):
        for n, blk in enumerate(blocks):
            if feature_states:
                f_ref[0, LANES * n:LANES * (n + 1), :] = blk.T
            else:
                f_ref[:, LANES * n:LANES * (n + 1)] = blk

    lane = _iota((tm, LANES), 1)
    pq = _dot_nt(hb, w_ref[C_Q:C_Q + NSA_W, :])
    for s in range(NSA_HPG):
        seg = rope(pq[:, LANES * s:LANES * (s + 1)]) * (SCALE * LOG2E)
        for g in range(NSA_GROUPS):
            qn_ref[g, s] = jnp.where(lane // HEAD_DIM == g, seg, 0.0).astype(BF16)

    for c0, f_ref, h_ref in ((C_CMP, kvc_ref, None), (C_SEL, kvs_ref, kbs_ref), (C_WIN, kvw_ref, kbw_ref)):
        p = _dot_nt(hb, w_ref[c0:c0 + KV_W, :])
        k, v = rope(p[:, 0:128]), p[:, 128:256]
        put_state(f_ref, [k, v])
        if h_ref is not None:
            h_ref[:, 0:128] = k.astype(BF16)
            h_ref[:, 128:256] = v.astype(BF16)

    lane_w = _iota((tm, SB_W), 1)
    for c0, q_ref, f_ref, h_ref in ((C_SB, qsb_ref, kvsb_ref, kbsb_ref), (C_FOX, qfx_ref, kvfx_ref, kbfx_ref)):
        p = _dot_nt(hb, w_ref[c0:c0 + 3 * SB_W, :])
        q = p[:, 0:SB_W] * (SCALE * LOG2E)
        for h in range(SB_HEADS):
            q_ref[h] = jnp.where(lane_w // HEAD_DIM == h, q, 0.0).astype(BF16)
        kv = p[:, SB_W:3 * SB_W]
        put_state(f_ref, [kv[:, LANES * n:LANES * (n + 1)] for n in range(2 * SB_W // LANES)])
        h_ref[...] = kv.astype(BF16)

    pm = _dot_nt(hb, w_ref[C_MISC:C_MISC + LANES, :])
    sg = jax.nn.sigmoid(pm)
    gate_ref[0] = sg
    gate_ref[1] = pltpu.roll(sg, LANES - GATE_W // 2, 1)
    zf = pm + b_ref[...]
    lf = jnp.minimum(zf, 0.0) - jnp.log1p(jnp.exp(-jnp.abs(zf)))
    if feature_states:
        logf_ref[0] = lf.T[LOGF_LANE:LOGF_LANE + FOX_HEADS, :]
    else:
        logf_ref[...] = lf

    @pl.when(i % tiles_per_seq == 0)
    def _():
        carry_scr[...] = jnp.zeros_like(carry_scr)

    c = _split_dot_l(tri_ref[...], lf, 3) + carry_scr[0:1, :]
    cum_ref[...] = c
    carry_scr[...] = jnp.broadcast_to(c[tm - 1:tm, :], carry_scr.shape)


def _proj(x, gain, w, rope_tab, bias, *, tm, tiles_per_seq, feature_states):
    m, d = x.shape
    tri = jnp.tril(jnp.ones((tm, tm), F32)).astype(BF16)
    row = lambda wid: pl.BlockSpec((tm, wid), lambda i: (i, 0))
    const = lambda shape: pl.BlockSpec(shape, lambda i: (0,) * len(shape))
    f32o = lambda wid: jax.ShapeDtypeStruct((m, wid), F32)
    b16o = lambda wid: jax.ShapeDtypeStruct((m, wid), BF16)
    if feature_states:
        nb, seq = m // (tm * tiles_per_seq), tm * tiles_per_seq
        st_shape = lambda wid: jax.ShapeDtypeStruct((nb, wid, seq), F32)
        st_spec = lambda wid: pl.BlockSpec((1, wid, tm), lambda i: (i // tiles_per_seq, 0, i % tiles_per_seq))
        lf_shape, lf_spec = st_shape(FOX_HEADS), st_spec(FOX_HEADS)
    else:
        st_shape, st_spec, lf_shape, lf_spec = f32o, row, f32o(LANES), row(LANES)
    out_shape = (
        jax.ShapeDtypeStruct((NSA_GROUPS, NSA_HPG, m, LANES), BF16),
        st_shape(KV_W), st_shape(KV_W), st_shape(KV_W), st_shape(2 * SB_W), st_shape(2 * SB_W),
        b16o(KV_W), b16o(KV_W), b16o(2 * SB_W), b16o(2 * SB_W),
        jax.ShapeDtypeStruct((SB_HEADS, m, SB_W), BF16), jax.ShapeDtypeStruct((SB_HEADS, m, SB_W), BF16),
        jax.ShapeDtypeStruct((NSA_GROUPS, m, LANES), F32), lf_shape, f32o(LANES),
    )
    out_specs = (
        pl.BlockSpec((NSA_GROUPS, NSA_HPG, tm, LANES), lambda i: (0, 0, i, 0)),
        st_spec(KV_W), st_spec(KV_W), st_spec(KV_W), st_spec(2 * SB_W), st_spec(2 * SB_W),
        row(KV_W), row(KV_W), row(2 * SB_W), row(2 * SB_W),
        pl.BlockSpec((SB_HEADS, tm, SB_W), lambda i: (0, i, 0)),
        pl.BlockSpec((SB_HEADS, tm, SB_W), lambda i: (0, i, 0)),
        pl.BlockSpec((NSA_GROUPS, tm, LANES), lambda i: (0, i, 0)),
        lf_spec, row(LANES),
    )
    return pl.pallas_call(
        functools.partial(_proj_kernel, tiles_per_seq=tiles_per_seq, feature_states=feature_states),
        grid=(m // tm,),
        in_specs=[row(d), const((1, d)), const((N_PROJ, d)),
                  pl.BlockSpec((tm, 3 * LANES), lambda i: (i % tiles_per_seq, 0)),
                  const((1, LANES)), const((tm, tm))],
        out_specs=out_specs, out_shape=out_shape,
        scratch_shapes=[pltpu.VMEM((tm, d), BF16), pltpu.VMEM((8, LANES), F32)],
        compiler_params=_cp("arbitrary"), name="proj",
    )(x, gain, w, rope_tab, bias, tri)


def _compress_kernel(pt_ref, *refs, pp, nch):
    pages, (perm_ref, w_ref, pe_ref, out_ref, x_scr, acc_scr, pos_scr) = refs[:pp], refs[pp:]
    j = pl.program_id(1)
    half = CMP_BLOCK // 2
    chunks = PAGE // CMP_STRIDE

    @pl.when((pl.program_id(0) == 0) & (j == 0))
    def _():
        c = jnp.zeros((8, KV_W), F32)
        for l in range(half):
            for off, cols in ((0, slice(0, KV_W)), (half, slice(KV_W, 2 * KV_W))):
                row = jnp.broadcast_to(pe_ref[off + l:off + l + 1, :], (8, KV_W))
                c = c + _split_dot(row, w_ref[l, :, cols], 2)
        pos_scr[...] = c

    for i in range(pp):
        gt = _dot_nt(perm_ref[...], pages[i][0].astype(BF16))
        base = pl.multiple_of((j * pp + i) * chunks, chunks)
        for l in range(half):
            x_scr[l, pl.ds(base, chunks), :] = gt[chunks * l:chunks * (l + 1), :]

    @pl.when(j == pl.num_programs(1) - 1)
    def _():
        acc_scr[...] = jnp.zeros_like(acc_scr)
        for l in range(half):
            acc_scr[...] += _dot(x_scr[l].astype(BF16), w_ref[l])
        nxt = pltpu.roll(acc_scr[:, KV_W:2 * KV_W], nch - 1, 0)
        nxt = jnp.where(_iota((nch, KV_W), 0) < nch - 1, nxt, 0.0)
        out_ref[0] = (acc_scr[:, 0:KV_W] + nxt + pos_scr[0:1, :]).astype(BF16)


def _compress(pool, page_table, w_cat, pe4, *, pp, paged):
    b, n_pages = page_table.shape
    chunks = PAGE // CMP_STRIDE
    nch = n_pages * chunks
    half = CMP_BLOCK // 2
    tok = np.arange(PAGE)
    perm = jnp.asarray(np.arange(PAGE)[:, None] == ((tok % CMP_STRIDE) * chunks + tok // CMP_STRIDE)[None, :], BF16)
    if paged:
        page_spec = lambda i: pl.BlockSpec((1, KV_W, PAGE), lambda bb, j, pt: (pt[bb, j * pp + i], 0, 0))
    else:
        page_spec = lambda i: pl.BlockSpec((1, KV_W, PAGE), lambda bb, j, pt: (bb, 0, j * pp + i))
    grid_spec = pltpu.PrefetchScalarGridSpec(
        num_scalar_prefetch=1, grid=(b, n_pages // pp),
        in_specs=[page_spec(i) for i in range(pp)] + [
            pl.BlockSpec((PAGE, PAGE), lambda bb, j, pt: (0, 0)),
            pl.BlockSpec((half, KV_W, 2 * KV_W), lambda bb, j, pt: (0, 0, 0)),
            pl.BlockSpec((CMP_BLOCK, KV_W), lambda bb, j, pt: (0, 0))],
        out_specs=pl.BlockSpec((1, nch, KV_W), lambda bb, j, pt: (bb, 0, 0)),
        scratch_shapes=[pltpu.VMEM((half, nch, KV_W), F32), pltpu.VMEM((nch, 2 * KV_W), F32),
                        pltpu.VMEM((8, KV_W), F32)])
    return pl.pallas_call(
        functools.partial(_compress_kernel, pp=pp, nch=nch), grid_spec=grid_spec,
        out_shape=jax.ShapeDtypeStruct((b, nch, KV_W), BF16),
        compiler_params=_cp("arbitrary", "arbitrary"), name="compress",
    )(page_table, *([pool] * pp), perm, w_cat, pe4)


def _masked_softmax_rows(s, mask):
    s = jnp.where(mask, s, NEG)
    e = jnp.where(mask, jnp.exp2(s - jnp.max(s, axis=1, keepdims=True)), 0.0)
    return e / jnp.maximum(jnp.sum(e, axis=1, keepdims=True), 1e-30)


def _online_init(m_scr, l_scr, acc_scr):
    m_scr[...] = jnp.full_like(m_scr, NEG)
    l_scr[...] = jnp.zeros_like(l_scr)
    acc_scr[...] = jnp.zeros_like(acc_scr)


def _select_blocks(imp, pos, n_sel):
    blk = _iota(imp.shape, 1)
    cur = pos // SEL_BLOCK
    valid = blk * SEL_BLOCK <= pos
    forced = jnp.where(blk == 0, 1.0, 0.0) + jnp.where(blk == cur, 1.0, 0.0) + jnp.where(blk == cur - 1, 1.0, 0.0)
    score = jnp.where(valid, jnp.where(forced > 0.5, SEL_FORCE_SCORE, imp), -1.0)
    score = jnp.where(blk < n_sel, score, -2.0)
    rank = jnp.zeros(imp.shape, F32)
    for b2 in range(n_sel):
        col = score[:, b2:b2 + 1]
        ge = jnp.where(col >= score, 1.0, 0.0)
        gt = jnp.where(col > score, 1.0, 0.0)
        rank = rank + jnp.where(blk > b2, ge, gt)
    return jnp.where(rank < min(TOP_N, n_sel), 1.0, 0.0)


def _rep(x, n):
    return x if n == 1 else jnp.concatenate([x] * n, axis=1)


def _select_blocks_t(p_grp, amat_t, q0, n_sel):
    t = p_grp.shape[0]
    nb = amat_t.shape[0]
    hi = p_grp.astype(BF16)
    lo = (p_grp - hi.astype(F32)).astype(BF16)
    imp = _dot_nt(amat_t, hi) + _dot_nt(amat_t, lo)
    pos = q0 + _iota((1, t), 1)
    blk = _iota((nb, t), 0)
    cur = pos // SEL_BLOCK
    forced = jnp.where(blk == 0, 1.0, 0.0) + jnp.where(blk == cur, 1.0, 0.0) + jnp.where(blk == cur - 1, 1.0, 0.0)
    score = jnp.where(blk * SEL_BLOCK <= pos, jnp.where(forced > 0.5, SEL_FORCE_SCORE, imp), -1.0)
    score = jnp.where(blk < n_sel, score, -2.0)
    groups = [score[8 * k:8 * (k + 1)] for k in range(nb // 8)]
    ranks = [jnp.zeros((8, t), F32) for _ in groups]
    sub = _iota((8, t), 0)
    for b2 in range(n_sel):
        rowv = score[b2:b2 + 1, :]
        k2, r2 = divmod(b2, 8)
        for k, grp in enumerate(groups):
            if k > k2:
                inc = jnp.where(rowv >= grp, 1.0, 0.0)
            elif k < k2:
                inc = jnp.where(rowv > grp, 1.0, 0.0)
            else:
                inc = jnp.where(sub > r2, jnp.where(rowv >= grp, 1.0, 0.0), jnp.where(rowv > grp, 1.0, 0.0))
            ranks[k] = ranks[k] + inc
    chosen = [jnp.where(rk < min(TOP_N, n_sel), 1.0, 0.0) for rk in ranks]
    pad = [jnp.zeros((LANES - nb, t), F32)] if nb < LANES else []
    return jnp.concatenate(chosen + pad, axis=0)


def _nsa_prompt_kernel(qn_ref, ckv_ref, ks_ref, kw_ref, gate_ref, amat_ref, e_ref, out_ref,
                       m_scr, l_scr, acc_scr, oc_scr, os_scr, *, n_sel):
    g, i = pl.program_id(1), pl.program_id(2)
    t = qn_ref.shape[2]
    q0 = i * t
    rowi, coli = _iota((t, t), 0), _iota((t, t), 1)
    reps = t // LANES

    nc = ckv_ref.shape[1]
    cmask = _iota((t, nc), 1) * CMP_STRIDE + (CMP_BLOCK - 1) <= q0 + _iota((t, 1), 0)
    ck, cv = ckv_ref[0, :, 0:128], ckv_ref[0, :, 128:256]
    p_grp = None
    for h in range(NSA_HPG):
        p = _masked_softmax_rows(_dot_nt(qn_ref[0, h], ck), cmask)
        oc_scr[h] = _dot(p.astype(BF16), cv)
        p_grp = p if p_grp is None else p_grp + p
    sel = _select_blocks_t(p_grp, amat_ref[...], q0, n_sel).T.astype(BF16)

    def online_head(h, s, v):
        m_prev = m_scr[h]
        m_next = jnp.maximum(m_prev, jnp.max(s, axis=1, keepdims=True))
        alpha = jnp.exp2(m_prev - m_next)
        e = jnp.exp2(s - _rep(m_next, reps))
        l_scr[h] = alpha * l_scr[h] + jnp.sum(e, axis=1, keepdims=True)
        m_scr[h] = m_next
        acc_scr[h] = acc_scr[h] * alpha + _dot(e.astype(BF16), v)

    def reset():
        m_scr[...] = jnp.full_like(m_scr, NEG)
        l_scr[...] = jnp.zeros_like(l_scr)
        acc_scr[...] = jnp.zeros_like(acc_scr)

    def sel_tile(j, diag):
        kt = ks_ref[0, pl.ds(pl.multiple_of(j * t, t), t), :]
        chosen = _dot(sel, e_ref[j])
        if diag:
            chosen = jnp.where(coli <= rowi, chosen, 0.0)
        keep = chosen > 0.5
        for h in range(NSA_HPG):
            online_head(h, jnp.where(keep, _dot_nt(qn_ref[0, h], kt[:, 0:128]), NEG), kt[:, 128:256])

    reset()
    lax.fori_loop(0, i, lambda j, c: (sel_tile(j, False), c)[1], 0)
    sel_tile(i, True)
    for h in range(NSA_HPG):
        os_scr[h] = acc_scr[h] / jnp.maximum(l_scr[h], 1e-30)

    w_len = WINDOW + t
    k0 = jnp.maximum(i - WINDOW // t, 0) * t
    kw = kw_ref[0, pl.ds(pl.multiple_of(k0, t), w_len), :]
    d = (q0 - k0) + _iota((t, w_len), 0) - _iota((t, w_len), 1)
    keep_w = jnp.where(d >= 0, jnp.where(d <= WINDOW, 1.0, 0.0), 0.0) > 0.5

    gates = gate_ref[0]
    lane = _iota((t, LANES), 1)
    for h in range(NSA_HPG):
        s = jnp.where(keep_w, _dot_nt(qn_ref[0, h], kw[:, 0:128]), NEG)
        e = jnp.exp2(s - jnp.max(s, axis=1, keepdims=True))
        o_w = _dot(e.astype(BF16), kw[:, 128:256]) / jnp.sum(e, axis=1, keepdims=True)
        o = (gates[:, 3 * h:3 * h + 1] * oc_scr[h] + gates[:, 3 * h + 1:3 * h + 2] * os_scr[h]
             + gates[:, 3 * h + 2:3 * h + 3] * o_w)
        out_ref[0, :, LANES * h:LANES * (h + 1)] = jnp.where(lane // HEAD_DIM == g, o, 0.0).astype(BF16)


def _imp_matrix(nc, width):
    n = np.arange(nc)[:, None]
    b = np.arange(width)[None, :]
    sub = SEL_BLOCK // CMP_STRIDE
    return jnp.asarray((n // sub == b).astype(np.float32) + ((n + 1) // sub == b), BF16)


def _expand_tiles(n_tiles, n_blk, tk):
    key = np.arange(n_tiles * tk).reshape(n_tiles, 1, tk)
    return jnp.asarray(key // SEL_BLOCK == np.arange(n_blk)[None, :, None], BF16)


def _nsa_prompt(qn, ckv, kb_sel, kb_win, gates, *, batch, seq, tq):
    nq = seq // tq
    nc = ckv.shape[1]
    n_sel = seq // SEL_BLOCK
    assert n_sel <= LANES and WINDOW % tq == 0 and seq >= WINDOW + tq
    amat = _imp_matrix(nc, -(-n_sel // 8) * 8).T
    etile = _expand_tiles(nq, LANES, tq)
    m = batch * seq
    return pl.pallas_call(
        functools.partial(_nsa_prompt_kernel, n_sel=n_sel),
        grid=(batch, NSA_GROUPS, nq),
        in_specs=[
            pl.BlockSpec((1, NSA_HPG, tq, LANES), lambda b, g, i: (g, 0, b * nq + i, 0)),
            pl.BlockSpec((1, nc, KV_W), lambda b, g, i: (b, 0, 0)),
            pl.BlockSpec((1, seq, KV_W), lambda b, g, i: (b, 0, 0)),
            pl.BlockSpec((1, seq, KV_W), lambda b, g, i: (b, 0, 0)),
            pl.BlockSpec((1, tq, LANES), lambda b, g, i: (g, b * nq + i, 0)),
            pl.BlockSpec(amat.shape, lambda b, g, i: (0, 0)),
            pl.BlockSpec(etile.shape, lambda b, g, i: (0, 0, 0)),
        ],
        out_specs=pl.BlockSpec((1, tq, NSA_W), lambda b, g, i: (g, b * nq + i, 0)),
        out_shape=jax.ShapeDtypeStruct((NSA_GROUPS, m, NSA_W), BF16),
        scratch_shapes=[pltpu.VMEM((NSA_HPG, tq, LANES), F32)] * 5,
        compiler_params=_cp("arbitrary", "arbitrary", "arbitrary"), name="nsa_prompt",
    )(qn, ckv, kb_sel.reshape(batch, seq, KV_W), kb_win.reshape(batch, seq, KV_W), gates, amat, etile)


def _upper_strict(n):
    return jnp.asarray(np.triu(np.ones((n, n), np.float32), 1).T, BF16)


def _stack_heads(v):
    lane = _iota(v.shape, 1)
    return jnp.concatenate([jnp.where(lane // HEAD_DIM == h, v, jnp.zeros_like(v)) for h in range(SB_HEADS)], axis=0)


def _head_lanes(parts):
    lane = _iota(parts[0].shape, 1)
    return jnp.concatenate([jnp.where(lane < HEAD_DIM, parts[0], parts[1]),
                            jnp.where(lane < HEAD_DIM, parts[2], parts[3])], axis=1)


def _sb_prompt_kernel(q_ref, kv_ref, u_ref, out_ref, acc_scr, r_scr):
    i = pl.program_id(1)
    t = q_ref.shape[1]
    rowi, coli = _iota((t, t), 0), _iota((t, t), 1)
    acc_scr[...] = jnp.zeros_like(acc_scr)
    r_scr[...] = jnp.zeros_like(r_scr)

    def tile(j, diag):
        kt = kv_ref[0, pl.ds(pl.multiple_of(j * t, t), t), :]
        ws = []
        for h in range(SB_HEADS):
            z = _dot_nt(q_ref[h], kt[:, 0:SB_W])
            sp = _softplus2(z)
            cost = jnp.where(coli < rowi, sp, 0.0) if diag else sp
            r = r_scr[h]
            w = jnp.exp2(z - sp - _split_dot(cost, u_ref[...], 1) - _rep(r, t // LANES))
            if diag:
                w = jnp.where(coli < rowi, w, 0.0)
            ws.append(w.astype(BF16))
            r_scr[h] = r + jnp.sum(cost, axis=1, keepdims=True)
        acc_scr[...] += _dot(jnp.concatenate(ws, axis=1), _stack_heads(kt[:, SB_W:2 * SB_W]))

    tile(i, True)
    lax.fori_loop(0, i, lambda n, c: (tile(i - 1 - n, False), c)[1], 0)
    out_ref[...] = acc_scr[...].astype(BF16)


def _sb_prompt(q, kb, *, batch, seq, tq):
    nq = seq // tq
    u = _upper_strict(tq)
    return pl.pallas_call(
        _sb_prompt_kernel, grid=(batch, nq),
        in_specs=[pl.BlockSpec((SB_HEADS, tq, SB_W), lambda b, i: (0, b * nq + i, 0)),
                  pl.BlockSpec((1, seq, 2 * SB_W), lambda b, i: (b, 0, 0)),
                  pl.BlockSpec(u.shape, lambda b, i: (0, 0))],
        out_specs=pl.BlockSpec((tq, SB_W), lambda b, i: (b * nq + i, 0)),
        out_shape=jax.ShapeDtypeStruct((batch * seq, SB_W), BF16),
        scratch_shapes=[pltpu.VMEM((tq, SB_W), F32), pltpu.VMEM((SB_HEADS, tq, LANES), F32)],
        compiler_params=_cp("arbitrary", "arbitrary"), name="sb_prompt",
    )(q, kb.reshape(batch, seq, 2 * SB_W), u)


def _fox_prompt_kernel(q_ref, kv_ref, cq_ref, ck_ref, out_ref, m_scr, l_scr, acc_scr, cq_scr):
    i = pl.program_id(1)
    t = q_ref.shape[1]
    rowi, coli = _iota((t, t), 0), _iota((t, t), 1)
    m_scr[...] = jnp.full_like(m_scr, NEG)
    l_scr[...] = jnp.zeros_like(l_scr)
    acc_scr[...] = jnp.zeros_like(acc_scr)
    for h in range(FOX_HEADS):
        cq_scr[h] = jnp.broadcast_to(cq_ref[:, LOGF_LANE + h:LOGF_LANE + h + 1] * LOG2E, (t, LANES))

    def tile(j, diag):
        kt = kv_ref[0, pl.ds(pl.multiple_of(j * t, t), t), :]
        es, alphas = [], []
        for h in range(FOX_HEADS):
            s = _dot_nt(q_ref[h], kt[:, 0:SB_W]) + (_rep(cq_scr[h], t // LANES) - ck_ref[0, h, j] * LOG2E)
            if diag:
                s = jnp.where(coli <= rowi, s, NEG)
            m_prev = m_scr[h]
            m_next = jnp.maximum(m_prev, jnp.max(s, axis=1, keepdims=True))
            alpha = jnp.exp2(m_prev - m_next)
            e = jnp.exp2(s - _rep(m_next, t // LANES))
            l_scr[h] = alpha * l_scr[h] + jnp.sum(e, axis=1, keepdims=True)
            m_scr[h] = m_next
            es.append(e.astype(BF16))
            alphas.append(alpha)
        acc_scr[...] = acc_scr[...] * _head_lanes(alphas) + _dot(jnp.concatenate(es, axis=1),
                                                                 _stack_heads(kt[:, SB_W:2 * SB_W]))

    lax.fori_loop(0, i, lambda j, c: (tile(j, False), c)[1], 0)
    tile(i, True)
    out_ref[...] = (acc_scr[...] / _head_lanes([jnp.maximum(l_scr[h], 1e-30) for h in range(FOX_HEADS)])).astype(BF16)


def _fox_prompt(q, kb, cum, cum_k, *, batch, seq, tq):
    nq = seq // tq
    nk = seq // tq
    return pl.pallas_call(
        _fox_prompt_kernel, grid=(batch, nq),
        in_specs=[pl.BlockSpec((FOX_HEADS, tq, SB_W), lambda b, i: (0, b * nq + i, 0)),
                  pl.BlockSpec((1, seq, 2 * SB_W), lambda b, i: (b, 0, 0)),
                  pl.BlockSpec((tq, LANES), lambda b, i: (b * nq + i, 0)),
                  pl.BlockSpec((1, FOX_HEADS, nk, 1, tq), lambda b, i: (b, 0, 0, 0, 0))],
        out_specs=pl.BlockSpec((tq, SB_W), lambda b, i: (b * nq + i, 0)),
        out_shape=jax.ShapeDtypeStruct((batch * seq, SB_W), BF16),
        scratch_shapes=[pltpu.VMEM((FOX_HEADS, tq, LANES), F32), pltpu.VMEM((FOX_HEADS, tq, LANES), F32),
                        pltpu.VMEM((tq, SB_W), F32), pltpu.VMEM((FOX_HEADS, tq, LANES), F32)],
        compiler_params=_cp("arbitrary", "arbitrary"), name="fox_prompt",
    )(q, kb.reshape(batch, seq, 2 * SB_W), cum, cum_k)


def _merge_kernel(x_ref, g_ref, oa_ref, ob_ref, oc_ref, wm_ref, wa_ref, wb_ref, wc_ref, wo_ref, out_ref):
    x = x_ref[...]
    d = x.shape[1]
    hb = _rms(x, g_ref[...]).astype(BF16)
    oa = oa_ref[0] + oa_ref[1]
    u = jax.nn.sigmoid(_dot_nt(hb, wm_ref[0:d, :])) * _dot(oa, wa_ref[...])
    u = u + jax.nn.sigmoid(_dot_nt(hb, wm_ref[d:2 * d, :])) * _dot(ob_ref[...], wb_ref[...])
    u = u + jax.nn.sigmoid(_dot_nt(hb, wm_ref[2 * d:3 * d, :])) * _dot(oc_ref[...], wc_ref[...])
    out_ref[...] = x + _dot(u.astype(BF16), wo_ref[...])


def _merge(x, gain, o_nsa, o_sb, o_fox, w_merge, w_a, w_b, w_c, w_o, *, tm):
    m, d = x.shape
    row = lambda wid: pl.BlockSpec((tm, wid), lambda i: (i, 0))
    const = lambda a: pl.BlockSpec(a.shape, lambda i: (0,) * a.ndim)
    return pl.pallas_call(
        _merge_kernel, grid=(m // tm,),
        in_specs=[row(d), const(gain), pl.BlockSpec((NSA_GROUPS, tm, NSA_W), lambda i: (0, i, 0)),
                  row(SB_W), row(SB_W), const(w_merge), const(w_a), const(w_b), const(w_c), const(w_o)],
        out_specs=row(d), out_shape=jax.ShapeDtypeStruct((m, d), F32),
        compiler_params=_cp("arbitrary"), name="merge",
    )(x, gain, o_nsa, o_sb, o_fox, w_merge, w_a, w_b, w_c, w_o)


def _ffn_kernel(x_ref, g_ref, wg_ref, wu_ref, wd_ref, gf_ref, out_ref, *, final_norm):
    x = x_ref[...]
    hb = _rms(x, g_ref[...]).astype(BF16)
    a = _dot(hb, wg_ref[...])
    act = a * jax.nn.sigmoid(a) * _dot(hb, wu_ref[...])
    y = x + _dot(act.astype(BF16), wd_ref[...])
    out_ref[...] = _rms(y, gf_ref[...]) if final_norm else y


def _ffn(x, gain, w_gate, w_up, w_down, final_gain, *, tm, final_norm):
    m, d = x.shape
    row = pl.BlockSpec((tm, d), lambda i: (i, 0))
    vec = pl.BlockSpec((1, d), lambda i: (0, 0))
    const = lambda a: pl.BlockSpec(a.shape, lambda i: (0,) * a.ndim, pipeline_mode=pl.Buffered(1))
    return pl.pallas_call(
        functools.partial(_ffn_kernel, final_norm=final_norm), grid=(m // tm,),
        in_specs=[row, vec, const(w_gate), const(w_up), const(w_down), vec],
        out_specs=row, out_shape=jax.ShapeDtypeStruct((m, d), F32),
        compiler_params=_cp("arbitrary"), name="ffn",
    )(x, gain, w_gate, w_up, w_down, final_gain)


def _page_specs(rows, n_pages, pp, descending):
    def spec(i):
        if descending:
            return pl.BlockSpec((1, rows, PAGE), lambda b, j, pt: (pt[b, n_pages - 1 - (j * pp + i)], 0, 0))
        return pl.BlockSpec((1, rows, PAGE), lambda b, j, pt: (pt[b, j * pp + i], 0, 0))
    return [spec(i) for i in range(pp)]


def _lanes(parts):
    return jnp.concatenate(parts, axis=1)


def _sb_dec_kernel(pt_ref, q_ref, *refs, pp):
    pages, (u_ref, out_ref, acc_scr, r_scr) = refs[:pp], refs[pp:]
    j = pl.program_id(1)

    @pl.when(j == 0)
    def _():
        acc_scr[...] = jnp.zeros_like(acc_scr)
        r_scr[...] = jnp.zeros_like(r_scr)

    q = q_ref[0]
    z_all = _dot(q, _lanes([p[0, 0:SB_W, :].astype(BF16) for p in pages]))
    zs = [z_all[:, i * PAGE:(i + 1) * PAGE] for i in range(pp)]
    sps = [_softplus2(z) for z in zs]
    afters = [_split_dot(sp, u_ref[...], 2) for sp in sps]
    r = r_scr[...]
    ws = []
    for i in range(pp):
        ws.append(jnp.exp2(zs[i] - sps[i] - afters[i] - r).astype(BF16))
        r = r + jnp.sum(sps[i], axis=1, keepdims=True)
    acc_scr[...] += _dot_nt(_lanes(ws), _lanes([p[0, SB_W:2 * SB_W, :].astype(BF16) for p in pages]))
    r_scr[...] = r

    @pl.when(j == pl.num_programs(1) - 1)
    def _():
        out_ref[0] = acc_scr[...]


def _sb_dec(q8, pool, page_table, *, pp):
    bd, n_pages = page_table.shape
    u = _upper_strict(PAGE)
    grid_spec = pltpu.PrefetchScalarGridSpec(
        num_scalar_prefetch=1, grid=(bd, n_pages // pp),
        in_specs=[pl.BlockSpec((1, 8, SB_W), lambda b, j, pt: (b, 0, 0))]
        + _page_specs(2 * SB_W, n_pages, pp, True) + [pl.BlockSpec(u.shape, lambda b, j, pt: (0, 0))],
        out_specs=pl.BlockSpec((1, 8, SB_W), lambda b, j, pt: (b, 0, 0)),
        scratch_shapes=[pltpu.VMEM((8, SB_W), F32), pltpu.VMEM((8, LANES), F32)])
    return pl.pallas_call(
        functools.partial(_sb_dec_kernel, pp=pp), grid_spec=grid_spec,
        out_shape=jax.ShapeDtypeStruct((bd, 8, SB_W), F32),
        compiler_params=_cp("arbitrary", "arbitrary"), name="sb_decode",
    )(page_table, q8, *([pool] * pp), u)


def _fox_dec_kernel(pt_ref, q_ref, kn_ref, vn_ref, lfn_ref, *refs, pp):
    pages, lf_pages = refs[:pp], refs[pp:2 * pp]
    u_ref, out_ref, m_scr, l_scr, acc_scr, r_scr = refs[2 * pp:]
    j = pl.program_id(1)
    q = q_ref[0]
    reps = SB_W // LANES

    @pl.when(j == 0)
    def _():
        kn = kn_ref[0].astype(BF16).astype(F32)
        m_scr[...] = jnp.broadcast_to(jnp.sum(q.astype(F32) * kn, axis=1, keepdims=True), m_scr.shape)
        l_scr[...] = jnp.ones_like(l_scr)
        acc_scr[...] = jnp.broadcast_to(vn_ref[0].astype(BF16).astype(F32), acc_scr.shape)
        r_scr[...] = lfn_ref[0]

    z_all = _dot(q, _lanes([p[0, 0:SB_W, :].astype(BF16) for p in pages]))
    pad = jnp.zeros((8 - FOX_HEADS, PAGE), F32)
    r = r_scr[...]
    ss = []
    for i in range(pp):
        lf = jnp.concatenate([lf_pages[i][0], pad], axis=0)
        ss.append(z_all[:, i * PAGE:(i + 1) * PAGE] + LOG2E * (_split_dot(lf, u_ref[...], 3) + r))
        r = r + jnp.sum(lf, axis=1, keepdims=True)
    r_scr[...] = r
    m_prev = m_scr[...]
    m_next = m_prev
    for s in ss:
        m_next = jnp.maximum(m_next, jnp.max(s, axis=1, keepdims=True))
    alpha = jnp.exp2(m_prev - m_next)
    es = [jnp.exp2(s - m_next) for s in ss]
    l_new = alpha * l_scr[...]
    for e in es:
        l_new = l_new + jnp.sum(e, axis=1, keepdims=True)
    l_scr[...] = l_new
    m_scr[...] = m_next
    acc_scr[...] = acc_scr[...] * _rep(alpha, reps) + _dot_nt(
        _lanes([e.astype(BF16) for e in es]), _lanes([p[0, SB_W:2 * SB_W, :].astype(BF16) for p in pages]))

    @pl.when(j == pl.num_programs(1) - 1)
    def _():
        out_ref[0] = acc_scr[...] / _rep(l_scr[...], reps)


def _fox_dec(q8, k_new, v_new, lf_new, pool, lf_pool, page_table, *, pp):
    bd, n_pages = page_table.shape
    u = _upper_strict(PAGE)
    req = lambda shape: pl.BlockSpec((1,) + shape, lambda b, j, pt: (b,) + (0,) * len(shape))
    lf_spec = lambda i: pl.BlockSpec((1, FOX_HEADS, PAGE), lambda b, j, pt: (pt[b, n_pages - 1 - (j * pp + i)], 0, 0))
    grid_spec = pltpu.PrefetchScalarGridSpec(
        num_scalar_prefetch=1, grid=(bd, n_pages // pp),
        in_specs=[req((8, SB_W)), req((1, SB_W)), req((1, SB_W)), req((8, LANES))]
        + _page_specs(2 * SB_W, n_pages, pp, True) + [lf_spec(i) for i in range(pp)]
        + [pl.BlockSpec(u.shape, lambda b, j, pt: (0, 0))],
        out_specs=req((8, SB_W)),
        scratch_shapes=[pltpu.VMEM((8, LANES), F32), pltpu.VMEM((8, LANES), F32),
                        pltpu.VMEM((8, SB_W), F32), pltpu.VMEM((8, LANES), F32)])
    return pl.pallas_call(
        functools.partial(_fox_dec_kernel, pp=pp), grid_spec=grid_spec,
        out_shape=jax.ShapeDtypeStruct((bd, 8, SB_W), F32),
        compiler_params=_cp("arbitrary", "arbitrary"), name="fox_decode",
    )(page_table, q8, k_new, v_new, lf_new, *([pool] * pp), *([lf_pool] * pp), u)


def _nsa_dec_kernel(pt_ref, q_ref, ckv_ref, gate_ref, win_ref, kvs_new_ref, kvw_new_ref, kvw_col_ref, amat_ref, e_ref,
                    *refs, pp, past_len, n_sel):
    pages, (out_ref, win_out_ref, m_scr, l_scr, acc_scr, sel_scr, oc_scr, ow_scr) = refs[:pp], refs[pp:]
    j = pl.program_id(1)
    q = q_ref[0]
    rows = q.shape[0]
    row = _iota((rows, LANES), 0)
    row1 = _iota((rows, 1), 0)
    lane = _iota((rows, LANES), 1)
    pos = jnp.full((rows, 1), past_len, I32)

    def new_token(kv_row):
        k = kv_row[:, 0:LANES].astype(BF16).astype(F32)
        return jnp.sum(q.astype(F32) * k, axis=1, keepdims=True), kv_row[:, LANES:2 * LANES].astype(BF16).astype(F32)

    @pl.when(j == 0)
    def _():
        nc = ckv_ref.shape[1]
        cmask = _iota((rows, nc), 1) * CMP_STRIDE + (CMP_BLOCK - 1) <= pos
        p = _masked_softmax_rows(_dot_nt(q, ckv_ref[0, :, 0:128]), cmask)
        oc_scr[...] = _dot(p.astype(BF16), ckv_ref[0, :, 128:256])
        g0 = jnp.sum(p[0:NSA_HPG], axis=0, keepdims=True)
        g1 = jnp.sum(p[NSA_HPG:2 * NSA_HPG], axis=0, keepdims=True)
        p_grp = jnp.where(row1 < NSA_HPG, g0, g1)
        imp = _split_dot(p_grp, amat_ref[...], 2)
        sel_scr[...] = _select_blocks(imp, pos, n_sel)

        wk = win_ref[0]
        n_w = wk.shape[1]
        s = _dot(q, wk[0:LANES, :].astype(BF16))
        kpos = past_len - n_w + _iota((rows, n_w), 1)
        keep = jnp.where(kpos >= 0, jnp.where(pos - kpos <= WINDOW, 1.0, 0.0), 0.0) > 0.5
        s = jnp.where(keep, s, NEG)
        s_new, v_new = new_token(kvw_new_ref[0])
        m = jnp.maximum(s_new, jnp.max(s, axis=1, keepdims=True))
        e = jnp.exp2(s - m)
        e_new = jnp.exp2(s_new - m)
        o_w = _dot_nt(e.astype(BF16), wk[LANES:2 * LANES, :].astype(BF16)) + e_new * v_new
        ow_scr[...] = o_w / (jnp.sum(e, axis=1, keepdims=True) + e_new)

        shifted = pltpu.roll(wk, n_w - 1, 1)
        win_out_ref[0] = jnp.where(_iota(wk.shape, 1) == n_w - 1, kvw_col_ref[0], shifted)
        _online_init(m_scr, l_scr, acc_scr)

    sel = sel_scr[...].astype(BF16)
    z_all = _dot(q, _lanes([p[0, 0:LANES, :].astype(BF16) for p in pages]))
    ss = []
    for i in range(pp):
        chosen = _dot(sel, e_ref[j * pp + i]) > 0.5
        ss.append(jnp.where(chosen, z_all[:, i * PAGE:(i + 1) * PAGE], NEG))
    m_prev = m_scr[...]
    m_next = m_prev
    for s in ss:
        m_next = jnp.maximum(m_next, jnp.max(s, axis=1, keepdims=True))
    alpha = jnp.exp2(m_prev - m_next)
    es = [jnp.exp2(s - m_next) for s in ss]
    l_new = alpha * l_scr[...]
    for e in es:
        l_new = l_new + jnp.sum(e, axis=1, keepdims=True)
    l_scr[...] = l_new
    m_scr[...] = m_next
    acc_scr[...] = acc_scr[...] * alpha + _dot_nt(_lanes([e.astype(BF16) for e in es]),
                                                  _lanes([p[0, LANES:2 * LANES, :].astype(BF16) for p in pages]))

    @pl.when(j == pl.num_programs(1) - 1)
    def _():
        s_new, v_new = new_token(kvs_new_ref[0])
        cur = past_len // SEL_BLOCK
        chosen = jnp.sum(jnp.where(_iota(sel_scr.shape, 1) == cur, sel_scr[...], 0.0), axis=1, keepdims=True)
        s_new = jnp.where(chosen > 0.5, s_new, NEG)
        m_prev = m_scr[...]
        m_next = jnp.maximum(m_prev, s_new)
        alpha = jnp.exp2(m_prev - m_next)
        e_new = jnp.exp2(s_new - m_next)
        o_s = (acc_scr[...] * alpha + e_new * v_new) / jnp.maximum(alpha * l_scr[...] + e_new, 1e-30)
        gates = jnp.where(row < NSA_HPG, gate_ref[0, 0:1, :], gate_ref[0, 1:2, :])
        hh = row % NSA_HPG

        def gate(c):
            return jnp.sum(jnp.where(lane == 3 * hh + c, gates, 0.0), axis=1, keepdims=True)

        out_ref[0] = gate(0) * oc_scr[...] + gate(1) * o_s + gate(2) * ow_scr[...]


def _nsa_dec(q8, ckv, gates, win_state, kvs_new, kvw_new, pool, page_table, *, pp, layer):
    bd, n_pages = page_table.shape
    past_len = n_pages * PAGE
    n_sel = -(-(past_len + 1) // SEL_BLOCK)
    nc = ckv.shape[1]
    wid = -(-n_sel // LANES) * LANES
    amat = _imp_matrix(nc, wid)
    etile = _expand_tiles(n_pages, wid, PAGE)
    n_w = win_state.shape[2]
    req = lambda shape: pl.BlockSpec((1,) + shape, lambda b, j, pt: (b,) + (0,) * len(shape))
    const = lambda a: pl.BlockSpec(a.shape, lambda b, j, pt: (0,) * a.ndim)
    win_spec = pl.BlockSpec((1, KV_W, n_w), lambda b, j, pt: (layer * bd + b, 0, 0))
    grid_spec = pltpu.PrefetchScalarGridSpec(
        num_scalar_prefetch=1, grid=(bd, n_pages // pp),
        in_specs=[req((8, LANES)), req((nc, KV_W)), req((NSA_GROUPS, LANES)), win_spec,
                  req((1, KV_W)), req((1, KV_W)), req((KV_W, 1)), const(amat), const(etile)]
        + _page_specs(KV_W, n_pages, pp, False),
        out_specs=(req((8, LANES)), req((KV_W, n_w))),
        scratch_shapes=[pltpu.VMEM((8, LANES), F32)] * 3 + [pltpu.VMEM((8, wid), F32)] + [pltpu.VMEM((8, LANES), F32)] * 2)
    return pl.pallas_call(
        functools.partial(_nsa_dec_kernel, pp=pp, past_len=past_len, n_sel=n_sel), grid_spec=grid_spec,
        out_shape=(jax.ShapeDtypeStruct((bd, 8, LANES), F32), jax.ShapeDtypeStruct((bd, KV_W, n_w), F32)),
        compiler_params=_cp("arbitrary", "arbitrary"), name="nsa_decode",
    )(page_table, q8, ckv, gates, win_state, kvs_new, kvw_new, kvw_new.reshape(bd, KV_W, 1), amat, etile,
      *([pool] * pp))


def _rope_table(pos):
    half = ROT_DIM // 2
    inv = ROPE_THETA ** (-jnp.arange(half, dtype=F32) / half)
    ang = pos.astype(F32)[:, None] * inv[None, :]
    cos, sin = jnp.cos(ang), jnp.sin(ang)
    n = pos.shape[0]
    pad = HEAD_DIM - ROT_DIM
    one_head = lambda a, b, fill: jnp.concatenate([a, b, jnp.full((n, pad), fill, F32)], axis=1)
    zero = jnp.zeros_like(sin)
    tabs = [one_head(cos, cos, 1.0), one_head(-sin, zero, 0.0), one_head(zero, sin, 0.0)]
    return jnp.concatenate([jnp.tile(t, (1, LANES // HEAD_DIM)) for t in tabs], axis=1)


_Q_HEAD_ORDER = (0, 4, 1, 5, 2, 6, 3, 7)


def _layer_weights(l, w_in, b_forget, cmp_pos, w_cmp_k, w_cmp_v, w_up_nsa):
    w = jnp.swapaxes(w_in, 1, 2)[l]
    d = w.shape[1]
    gate0 = NSA_W + 3 * KV_W
    pf0 = gate0 + GATE_W + 6 * SB_W
    merge0 = pf0 + FOX_HEADS
    rows_ = [w[HEAD_DIM * h:HEAD_DIM * (h + 1)] for h in _Q_HEAD_ORDER]
    rows_ += [w[NSA_W:gate0], w[gate0 + GATE_W:pf0], w[gate0:gate0 + GATE_W], w[pf0:merge0]]
    rows_.append(jnp.zeros((N_PROJ - merge0, d), F32))
    w_proj = jnp.concatenate(rows_, axis=0).astype(BF16)
    w_merge = w[merge0:].astype(BF16)
    zpad = lambda n: jnp.zeros((1, n), F32)
    bias = jnp.concatenate([zpad(LOGF_LANE), b_forget[l].reshape(1, FOX_HEADS), zpad(LANES - LOGF_LANE - FOX_HEADS)], axis=1)
    wk = w_cmp_k[l].reshape(CMP_BLOCK, HEAD_DIM, HEAD_DIM)
    wv = w_cmp_v[l].reshape(CMP_BLOCK, HEAD_DIM, HEAD_DIM)
    zero = jnp.zeros_like(wk)
    w_blk = jnp.concatenate([jnp.concatenate([wsrc if c == n else zero for c in range(4)], axis=2)
                             for n, wsrc in enumerate((wk, wk, wv, wv))], axis=1)
    w_cat = jnp.concatenate([w_blk[0:CMP_BLOCK // 2], w_blk[CMP_BLOCK // 2:]], axis=2).astype(BF16)
    pe4 = jnp.tile(cmp_pos[l], (1, KV_W // HEAD_DIM))
    w_a = jnp.concatenate([w_up_nsa[l][HEAD_DIM * h:HEAD_DIM * (h + 1)] for h in _Q_HEAD_ORDER], axis=0).astype(BF16)
    return w_proj, w_merge, bias, w_cat, pe4, w_a


def _pad_heads(a):
    return jnp.concatenate([a, jnp.zeros((a.shape[0], 8 - a.shape[1], a.shape[2]), a.dtype)], axis=1)


def _token_lanes(cache):
    l, p, t = cache.shape[:3]
    return cache.transpose(0, 1, 3, 4, 5, 2).reshape(l * p, -1, t)


def _own_lanes(o8, n_heads):
    n = o8.shape[0]
    o = o8[:, 0:n_heads].reshape(n, n_heads, n_heads, HEAD_DIM)
    return jnp.stack([o[:, h, h] for h in range(n_heads)], axis=1).reshape(n, n_heads * HEAD_DIM)


def kernel(x_prompt, x_sample, cache_nsa_cmp_kv, cache_nsa_sel_kv, cache_sb_kv, cache_fox_kv, cache_fox_logf,
           state_nsa_win_kv, page_table, norm_mix_g, norm_ffn_g, norm_final_g, w_in, b_forget, cmp_pos,
           w_cmp_k, w_cmp_v, w_up_nsa, w_up_sb, w_up_fox, w_out, w_ffn_gate, w_ffn_up, w_ffn_down):
    batch, seq, d = x_prompt.shape
    bd = x_sample.shape[0]
    depth = w_in.shape[0]
    n_pool = cache_sb_kv.shape[1]
    n_pages = page_table.shape[1]
    past_len = n_pages * PAGE
    m = batch * seq
    tm = min(256, seq)
    tq = min(256, seq)
    tq_soft = min(512, seq)
    pp = min(PAGES_PER_STEP, n_pages)
    pp_prompt = min(PAGES_PER_STEP, seq // PAGE)

    rope_p = _rope_table(jnp.arange(seq, dtype=I32))
    rope_s = _rope_table(jnp.full((bd,), past_len, I32))
    ident_pt = jnp.arange(batch * (seq // PAGE), dtype=I32).reshape(batch, seq // PAGE)
    pool_cmp, pool_sel = _token_lanes(cache_nsa_cmp_kv), _token_lanes(cache_nsa_sel_kv)
    pool_sb, pool_fox = _token_lanes(cache_sb_kv), _token_lanes(cache_fox_kv)
    pool_lf = cache_fox_logf.astype(F32).transpose(0, 1, 3, 2).reshape(depth * n_pool, FOX_HEADS, PAGE)
    win_state = _token_lanes(state_nsa_win_kv)

    xp = x_prompt.reshape(m, d)
    xs = x_sample.reshape(bd, d)
    g_fin = norm_final_g.reshape(1, d)
    p_states, s_states = [], []
    for l in range(depth):
        w_proj, w_merge, bias, w_blk, pe4, w_a = _layer_weights(l, w_in, b_forget, cmp_pos, w_cmp_k, w_cmp_v, w_up_nsa)
        w_b, w_c, w_o = w_up_sb[l].astype(BF16), w_up_fox[l].astype(BF16), w_out[l].astype(BF16)
        w_g, w_u, w_d = w_ffn_gate[l].astype(BF16), w_ffn_up[l].astype(BF16), w_ffn_down[l].astype(BF16)
        g_mix, g_ffn = norm_mix_g[l].reshape(1, d), norm_ffn_g[l].reshape(1, d)

        (qn, kv_cmp, kv_sel, kv_win, kv_sb, kv_fox, kb_sel, kb_win, kb_sb, kb_fox, q_sb, q_fox, gates, logf, cum
         ) = _proj(xp, g_mix, w_proj, rope_p, bias, tm=tm, tiles_per_seq=seq // tm, feature_states=True)
        ckv = _compress(kv_cmp, ident_pt, w_blk, pe4, pp=pp_prompt, paged=False)
        o_nsa = _nsa_prompt(qn, ckv, kb_sel, kb_win, gates, batch=batch, seq=seq, tq=tq_soft)
        o_sb = _sb_prompt(q_sb, kb_sb, batch=batch, seq=seq, tq=tq)
        cum4 = cum[:, LOGF_LANE:LOGF_LANE + FOX_HEADS]
        cum_k = cum4.reshape(batch, seq, FOX_HEADS).transpose(0, 2, 1).reshape(
            batch, FOX_HEADS, seq // tq_soft, 1, tq_soft)
        o_fox = _fox_prompt(q_fox, kb_fox, cum, cum_k, batch=batch, seq=seq, tq=tq_soft)
        xp = _merge(xp, g_mix, o_nsa, o_sb, o_fox, w_merge, w_a, w_b, w_c, w_o, tm=tm)
        xp = _ffn(xp, g_ffn, w_g, w_u, w_d, g_fin, tm=tm, final_norm=(l == depth - 1))
        win_keep = min(WINDOW, seq)
        p_states.append((kv_cmp, kv_sel, kv_sb, kv_fox, logf, kv_win[:, :, seq - win_keep:]))

        (qn, kv_cmp, kv_sel, kv_win, kv_sb, kv_fox, _, _, _, _, q_sb, q_fox, gates, logf, _
         ) = _proj(xs, g_mix, w_proj, rope_s, bias, tm=bd, tiles_per_seq=1, feature_states=False)
        pt_l = page_table + l * n_pool
        ckv = _compress(pool_cmp, pt_l, w_blk, pe4, pp=pp, paged=True)
        q8 = qn.transpose(2, 0, 1, 3).reshape(bd, NSA_HEADS, LANES)
        o_nsa8, win_new = _nsa_dec(q8, ckv, gates.transpose(1, 0, 2), win_state, kv_sel.reshape(bd, 1, KV_W),
                                   kv_win.reshape(bd, 1, KV_W), pool_sel, pt_l, pp=pp, layer=l)
        o_sb8 = _sb_dec(_pad_heads(q_sb.transpose(1, 0, 2)), pool_sb, pt_l, pp=pp)
        lf4 = logf[:, LOGF_LANE:LOGF_LANE + FOX_HEADS]
        lf_new = _pad_heads(jnp.broadcast_to(lf4[:, :, None], (bd, FOX_HEADS, LANES)))
        o_fox8 = _fox_dec(_pad_heads(q_fox.transpose(1, 0, 2)), kv_fox[:, 0:SB_W].reshape(bd, 1, SB_W),
                          kv_fox[:, SB_W:2 * SB_W].reshape(bd, 1, SB_W), lf_new, pool_fox, pool_lf, pt_l, pp=pp)
        o_a = o_nsa8.reshape(bd, NSA_GROUPS, NSA_HPG, NSA_GROUPS, HEAD_DIM)
        o_a = jnp.stack([o_a[:, g, :, g] for g in range(NSA_GROUPS)], axis=2).reshape(bd, NSA_W)
        o_a = jnp.stack([o_a, jnp.zeros_like(o_a)], axis=0).astype(BF16)
        o_b = _own_lanes(o_sb8, SB_HEADS).astype(BF16)
        o_c = _own_lanes(o_fox8, FOX_HEADS).astype(BF16)
        xs = _merge(xs, g_mix, o_a, o_b, o_c, w_merge, w_a, w_b, w_c, w_o, tm=bd)
        xs = _ffn(xs, g_ffn, w_g, w_u, w_d, g_fin, tm=bd, final_norm=(l == depth - 1))
        s_states.append((
            kv_cmp.reshape(bd, 1, 2, NSA_GROUPS, HEAD_DIM), kv_sel.reshape(bd, 1, 2, NSA_GROUPS, HEAD_DIM),
            kv_sb.reshape(bd, 1, 2, SB_HEADS, HEAD_DIM), kv_fox.reshape(bd, 1, 2, FOX_HEADS, HEAD_DIM),
            lf4.reshape(bd, 1, FOX_HEADS),
            win_new.reshape(bd, 2, NSA_GROUPS, HEAD_DIM, -1).transpose(0, 4, 1, 2, 3)))

    y_prompt = xp.reshape(batch, seq, d)
    y_sample = xs.reshape(bd, 1, d)
    stk = lambda sts, i: jnp.stack([st[i] for st in sts], axis=0)

    def kv_state(i, heads):
        a = stk(p_states, i)
        return a.reshape(depth, batch, 2, heads, HEAD_DIM, a.shape[-1]).transpose(0, 1, 5, 2, 3, 4)

    p_out = (kv_state(0, NSA_GROUPS), kv_state(1, NSA_GROUPS), kv_state(2, SB_HEADS), kv_state(3, FOX_HEADS),
             stk(p_states, 4).transpose(0, 1, 3, 2), kv_state(5, NSA_GROUPS))
    return (y_prompt, y_sample) + p_out + tuple(stk(s_states, i) for i in range(6))
```

```python
import functools

import numpy as np
import jax
import jax.numpy as jnp
from jax import lax
from jax.experimental import pallas as pl
from jax.experimental.pallas import tpu as pltpu

F32, BF16, I32 = jnp.float32, jnp.bfloat16, jnp.int32

HEAD_DIM = 64
ROT_DIM = HEAD_DIM // 4
ROPE_THETA = 500000.0
NSA_HEADS = 8
NSA_GROUPS = 2
NSA_HPG = NSA_HEADS // NSA_GROUPS
CMP_STRIDE = 16
CMP_BLOCK = 32
SEL_BLOCK = 64
TOP_N = 16
WINDOW = 512
SB_HEADS = 4
FOX_HEADS = 4
PAGE = 128
SEL_FORCE_SCORE = 1.0e4
NEG = -1.0e30
RMS_EPS = 1e-6
SCALE = HEAD_DIM ** -0.5
LOG2E = 1.4426950408889634

LANES = 128
NSA_W = NSA_HEADS * HEAD_DIM
KV_W = 2 * NSA_GROUPS * HEAD_DIM
SB_W = SB_HEADS * HEAD_DIM
GATE_W = 3 * NSA_HEADS
C_Q, C_CMP, C_SEL, C_WIN, C_SB, C_FOX, C_MISC = 0, 512, 768, 1024, 1280, 2048, 2816
N_PROJ = C_MISC + LANES
LOGF_LANE = GATE_W
VMEM_LIMIT = 56 * 1024 * 1024
PAGES_PER_STEP = 32


def _cp(*sem):
    return pltpu.CompilerParams(dimension_semantics=sem, vmem_limit_bytes=VMEM_LIMIT)


def _dot(a, b):
    return jnp.dot(a, b, preferred_element_type=F32)


def _dot_nt(a, b):
    return lax.dot_general(a, b, (((1,), (1,)), ((), ())), preferred_element_type=F32)


def _split_dot(a, r, parts):
    out, rem = None, a
    for _ in range(parts):
        hi = rem.astype(BF16)
        d = _dot(hi, r)
        out = d if out is None else out + d
        rem = rem - hi.astype(F32)
    return out


def _split_dot_l(l, a, parts):
    out, rem = None, a
    for _ in range(parts):
        hi = rem.astype(BF16)
        d = _dot(l, hi)
        out = d if out is None else out + d
        rem = rem - hi.astype(F32)
    return out


def _rms(x, g):
    return x * lax.rsqrt(jnp.mean(x * x, axis=-1, keepdims=True) + RMS_EPS) * g


def _softplus2(z2):
    return jnp.maximum(z2, 0.0) + jnp.log2(1.0 + jnp.exp2(-jnp.abs(z2)))


def _iota(shape, axis):
    return lax.broadcasted_iota(I32, shape, axis)


def _proj_kernel(x_ref, g_ref, w_ref, rope_ref, b_ref, tri_ref,
                 qn_ref, kvc_ref, kvs_ref, kvw_ref, kvsb_ref, kvfx_ref,
                 kbs_ref, kbw_ref, kbsb_ref, kbfx_ref, qsb_ref, qfx_ref,
                 gate_ref, logf_ref, cum_ref, h_scr, carry_scr, *, tiles_per_seq, feature_states):
    i = pl.program_id(0)
    tm = x_ref.shape[0]
    h_scr[...] = _rms(x_ref[...], g_ref[...]).astype(BF16)
    hb = h_scr[...]
    cos, sin_lo, sin_hi = rope_ref[:, 0:128], rope_ref[:, 128:256], rope_ref[:, 256:384]

    def rope(seg):
        return seg * cos + pltpu.roll(seg, LANES - 8, 1) * sin_lo + pltpu.roll(seg, 8, 1) * sin_hi

    def put_state(f_ref, blocks):
        for n, blk in enumerate(blocks):
            if feature_states:
                f_ref[0, LANES * n:LANES * (n + 1), :] = blk.T
            else:
                f_ref[:, LANES * n:LANES * (n + 1)] = blk

    lane = _iota((tm, LANES), 1)
    pq = _dot_nt(hb, w_ref[C_Q:C_Q + NSA_W, :])
    for s in range(NSA_HPG):
        seg = rope(pq[:, LANES * s:LANES * (s + 1)]) * (SCALE * LOG2E)
        for g in range(NSA_GROUPS):
            qn_ref[g, s] = jnp.where(lane // HEAD_DIM == g, seg, 0.0).astype(BF16)

    for c0, f_ref, h_ref in ((C_CMP, kvc_ref, None), (C_SEL, kvs_ref, kbs_ref), (C_WIN, kvw_ref, kbw_ref)):
        p = _dot_nt(hb, w_ref[c0:c0 + KV_W, :])
        k, v = rope(p[:, 0:128]), p[:, 128:256]
        put_state(f_ref, [k, v])
        if h_ref is not None:
            h_ref[:, 0:128] = k.astype(BF16)
            h_ref[:, 128:256] = v.astype(BF16)

    lane_w = _iota((tm, SB_W), 1)
    for c0, q_ref, f_ref, h_ref in ((C_SB, qsb_ref, kvsb_ref, kbsb_ref), (C_FOX, qfx_ref, kvfx_ref, kbfx_ref)):
        p = _dot_nt(hb, w_ref[c0:c0 + 3 * SB_W, :])
        q = p[:, 0:SB_W] * (SCALE * LOG2E)
        for h in range(SB_HEADS):
            q_ref[h] = jnp.where(lane_w // HEAD_DIM == h, q, 0.0).astype(BF16)
        kv = p[:, SB_W:3 * SB_W]
        put_state(f_ref, [kv[:, LANES * n:LANES * (n + 1)] for n in range(2 * SB_W // LANES)])
        h_ref[...] = kv.astype(BF16)

    pm = _dot_nt(hb, w_ref[C_MISC:C_MISC + LANES, :])
    sg = jax.nn.sigmoid(pm)
    gate_ref[0] = sg
    gate_ref[1] = pltpu.roll(sg, LANES - GATE_W // 2, 1)
    zf = pm + b_ref[...]
    lf = jnp.minimum(zf, 0.0) - jnp.log1p(jnp.exp(-jnp.abs(zf)))
    if feature_states:
        logf_ref[0] = lf.T[LOGF_LANE:LOGF_LANE + FOX_HEADS, :]
    else:
        logf_ref[...] = lf

    @pl.when(i % tiles_per_seq == 0)
    def _():
        carry_scr[...] = jnp.zeros_like(carry_scr)

    c = _split_dot_l(tri_ref[...], lf, 3) + carry_scr[0:1, :]
    cum_ref[...] = c
    carry_scr[...] = jnp.broadcast_to(c[tm - 1:tm, :], carry_scr.shape)


def _proj(x, gain, w, rope_tab, bias, *, tm, tiles_per_seq, feature_states):
    m, d = x.shape
    tri = jnp.tril(jnp.ones((tm, tm), F32)).astype(BF16)
    row = lambda wid: pl.BlockSpec((tm, wid), lambda i: (i, 0))
    const = lambda shape: pl.BlockSpec(shape, lambda i: (0,) * len(shape))
    f32o = lambda wid: jax.ShapeDtypeStruct((m, wid), F32)
    b16o = lambda wid: jax.ShapeDtypeStruct((m, wid), BF16)
    if feature_states:
        nb, seq = m // (tm * tiles_per_seq), tm * tiles_per_seq
        st_shape = lambda wid: jax.ShapeDtypeStruct((nb, wid, seq), F32)
        st_spec = lambda wid: pl.BlockSpec((1, wid, tm), lambda i: (i // tiles_per_seq, 0, i % tiles_per_seq))
        lf_shape, lf_spec = st_shape(FOX_HEADS), st_spec(FOX_HEADS)
    else:
        st_shape, st_spec, lf_shape, lf_spec = f32o, row, f32o(LANES), row(LANES)
    out_shape = (
        jax.ShapeDtypeStruct((NSA_GROUPS, NSA_HPG, m, LANES), BF16),
        st_shape(KV_W), st_shape(KV_W), st_shape(KV_W), st_shape(2 * SB_W), st_shape(2 * SB_W),
        b16o(KV_W), b16o(KV_W), b16o(2 * SB_W), b16o(2 * SB_W),
        jax.ShapeDtypeStruct((SB_HEADS, m, SB_W), BF16), jax.ShapeDtypeStruct((SB_HEADS, m, SB_W), BF16),
        jax.ShapeDtypeStruct((NSA_GROUPS, m, LANES), F32), lf_shape, f32o(LANES),
    )
    out_specs = (
        pl.BlockSpec((NSA_GROUPS, NSA_HPG, tm, LANES), lambda i: (0, 0, i, 0)),
        st_spec(KV_W), st_spec(KV_W), st_spec(KV_W), st_spec(2 * SB_W), st_spec(2 * SB_W),
        row(KV_W), row(KV_W), row(2 * SB_W), row(2 * SB_W),
        pl.BlockSpec((SB_HEADS, tm, SB_W), lambda i: (0, i, 0)),
        pl.BlockSpec((SB_HEADS, tm, SB_W), lambda i: (0, i, 0)),
        pl.BlockSpec((NSA_GROUPS, tm, LANES), lambda i: (0, i, 0)),
        lf_spec, row(LANES),
    )
    return pl.pallas_call(
        functools.partial(_proj_kernel, tiles_per_seq=tiles_per_seq, feature_states=feature_states),
        grid=(m // tm,),
        in_specs=[row(d), const((1, d)), const((N_PROJ, d)),
                  pl.BlockSpec((tm, 3 * LANES), lambda i: (i % tiles_per_seq, 0)),
                  const((1, LANES)), const((tm, tm))],
        out_specs=out_specs, out_shape=out_shape,
        scratch_shapes=[pltpu.VMEM((tm, d), BF16), pltpu.VMEM((8, LANES), F32)],
        compiler_params=_cp("arbitrary"), name="proj",
    )(x, gain, w, rope_tab, bias, tri)


def _compress_kernel(pt_ref, *refs, pp, nch):
    pages, (perm_ref, w_ref, pe_ref, out_ref, x_scr, acc_scr, pos_scr) = refs[:pp], refs[pp:]
    j = pl.program_id(1)
    half = CMP_BLOCK // 2
    chunks = PAGE // CMP_STRIDE

    @pl.when((pl.program_id(0) == 0) & (j == 0))
    def _():
        c = jnp.zeros((8, KV_W), F32)
        for l in range(half):
            for off, cols in ((0, slice(0, KV_W)), (half, slice(KV_W, 2 * KV_W))):
                row = jnp.broadcast_to(pe_ref[off + l:off + l + 1, :], (8, KV_W))
                c = c + _split_dot(row, w_ref[l, :, cols], 2)
        pos_scr[...] = c

    for i in range(pp):
        gt = _dot_nt(perm_ref[...], pages[i][0].astype(BF16))
        base = pl.multiple_of((j * pp + i) * chunks, chunks)
        for l in range(half):
            x_scr[l, pl.ds(base, chunks), :] = gt[chunks * l:chunks * (l + 1), :]

    @pl.when(j == pl.num_programs(1) - 1)
    def _():
        acc_scr[...] = jnp.zeros_like(acc_scr)
        for l in range(half):
            acc_scr[...] += _dot(x_scr[l].astype(BF16), w_ref[l])
        nxt = pltpu.roll(acc_scr[:, KV_W:2 * KV_W], nch - 1, 0)
        nxt = jnp.where(_iota((nch, KV_W), 0) < nch - 1, nxt, 0.0)
        out_ref[0] = (acc_scr[:, 0:KV_W] + nxt + pos_scr[0:1, :]).astype(BF16)


def _compress(pool, page_table, w_cat, pe4, *, pp, paged):
    b, n_pages = page_table.shape
    chunks = PAGE // CMP_STRIDE
    nch = n_pages * chunks
    half = CMP_BLOCK // 2
    tok = np.arange(PAGE)
    perm = jnp.asarray(np.arange(PAGE)[:, None] == ((tok % CMP_STRIDE) * chunks + tok // CMP_STRIDE)[None, :], BF16)
    if paged:
        page_spec = lambda i: pl.BlockSpec((1, KV_W, PAGE), lambda bb, j, pt: (pt[bb, j * pp + i], 0, 0))
    else:
        page_spec = lambda i: pl.BlockSpec((1, KV_W, PAGE), lambda bb, j, pt: (bb, 0, j * pp + i))
    grid_spec = pltpu.PrefetchScalarGridSpec(
        num_scalar_prefetch=1, grid=(b, n_pages // pp),
        in_specs=[page_spec(i) for i in range(pp)] + [
            pl.BlockSpec((PAGE, PAGE), lambda bb, j, pt: (0, 0)),
            pl.BlockSpec((half, KV_W, 2 * KV_W), lambda bb, j, pt: (0, 0, 0)),
            pl.BlockSpec((CMP_BLOCK, KV_W), lambda bb, j, pt: (0, 0))],
        out_specs=pl.BlockSpec((1, nch, KV_W), lambda bb, j, pt: (bb, 0, 0)),
        scratch_shapes=[pltpu.VMEM((half, nch, KV_W), F32), pltpu.VMEM((nch, 2 * KV_W), F32),
                        pltpu.VMEM((8, KV_W), F32)])
    return pl.pallas_call(
        functools.partial(_compress_kernel, pp=pp, nch=nch), grid_spec=grid_spec,
        out_shape=jax.ShapeDtypeStruct((b, nch, KV_W), BF16),
        compiler_params=_cp("arbitrary", "arbitrary"), name="compress",
    )(page_table, *([pool] * pp), perm, w_cat, pe4)


def _masked_softmax_rows(s, mask):
    s = jnp.where(mask, s, NEG)
    e = jnp.where(mask, jnp.exp2(s - jnp.max(s, axis=1, keepdims=True)), 0.0)
    return e / jnp.maximum(jnp.sum(e, axis=1, keepdims=True), 1e-30)


def _online_init(m_scr, l_scr, acc_scr):
    m_scr[...] = jnp.full_like(m_scr, NEG)
    l_scr[...] = jnp.zeros_like(l_scr)
    acc_scr[...] = jnp.zeros_like(acc_scr)


def _select_blocks(imp, pos, n_sel):
    blk = _iota(imp.shape, 1)
    cur = pos // SEL_BLOCK
    valid = blk * SEL_BLOCK <= pos
    forced = jnp.where(blk == 0, 1.0, 0.0) + jnp.where(blk == cur, 1.0, 0.0) + jnp.where(blk == cur - 1, 1.0, 0.0)
    score = jnp.where(valid, jnp.where(forced > 0.5, SEL_FORCE_SCORE, imp), -1.0)
    score = jnp.where(blk < n_sel, score, -2.0)
    rank = jnp.zeros(imp.shape, F32)
    for b2 in range(n_sel):
        col = score[:, b2:b2 + 1]
        ge = jnp.where(col >= score, 1.0, 0.0)
        gt = jnp.where(col > score, 1.0, 0.0)
        rank = rank + jnp.where(blk > b2, ge, gt)
    return jnp.where(rank < min(TOP_N, n_sel), 1.0, 0.0)


def _rep(x, n):
    return x if n == 1 else jnp.concatenate([x] * n, axis=1)


def _select_blocks_t(p_grp, amat_t, q0, n_sel):
    t = p_grp.shape[0]
    nb = amat_t.shape[0]
    hi = p_grp.astype(BF16)
    lo = (p_grp - hi.astype(F32)).astype(BF16)
    imp = _dot_nt(amat_t, hi) + _dot_nt(amat_t, lo)
    pos = q0 + _iota((1, t), 1)
    blk = _iota((nb, t), 0)
    cur = pos // SEL_BLOCK
    forced = jnp.where(blk == 0, 1.0, 0.0) + jnp.where(blk == cur, 1.0, 0.0) + jnp.where(blk == cur - 1, 1.0, 0.0)
    score = jnp.where(blk * SEL_BLOCK <= pos, jnp.where(forced > 0.5, SEL_FORCE_SCORE, imp), -1.0)
    score = jnp.where(blk < n_sel, score, -2.0)
    groups = [score[8 * k:8 * (k + 1)] for k in range(nb // 8)]
    ranks = [jnp.zeros((8, t), F32) for _ in groups]
    sub = _iota((8, t), 0)
    for b2 in range(n_sel):
        rowv = score[b2:b2 + 1, :]
        k2, r2 = divmod(b2, 8)
        for k, grp in enumerate(groups):
            if k > k2:
                inc = jnp.where(rowv >= grp, 1.0, 0.0)
            elif k < k2:
                inc = jnp.where(rowv > grp, 1.0, 0.0)
            else:
                inc = jnp.where(sub > r2, jnp.where(rowv >= grp, 1.0, 0.0), jnp.where(rowv > grp, 1.0, 0.0))
            ranks[k] = ranks[k] + inc
    chosen = [jnp.where(rk < min(TOP_N, n_sel), 1.0, 0.0) for rk in ranks]
    pad = [jnp.zeros((LANES - nb, t), F32)] if nb < LANES else []
    return jnp.concatenate(chosen + pad, axis=0)


def _nsa_prompt_kernel(qn_ref, ckv_ref, ks_ref, kw_ref, gate_ref, amat_ref, e_ref, out_ref,
                       m_scr, l_scr, acc_scr, oc_scr, os_scr, *, n_sel):
    g, i = pl.program_id(1), pl.program_id(2)
    t = qn_ref.shape[2]
    q0 = i * t
    rowi, coli = _iota((t, t), 0), _iota((t, t), 1)
    reps = t // LANES

    nc = ckv_ref.shape[1]
    cmask = _iota((t, nc), 1) * CMP_STRIDE + (CMP_BLOCK - 1) <= q0 + _iota((t, 1), 0)
    ck, cv = ckv_ref[0, :, 0:128], ckv_ref[0, :, 128:256]
    p_grp = None
    for h in range(NSA_HPG):
        p = _masked_softmax_rows(_dot_nt(qn_ref[0, h], ck), cmask)
        oc_scr[h] = _dot(p.astype(BF16), cv)
        p_grp = p if p_grp is None else p_grp + p
    sel = _select_blocks_t(p_grp, amat_ref[...], q0, n_sel).T.astype(BF16)

    def online_head(h, s, v):
        m_prev = m_scr[h]
        m_next = jnp.maximum(m_prev, jnp.max(s, axis=1, keepdims=True))
        alpha = jnp.exp2(m_prev - m_next)
        e = jnp.exp2(s - _rep(m_next, reps))
        l_scr[h] = alpha * l_scr[h] + jnp.sum(e, axis=1, keepdims=True)
        m_scr[h] = m_next
        acc_scr[h] = acc_scr[h] * alpha + _dot(e.astype(BF16), v)

    def reset():
        m_scr[...] = jnp.full_like(m_scr, NEG)
        l_scr[...] = jnp.zeros_like(l_scr)
        acc_scr[...] = jnp.zeros_like(acc_scr)

    def sel_tile(j, diag):
        kt = ks_ref[0, pl.ds(pl.multiple_of(j * t, t), t), :]
        chosen = _dot(sel, e_ref[j])
        if diag:
            chosen = jnp.where(coli <= rowi, chosen, 0.0)
        keep = chosen > 0.5
        for h in range(NSA_HPG):
            online_head(h, jnp.where(keep, _dot_nt(qn_ref[0, h], kt[:, 0:128]), NEG), kt[:, 128:256])

    reset()
    lax.fori_loop(0, i, lambda j, c: (sel_tile(j, False), c)[1], 0)
    sel_tile(i, True)
    for h in range(NSA_HPG):
        os_scr[h] = acc_scr[h] / jnp.maximum(l_scr[h], 1e-30)

    tw = min(t, 2 * LANES)
    w_len = WINDOW + tw
    gates = gate_ref[0]
    lane = _iota((tw, LANES), 1)
    for rb in range(t // tw):
        rows = slice(rb * tw, (rb + 1) * tw)
        r0 = q0 + rb * tw
        k0 = jnp.maximum(r0 - WINDOW, 0)
        kw = kw_ref[0, pl.ds(pl.multiple_of(k0, tw), w_len), :]
        d = (r0 - k0) + _iota((tw, w_len), 0) - _iota((tw, w_len), 1)
        keep_w = jnp.where(d >= 0, jnp.where(d <= WINDOW, 1.0, 0.0), 0.0) > 0.5
        for h in range(NSA_HPG):
            s = jnp.where(keep_w, _dot_nt(qn_ref[0, h, rows, :], kw[:, 0:128]), NEG)
            e = jnp.exp2(s - jnp.max(s, axis=1, keepdims=True))
            o_w = _dot(e.astype(BF16), kw[:, 128:256]) / jnp.sum(e, axis=1, keepdims=True)
            o = (gates[rows, 3 * h:3 * h + 1] * oc_scr[h, rows, :] + gates[rows, 3 * h + 1:3 * h + 2] * os_scr[h, rows, :]
                 + gates[rows, 3 * h + 2:3 * h + 3] * o_w)
            out_ref[0, rows, LANES * h:LANES * (h + 1)] = jnp.where(lane // HEAD_DIM == g, o, 0.0).astype(BF16)


def _imp_matrix(nc, width):
    n = np.arange(nc)[:, None]
    b = np.arange(width)[None, :]
    sub = SEL_BLOCK // CMP_STRIDE
    return jnp.asarray((n // sub == b).astype(np.float32) + ((n + 1) // sub == b), BF16)


def _expand_tiles(n_tiles, n_blk, tk):
    key = np.arange(n_tiles * tk).reshape(n_tiles, 1, tk)
    return jnp.asarray(key // SEL_BLOCK == np.arange(n_blk)[None, :, None], BF16)


def _nsa_prompt(qn, ckv, kb_sel, kb_win, gates, *, batch, seq, tq):
    nq = seq // tq
    nc = ckv.shape[1]
    n_sel = seq // SEL_BLOCK
    assert n_sel <= LANES and WINDOW % tq == 0 and seq >= WINDOW + tq
    amat = _imp_matrix(nc, -(-n_sel // 8) * 8).T
    etile = _expand_tiles(nq, LANES, tq)
    m = batch * seq
    return pl.pallas_call(
        functools.partial(_nsa_prompt_kernel, n_sel=n_sel),
        grid=(batch, NSA_GROUPS, nq),
        in_specs=[
            pl.BlockSpec((1, NSA_HPG, tq, LANES), lambda b, g, i: (g, 0, b * nq + i, 0)),
            pl.BlockSpec((1, nc, KV_W), lambda b, g, i: (b, 0, 0)),
            pl.BlockSpec((1, seq, KV_W), lambda b, g, i: (b, 0, 0)),
            pl.BlockSpec((1, seq, KV_W), lambda b, g, i: (b, 0, 0)),
            pl.BlockSpec((1, tq, LANES), lambda b, g, i: (g, b * nq + i, 0)),
            pl.BlockSpec(amat.shape, lambda b, g, i: (0, 0)),
            pl.BlockSpec(etile.shape, lambda b, g, i: (0, 0, 0)),
        ],
        out_specs=pl.BlockSpec((1, tq, NSA_W), lambda b, g, i: (g, b * nq + i, 0)),
        out_shape=jax.ShapeDtypeStruct((NSA_GROUPS, m, NSA_W), BF16),
        scratch_shapes=[pltpu.VMEM((NSA_HPG, tq, LANES), F32)] * 5,
        compiler_params=_cp("arbitrary", "arbitrary", "arbitrary"), name="nsa_prompt",
    )(qn, ckv, kb_sel.reshape(batch, seq, KV_W), kb_win.reshape(batch, seq, KV_W), gates, amat, etile)


def _upper_strict(n):
    return jnp.asarray(np.triu(np.ones((n, n), np.float32), 1).T, BF16)


def _stack_heads(v):
    lane = _iota(v.shape, 1)
    return jnp.concatenate([jnp.where(lane // HEAD_DIM == h, v, jnp.zeros_like(v)) for h in range(SB_HEADS)], axis=0)


def _head_lanes(parts):
    lane = _iota(parts[0].shape, 1)
    return jnp.concatenate([jnp.where(lane < HEAD_DIM, parts[0], parts[1]),
                            jnp.where(lane < HEAD_DIM, parts[2], parts[3])], axis=1)


def _sb_prompt_kernel(q_ref, kv_ref, u_ref, out_ref, acc_scr, r_scr):
    i = pl.program_id(1)
    t = q_ref.shape[1]
    rowi, coli = _iota((t, t), 0), _iota((t, t), 1)
    acc_scr[...] = jnp.zeros_like(acc_scr)
    r_scr[...] = jnp.zeros_like(r_scr)

    def tile(j, diag):
        kt = kv_ref[0, pl.ds(pl.multiple_of(j * t, t), t), :]
        ws = []
        for h in range(SB_HEADS):
            z = _dot_nt(q_ref[h], kt[:, 0:SB_W])
            sp = _softplus2(z)
            cost = jnp.where(coli < rowi, sp, 0.0) if diag else sp
            r = r_scr[h]
            w = jnp.exp2(z - sp - _split_dot(cost, u_ref[...], 1) - _rep(r, t // LANES))
            if diag:
                w = jnp.where(coli < rowi, w, 0.0)
            ws.append(w.astype(BF16))
            r_scr[h] = r + jnp.sum(cost, axis=1, keepdims=True)
        acc_scr[...] += _dot(jnp.concatenate(ws, axis=1), _stack_heads(kt[:, SB_W:2 * SB_W]))

    tile(i, True)
    lax.fori_loop(0, i, lambda n, c: (tile(i - 1 - n, False), c)[1], 0)
    out_ref[...] = acc_scr[...].astype(BF16)


def _sb_prompt(q, kb, *, batch, seq, tq):
    nq = seq // tq
    u = _upper_strict(tq)
    return pl.pallas_call(
        _sb_prompt_kernel, grid=(batch, nq),
        in_specs=[pl.BlockSpec((SB_HEADS, tq, SB_W), lambda b, i: (0, b * nq + i, 0)),
                  pl.BlockSpec((1, seq, 2 * SB_W), lambda b, i: (b, 0, 0)),
                  pl.BlockSpec(u.shape, lambda b, i: (0, 0))],
        out_specs=pl.BlockSpec((tq, SB_W), lambda b, i: (b * nq + i, 0)),
        out_shape=jax.ShapeDtypeStruct((batch * seq, SB_W), BF16),
        scratch_shapes=[pltpu.VMEM((tq, SB_W), F32), pltpu.VMEM((SB_HEADS, tq, LANES), F32)],
        compiler_params=_cp("arbitrary", "arbitrary"), name="sb_prompt",
    )(q, kb.reshape(batch, seq, 2 * SB_W), u)


def _fox_prompt_kernel(q_ref, kv_ref, cq_ref, ck_ref, out_ref, m_scr, l_scr, acc_scr, cq_scr):
    i = pl.program_id(1)
    t = q_ref.shape[1]
    rowi, coli = _iota((t, t), 0), _iota((t, t), 1)
    m_scr[...] = jnp.full_like(m_scr, NEG)
    l_scr[...] = jnp.zeros_like(l_scr)
    acc_scr[...] = jnp.zeros_like(acc_scr)
    for h in range(FOX_HEADS):
        cq_scr[h] = jnp.broadcast_to(cq_ref[:, LOGF_LANE + h:LOGF_LANE + h + 1] * LOG2E, (t, LANES))

    def tile(j, diag):
        kt = kv_ref[0, pl.ds(pl.multiple_of(j * t, t), t), :]
        es, alphas = [], []
        for h in range(FOX_HEADS):
            s = _dot_nt(q_ref[h], kt[:, 0:SB_W]) + (_rep(cq_scr[h], t // LANES) - ck_ref[0, h, j] * LOG2E)
            if diag:
                s = jnp.where(coli <= rowi, s, NEG)
            m_prev = m_scr[h]
            m_next = jnp.maximum(m_prev, jnp.max(s, axis=1, keepdims=True))
            alpha = jnp.exp2(m_prev - m_next)
            e = jnp.exp2(s - _rep(m_next, t // LANES))
            l_scr[h] = alpha * l_scr[h] + jnp.sum(e, axis=1, keepdims=True)
            m_scr[h] = m_next
            es.append(e.astype(BF16))
            alphas.append(alpha)
        acc_scr[...] = acc_scr[...] * _head_lanes(alphas) + _dot(jnp.concatenate(es, axis=1),
                                                                 _stack_heads(kt[:, SB_W:2 * SB_W]))

    lax.fori_loop(0, i, lambda j, c: (tile(j, False), c)[1], 0)
    tile(i, True)
    out_ref[...] = (acc_scr[...] / _head_lanes([jnp.maximum(l_scr[h], 1e-30) for h in range(FOX_HEADS)])).astype(BF16)


def _fox_prompt(q, kb, cum, cum_k, *, batch, seq, tq):
    nq = seq // tq
    nk = seq // tq
    return pl.pallas_call(
        _fox_prompt_kernel, grid=(batch, nq),
        in_specs=[pl.BlockSpec((FOX_HEADS, tq, SB_W), lambda b, i: (0, b * nq + i, 0)),
                  pl.BlockSpec((1, seq, 2 * SB_W), lambda b, i: (b, 0, 0)),
                  pl.BlockSpec((tq, LANES), lambda b, i: (b * nq + i, 0)),
                  pl.BlockSpec((1, FOX_HEADS, nk, 1, tq), lambda b, i: (b, 0, 0, 0, 0))],
        out_specs=pl.BlockSpec((tq, SB_W), lambda b, i: (b * nq + i, 0)),
        out_shape=jax.ShapeDtypeStruct((batch * seq, SB_W), BF16),
        scratch_shapes=[pltpu.VMEM((FOX_HEADS, tq, LANES), F32), pltpu.VMEM((FOX_HEADS, tq, LANES), F32),
                        pltpu.VMEM((tq, SB_W), F32), pltpu.VMEM((FOX_HEADS, tq, LANES), F32)],
        compiler_params=_cp("arbitrary", "arbitrary"), name="fox_prompt",
    )(q, kb.reshape(batch, seq, 2 * SB_W), cum, cum_k)


def _merge_kernel(x_ref, g_ref, oa_ref, ob_ref, oc_ref, wm_ref, wa_ref, wb_ref, wc_ref, wo_ref, out_ref):
    x = x_ref[...]
    d = x.shape[1]
    hb = _rms(x, g_ref[...]).astype(BF16)
    oa = oa_ref[0] + oa_ref[1]
    u = jax.nn.sigmoid(_dot_nt(hb, wm_ref[0:d, :])) * _dot(oa, wa_ref[...])
    u = u + jax.nn.sigmoid(_dot_nt(hb, wm_ref[d:2 * d, :])) * _dot(ob_ref[...], wb_ref[...])
    u = u + jax.nn.sigmoid(_dot_nt(hb, wm_ref[2 * d:3 * d, :])) * _dot(oc_ref[...], wc_ref[...])
    out_ref[...] = x + _dot(u.astype(BF16), wo_ref[...])


def _merge(x, gain, o_nsa, o_sb, o_fox, w_merge, w_a, w_b, w_c, w_o, *, tm):
    m, d = x.shape
    row = lambda wid: pl.BlockSpec((tm, wid), lambda i: (i, 0))
    const = lambda a: pl.BlockSpec(a.shape, lambda i: (0,) * a.ndim)
    return pl.pallas_call(
        _merge_kernel, grid=(m // tm,),
        in_specs=[row(d), const(gain), pl.BlockSpec((NSA_GROUPS, tm, NSA_W), lambda i: (0, i, 0)),
                  row(SB_W), row(SB_W), const(w_merge), const(w_a), const(w_b), const(w_c), const(w_o)],
        out_specs=row(d), out_shape=jax.ShapeDtypeStruct((m, d), F32),
        compiler_params=_cp("arbitrary"), name="merge",
    )(x, gain, o_nsa, o_sb, o_fox, w_merge, w_a, w_b, w_c, w_o)


def _ffn_kernel(x_ref, g_ref, wg_ref, wu_ref, wd_ref, gf_ref, out_ref, *, final_norm):
    x = x_ref[...]
    hb = _rms(x, g_ref[...]).astype(BF16)
    a = _dot(hb, wg_ref[...])
    act = a * jax.nn.sigmoid(a) * _dot(hb, wu_ref[...])
    y = x + _dot(act.astype(BF16), wd_ref[...])
    out_ref[...] = _rms(y, gf_ref[...]) if final_norm else y


def _ffn(x, gain, w_gate, w_up, w_down, final_gain, *, tm, final_norm):
    m, d = x.shape
    row = pl.BlockSpec((tm, d), lambda i: (i, 0))
    vec = pl.BlockSpec((1, d), lambda i: (0, 0))
    const = lambda a: pl.BlockSpec(a.shape, lambda i: (0,) * a.ndim, pipeline_mode=pl.Buffered(1))
    return pl.pallas_call(
        functools.partial(_ffn_kernel, final_norm=final_norm), grid=(m // tm,),
        in_specs=[row, vec, const(w_gate), const(w_up), const(w_down), vec],
        out_specs=row, out_shape=jax.ShapeDtypeStruct((m, d), F32),
        compiler_params=_cp("arbitrary"), name="ffn",
    )(x, gain, w_gate, w_up, w_down, final_gain)


def _page_specs(rows, n_pages, pp, descending):
    def spec(i):
        if descending:
            return pl.BlockSpec((1, rows, PAGE), lambda b, j, pt: (pt[b, n_pages - 1 - (j * pp + i)], 0, 0))
        return pl.BlockSpec((1, rows, PAGE), lambda b, j, pt: (pt[b, j * pp + i], 0, 0))
    return [spec(i) for i in range(pp)]


def _lanes(parts):
    return jnp.concatenate(parts, axis=1)


def _sb_dec_kernel(pt_ref, q_ref, *refs, pp):
    pages, (u_ref, out_ref, acc_scr, r_scr) = refs[:pp], refs[pp:]
    j = pl.program_id(1)

    @pl.when(j == 0)
    def _():
        acc_scr[...] = jnp.zeros_like(acc_scr)
        r_scr[...] = jnp.zeros_like(r_scr)

    q = q_ref[0]
    z_all = _dot(q, _lanes([p[0, 0:SB_W, :].astype(BF16) for p in pages]))
    zs = [z_all[:, i * PAGE:(i + 1) * PAGE] for i in range(pp)]
    sps = [_softplus2(z) for z in zs]
    afters = [_split_dot(sp, u_ref[...], 2) for sp in sps]
    r = r_scr[...]
    ws = []
    for i in range(pp):
        ws.append(jnp.exp2(zs[i] - sps[i] - afters[i] - r).astype(BF16))
        r = r + jnp.sum(sps[i], axis=1, keepdims=True)
    acc_scr[...] += _dot_nt(_lanes(ws), _lanes([p[0, SB_W:2 * SB_W, :].astype(BF16) for p in pages]))
    r_scr[...] = r

    @pl.when(j == pl.num_programs(1) - 1)
    def _():
        out_ref[0] = acc_scr[...]


def _sb_dec(q8, pool, page_table, *, pp):
    bd, n_pages = page_table.shape
    u = _upper_strict(PAGE)
    grid_spec = pltpu.PrefetchScalarGridSpec(
        num_scalar_prefetch=1, grid=(bd, n_pages // pp),
        in_specs=[pl.BlockSpec((1, 8, SB_W), lambda b, j, pt: (b, 0, 0))]
        + _page_specs(2 * SB_W, n_pages, pp, True) + [pl.BlockSpec(u.shape, lambda b, j, pt: (0, 0))],
        out_specs=pl.BlockSpec((1, 8, SB_W), lambda b, j, pt: (b, 0, 0)),
        scratch_shapes=[pltpu.VMEM((8, SB_W), F32), pltpu.VMEM((8, LANES), F32)])
    return pl.pallas_call(
        functools.partial(_sb_dec_kernel, pp=pp), grid_spec=grid_spec,
        out_shape=jax.ShapeDtypeStruct((bd, 8, SB_W), F32),
        compiler_params=_cp("arbitrary", "arbitrary"), name="sb_decode",
    )(page_table, q8, *([pool] * pp), u)


def _fox_dec_kernel(pt_ref, q_ref, kn_ref, vn_ref, lfn_ref, *refs, pp):
    pages, lf_pages = refs[:pp], refs[pp:2 * pp]
    u_ref, out_ref, m_scr, l_scr, acc_scr, r_scr = refs[2 * pp:]
    j = pl.program_id(1)
    q = q_ref[0]
    reps = SB_W // LANES

    @pl.when(j == 0)
    def _():
        kn = kn_ref[0].astype(BF16).astype(F32)
        m_scr[...] = jnp.broadcast_to(jnp.sum(q.astype(F32) * kn, axis=1, keepdims=True), m_scr.shape)
        l_scr[...] = jnp.ones_like(l_scr)
        acc_scr[...] = jnp.broadcast_to(vn_ref[0].astype(BF16).astype(F32), acc_scr.shape)
        r_scr[...] = lfn_ref[0]

    z_all = _dot(q, _lanes([p[0, 0:SB_W, :].astype(BF16) for p in pages]))
    pad = jnp.zeros((8 - FOX_HEADS, PAGE), F32)
    r = r_scr[...]
    ss = []
    for i in range(pp):
        lf = jnp.concatenate([lf_pages[i][0], pad], axis=0)
        ss.append(z_all[:, i * PAGE:(i + 1) * PAGE] + LOG2E * (_split_dot(lf, u_ref[...], 3) + r))
        r = r + jnp.sum(lf, axis=1, keepdims=True)
    r_scr[...] = r
    m_prev = m_scr[...]
    m_next = m_prev
    for s in ss:
        m_next = jnp.maximum(m_next, jnp.max(s, axis=1, keepdims=True))
    alpha = jnp.exp2(m_prev - m_next)
    es = [jnp.exp2(s - m_next) for s in ss]
    l_new = alpha * l_scr[...]
    for e in es:
        l_new = l_new + jnp.sum(e, axis=1, keepdims=True)
    l_scr[...] = l_new
    m_scr[...] = m_next
    acc_scr[...] = acc_scr[...] * _rep(alpha, reps) + _dot_nt(
        _lanes([e.astype(BF16) for e in es]), _lanes([p[0, SB_W:2 * SB_W, :].astype(BF16) for p in pages]))

    @pl.when(j == pl.num_programs(1) - 1)
    def _():
        out_ref[0] = acc_scr[...] / _rep(l_scr[...], reps)


def _fox_dec(q8, k_new, v_new, lf_new, pool, lf_pool, page_table, *, pp):
    bd, n_pages = page_table.shape
    u = _upper_strict(PAGE)
    req = lambda shape: pl.BlockSpec((1,) + shape, lambda b, j, pt: (b,) + (0,) * len(shape))
    lf_spec = lambda i: pl.BlockSpec((1, FOX_HEADS, PAGE), lambda b, j, pt: (pt[b, n_pages - 1 - (j * pp + i)], 0, 0))
    grid_spec = pltpu.PrefetchScalarGridSpec(
        num_scalar_prefetch=1, grid=(bd, n_pages // pp),
        in_specs=[req((8, SB_W)), req((1, SB_W)), req((1, SB_W)), req((8, LANES))]
        + _page_specs(2 * SB_W, n_pages, pp, True) + [lf_spec(i) for i in range(pp)]
        + [pl.BlockSpec(u.shape, lambda b, j, pt: (0, 0))],
        out_specs=req((8, SB_W)),
        scratch_shapes=[pltpu.VMEM((8, LANES), F32), pltpu.VMEM((8, LANES), F32),
                        pltpu.VMEM((8, SB_W), F32), pltpu.VMEM((8, LANES), F32)])
    return pl.pallas_call(
        functools.partial(_fox_dec_kernel, pp=pp), grid_spec=grid_spec,
        out_shape=jax.ShapeDtypeStruct((bd, 8, SB_W), F32),
        compiler_params=_cp("arbitrary", "arbitrary"), name="fox_decode",
    )(page_table, q8, k_new, v_new, lf_new, *([pool] * pp), *([lf_pool] * pp), u)


def _nsa_dec_kernel(pt_ref, q_ref, ckv_ref, gate_ref, win_ref, kvs_new_ref, kvw_new_ref, kvw_col_ref, amat_ref, e_ref,
                    *refs, pp, past_len, n_sel):
    pages, (out_ref, win_out_ref, m_scr, l_scr, acc_scr, sel_scr, oc_scr, ow_scr) = refs[:pp], refs[pp:]
    j = pl.program_id(1)
    q = q_ref[0]
    rows = q.shape[0]
    row = _iota((rows, LANES), 0)
    row1 = _iota((rows, 1), 0)
    lane = _iota((rows, LANES), 1)
    pos = jnp.full((rows, 1), past_len, I32)

    def new_token(kv_row):
        k = kv_row[:, 0:LANES].astype(BF16).astype(F32)
        return jnp.sum(q.astype(F32) * k, axis=1, keepdims=True), kv_row[:, LANES:2 * LANES].astype(BF16).astype(F32)

    @pl.when(j == 0)
    def _():
        nc = ckv_ref.shape[1]
        cmask = _iota((rows, nc), 1) * CMP_STRIDE + (CMP_BLOCK - 1) <= pos
        p = _masked_softmax_rows(_dot_nt(q, ckv_ref[0, :, 0:128]), cmask)
        oc_scr[...] = _dot(p.astype(BF16), ckv_ref[0, :, 128:256])
        g0 = jnp.sum(p[0:NSA_HPG], axis=0, keepdims=True)
        g1 = jnp.sum(p[NSA_HPG:2 * NSA_HPG], axis=0, keepdims=True)
        p_grp = jnp.where(row1 < NSA_HPG, g0, g1)
        imp = _split_dot(p_grp, amat_ref[...], 2)
        sel_scr[...] = _select_blocks(imp, pos, n_sel)

        wk = win_ref[0]
        n_w = wk.shape[1]
        s = _dot(q, wk[0:LANES, :].astype(BF16))
        kpos = past_len - n_w + _iota((rows, n_w), 1)
        keep = jnp.where(kpos >= 0, jnp.where(pos - kpos <= WINDOW, 1.0, 0.0), 0.0) > 0.5
        s = jnp.where(keep, s, NEG)
        s_new, v_new = new_token(kvw_new_ref[0])
        m = jnp.maximum(s_new, jnp.max(s, axis=1, keepdims=True))
        e = jnp.exp2(s - m)
        e_new = jnp.exp2(s_new - m)
        o_w = _dot_nt(e.astype(BF16), wk[LANES:2 * LANES, :].astype(BF16)) + e_new * v_new
        ow_scr[...] = o_w / (jnp.sum(e, axis=1, keepdims=True) + e_new)

        shifted = pltpu.roll(wk, n_w - 1, 1)
        win_out_ref[0] = jnp.where(_iota(wk.shape, 1) == n_w - 1, kvw_col_ref[0], shifted)
        _online_init(m_scr, l_scr, acc_scr)

    sel = sel_scr[...].astype(BF16)
    z_all = _dot(q, _lanes([p[0, 0:LANES, :].astype(BF16) for p in pages]))
    ss = []
    for i in range(pp):
        chosen = _dot(sel, e_ref[j * pp + i]) > 0.5
        ss.append(jnp.where(chosen, z_all[:, i * PAGE:(i + 1) * PAGE], NEG))
    m_prev = m_scr[...]
    m_next = m_prev
    for s in ss:
        m_next = jnp.maximum(m_next, jnp.max(s, axis=1, keepdims=True))
    alpha = jnp.exp2(m_prev - m_next)
    es = [jnp.exp2(s - m_next) for s in ss]
    l_new = alpha * l_scr[...]
    for e in es:
        l_new = l_new + jnp.sum(e, axis=1, keepdims=True)
    l_scr[...] = l_new
    m_scr[...] = m_next
    acc_scr[...] = acc_scr[...] * alpha + _dot_nt(_lanes([e.astype(BF16) for e in es]),
                                                  _lanes([p[0, LANES:2 * LANES, :].astype(BF16) for p in pages]))

    @pl.when(j == pl.num_programs(1) - 1)
    def _():
        s_new, v_new = new_token(kvs_new_ref[0])
        cur = past_len // SEL_BLOCK
        chosen = jnp.sum(jnp.where(_iota(sel_scr.shape, 1) == cur, sel_scr[...], 0.0), axis=1, keepdims=True)
        s_new = jnp.where(chosen > 0.5, s_new, NEG)
        m_prev = m_scr[...]
        m_next = jnp.maximum(m_prev, s_new)
        alpha = jnp.exp2(m_prev - m_next)
        e_new = jnp.exp2(s_new - m_next)
        o_s = (acc_scr[...] * alpha + e_new * v_new) / jnp.maximum(alpha * l_scr[...] + e_new, 1e-30)
        gates = jnp.where(row < NSA_HPG, gate_ref[0, 0:1, :], gate_ref[0, 1:2, :])
        hh = row % NSA_HPG

        def gate(c):
            return jnp.sum(jnp.where(lane == 3 * hh + c, gates, 0.0), axis=1, keepdims=True)

        out_ref[0] = gate(0) * oc_scr[...] + gate(1) * o_s + gate(2) * ow_scr[...]


def _nsa_dec(q8, ckv, gates, win_state, kvs_new, kvw_new, pool, page_table, *, pp, layer):
    bd, n_pages = page_table.shape
    past_len = n_pages * PAGE
    n_sel = -(-(past_len + 1) // SEL_BLOCK)
    nc = ckv.shape[1]
    wid = -(-n_sel // LANES) * LANES
    amat = _imp_matrix(nc, wid)
    etile = _expand_tiles(n_pages, wid, PAGE)
    n_w = win_state.shape[2]
    req = lambda shape: pl.BlockSpec((1,) + shape, lambda b, j, pt: (b,) + (0,) * len(shape))
    const = lambda a: pl.BlockSpec(a.shape, lambda b, j, pt: (0,) * a.ndim)
    win_spec = pl.BlockSpec((1, KV_W, n_w), lambda b, j, pt: (layer * bd + b, 0, 0))
    grid_spec = pltpu.PrefetchScalarGridSpec(
        num_scalar_prefetch=1, grid=(bd, n_pages // pp),
        in_specs=[req((8, LANES)), req((nc, KV_W)), req((NSA_GROUPS, LANES)), win_spec,
                  req((1, KV_W)), req((1, KV_W)), req((KV_W, 1)), const(amat), const(etile)]
        + _page_specs(KV_W, n_pages, pp, False),
        out_specs=(req((8, LANES)), req((KV_W, n_w))),
        scratch_shapes=[pltpu.VMEM((8, LANES), F32)] * 3 + [pltpu.VMEM((8, wid), F32)] + [pltpu.VMEM((8, LANES), F32)] * 2)
    return pl.pallas_call(
        functools.partial(_nsa_dec_kernel, pp=pp, past_len=past_len, n_sel=n_sel), grid_spec=grid_spec,
        out_shape=(jax.ShapeDtypeStruct((bd, 8, LANES), F32), jax.ShapeDtypeStruct((bd, KV_W, n_w), F32)),
        compiler_params=_cp("arbitrary", "arbitrary"), name="nsa_decode",
    )(page_table, q8, ckv, gates, win_state, kvs_new, kvw_new, kvw_new.reshape(bd, KV_W, 1), amat, etile,
      *([pool] * pp))


def _rope_table(pos):
    half = ROT_DIM // 2
    inv = ROPE_THETA ** (-jnp.arange(half, dtype=F32) / half)
    ang = pos.astype(F32)[:, None] * inv[None, :]
    cos, sin = jnp.cos(ang), jnp.sin(ang)
    n = pos.shape[0]
    pad = HEAD_DIM - ROT_DIM
    one_head = lambda a, b, fill: jnp.concatenate([a, b, jnp.full((n, pad), fill, F32)], axis=1)
    zero = jnp.zeros_like(sin)
    tabs = [one_head(cos, cos, 1.0), one_head(-sin, zero, 0.0), one_head(zero, sin, 0.0)]
    return jnp.concatenate([jnp.tile(t, (1, LANES // HEAD_DIM)) for t in tabs], axis=1)


_Q_HEAD_ORDER = (0, 4, 1, 5, 2, 6, 3, 7)


def _layer_weights(l, w_in, b_forget, cmp_pos, w_cmp_k, w_cmp_v, w_up_nsa):
    w = jnp.swapaxes(w_in, 1, 2)[l]
    d = w.shape[1]
    gate0 = NSA_W + 3 * KV_W
    pf0 = gate0 + GATE_W + 6 * SB_W
    merge0 = pf0 + FOX_HEADS
    rows_ = [w[HEAD_DIM * h:HEAD_DIM * (h + 1)] for h in _Q_HEAD_ORDER]
    rows_ += [w[NSA_W:gate0], w[gate0 + GATE_W:pf0], w[gate0:gate0 + GATE_W], w[pf0:merge0]]
    rows_.append(jnp.zeros((N_PROJ - merge0, d), F32))
    w_proj = jnp.concatenate(rows_, axis=0).astype(BF16)
    w_merge = w[merge0:].astype(BF16)
    zpad = lambda n: jnp.zeros((1, n), F32)
    bias = jnp.concatenate([zpad(LOGF_LANE), b_forget[l].reshape(1, FOX_HEADS), zpad(LANES - LOGF_LANE - FOX_HEADS)], axis=1)
    wk = w_cmp_k[l].reshape(CMP_BLOCK, HEAD_DIM, HEAD_DIM)
    wv = w_cmp_v[l].reshape(CMP_BLOCK, HEAD_DIM, HEAD_DIM)
    zero = jnp.zeros_like(wk)
    w_blk = jnp.concatenate([jnp.concatenate([wsrc if c == n else zero for c in range(4)], axis=2)
                             for n, wsrc in enumerate((wk, wk, wv, wv))], axis=1)
    w_cat = jnp.concatenate([w_blk[0:CMP_BLOCK // 2], w_blk[CMP_BLOCK // 2:]], axis=2).astype(BF16)
    pe4 = jnp.tile(cmp_pos[l], (1, KV_W // HEAD_DIM))
    w_a = jnp.concatenate([w_up_nsa[l][HEAD_DIM * h:HEAD_DIM * (h + 1)] for h in _Q_HEAD_ORDER], axis=0).astype(BF16)
    return w_proj, w_merge, bias, w_cat, pe4, w_a


def _pad_heads(a):
    return jnp.concatenate([a, jnp.zeros((a.shape[0], 8 - a.shape[1], a.shape[2]), a.dtype)], axis=1)


def _token_lanes(cache):
    l, p, t = cache.shape[:3]
    return cache.transpose(0, 1, 3, 4, 5, 2).reshape(l * p, -1, t)


def _own_lanes(o8, n_heads):
    n = o8.shape[0]
    o = o8[:, 0:n_heads].reshape(n, n_heads, n_heads, HEAD_DIM)
    return jnp.stack([o[:, h, h] for h in range(n_heads)], axis=1).reshape(n, n_heads * HEAD_DIM)


def kernel(x_prompt, x_sample, cache_nsa_cmp_kv, cache_nsa_sel_kv, cache_sb_kv, cache_fox_kv, cache_fox_logf,
           state_nsa_win_kv, page_table, norm_mix_g, norm_ffn_g, norm_final_g, w_in, b_forget, cmp_pos,
           w_cmp_k, w_cmp_v, w_up_nsa, w_up_sb, w_up_fox, w_out, w_ffn_gate, w_ffn_up, w_ffn_down):
    batch, seq, d = x_prompt.shape
    bd = x_sample.shape[0]
    depth = w_in.shape[0]
    n_pool = cache_sb_kv.shape[1]
    n_pages = page_table.shape[1]
    past_len = n_pages * PAGE
    m = batch * seq
    tm = min(512, seq)
    tq = min(256, seq)
    tq_soft = min(512, seq)
    pp = min(PAGES_PER_STEP, n_pages)
    pp_prompt = min(PAGES_PER_STEP, seq // PAGE)

    rope_p = _rope_table(jnp.arange(seq, dtype=I32))
    rope_s = _rope_table(jnp.full((bd,), past_len, I32))
    ident_pt = jnp.arange(batch * (seq // PAGE), dtype=I32).reshape(batch, seq // PAGE)
    pool_cmp, pool_sel = _token_lanes(cache_nsa_cmp_kv), _token_lanes(cache_nsa_sel_kv)
    pool_sb, pool_fox = _token_lanes(cache_sb_kv), _token_lanes(cache_fox_kv)
    pool_lf = cache_fox_logf.astype(F32).transpose(0, 1, 3, 2).reshape(depth * n_pool, FOX_HEADS, PAGE)
    win_state = _token_lanes(state_nsa_win_kv)

    xp = x_prompt.reshape(m, d)
    xs = x_sample.reshape(bd, d)
    g_fin = norm_final_g.reshape(1, d)
    p_states, s_states = [], []
    for l in range(depth):
        w_proj, w_merge, bias, w_blk, pe4, w_a = _layer_weights(l, w_in, b_forget, cmp_pos, w_cmp_k, w_cmp_v, w_up_nsa)
        w_b, w_c, w_o = w_up_sb[l].astype(BF16), w_up_fox[l].astype(BF16), w_out[l].astype(BF16)
        w_g, w_u, w_d = w_ffn_gate[l].astype(BF16), w_ffn_up[l].astype(BF16), w_ffn_down[l].astype(BF16)
        g_mix, g_ffn = norm_mix_g[l].reshape(1, d), norm_ffn_g[l].reshape(1, d)

        (qn, kv_cmp, kv_sel, kv_win, kv_sb, kv_fox, kb_sel, kb_win, kb_sb, kb_fox, q_sb, q_fox, gates, logf, cum
         ) = _proj(xp, g_mix, w_proj, rope_p, bias, tm=tm, tiles_per_seq=seq // tm, feature_states=True)
        ckv = _compress(kv_cmp, ident_pt, w_blk, pe4, pp=pp_prompt, paged=False)
        o_nsa = _nsa_prompt(qn, ckv, kb_sel, kb_win, gates, batch=batch, seq=seq, tq=tq_soft)
        o_sb = _sb_prompt(q_sb, kb_sb, batch=batch, seq=seq, tq=tq)
        cum4 = cum[:, LOGF_LANE:LOGF_LANE + FOX_HEADS]
        cum_k = cum4.reshape(batch, seq, FOX_HEADS).transpose(0, 2, 1).reshape(
            batch, FOX_HEADS, seq // tq_soft, 1, tq_soft)
        o_fox = _fox_prompt(q_fox, kb_fox, cum, cum_k, batch=batch, seq=seq, tq=tq_soft)
        xp = _merge(xp, g_mix, o_nsa, o_sb, o_fox, w_merge, w_a, w_b, w_c, w_o, tm=tm)
        xp = _ffn(xp, g_ffn, w_g, w_u, w_d, g_fin, tm=tm, final_norm=(l == depth - 1))
        win_keep = min(WINDOW, seq)
        p_states.append((kv_cmp, kv_sel, kv_sb, kv_fox, logf, kv_win[:, :, seq - win_keep:]))

        (qn, kv_cmp, kv_sel, kv_win, kv_sb, kv_fox, _, _, _, _, q_sb, q_fox, gates, logf, _
         ) = _proj(xs, g_mix, w_proj, rope_s, bias, tm=bd, tiles_per_seq=1, feature_states=False)
        pt_l = page_table + l * n_pool
        ckv = _compress(pool_cmp, pt_l, w_blk, pe4, pp=pp, paged=True)
        q8 = qn.transpose(2, 0, 1, 3).reshape(bd, NSA_HEADS, LANES)
        o_nsa8, win_new = _nsa_dec(q8, ckv, gates.transpose(1, 0, 2), win_state, kv_sel.reshape(bd, 1, KV_W),
                                   kv_win.reshape(bd, 1, KV_W), pool_sel, pt_l, pp=pp, layer=l)
        o_sb8 = _sb_dec(_pad_heads(q_sb.transpose(1, 0, 2)), pool_sb, pt_l, pp=pp)
        lf4 = logf[:, LOGF_LANE:LOGF_LANE + FOX_HEADS]
        lf_new = _pad_heads(jnp.broadcast_to(lf4[:, :, None], (bd, FOX_HEADS, LANES)))
        o_fox8 = _fox_dec(_pad_heads(q_fox.transpose(1, 0, 2)), kv_fox[:, 0:SB_W].reshape(bd, 1, SB_W),
                          kv_fox[:, SB_W:2 * SB_W].reshape(bd, 1, SB_W), lf_new, pool_fox, pool_lf, pt_l, pp=pp)
        o_a = o_nsa8.reshape(bd, NSA_GROUPS, NSA_HPG, NSA_GROUPS, HEAD_DIM)
        o_a = jnp.stack([o_a[:, g, :, g] for g in range(NSA_GROUPS)], axis=2).reshape(bd, NSA_W)
        o_a = jnp.stack([o_a, jnp.zeros_like(o_a)], axis=0).astype(BF16)
        o_b = _own_lanes(o_sb8, SB_HEADS).astype(BF16)
        o_c = _own_lanes(o_fox8, FOX_HEADS).astype(BF16)
        xs = _merge(xs, g_mix, o_a, o_b, o_c, w_merge, w_a, w_b, w_c, w_o, tm=bd)
        xs = _ffn(xs, g_ffn, w_g, w_u, w_d, g_fin, tm=bd, final_norm=(l == depth - 1))
        s_states.append((
            kv_cmp.reshape(bd, 1, 2, NSA_GROUPS, HEAD_DIM), kv_sel.reshape(bd, 1, 2, NSA_GROUPS, HEAD_DIM),
            kv_sb.reshape(bd, 1, 2, SB_HEADS, HEAD_DIM), kv_fox.reshape(bd, 1, 2, FOX_HEADS, HEAD_DIM),
            lf4.reshape(bd, 1, FOX_HEADS),
            win_new.reshape(bd, 2, NSA_GROUPS, HEAD_DIM, -1).transpose(0, 4, 1, 2, 3)))

    y_prompt = xp.reshape(batch, seq, d)
    y_sample = xs.reshape(bd, 1, d)
    stk = lambda sts, i: jnp.stack([st[i] for st in sts], axis=0)

    def kv_state(i, heads):
        a = stk(p_states, i)
        return a.reshape(depth, batch, 2, heads, HEAD_DIM, a.shape[-1]).transpose(0, 1, 5, 2, 3, 4)

    p_out = (kv_state(0, NSA_GROUPS), kv_state(1, NSA_GROUPS), kv_state(2, SB_HEADS), kv_state(3, FOX_HEADS),
             stk(p_states, 4).transpose(0, 1, 3, 2), kv_state(5, NSA_GROUPS))
    return (y_prompt, y_sample) + p_out + tuple(stk(s_states, i) for i in range(6))
```

```python
import functools

import numpy as np
import jax
import jax.numpy as jnp
from jax import lax
from jax.experimental import pallas as pl
from jax.experimental.pallas import tpu as pltpu

F32, BF16, I32 = jnp.float32, jnp.bfloat16, jnp.int32

HEAD_DIM = 64
ROT_DIM = HEAD_DIM // 4
ROPE_THETA = 500000.0
NSA_HEADS = 8
NSA_GROUPS = 2
NSA_HPG = NSA_HEADS // NSA_GROUPS
CMP_STRIDE = 16
CMP_BLOCK = 32
SEL_BLOCK = 64
TOP_N = 16
WINDOW = 512
SB_HEADS = 4
FOX_HEADS = 4
PAGE = 128
SEL_FORCE_SCORE = 1.0e4
NEG = -1.0e30
RMS_EPS = 1e-6
SCALE = HEAD_DIM ** -0.5
LOG2E = 1.4426950408889634

LANES = 128
NSA_W = NSA_HEADS * HEAD_DIM
KV_W = 2 * NSA_GROUPS * HEAD_DIM
SB_W = SB_HEADS * HEAD_DIM
GATE_W = 3 * NSA_HEADS
C_Q, C_CMP, C_SEL, C_WIN, C_SB, C_FOX, C_MISC = 0, 512, 768, 1024, 1280, 2048, 2816
N_PROJ = C_MISC + LANES
LOGF_LANE = GATE_W
V7X_VMEM_BYTES = 64 * 1024 * 1024
VMEM_LIMIT = V7X_VMEM_BYTES * 7 // 8
PAGES_PER_STEP = 32
MATMUL_ROWS = 512
SOFTMAX_TILE = 512
STICK_TILE = 256


def _cp(*sem):
    return pltpu.CompilerParams(dimension_semantics=sem, vmem_limit_bytes=VMEM_LIMIT)


def _dot(a, b):
    return jnp.dot(a, b, preferred_element_type=F32)


def _dot_nt(a, b):
    return lax.dot_general(a, b, (((1,), (1,)), ((), ())), preferred_element_type=F32)


def _split_dot(a, r, parts):
    out, rem = None, a
    for _ in range(parts):
        hi = rem.astype(BF16)
        d = _dot(hi, r)
        out = d if out is None else out + d
        rem = rem - hi.astype(F32)
    return out


def _split_dot_l(l, a, parts):
    out, rem = None, a
    for _ in range(parts):
        hi = rem.astype(BF16)
        d = _dot(l, hi)
        out = d if out is None else out + d
        rem = rem - hi.astype(F32)
    return out


def _rms(x, g):
    return x * lax.rsqrt(jnp.mean(x * x, axis=-1, keepdims=True) + RMS_EPS) * g


def _softplus2(z2):
    return jnp.maximum(z2, 0.0) + jnp.log2(1.0 + jnp.exp2(-jnp.abs(z2)))


def _iota(shape, axis):
    return lax.broadcasted_iota(I32, shape, axis)


def _proj_kernel(x_ref, g_ref, w_ref, rope_ref, b_ref, tri_ref,
                 qn_ref, kvc_ref, kvs_ref, kvw_ref, kvsb_ref, kvfx_ref,
                 kbs_ref, kbw_ref, kbsb_ref, kbfx_ref, qsb_ref, qfx_ref,
                 gate_ref, logf_ref, cum_ref, h_scr, carry_scr, *, tiles_per_seq, feature_states):
    i = pl.program_id(0)
    tm = x_ref.shape[0]
    h_scr[...] = _rms(x_ref[...], g_ref[...]).astype(BF16)
    hb = h_scr[...]
    cos, sin_lo, sin_hi = rope_ref[:, 0:128], rope_ref[:, 128:256], rope_ref[:, 256:384]

    def rope(seg):
        return seg * cos + pltpu.roll(seg, LANES - 8, 1) * sin_lo + pltpu.roll(seg, 8, 1) * sin_hi

    def put_state(f_ref, blocks):
        for n, blk in enumerate(blocks):
            if feature_states:
                f_ref[0, LANES * n:LANES * (n + 1), :] = blk.T
            else:
                f_ref[:, LANES * n:LANES * (n + 1)] = blk

    lane = _iota((tm, LANES), 1)
    pq = _dot_nt(hb, w_ref[C_Q:C_Q + NSA_W, :])
    for s in range(NSA_HPG):
        seg = rope(pq[:, LANES * s:LANES * (s + 1)]) * (SCALE * LOG2E)
        for g in range(NSA_GROUPS):
            qn_ref[g, s] = jnp.where(lane // HEAD_DIM == g, seg, 0.0).astype(BF16)

    for c0, f_ref, h_ref in ((C_CMP, kvc_ref, None), (C_SEL, kvs_ref, kbs_ref), (C_WIN, kvw_ref, kbw_ref)):
        p = _dot_nt(hb, w_ref[c0:c0 + KV_W, :])
        k, v = rope(p[:, 0:128]), p[:, 128:256]
        put_state(f_ref, [k, v])
        if h_ref is not None:
            h_ref[:, 0:128] = k.astype(BF16)
            h_ref[:, 128:256] = v.astype(BF16)

    lane_w = _iota((tm, SB_W), 1)
    for c0, q_ref, f_ref, h_ref in ((C_SB, qsb_ref, kvsb_ref, kbsb_ref), (C_FOX, qfx_ref, kvfx_ref, kbfx_ref)):
        p = _dot_nt(hb, w_ref[c0:c0 + 3 * SB_W, :])
        q = p[:, 0:SB_W] * (SCALE * LOG2E)
        for h in range(SB_HEADS):
            q_ref[h] = jnp.where(lane_w // HEAD_DIM == h, q, 0.0).astype(BF16)
        kv = p[:, SB_W:3 * SB_W]
        put_state(f_ref, [kv[:, LANES * n:LANES * (n + 1)] for n in range(2 * SB_W // LANES)])
        h_ref[...] = kv.astype(BF16)

    pm = _dot_nt(hb, w_ref[C_MISC:C_MISC + LANES, :])
    sg = jax.nn.sigmoid(pm)
    gate_ref[0] = sg
    gate_ref[1] = pltpu.roll(sg, LANES - GATE_W // 2, 1)
    zf = pm + b_ref[...]
    lf = jnp.minimum(zf, 0.0) - jnp.log1p(jnp.exp(-jnp.abs(zf)))
    if feature_states:
        logf_ref[0] = lf.T[LOGF_LANE:LOGF_LANE + FOX_HEADS, :]
    else:
        logf_ref[...] = lf

    @pl.when(i % tiles_per_seq == 0)
    def _():
        carry_scr[...] = jnp.zeros_like(carry_scr)

    c = _split_dot_l(tri_ref[...], lf, 3) + carry_scr[0:1, :]
    cum_ref[...] = c
    carry_scr[...] = jnp.broadcast_to(c[tm - 1:tm, :], carry_scr.shape)


def _proj(x, gain, w, rope_tab, bias, *, tm, tiles_per_seq, feature_states):
    m, d = x.shape
    tri = jnp.tril(jnp.ones((tm, tm), F32)).astype(BF16)
    row = lambda wid: pl.BlockSpec((tm, wid), lambda i: (i, 0))
    const = lambda shape: pl.BlockSpec(shape, lambda i: (0,) * len(shape))
    f32o = lambda wid: jax.ShapeDtypeStruct((m, wid), F32)
    b16o = lambda wid: jax.ShapeDtypeStruct((m, wid), BF16)
    if feature_states:
        nb, seq = m // (tm * tiles_per_seq), tm * tiles_per_seq
        st_shape = lambda wid: jax.ShapeDtypeStruct((nb, wid, seq), F32)
        st_spec = lambda wid: pl.BlockSpec((1, wid, tm), lambda i: (i // tiles_per_seq, 0, i % tiles_per_seq))
        lf_shape, lf_spec = st_shape(FOX_HEADS), st_spec(FOX_HEADS)
    else:
        st_shape, st_spec, lf_shape, lf_spec = f32o, row, f32o(LANES), row(LANES)
    out_shape = (
        jax.ShapeDtypeStruct((NSA_GROUPS, NSA_HPG, m, LANES), BF16),
        st_shape(KV_W), st_shape(KV_W), st_shape(KV_W), st_shape(2 * SB_W), st_shape(2 * SB_W),
        b16o(KV_W), b16o(KV_W), b16o(2 * SB_W), b16o(2 * SB_W),
        jax.ShapeDtypeStruct((SB_HEADS, m, SB_W), BF16), jax.ShapeDtypeStruct((SB_HEADS, m, SB_W), BF16),
        jax.ShapeDtypeStruct((NSA_GROUPS, m, LANES), F32), lf_shape, f32o(LANES),
    )
    out_specs = (
        pl.BlockSpec((NSA_GROUPS, NSA_HPG, tm, LANES), lambda i: (0, 0, i, 0)),
        st_spec(KV_W), st_spec(KV_W), st_spec(KV_W), st_spec(2 * SB_W), st_spec(2 * SB_W),
        row(KV_W), row(KV_W), row(2 * SB_W), row(2 * SB_W),
        pl.BlockSpec((SB_HEADS, tm, SB_W), lambda i: (0, i, 0)),
        pl.BlockSpec((SB_HEADS, tm, SB_W), lambda i: (0, i, 0)),
        pl.BlockSpec((NSA_GROUPS, tm, LANES), lambda i: (0, i, 0)),
        lf_spec, row(LANES),
    )
    return pl.pallas_call(
        functools.partial(_proj_kernel, tiles_per_seq=tiles_per_seq, feature_states=feature_states),
        grid=(m // tm,),
        in_specs=[row(d), const((1, d)), const((N_PROJ, d)),
                  pl.BlockSpec((tm, 3 * LANES), lambda i: (i % tiles_per_seq, 0)),
                  const((1, LANES)), const((tm, tm))],
        out_specs=out_specs, out_shape=out_shape,
        scratch_shapes=[pltpu.VMEM((tm, d), BF16), pltpu.VMEM((8, LANES), F32)],
        compiler_params=_cp("arbitrary"), name="proj",
    )(x, gain, w, rope_tab, bias, tri)


def _compress_kernel(pt_ref, *refs, pp, nch):
    pages, (perm_ref, w_ref, pe_ref, out_ref, x_scr, acc_scr, pos_scr) = refs[:pp], refs[pp:]
    j = pl.program_id(1)
    half = CMP_BLOCK // 2
    chunks = PAGE // CMP_STRIDE

    @pl.when((pl.program_id(0) == 0) & (j == 0))
    def _():
        c = jnp.zeros((8, KV_W), F32)
        for l in range(half):
            for off, cols in ((0, slice(0, KV_W)), (half, slice(KV_W, 2 * KV_W))):
                row = jnp.broadcast_to(pe_ref[off + l:off + l + 1, :], (8, KV_W))
                c = c + _split_dot(row, w_ref[l, :, cols], 2)
        pos_scr[...] = c

    for i in range(pp):
        gt = _dot_nt(perm_ref[...], pages[i][0].astype(BF16))
        base = pl.multiple_of((j * pp + i) * chunks, chunks)
        for l in range(half):
            x_scr[l, pl.ds(base, chunks), :] = gt[chunks * l:chunks * (l + 1), :]

    @pl.when(j == pl.num_programs(1) - 1)
    def _():
        acc_scr[...] = jnp.zeros_like(acc_scr)
        for l in range(half):
            acc_scr[...] += _dot(x_scr[l].astype(BF16), w_ref[l])
        nxt = pltpu.roll(acc_scr[:, KV_W:2 * KV_W], nch - 1, 0)
        nxt = jnp.where(_iota((nch, KV_W), 0) < nch - 1, nxt, 0.0)
        out_ref[0] = (acc_scr[:, 0:KV_W] + nxt + pos_scr[0:1, :]).astype(BF16)


def _compress(pool, page_table, w_cat, pe4, *, pp, paged):
    b, n_pages = page_table.shape
    chunks = PAGE // CMP_STRIDE
    nch = n_pages * chunks
    half = CMP_BLOCK // 2
    tok = np.arange(PAGE)
    perm = jnp.asarray(np.arange(PAGE)[:, None] == ((tok % CMP_STRIDE) * chunks + tok // CMP_STRIDE)[None, :], BF16)
    if paged:
        page_spec = lambda i: pl.BlockSpec((1, KV_W, PAGE), lambda bb, j, pt: (pt[bb, j * pp + i], 0, 0))
    else:
        page_spec = lambda i: pl.BlockSpec((1, KV_W, PAGE), lambda bb, j, pt: (bb, 0, j * pp + i))
    grid_spec = pltpu.PrefetchScalarGridSpec(
        num_scalar_prefetch=1, grid=(b, n_pages // pp),
        in_specs=[page_spec(i) for i in range(pp)] + [
            pl.BlockSpec((PAGE, PAGE), lambda bb, j, pt: (0, 0)),
            pl.BlockSpec((half, KV_W, 2 * KV_W), lambda bb, j, pt: (0, 0, 0)),
            pl.BlockSpec((CMP_BLOCK, KV_W), lambda bb, j, pt: (0, 0))],
        out_specs=pl.BlockSpec((1, nch, KV_W), lambda bb, j, pt: (bb, 0, 0)),
        scratch_shapes=[pltpu.VMEM((half, nch, KV_W), F32), pltpu.VMEM((nch, 2 * KV_W), F32),
                        pltpu.VMEM((8, KV_W), F32)])
    return pl.pallas_call(
        functools.partial(_compress_kernel, pp=pp, nch=nch), grid_spec=grid_spec,
        out_shape=jax.ShapeDtypeStruct((b, nch, KV_W), BF16),
        compiler_params=_cp("arbitrary", "arbitrary"), name="compress",
    )(page_table, *([pool] * pp), perm, w_cat, pe4)


def _masked_softmax_rows(s, mask):
    s = jnp.where(mask, s, NEG)
    e = jnp.where(mask, jnp.exp2(s - jnp.max(s, axis=1, keepdims=True)), 0.0)
    return e / jnp.maximum(jnp.sum(e, axis=1, keepdims=True), 1e-30)


def _online_init(m_scr, l_scr, acc_scr):
    m_scr[...] = jnp.full_like(m_scr, NEG)
    l_scr[...] = jnp.zeros_like(l_scr)
    acc_scr[...] = jnp.zeros_like(acc_scr)


def _select_blocks(imp, pos, n_sel):
    blk = _iota(imp.shape, 1)
    cur = pos // SEL_BLOCK
    valid = blk * SEL_BLOCK <= pos
    forced = jnp.where(blk == 0, 1.0, 0.0) + jnp.where(blk == cur, 1.0, 0.0) + jnp.where(blk == cur - 1, 1.0, 0.0)
    score = jnp.where(valid, jnp.where(forced > 0.5, SEL_FORCE_SCORE, imp), -1.0)
    score = jnp.where(blk < n_sel, score, -2.0)
    rank = jnp.zeros(imp.shape, F32)
    for b2 in range(n_sel):
        col = score[:, b2:b2 + 1]
        ge = jnp.where(col >= score, 1.0, 0.0)
        gt = jnp.where(col > score, 1.0, 0.0)
        rank = rank + jnp.where(blk > b2, ge, gt)
    return jnp.where(rank < min(TOP_N, n_sel), 1.0, 0.0)


def _rep(x, n):
    return x if n == 1 else jnp.concatenate([x] * n, axis=1)


def _select_blocks_t(p_grp, amat_t, q0, n_sel):
    t = p_grp.shape[0]
    nb = amat_t.shape[0]
    hi = p_grp.astype(BF16)
    lo = (p_grp - hi.astype(F32)).astype(BF16)
    imp = _dot_nt(amat_t, hi) + _dot_nt(amat_t, lo)
    pos = q0 + _iota((1, t), 1)
    blk = _iota((nb, t), 0)
    cur = pos // SEL_BLOCK
    forced = jnp.where(blk == 0, 1.0, 0.0) + jnp.where(blk == cur, 1.0, 0.0) + jnp.where(blk == cur - 1, 1.0, 0.0)
    score = jnp.where(blk * SEL_BLOCK <= pos, jnp.where(forced > 0.5, SEL_FORCE_SCORE, imp), -1.0)
    score = jnp.where(blk < n_sel, score, -2.0)
    groups = [score[8 * k:8 * (k + 1)] for k in range(nb // 8)]
    ranks = [jnp.zeros((8, t), F32) for _ in groups]
    sub = _iota((8, t), 0)
    for b2 in range(n_sel):
        rowv = score[b2:b2 + 1, :]
        k2, r2 = divmod(b2, 8)
        for k, grp in enumerate(groups):
            if k > k2:
                inc = jnp.where(rowv >= grp, 1.0, 0.0)
            elif k < k2:
                inc = jnp.where(rowv > grp, 1.0, 0.0)
            else:
                inc = jnp.where(sub > r2, jnp.where(rowv >= grp, 1.0, 0.0), jnp.where(rowv > grp, 1.0, 0.0))
            ranks[k] = ranks[k] + inc
    chosen = [jnp.where(rk < min(TOP_N, n_sel), 1.0, 0.0) for rk in ranks]
    pad = [jnp.zeros((LANES - nb, t), F32)] if nb < LANES else []
    return jnp.concatenate(chosen + pad, axis=0)


def _nsa_prompt_kernel(qn_ref, ckv_ref, ks_ref, kw_ref, gate_ref, amat_ref, e_ref, out_ref,
                       m_scr, l_scr, acc_scr, oc_scr, os_scr, *, n_sel):
    g, i = pl.program_id(1), pl.program_id(2)
    t = qn_ref.shape[2]
    q0 = i * t
    rowi, coli = _iota((t, t), 0), _iota((t, t), 1)
    reps = t // LANES

    nc = ckv_ref.shape[1]
    cmask = _iota((t, nc), 1) * CMP_STRIDE + (CMP_BLOCK - 1) <= q0 + _iota((t, 1), 0)
    ck, cv = ckv_ref[0, :, 0:128], ckv_ref[0, :, 128:256]
    p_grp = None
    for h in range(NSA_HPG):
        p = _masked_softmax_rows(_dot_nt(qn_ref[0, h], ck), cmask)
        oc_scr[h] = _dot(p.astype(BF16), cv)
        p_grp = p if p_grp is None else p_grp + p
    sel = _select_blocks_t(p_grp, amat_ref[...], q0, n_sel).T.astype(BF16)

    def online_head(h, s, v):
        m_prev = m_scr[h]
        m_next = jnp.maximum(m_prev, jnp.max(s, axis=1, keepdims=True))
        alpha = jnp.exp2(m_prev - m_next)
        e = jnp.exp2(s - _rep(m_next, reps))
        l_scr[h] = alpha * l_scr[h] + jnp.sum(e, axis=1, keepdims=True)
        m_scr[h] = m_next
        acc_scr[h] = acc_scr[h] * alpha + _dot(e.astype(BF16), v)

    def reset():
        m_scr[...] = jnp.full_like(m_scr, NEG)
        l_scr[...] = jnp.zeros_like(l_scr)
        acc_scr[...] = jnp.zeros_like(acc_scr)

    def sel_tile(j, diag):
        kt = ks_ref[0, pl.ds(pl.multiple_of(j * t, t), t), :]
        chosen = _dot(sel, e_ref[j])
        if diag:
            chosen = jnp.where(coli <= rowi, chosen, 0.0)
        keep = chosen > 0.5
        for h in range(NSA_HPG):
            online_head(h, jnp.where(keep, _dot_nt(qn_ref[0, h], kt[:, 0:128]), NEG), kt[:, 128:256])

    reset()
    lax.fori_loop(0, i, lambda j, c: (sel_tile(j, False), c)[1], 0)
    sel_tile(i, True)
    for h in range(NSA_HPG):
        os_scr[h] = acc_scr[h] / jnp.maximum(l_scr[h], 1e-30)

    tw = min(t, 2 * LANES)
    w_len = WINDOW + tw
    gates = gate_ref[0]
    lane = _iota((tw, LANES), 1)
    for rb in range(t // tw):
        rows = slice(rb * tw, (rb + 1) * tw)
        r0 = q0 + rb * tw
        k0 = jnp.maximum(r0 - WINDOW, 0)
        kw = kw_ref[0, pl.ds(pl.multiple_of(k0, tw), w_len), :]
        d = (r0 - k0) + _iota((tw, w_len), 0) - _iota((tw, w_len), 1)
        keep_w = jnp.where(d >= 0, jnp.where(d <= WINDOW, 1.0, 0.0), 0.0) > 0.5
        for h in range(NSA_HPG):
            s = jnp.where(keep_w, _dot_nt(qn_ref[0, h, rows, :], kw[:, 0:128]), NEG)
            e = jnp.exp2(s - jnp.max(s, axis=1, keepdims=True))
            o_w = _dot(e.astype(BF16), kw[:, 128:256]) / jnp.sum(e, axis=1, keepdims=True)
            o = (gates[rows, 3 * h:3 * h + 1] * oc_scr[h, rows, :] + gates[rows, 3 * h + 1:3 * h + 2] * os_scr[h, rows, :]
                 + gates[rows, 3 * h + 2:3 * h + 3] * o_w)
            out_ref[0, rows, LANES * h:LANES * (h + 1)] = jnp.where(lane // HEAD_DIM == g, o, 0.0).astype(BF16)


def _imp_matrix(nc, width):
    n = np.arange(nc)[:, None]
    b = np.arange(width)[None, :]
    sub = SEL_BLOCK // CMP_STRIDE
    return jnp.asarray((n // sub == b).astype(np.float32) + ((n + 1) // sub == b), BF16)


def _expand_tiles(n_tiles, n_blk, tk):
    key = np.arange(n_tiles * tk).reshape(n_tiles, 1, tk)
    return jnp.asarray(key // SEL_BLOCK == np.arange(n_blk)[None, :, None], BF16)


def _nsa_prompt(qn, ckv, kb_sel, kb_win, gates, *, batch, seq, tq):
    nq = seq // tq
    nc = ckv.shape[1]
    n_sel = seq // SEL_BLOCK
    assert n_sel <= LANES and WINDOW % tq == 0 and seq >= WINDOW + tq
    amat = _imp_matrix(nc, -(-n_sel // 8) * 8).T
    etile = _expand_tiles(nq, LANES, tq)
    m = batch * seq
    return pl.pallas_call(
        functools.partial(_nsa_prompt_kernel, n_sel=n_sel),
        grid=(batch, NSA_GROUPS, nq),
        in_specs=[
            pl.BlockSpec((1, NSA_HPG, tq, LANES), lambda b, g, i: (g, 0, b * nq + i, 0)),
            pl.BlockSpec((1, nc, KV_W), lambda b, g, i: (b, 0, 0)),
            pl.BlockSpec((1, seq, KV_W), lambda b, g, i: (b, 0, 0)),
            pl.BlockSpec((1, seq, KV_W), lambda b, g, i: (b, 0, 0)),
            pl.BlockSpec((1, tq, LANES), lambda b, g, i: (g, b * nq + i, 0)),
            pl.BlockSpec(amat.shape, lambda b, g, i: (0, 0)),
            pl.BlockSpec(etile.shape, lambda b, g, i: (0, 0, 0)),
        ],
        out_specs=pl.BlockSpec((1, tq, NSA_W), lambda b, g, i: (g, b * nq + i, 0)),
        out_shape=jax.ShapeDtypeStruct((NSA_GROUPS, m, NSA_W), BF16),
        scratch_shapes=[pltpu.VMEM((NSA_HPG, tq, LANES), F32)] * 5,
        compiler_params=_cp("arbitrary", "arbitrary", "arbitrary"), name="nsa_prompt",
    )(qn, ckv, kb_sel.reshape(batch, seq, KV_W), kb_win.reshape(batch, seq, KV_W), gates, amat, etile)


def _upper_strict(n):
    return jnp.asarray(np.triu(np.ones((n, n), np.float32), 1).T, BF16)


def _stack_heads(v):
    lane = _iota(v.shape, 1)
    return jnp.concatenate([jnp.where(lane // HEAD_DIM == h, v, jnp.zeros_like(v)) for h in range(SB_HEADS)], axis=0)


def _head_lanes(parts):
    lane = _iota(parts[0].shape, 1)
    return jnp.concatenate([jnp.where(lane < HEAD_DIM, parts[0], parts[1]),
                            jnp.where(lane < HEAD_DIM, parts[2], parts[3])], axis=1)


def _sb_prompt_kernel(q_ref, kv_ref, u_ref, out_ref, acc_scr, r_scr):
    i = pl.program_id(1)
    t = q_ref.shape[1]
    rowi, coli = _iota((t, t), 0), _iota((t, t), 1)
    acc_scr[...] = jnp.zeros_like(acc_scr)
    r_scr[...] = jnp.zeros_like(r_scr)

    def tile(j, diag):
        kt = kv_ref[0, pl.ds(pl.multiple_of(j * t, t), t), :]
        ws = []
        for h in range(SB_HEADS):
            z = _dot_nt(q_ref[h], kt[:, 0:SB_W])
            sp = _softplus2(z)
            cost = jnp.where(coli < rowi, sp, 0.0) if diag else sp
            r = r_scr[h]
            w = jnp.exp2(z - sp - _split_dot(cost, u_ref[...], 1) - _rep(r, t // LANES))
            if diag:
                w = jnp.where(coli < rowi, w, 0.0)
            ws.append(w.astype(BF16))
            r_scr[h] = r + jnp.sum(cost, axis=1, keepdims=True)
        acc_scr[...] += _dot(jnp.concatenate(ws, axis=1), _stack_heads(kt[:, SB_W:2 * SB_W]))

    tile(i, True)

    def four_tiles(n, c):
        for u in range(4):
            tile(i - 1 - 4 * n - u, False)
        return c

    lax.fori_loop(0, i // 4, four_tiles, 0)
    rem = i % 4

    @pl.when(rem >= 2)
    def _():
        tile(rem - 1, False)
        tile(rem - 2, False)

    @pl.when(rem % 2 == 1)
    def _():
        tile(0, False)

    out_ref[...] = acc_scr[...].astype(BF16)


def _sb_prompt(q, kb, *, batch, seq, tq):
    nq = seq // tq
    u = _upper_strict(tq)
    return pl.pallas_call(
        _sb_prompt_kernel, grid=(batch, nq),
        in_specs=[pl.BlockSpec((SB_HEADS, tq, SB_W), lambda b, i: (0, b * nq + i, 0)),
                  pl.BlockSpec((1, seq, 2 * SB_W), lambda b, i: (b, 0, 0)),
                  pl.BlockSpec(u.shape, lambda b, i: (0, 0))],
        out_specs=pl.BlockSpec((tq, SB_W), lambda b, i: (b * nq + i, 0)),
        out_shape=jax.ShapeDtypeStruct((batch * seq, SB_W), BF16),
        scratch_shapes=[pltpu.VMEM((tq, SB_W), F32), pltpu.VMEM((SB_HEADS, tq, LANES), F32)],
        compiler_params=_cp("arbitrary", "arbitrary"), name="sb_prompt",
    )(q, kb.reshape(batch, seq, 2 * SB_W), u)


def _fox_prompt_kernel(q_ref, kv_ref, cq_ref, ck_ref, out_ref, m_scr, l_scr, acc_scr, cq_scr):
    i = pl.program_id(1)
    t = q_ref.shape[1]
    rowi, coli = _iota((t, t), 0), _iota((t, t), 1)
    m_scr[...] = jnp.full_like(m_scr, NEG)
    l_scr[...] = jnp.zeros_like(l_scr)
    acc_scr[...] = jnp.zeros_like(acc_scr)
    for h in range(FOX_HEADS):
        cq_scr[h] = jnp.broadcast_to(cq_ref[:, LOGF_LANE + h:LOGF_LANE + h + 1] * LOG2E, (t, LANES))

    def tile(j, diag):
        kt = kv_ref[0, pl.ds(pl.multiple_of(j * t, t), t), :]
        es, alphas = [], []
        for h in range(FOX_HEADS):
            s = _dot_nt(q_ref[h], kt[:, 0:SB_W]) + (_rep(cq_scr[h], t // LANES) - ck_ref[0, h, j] * LOG2E)
            if diag:
                s = jnp.where(coli <= rowi, s, NEG)
            m_prev = m_scr[h]
            m_next = jnp.maximum(m_prev, jnp.max(s, axis=1, keepdims=True))
            alpha = jnp.exp2(m_prev - m_next)
            e = jnp.exp2(s - _rep(m_next, t // LANES))
            l_scr[h] = alpha * l_scr[h] + jnp.sum(e, axis=1, keepdims=True)
            m_scr[h] = m_next
            es.append(e.astype(BF16))
            alphas.append(alpha)
        acc_scr[...] = acc_scr[...] * _head_lanes(alphas) + _dot(jnp.concatenate(es, axis=1),
                                                                 _stack_heads(kt[:, SB_W:2 * SB_W]))

    lax.fori_loop(0, i, lambda j, c: (tile(j, False), c)[1], 0)
    tile(i, True)
    out_ref[...] = (acc_scr[...] / _head_lanes([jnp.maximum(l_scr[h], 1e-30) for h in range(FOX_HEADS)])).astype(BF16)


def _fox_prompt(q, kb, cum, cum_k, *, batch, seq, tq):
    nq = seq // tq
    nk = seq // tq
    return pl.pallas_call(
        _fox_prompt_kernel, grid=(batch, nq),
        in_specs=[pl.BlockSpec((FOX_HEADS, tq, SB_W), lambda b, i: (0, b * nq + i, 0)),
                  pl.BlockSpec((1, seq, 2 * SB_W), lambda b, i: (b, 0, 0)),
                  pl.BlockSpec((tq, LANES), lambda b, i: (b * nq + i, 0)),
                  pl.BlockSpec((1, FOX_HEADS, nk, 1, tq), lambda b, i: (b, 0, 0, 0, 0))],
        out_specs=pl.BlockSpec((tq, SB_W), lambda b, i: (b * nq + i, 0)),
        out_shape=jax.ShapeDtypeStruct((batch * seq, SB_W), BF16),
        scratch_shapes=[pltpu.VMEM((FOX_HEADS, tq, LANES), F32), pltpu.VMEM((FOX_HEADS, tq, LANES), F32),
                        pltpu.VMEM((tq, SB_W), F32), pltpu.VMEM((FOX_HEADS, tq, LANES), F32)],
        compiler_params=_cp("arbitrary", "arbitrary"), name="fox_prompt",
    )(q, kb.reshape(batch, seq, 2 * SB_W), cum, cum_k)


def _merge_kernel(x_ref, g_ref, oa_ref, ob_ref, oc_ref, wm_ref, wa_ref, wb_ref, wc_ref, wo_ref, out_ref):
    x = x_ref[...]
    d = x.shape[1]
    hb = _rms(x, g_ref[...]).astype(BF16)
    oa = oa_ref[0] + oa_ref[1]
    u = jax.nn.sigmoid(_dot_nt(hb, wm_ref[0:d, :])) * _dot(oa, wa_ref[...])
    u = u + jax.nn.sigmoid(_dot_nt(hb, wm_ref[d:2 * d, :])) * _dot(ob_ref[...], wb_ref[...])
    u = u + jax.nn.sigmoid(_dot_nt(hb, wm_ref[2 * d:3 * d, :])) * _dot(oc_ref[...], wc_ref[...])
    out_ref[...] = x + _dot(u.astype(BF16), wo_ref[...])


def _merge(x, gain, o_nsa, o_sb, o_fox, w_merge, w_a, w_b, w_c, w_o, *, tm):
    m, d = x.shape
    row = lambda wid: pl.BlockSpec((tm, wid), lambda i: (i, 0))
    const = lambda a: pl.BlockSpec(a.shape, lambda i: (0,) * a.ndim)
    return pl.pallas_call(
        _merge_kernel, grid=(m // tm,),
        in_specs=[row(d), const(gain), pl.BlockSpec((NSA_GROUPS, tm, NSA_W), lambda i: (0, i, 0)),
                  row(SB_W), row(SB_W), const(w_merge), const(w_a), const(w_b), const(w_c), const(w_o)],
        out_specs=row(d), out_shape=jax.ShapeDtypeStruct((m, d), F32),
        compiler_params=_cp("arbitrary"), name="merge",
    )(x, gain, o_nsa, o_sb, o_fox, w_merge, w_a, w_b, w_c, w_o)


def _ffn_kernel(x_ref, g_ref, wg_ref, wu_ref, wd_ref, gf_ref, out_ref, *, final_norm):
    x = x_ref[...]
    hb = _rms(x, g_ref[...]).astype(BF16)
    a = _dot(hb, wg_ref[...])
    act = a * jax.nn.sigmoid(a) * _dot(hb, wu_ref[...])
    y = x + _dot(act.astype(BF16), wd_ref[...])
    out_ref[...] = _rms(y, gf_ref[...]) if final_norm else y


def _ffn(x, gain, w_gate, w_up, w_down, final_gain, *, tm, final_norm):
    m, d = x.shape
    row = pl.BlockSpec((tm, d), lambda i: (i, 0))
    vec = pl.BlockSpec((1, d), lambda i: (0, 0))
    const = lambda a: pl.BlockSpec(a.shape, lambda i: (0,) * a.ndim, pipeline_mode=pl.Buffered(1))
    return pl.pallas_call(
        functools.partial(_ffn_kernel, final_norm=final_norm), grid=(m // tm,),
        in_specs=[row, vec, const(w_gate), const(w_up), const(w_down), vec],
        out_specs=row, out_shape=jax.ShapeDtypeStruct((m, d), F32),
        compiler_params=_cp("arbitrary"), name="ffn",
    )(x, gain, w_gate, w_up, w_down, final_gain)


def _page_specs(rows, n_pages, pp, descending):
    def spec(i):
        if descending:
            return pl.BlockSpec((1, rows, PAGE), lambda b, j, pt: (pt[b, n_pages - 1 - (j * pp + i)], 0, 0))
        return pl.BlockSpec((1, rows, PAGE), lambda b, j, pt: (pt[b, j * pp + i], 0, 0))
    return [spec(i) for i in range(pp)]


def _lanes(parts):
    return jnp.concatenate(parts, axis=1)


def _sb_dec_kernel(pt_ref, q_ref, *refs, pp):
    pages, (u_ref, out_ref, acc_scr, r_scr) = refs[:pp], refs[pp:]
    j = pl.program_id(1)

    @pl.when(j == 0)
    def _():
        acc_scr[...] = jnp.zeros_like(acc_scr)
        r_scr[...] = jnp.zeros_like(r_scr)

    q = q_ref[0]
    z_all = _dot(q, _lanes([p[0, 0:SB_W, :].astype(BF16) for p in pages]))
    zs = [z_all[:, i * PAGE:(i + 1) * PAGE] for i in range(pp)]
    sps = [_softplus2(z) for z in zs]
    afters = [_split_dot(sp, u_ref[...], 2) for sp in sps]
    r = r_scr[...]
    ws = []
    for i in range(pp):
        ws.append(jnp.exp2(zs[i] - sps[i] - afters[i] - r).astype(BF16))
        r = r + jnp.sum(sps[i], axis=1, keepdims=True)
    acc_scr[...] += _dot_nt(_lanes(ws), _lanes([p[0, SB_W:2 * SB_W, :].astype(BF16) for p in pages]))
    r_scr[...] = r

    @pl.when(j == pl.num_programs(1) - 1)
    def _():
        out_ref[0] = acc_scr[...]


def _sb_dec(q8, pool, page_table, *, pp):
    bd, n_pages = page_table.shape
    u = _upper_strict(PAGE)
    grid_spec = pltpu.PrefetchScalarGridSpec(
        num_scalar_prefetch=1, grid=(bd, n_pages // pp),
        in_specs=[pl.BlockSpec((1, 8, SB_W), lambda b, j, pt: (b, 0, 0))]
        + _page_specs(2 * SB_W, n_pages, pp, True) + [pl.BlockSpec(u.shape, lambda b, j, pt: (0, 0))],
        out_specs=pl.BlockSpec((1, 8, SB_W), lambda b, j, pt: (b, 0, 0)),
        scratch_shapes=[pltpu.VMEM((8, SB_W), F32), pltpu.VMEM((8, LANES), F32)])
    return pl.pallas_call(
        functools.partial(_sb_dec_kernel, pp=pp), grid_spec=grid_spec,
        out_shape=jax.ShapeDtypeStruct((bd, 8, SB_W), F32),
        compiler_params=_cp("arbitrary", "arbitrary"), name="sb_decode",
    )(page_table, q8, *([pool] * pp), u)


def _fox_dec_kernel(pt_ref, q_ref, kn_ref, vn_ref, lfn_ref, *refs, pp):
    pages, lf_pages = refs[:pp], refs[pp:2 * pp]
    u_ref, out_ref, m_scr, l_scr, acc_scr, r_scr = refs[2 * pp:]
    j = pl.program_id(1)
    q = q_ref[0]
    reps = SB_W // LANES

    @pl.when(j == 0)
    def _():
        kn = kn_ref[0].astype(BF16).astype(F32)
        m_scr[...] = jnp.broadcast_to(jnp.sum(q.astype(F32) * kn, axis=1, keepdims=True), m_scr.shape)
        l_scr[...] = jnp.ones_like(l_scr)
        acc_scr[...] = jnp.broadcast_to(vn_ref[0].astype(BF16).astype(F32), acc_scr.shape)
        r_scr[...] = lfn_ref[0]

    z_all = _dot(q, _lanes([p[0, 0:SB_W, :].astype(BF16) for p in pages]))
    pad = jnp.zeros((8 - FOX_HEADS, PAGE), F32)
    r = r_scr[...]
    ss = []
    for i in range(pp):
        lf = jnp.concatenate([lf_pages[i][0], pad], axis=0)
        ss.append(z_all[:, i * PAGE:(i + 1) * PAGE] + LOG2E * (_split_dot(lf, u_ref[...], 3) + r))
        r = r + jnp.sum(lf, axis=1, keepdims=True)
    r_scr[...] = r
    m_prev = m_scr[...]
    m_next = m_prev
    for s in ss:
        m_next = jnp.maximum(m_next, jnp.max(s, axis=1, keepdims=True))
    alpha = jnp.exp2(m_prev - m_next)
    es = [jnp.exp2(s - m_next) for s in ss]
    l_new = alpha * l_scr[...]
    for e in es:
        l_new = l_new + jnp.sum(e, axis=1, keepdims=True)
    l_scr[...] = l_new
    m_scr[...] = m_next
    acc_scr[...] = acc_scr[...] * _rep(alpha, reps) + _dot_nt(
        _lanes([e.astype(BF16) for e in es]), _lanes([p[0, SB_W:2 * SB_W, :].astype(BF16) for p in pages]))

    @pl.when(j == pl.num_programs(1) - 1)
    def _():
        out_ref[0] = acc_scr[...] / _rep(l_scr[...], reps)


def _fox_dec(q8, k_new, v_new, lf_new, pool, lf_pool, page_table, *, pp):
    bd, n_pages = page_table.shape
    u = _upper_strict(PAGE)
    req = lambda shape: pl.BlockSpec((1,) + shape, lambda b, j, pt: (b,) + (0,) * len(shape))
    lf_spec = lambda i: pl.BlockSpec((1, FOX_HEADS, PAGE), lambda b, j, pt: (pt[b, n_pages - 1 - (j * pp + i)], 0, 0))
    grid_spec = pltpu.PrefetchScalarGridSpec(
        num_scalar_prefetch=1, grid=(bd, n_pages // pp),
        in_specs=[req((8, SB_W)), req((1, SB_W)), req((1, SB_W)), req((8, LANES))]
        + _page_specs(2 * SB_W, n_pages, pp, True) + [lf_spec(i) for i in range(pp)]
        + [pl.BlockSpec(u.shape, lambda b, j, pt: (0, 0))],
        out_specs=req((8, SB_W)),
        scratch_shapes=[pltpu.VMEM((8, LANES), F32), pltpu.VMEM((8, LANES), F32),
                        pltpu.VMEM((8, SB_W), F32), pltpu.VMEM((8, LANES), F32)])
    return pl.pallas_call(
        functools.partial(_fox_dec_kernel, pp=pp), grid_spec=grid_spec,
        out_shape=jax.ShapeDtypeStruct((bd, 8, SB_W), F32),
        compiler_params=_cp("arbitrary", "arbitrary"), name="fox_decode",
    )(page_table, q8, k_new, v_new, lf_new, *([pool] * pp), *([lf_pool] * pp), u)


def _nsa_dec_kernel(pt_ref, q_ref, ckv_ref, gate_ref, win_ref, kvs_new_ref, kvw_new_ref, kvw_col_ref, amat_ref, e_ref,
                    *refs, pp, past_len, n_sel):
    pages, (out_ref, win_out_ref, m_scr, l_scr, acc_scr, sel_scr, oc_scr, ow_scr) = refs[:pp], refs[pp:]
    j = pl.program_id(1)
    q = q_ref[0]
    rows = q.shape[0]
    row = _iota((rows, LANES), 0)
    row1 = _iota((rows, 1), 0)
    lane = _iota((rows, LANES), 1)
    pos = jnp.full((rows, 1), past_len, I32)

    def new_token(kv_row):
        k = kv_row[:, 0:LANES].astype(BF16).astype(F32)
        return jnp.sum(q.astype(F32) * k, axis=1, keepdims=True), kv_row[:, LANES:2 * LANES].astype(BF16).astype(F32)

    @pl.when(j == 0)
    def _():
        nc = ckv_ref.shape[1]
        cmask = _iota((rows, nc), 1) * CMP_STRIDE + (CMP_BLOCK - 1) <= pos
        p = _masked_softmax_rows(_dot_nt(q, ckv_ref[0, :, 0:128]), cmask)
        oc_scr[...] = _dot(p.astype(BF16), ckv_ref[0, :, 128:256])
        g0 = jnp.sum(p[0:NSA_HPG], axis=0, keepdims=True)
        g1 = jnp.sum(p[NSA_HPG:2 * NSA_HPG], axis=0, keepdims=True)
        p_grp = jnp.where(row1 < NSA_HPG, g0, g1)
        imp = _split_dot(p_grp, amat_ref[...], 2)
        sel_scr[...] = _select_blocks(imp, pos, n_sel)

        wk = win_ref[0]
        n_w = wk.shape[1]
        s = _dot(q, wk[0:LANES, :].astype(BF16))
        kpos = past_len - n_w + _iota((rows, n_w), 1)
        keep = jnp.where(kpos >= 0, jnp.where(pos - kpos <= WINDOW, 1.0, 0.0), 0.0) > 0.5
        s = jnp.where(keep, s, NEG)
        s_new, v_new = new_token(kvw_new_ref[0])
        m = jnp.maximum(s_new, jnp.max(s, axis=1, keepdims=True))
        e = jnp.exp2(s - m)
        e_new = jnp.exp2(s_new - m)
        o_w = _dot_nt(e.astype(BF16), wk[LANES:2 * LANES, :].astype(BF16)) + e_new * v_new
        ow_scr[...] = o_w / (jnp.sum(e, axis=1, keepdims=True) + e_new)

        shifted = pltpu.roll(wk, n_w - 1, 1)
        win_out_ref[0] = jnp.where(_iota(wk.shape, 1) == n_w - 1, kvw_col_ref[0], shifted)
        _online_init(m_scr, l_scr, acc_scr)

    sel = sel_scr[...].astype(BF16)
    z_all = _dot(q, _lanes([p[0, 0:LANES, :].astype(BF16) for p in pages]))
    ss = []
    for i in range(pp):
        chosen = _dot(sel, e_ref[j * pp + i]) > 0.5
        ss.append(jnp.where(chosen, z_all[:, i * PAGE:(i + 1) * PAGE], NEG))
    m_prev = m_scr[...]
    m_next = m_prev
    for s in ss:
        m_next = jnp.maximum(m_next, jnp.max(s, axis=1, keepdims=True))
    alpha = jnp.exp2(m_prev - m_next)
    es = [jnp.exp2(s - m_next) for s in ss]
    l_new = alpha * l_scr[...]
    for e in es:
        l_new = l_new + jnp.sum(e, axis=1, keepdims=True)
    l_scr[...] = l_new
    m_scr[...] = m_next
    acc_scr[...] = acc_scr[...] * alpha + _dot_nt(_lanes([e.astype(BF16) for e in es]),
                                                  _lanes([p[0, LANES:2 * LANES, :].astype(BF16) for p in pages]))

    @pl.when(j == pl.num_programs(1) - 1)
    def _():
        s_new, v_new = new_token(kvs_new_ref[0])
        cur = past_len // SEL_BLOCK
        chosen = jnp.sum(jnp.where(_iota(sel_scr.shape, 1) == cur, sel_scr[...], 0.0), axis=1, keepdims=True)
        s_new = jnp.where(chosen > 0.5, s_new, NEG)
        m_prev = m_scr[...]
        m_next = jnp.maximum(m_prev, s_new)
        alpha = jnp.exp2(m_prev - m_next)
        e_new = jnp.exp2(s_new - m_next)
        o_s = (acc_scr[...] * alpha + e_new * v_new) / jnp.maximum(alpha * l_scr[...] + e_new, 1e-30)
        gates = jnp.where(row < NSA_HPG, gate_ref[0, 0:1, :], gate_ref[0, 1:2, :])
        hh = row % NSA_HPG

        def gate(c):
            return jnp.sum(jnp.where(lane == 3 * hh + c, gates, 0.0), axis=1, keepdims=True)

        out_ref[0] = gate(0) * oc_scr[...] + gate(1) * o_s + gate(2) * ow_scr[...]


def _nsa_dec(q8, ckv, gates, win_state, kvs_new, kvw_new, pool, page_table, *, pp, layer):
    bd, n_pages = page_table.shape
    past_len = n_pages * PAGE
    n_sel = -(-(past_len + 1) // SEL_BLOCK)
    nc = ckv.shape[1]
    wid = -(-n_sel // LANES) * LANES
    amat = _imp_matrix(nc, wid)
    etile = _expand_tiles(n_pages, wid, PAGE)
    n_w = win_state.shape[2]
    req = lambda shape: pl.BlockSpec((1,) + shape, lambda b, j, pt: (b,) + (0,) * len(shape))
    const = lambda a: pl.BlockSpec(a.shape, lambda b, j, pt: (0,) * a.ndim)
    win_spec = pl.BlockSpec((1, KV_W, n_w), lambda b, j, pt: (layer * bd + b, 0, 0))
    grid_spec = pltpu.PrefetchScalarGridSpec(
        num_scalar_prefetch=1, grid=(bd, n_pages // pp),
        in_specs=[req((8, LANES)), req((nc, KV_W)), req((NSA_GROUPS, LANES)), win_spec,
                  req((1, KV_W)), req((1, KV_W)), req((KV_W, 1)), const(amat), const(etile)]
        + _page_specs(KV_W, n_pages, pp, False),
        out_specs=(req((8, LANES)), req((KV_W, n_w))),
        scratch_shapes=[pltpu.VMEM((8, LANES), F32)] * 3 + [pltpu.VMEM((8, wid), F32)] + [pltpu.VMEM((8, LANES), F32)] * 2)
    return pl.pallas_call(
        functools.partial(_nsa_dec_kernel, pp=pp, past_len=past_len, n_sel=n_sel), grid_spec=grid_spec,
        out_shape=(jax.ShapeDtypeStruct((bd, 8, LANES), F32), jax.ShapeDtypeStruct((bd, KV_W, n_w), F32)),
        compiler_params=_cp("arbitrary", "arbitrary"), name="nsa_decode",
    )(page_table, q8, ckv, gates, win_state, kvs_new, kvw_new, kvw_new.reshape(bd, KV_W, 1), amat, etile,
      *([pool] * pp))


def _rope_table(pos):
    half = ROT_DIM // 2
    inv = ROPE_THETA ** (-jnp.arange(half, dtype=F32) / half)
    ang = pos.astype(F32)[:, None] * inv[None, :]
    cos, sin = jnp.cos(ang), jnp.sin(ang)
    n = pos.shape[0]
    pad = HEAD_DIM - ROT_DIM
    one_head = lambda a, b, fill: jnp.concatenate([a, b, jnp.full((n, pad), fill, F32)], axis=1)
    zero = jnp.zeros_like(sin)
    tabs = [one_head(cos, cos, 1.0), one_head(-sin, zero, 0.0), one_head(zero, sin, 0.0)]
    return jnp.concatenate([jnp.tile(t, (1, LANES // HEAD_DIM)) for t in tabs], axis=1)


_Q_HEAD_ORDER = (0, 4, 1, 5, 2, 6, 3, 7)


def _layer_weights(l, w_in, b_forget, cmp_pos, w_cmp_k, w_cmp_v, w_up_nsa):
    w = jnp.swapaxes(w_in, 1, 2)[l]
    d = w.shape[1]
    gate0 = NSA_W + 3 * KV_W
    pf0 = gate0 + GATE_W + 6 * SB_W
    merge0 = pf0 + FOX_HEADS
    rows_ = [w[HEAD_DIM * h:HEAD_DIM * (h + 1)] for h in _Q_HEAD_ORDER]
    rows_ += [w[NSA_W:gate0], w[gate0 + GATE_W:pf0], w[gate0:gate0 + GATE_W], w[pf0:merge0]]
    rows_.append(jnp.zeros((N_PROJ - merge0, d), F32))
    w_proj = jnp.concatenate(rows_, axis=0).astype(BF16)
    w_merge = w[merge0:].astype(BF16)
    zpad = lambda n: jnp.zeros((1, n), F32)
    bias = jnp.concatenate([zpad(LOGF_LANE), b_forget[l].reshape(1, FOX_HEADS), zpad(LANES - LOGF_LANE - FOX_HEADS)], axis=1)
    wk = w_cmp_k[l].reshape(CMP_BLOCK, HEAD_DIM, HEAD_DIM)
    wv = w_cmp_v[l].reshape(CMP_BLOCK, HEAD_DIM, HEAD_DIM)
    zero = jnp.zeros_like(wk)
    w_blk = jnp.concatenate([jnp.concatenate([wsrc if c == n else zero for c in range(4)], axis=2)
                             for n, wsrc in enumerate((wk, wk, wv, wv))], axis=1)
    w_cat = jnp.concatenate([w_blk[0:CMP_BLOCK // 2], w_blk[CMP_BLOCK // 2:]], axis=2).astype(BF16)
    pe4 = jnp.tile(cmp_pos[l], (1, KV_W // HEAD_DIM))
    w_a = jnp.concatenate([w_up_nsa[l][HEAD_DIM * h:HEAD_DIM * (h + 1)] for h in _Q_HEAD_ORDER], axis=0).astype(BF16)
    return w_proj, w_merge, bias, w_cat, pe4, w_a


def _pad_heads(a):
    return jnp.concatenate([a, jnp.zeros((a.shape[0], 8 - a.shape[1], a.shape[2]), a.dtype)], axis=1)


def _token_lanes(cache):
    l, p, t = cache.shape[:3]
    return cache.transpose(0, 1, 3, 4, 5, 2).reshape(l * p, -1, t)


def _own_lanes(o8, n_heads):
    n = o8.shape[0]
    o = o8[:, 0:n_heads].reshape(n, n_heads, n_heads, HEAD_DIM)
    return jnp.stack([o[:, h, h] for h in range(n_heads)], axis=1).reshape(n, n_heads * HEAD_DIM)


def kernel(x_prompt, x_sample, cache_nsa_cmp_kv, cache_nsa_sel_kv, cache_sb_kv, cache_fox_kv, cache_fox_logf,
           state_nsa_win_kv, page_table, norm_mix_g, norm_ffn_g, norm_final_g, w_in, b_forget, cmp_pos,
           w_cmp_k, w_cmp_v, w_up_nsa, w_up_sb, w_up_fox, w_out, w_ffn_gate, w_ffn_up, w_ffn_down):
    batch, seq, d = x_prompt.shape
    bd = x_sample.shape[0]
    depth = w_in.shape[0]
    n_pool = cache_sb_kv.shape[1]
    n_pages = page_table.shape[1]
    past_len = n_pages * PAGE
    m = batch * seq
    tm, tq, tq_soft = min(MATMUL_ROWS, seq), min(STICK_TILE, seq), min(SOFTMAX_TILE, seq)
    pp = min(PAGES_PER_STEP, n_pages)
    pp_prompt = min(PAGES_PER_STEP, seq // PAGE)

    rope_p = _rope_table(jnp.arange(seq, dtype=I32))
    rope_s = _rope_table(jnp.full((bd,), past_len, I32))
    ident_pt = jnp.arange(batch * (seq // PAGE), dtype=I32).reshape(batch, seq // PAGE)
    pool_cmp, pool_sel = _token_lanes(cache_nsa_cmp_kv), _token_lanes(cache_nsa_sel_kv)
    pool_sb, pool_fox = _token_lanes(cache_sb_kv), _token_lanes(cache_fox_kv)
    pool_lf = cache_fox_logf.astype(F32).transpose(0, 1, 3, 2).reshape(depth * n_pool, FOX_HEADS, PAGE)
    win_state = _token_lanes(state_nsa_win_kv)

    xp = x_prompt.reshape(m, d)
    xs = x_sample.reshape(bd, d)
    g_fin = norm_final_g.reshape(1, d)
    p_states, s_states = [], []
    for l in range(depth):
        w_proj, w_merge, bias, w_blk, pe4, w_a = _layer_weights(l, w_in, b_forget, cmp_pos, w_cmp_k, w_cmp_v, w_up_nsa)
        w_b, w_c, w_o = w_up_sb[l].astype(BF16), w_up_fox[l].astype(BF16), w_out[l].astype(BF16)
        w_g, w_u, w_d = w_ffn_gate[l].astype(BF16), w_ffn_up[l].astype(BF16), w_ffn_down[l].astype(BF16)
        g_mix, g_ffn = norm_mix_g[l].reshape(1, d), norm_ffn_g[l].reshape(1, d)

        (qn, kv_cmp, kv_sel, kv_win, kv_sb, kv_fox, kb_sel, kb_win, kb_sb, kb_fox, q_sb, q_fox, gates, logf, cum
         ) = _proj(xp, g_mix, w_proj, rope_p, bias, tm=tm, tiles_per_seq=seq // tm, feature_states=True)
        ckv = _compress(kv_cmp, ident_pt, w_blk, pe4, pp=pp_prompt, paged=False)
        o_nsa = _nsa_prompt(qn, ckv, kb_sel, kb_win, gates, batch=batch, seq=seq, tq=tq_soft)
        o_sb = _sb_prompt(q_sb, kb_sb, batch=batch, seq=seq, tq=tq)
        cum4 = cum[:, LOGF_LANE:LOGF_LANE + FOX_HEADS]
        cum_k = cum4.reshape(batch, seq, FOX_HEADS).transpose(0, 2, 1).reshape(
            batch, FOX_HEADS, seq // tq_soft, 1, tq_soft)
        o_fox = _fox_prompt(q_fox, kb_fox, cum, cum_k, batch=batch, seq=seq, tq=tq_soft)
        xp = _merge(xp, g_mix, o_nsa, o_sb, o_fox, w_merge, w_a, w_b, w_c, w_o, tm=tm)
        xp = _ffn(xp, g_ffn, w_g, w_u, w_d, g_fin, tm=tm, final_norm=(l == depth - 1))
        win_keep = min(WINDOW, seq)
        p_states.append((kv_cmp, kv_sel, kv_sb, kv_fox, logf, kv_win[:, :, seq - win_keep:]))

        (qn, kv_cmp, kv_sel, kv_win, kv_sb, kv_fox, _, _, _, _, q_sb, q_fox, gates, logf, _
         ) = _proj(xs, g_mix, w_proj, rope_s, bias, tm=bd, tiles_per_seq=1, feature_states=False)
        pt_l = page_table + l * n_pool
        ckv = _compress(pool_cmp, pt_l, w_blk, pe4, pp=pp, paged=True)
        q8 = qn.transpose(2, 0, 1, 3).reshape(bd, NSA_HEADS, LANES)
        o_nsa8, win_new = _nsa_dec(q8, ckv, gates.transpose(1, 0, 2), win_state, kv_sel.reshape(bd, 1, KV_W),
                                   kv_win.reshape(bd, 1, KV_W), pool_sel, pt_l, pp=pp, layer=l)
        o_sb8 = _sb_dec(_pad_heads(q_sb.transpose(1, 0, 2)), pool_sb, pt_l, pp=pp)
        lf4 = logf[:, LOGF_LANE:LOGF_LANE + FOX_HEADS]
        lf_new = _pad_heads(jnp.broadcast_to(lf4[:, :, None], (bd, FOX_HEADS, LANES)))
        o_fox8 = _fox_dec(_pad_heads(q_fox.transpose(1, 0, 2)), kv_fox[:, 0:SB_W].reshape(bd, 1, SB_W),
                          kv_fox[:, SB_W:2 * SB_W].reshape(bd, 1, SB_W), lf_new, pool_fox, pool_lf, pt_l, pp=pp)
        o_a = o_nsa8.reshape(bd, NSA_GROUPS, NSA_HPG, NSA_GROUPS, HEAD_DIM)
        o_a = jnp.stack([o_a[:, g, :, g] for g in range(NSA_GROUPS)], axis=2).reshape(bd, NSA_W)
        o_a = jnp.stack([o_a, jnp.zeros_like(o_a)], axis=0).astype(BF16)
        o_b = _own_lanes(o_sb8, SB_HEADS).astype(BF16)
        o_c = _own_lanes(o_fox8, FOX_HEADS).astype(BF16)
        xs = _merge(xs, g_mix, o_a, o_b, o_c, w_merge, w_a, w_b, w_c, w_o, tm=bd)
        xs = _ffn(xs, g_ffn, w_g, w_u, w_d, g_fin, tm=bd, final_norm=(l == depth - 1))
        s_states.append((
            kv_cmp.reshape(bd, 1, 2, NSA_GROUPS, HEAD_DIM), kv_sel.reshape(bd, 1, 2, NSA_GROUPS, HEAD_DIM),
            kv_sb.reshape(bd, 1, 2, SB_HEADS, HEAD_DIM), kv_fox.reshape(bd, 1, 2, FOX_HEADS, HEAD_DIM),
            lf4.reshape(bd, 1, FOX_HEADS),
            win_new.reshape(bd, 2, NSA_GROUPS, HEAD_DIM, -1).transpose(0, 4, 1, 2, 3)))

    y_prompt = xp.reshape(batch, seq, d)
    y_sample = xs.reshape(bd, 1, d)
    stk = lambda sts, i: jnp.stack([st[i] for st in sts], axis=0)

    def kv_state(i, heads):
        a = stk(p_states, i)
        return a.reshape(depth, batch, 2, heads, HEAD_DIM, a.shape[-1]).transpose(0, 1, 5, 2, 3, 4)

    p_out = (kv_state(0, NSA_GROUPS), kv_state(1, NSA_GROUPS), kv_state(2, SB_HEADS), kv_state(3, FOX_HEADS),
             stk(p_states, 4).transpose(0, 1, 3, 2), kv_state(5, NSA_GROUPS))
    return (y_prompt, y_sample) + p_out + tuple(stk(s_states, i) for i in range(6))
```

```python
import functools

import numpy as np
import jax
import jax.numpy as jnp
from jax import lax
from jax.experimental import pallas as pl
from jax.experimental.pallas import tpu as pltpu

F32, BF16, I32 = jnp.float32, jnp.bfloat16, jnp.int32

HEAD_DIM = 64
ROT_DIM = HEAD_DIM // 4
ROPE_THETA = 500000.0
NSA_HEADS = 8
NSA_GROUPS = 2
NSA_HPG = NSA_HEADS // NSA_GROUPS
CMP_STRIDE = 16
CMP_BLOCK = 32
SEL_BLOCK = 64
TOP_N = 16
WINDOW = 512
SB_HEADS = 4
FOX_HEADS = 4
PAGE = 128
SEL_FORCE_SCORE = 1.0e4
NEG = -1.0e30
RMS_EPS = 1e-6
SCALE = HEAD_DIM ** -0.5
LOG2E = 1.4426950408889634

LANES = 128
NSA_W = NSA_HEADS * HEAD_DIM
KV_W = 2 * NSA_GROUPS * HEAD_DIM
SB_W = SB_HEADS * HEAD_DIM
GATE_W = 3 * NSA_HEADS
C_Q, C_CMP, C_SEL, C_WIN, C_SB, C_FOX, C_MISC = 0, 512, 768, 1024, 1280, 2048, 2816
N_PROJ = C_MISC + LANES
LOGF_LANE = GATE_W
V7X_VMEM_BYTES = 64 * 1024 * 1024
VMEM_LIMIT = V7X_VMEM_BYTES * 7 // 8
PAGES_PER_STEP = 32
MATMUL_ROWS = 512
SOFTMAX_TILE = 512
STICK_TILE = 256


def _cp(*sem):
    return pltpu.CompilerParams(dimension_semantics=sem, vmem_limit_bytes=VMEM_LIMIT)


def _dot(a, b):
    return jnp.dot(a, b, preferred_element_type=F32)


def _dot_nt(a, b):
    return lax.dot_general(a, b, (((1,), (1,)), ((), ())), preferred_element_type=F32)


def _split_dot(a, r, parts):
    out, rem = None, a
    for _ in range(parts):
        hi = rem.astype(BF16)
        d = _dot(hi, r)
        out = d if out is None else out + d
        rem = rem - hi.astype(F32)
    return out


def _split_dot_l(l, a, parts):
    out, rem = None, a
    for _ in range(parts):
        hi = rem.astype(BF16)
        d = _dot(l, hi)
        out = d if out is None else out + d
        rem = rem - hi.astype(F32)
    return out


def _rms(x, g):
    return x * lax.rsqrt(jnp.mean(x * x, axis=-1, keepdims=True) + RMS_EPS) * g


def _softplus2(z2):
    return jnp.maximum(z2, 0.0) + jnp.log2(1.0 + jnp.exp2(-jnp.abs(z2)))


def _iota(shape, axis):
    return lax.broadcasted_iota(I32, shape, axis)


def _proj_kernel(x_ref, g_ref, w_ref, rope_ref, b_ref, tri_ref,
                 qn_ref, kvc_ref, kvs_ref, kvw_ref, kvsb_ref, kvfx_ref,
                 kbs_ref, kbw_ref, kbsb_ref, kbfx_ref, qsb_ref, qfx_ref,
                 gate_ref, logf_ref, cum_ref, h_scr, carry_scr, *, tiles_per_seq, feature_states):
    i = pl.program_id(0)
    tm = x_ref.shape[0]
    h_scr[...] = _rms(x_ref[...], g_ref[...]).astype(BF16)
    hb = h_scr[...]
    cos, sin_lo, sin_hi = rope_ref[:, 0:128], rope_ref[:, 128:256], rope_ref[:, 256:384]

    def rope(seg):
        return seg * cos + pltpu.roll(seg, LANES - 8, 1) * sin_lo + pltpu.roll(seg, 8, 1) * sin_hi

    def put_state(f_ref, blocks):
        for n, blk in enumerate(blocks):
            if feature_states:
                f_ref[0, LANES * n:LANES * (n + 1), :] = blk.T
            else:
                f_ref[:, LANES * n:LANES * (n + 1)] = blk

    lane = _iota((tm, LANES), 1)
    pq = _dot_nt(hb, w_ref[C_Q:C_Q + NSA_W, :])
    for s in range(NSA_HPG):
        seg = rope(pq[:, LANES * s:LANES * (s + 1)]) * (SCALE * LOG2E)
        for g in range(NSA_GROUPS):
            qn_ref[g, s] = jnp.where(lane // HEAD_DIM == g, seg, 0.0).astype(BF16)

    for c0, f_ref, h_ref in ((C_CMP, kvc_ref, None), (C_SEL, kvs_ref, kbs_ref), (C_WIN, kvw_ref, kbw_ref)):
        p = _dot_nt(hb, w_ref[c0:c0 + KV_W, :])
        k, v = rope(p[:, 0:128]), p[:, 128:256]
        put_state(f_ref, [k, v])
        if h_ref is not None:
            h_ref[:, 0:128] = k.astype(BF16)
            h_ref[:, 128:256] = v.astype(BF16)

    lane_w = _iota((tm, SB_W), 1)
    for c0, q_ref, f_ref, h_ref in ((C_SB, qsb_ref, kvsb_ref, kbsb_ref), (C_FOX, qfx_ref, kvfx_ref, kbfx_ref)):
        p = _dot_nt(hb, w_ref[c0:c0 + 3 * SB_W, :])
        q = p[:, 0:SB_W] * (SCALE * LOG2E)
        for h in range(SB_HEADS):
            q_ref[h] = jnp.where(lane_w // HEAD_DIM == h, q, 0.0).astype(BF16)
        kv = p[:, SB_W:3 * SB_W]
        put_state(f_ref, [kv[:, LANES * n:LANES * (n + 1)] for n in range(2 * SB_W // LANES)])
        h_ref[...] = kv.astype(BF16)

    pm = _dot_nt(hb, w_ref[C_MISC:C_MISC + LANES, :])
    sg = jax.nn.sigmoid(pm)
    gate_ref[0] = sg
    gate_ref[1] = pltpu.roll(sg, LANES - GATE_W // 2, 1)
    zf = pm + b_ref[...]
    lf = jnp.minimum(zf, 0.0) - jnp.log1p(jnp.exp(-jnp.abs(zf)))
    if feature_states:
        logf_ref[0] = lf.T[LOGF_LANE:LOGF_LANE + FOX_HEADS, :]
    else:
        logf_ref[...] = lf

    @pl.when(i % tiles_per_seq == 0)
    def _():
        carry_scr[...] = jnp.zeros_like(carry_scr)

    c = _split_dot_l(tri_ref[...], lf, 3) + carry_scr[0:1, :]
    cum_ref[...] = c
    carry_scr[...] = jnp.broadcast_to(c[tm - 1:tm, :], carry_scr.shape)


def _proj(x, gain, w, rope_tab, bias, *, tm, tiles_per_seq, feature_states):
    m, d = x.shape
    tri = jnp.tril(jnp.ones((tm, tm), F32)).astype(BF16)
    row = lambda wid: pl.BlockSpec((tm, wid), lambda i: (i, 0))
    const = lambda shape: pl.BlockSpec(shape, lambda i: (0,) * len(shape))
    f32o = lambda wid: jax.ShapeDtypeStruct((m, wid), F32)
    b16o = lambda wid: jax.ShapeDtypeStruct((m, wid), BF16)
    if feature_states:
        nb, seq = m // (tm * tiles_per_seq), tm * tiles_per_seq
        st_shape = lambda wid: jax.ShapeDtypeStruct((nb, wid, seq), F32)
        st_spec = lambda wid: pl.BlockSpec((1, wid, tm), lambda i: (i // tiles_per_seq, 0, i % tiles_per_seq))
        lf_shape, lf_spec = st_shape(FOX_HEADS), st_spec(FOX_HEADS)
    else:
        st_shape, st_spec, lf_shape, lf_spec = f32o, row, f32o(LANES), row(LANES)
    out_shape = (
        jax.ShapeDtypeStruct((NSA_GROUPS, NSA_HPG, m, LANES), BF16),
        st_shape(KV_W), st_shape(KV_W), st_shape(KV_W), st_shape(2 * SB_W), st_shape(2 * SB_W),
        b16o(KV_W), b16o(KV_W), b16o(2 * SB_W), b16o(2 * SB_W),
        jax.ShapeDtypeStruct((SB_HEADS, m, SB_W), BF16), jax.ShapeDtypeStruct((SB_HEADS, m, SB_W), BF16),
        jax.ShapeDtypeStruct((NSA_GROUPS, m, LANES), F32), lf_shape, f32o(LANES),
    )
    out_specs = (
        pl.BlockSpec((NSA_GROUPS, NSA_HPG, tm, LANES), lambda i: (0, 0, i, 0)),
        st_spec(KV_W), st_spec(KV_W), st_spec(KV_W), st_spec(2 * SB_W), st_spec(2 * SB_W),
        row(KV_W), row(KV_W), row(2 * SB_W), row(2 * SB_W),
        pl.BlockSpec((SB_HEADS, tm, SB_W), lambda i: (0, i, 0)),
        pl.BlockSpec((SB_HEADS, tm, SB_W), lambda i: (0, i, 0)),
        pl.BlockSpec((NSA_GROUPS, tm, LANES), lambda i: (0, i, 0)),
        lf_spec, row(LANES),
    )
    return pl.pallas_call(
        functools.partial(_proj_kernel, tiles_per_seq=tiles_per_seq, feature_states=feature_states),
        grid=(m // tm,),
        in_specs=[row(d), const((1, d)), const((N_PROJ, d)),
                  pl.BlockSpec((tm, 3 * LANES), lambda i: (i % tiles_per_seq, 0)),
                  const((1, LANES)), const((tm, tm))],
        out_specs=out_specs, out_shape=out_shape,
        scratch_shapes=[pltpu.VMEM((tm, d), BF16), pltpu.VMEM((8, LANES), F32)],
        compiler_params=_cp("arbitrary"), name="proj",
    )(x, gain, w, rope_tab, bias, tri)


def _compress_kernel(pt_ref, *refs, pp, nch):
    pages, (perm_ref, w_ref, pe_ref, out_ref, x_scr, acc_scr, pos_scr) = refs[:pp], refs[pp:]
    j = pl.program_id(1)
    half = CMP_BLOCK // 2
    chunks = PAGE // CMP_STRIDE

    @pl.when((pl.program_id(0) == 0) & (j == 0))
    def _():
        c = jnp.zeros((8, KV_W), F32)
        for l in range(half):
            for off, cols in ((0, slice(0, KV_W)), (half, slice(KV_W, 2 * KV_W))):
                row = jnp.broadcast_to(pe_ref[off + l:off + l + 1, :], (8, KV_W))
                c = c + _split_dot(row, w_ref[l, :, cols], 2)
        pos_scr[...] = c

    for i in range(pp):
        gt = _dot_nt(perm_ref[...], pages[i][0].astype(BF16))
        base = pl.multiple_of((j * pp + i) * chunks, chunks)
        for l in range(half):
            x_scr[l, pl.ds(base, chunks), :] = gt[chunks * l:chunks * (l + 1), :]

    @pl.when(j == pl.num_programs(1) - 1)
    def _():
        acc_scr[...] = jnp.zeros_like(acc_scr)
        for l in range(half):
            acc_scr[...] += _dot(x_scr[l].astype(BF16), w_ref[l])
        nxt = pltpu.roll(acc_scr[:, KV_W:2 * KV_W], nch - 1, 0)
        nxt = jnp.where(_iota((nch, KV_W), 0) < nch - 1, nxt, 0.0)
        out_ref[0] = (acc_scr[:, 0:KV_W] + nxt + pos_scr[0:1, :]).astype(BF16)


def _compress(pool, page_table, w_cat, pe4, *, pp, paged):
    b, n_pages = page_table.shape
    chunks = PAGE // CMP_STRIDE
    nch = n_pages * chunks
    half = CMP_BLOCK // 2
    tok = np.arange(PAGE)
    perm = jnp.asarray(np.arange(PAGE)[:, None] == ((tok % CMP_STRIDE) * chunks + tok // CMP_STRIDE)[None, :], BF16)
    if paged:
        page_spec = lambda i: pl.BlockSpec((1, KV_W, PAGE), lambda bb, j, pt: (pt[bb, j * pp + i], 0, 0))
    else:
        page_spec = lambda i: pl.BlockSpec((1, KV_W, PAGE), lambda bb, j, pt: (bb, 0, j * pp + i))
    grid_spec = pltpu.PrefetchScalarGridSpec(
        num_scalar_prefetch=1, grid=(b, n_pages // pp),
        in_specs=[page_spec(i) for i in range(pp)] + [
            pl.BlockSpec((PAGE, PAGE), lambda bb, j, pt: (0, 0)),
            pl.BlockSpec((half, KV_W, 2 * KV_W), lambda bb, j, pt: (0, 0, 0)),
            pl.BlockSpec((CMP_BLOCK, KV_W), lambda bb, j, pt: (0, 0))],
        out_specs=pl.BlockSpec((1, nch, KV_W), lambda bb, j, pt: (bb, 0, 0)),
        scratch_shapes=[pltpu.VMEM((half, nch, KV_W), F32), pltpu.VMEM((nch, 2 * KV_W), F32),
                        pltpu.VMEM((8, KV_W), F32)])
    return pl.pallas_call(
        functools.partial(_compress_kernel, pp=pp, nch=nch), grid_spec=grid_spec,
        out_shape=jax.ShapeDtypeStruct((b, nch, KV_W), BF16),
        compiler_params=_cp("arbitrary", "arbitrary"), name="compress",
    )(page_table, *([pool] * pp), perm, w_cat, pe4)


def _masked_softmax_rows(s, mask):
    s = jnp.where(mask, s, NEG)
    e = jnp.where(mask, jnp.exp2(s - jnp.max(s, axis=1, keepdims=True)), 0.0)
    return e / jnp.maximum(jnp.sum(e, axis=1, keepdims=True), 1e-30)


def _online_init(m_scr, l_scr, acc_scr):
    m_scr[...] = jnp.full_like(m_scr, NEG)
    l_scr[...] = jnp.zeros_like(l_scr)
    acc_scr[...] = jnp.zeros_like(acc_scr)


def _select_blocks(imp, pos, n_sel):
    blk = _iota(imp.shape, 1)
    cur = pos // SEL_BLOCK
    valid = blk * SEL_BLOCK <= pos
    forced = jnp.where(blk == 0, 1.0, 0.0) + jnp.where(blk == cur, 1.0, 0.0) + jnp.where(blk == cur - 1, 1.0, 0.0)
    score = jnp.where(valid, jnp.where(forced > 0.5, SEL_FORCE_SCORE, imp), -1.0)
    score = jnp.where(blk < n_sel, score, -2.0)
    rank = jnp.zeros(imp.shape, F32)
    for b2 in range(n_sel):
        col = score[:, b2:b2 + 1]
        ge = jnp.where(col >= score, 1.0, 0.0)
        gt = jnp.where(col > score, 1.0, 0.0)
        rank = rank + jnp.where(blk > b2, ge, gt)
    return jnp.where(rank < min(TOP_N, n_sel), 1.0, 0.0)


def _rep(x, n):
    return x if n == 1 else jnp.concatenate([x] * n, axis=1)


def _select_blocks_t(p_grp, amat_t, q0, n_sel):
    t = p_grp.shape[0]
    nb = amat_t.shape[0]
    hi = p_grp.astype(BF16)
    lo = (p_grp - hi.astype(F32)).astype(BF16)
    imp = _dot_nt(amat_t, hi) + _dot_nt(amat_t, lo)
    pos = q0 + _iota((1, t), 1)
    blk = _iota((nb, t), 0)
    cur = pos // SEL_BLOCK
    forced = jnp.where(blk == 0, 1.0, 0.0) + jnp.where(blk == cur, 1.0, 0.0) + jnp.where(blk == cur - 1, 1.0, 0.0)
    score = jnp.where(blk * SEL_BLOCK <= pos, jnp.where(forced > 0.5, SEL_FORCE_SCORE, imp), -1.0)
    score = jnp.where(blk < n_sel, score, -2.0)
    groups = [score[8 * k:8 * (k + 1)] for k in range(nb // 8)]
    ranks = [jnp.zeros((8, t), F32) for _ in groups]
    sub = _iota((8, t), 0)
    for b2 in range(n_sel):
        rowv = score[b2:b2 + 1, :]
        k2, r2 = divmod(b2, 8)
        for k, grp in enumerate(groups):
            if k > k2:
                inc = jnp.where(rowv >= grp, 1.0, 0.0)
            elif k < k2:
                inc = jnp.where(rowv > grp, 1.0, 0.0)
            else:
                inc = jnp.where(sub > r2, jnp.where(rowv >= grp, 1.0, 0.0), jnp.where(rowv > grp, 1.0, 0.0))
            ranks[k] = ranks[k] + inc
    chosen = [jnp.where(rk < min(TOP_N, n_sel), 1.0, 0.0) for rk in ranks]
    pad = [jnp.zeros((LANES - nb, t), F32)] if nb < LANES else []
    return jnp.concatenate(chosen + pad, axis=0)


def _nsa_prompt_kernel(qn_ref, ckv_ref, ks_ref, kw_ref, gate_ref, amat_ref, e_ref, out_ref,
                       m_scr, l_scr, acc_scr, oc_scr, os_scr, *, n_sel):
    g, i = pl.program_id(1), pl.program_id(2)
    t = qn_ref.shape[2]
    q0 = i * t
    rowi, coli = _iota((t, t), 0), _iota((t, t), 1)
    reps = t // LANES

    nc = ckv_ref.shape[1]
    cmask = _iota((t, nc), 1) * CMP_STRIDE + (CMP_BLOCK - 1) <= q0 + _iota((t, 1), 0)
    ck, cv = ckv_ref[0, :, 0:128], ckv_ref[0, :, 128:256]
    p_grp = None
    for h in range(NSA_HPG):
        p = _masked_softmax_rows(_dot_nt(qn_ref[0, h], ck), cmask)
        oc_scr[h] = _dot(p.astype(BF16), cv)
        p_grp = p if p_grp is None else p_grp + p
    sel = _select_blocks_t(p_grp, amat_ref[...], q0, n_sel).T.astype(BF16)

    def online_head(h, s, v):
        m_prev = m_scr[h]
        m_next = jnp.maximum(m_prev, jnp.max(s, axis=1, keepdims=True))
        alpha = jnp.exp2(m_prev - m_next)
        e = jnp.exp2(s - _rep(m_next, reps))
        l_scr[h] = alpha * l_scr[h] + jnp.sum(e, axis=1, keepdims=True)
        m_scr[h] = m_next
        acc_scr[h] = acc_scr[h] * alpha + _dot(e.astype(BF16), v)

    def reset():
        m_scr[...] = jnp.full_like(m_scr, NEG)
        l_scr[...] = jnp.zeros_like(l_scr)
        acc_scr[...] = jnp.zeros_like(acc_scr)

    def sel_tile(j, diag):
        kt = ks_ref[0, pl.ds(pl.multiple_of(j * t, t), t), :]
        chosen = _dot(sel, e_ref[j])
        if diag:
            chosen = jnp.where(coli <= rowi, chosen, 0.0)
        keep = chosen > 0.5
        for h in range(NSA_HPG):
            online_head(h, jnp.where(keep, _dot_nt(qn_ref[0, h], kt[:, 0:128]), NEG), kt[:, 128:256])

    reset()
    lax.fori_loop(0, i, lambda j, c: (sel_tile(j, False), c)[1], 0)
    sel_tile(i, True)
    for h in range(NSA_HPG):
        os_scr[h] = acc_scr[h] / jnp.maximum(l_scr[h], 1e-30)

    tw = min(t, 2 * LANES)
    w_len = WINDOW + tw
    gates = gate_ref[0]
    lane = _iota((tw, LANES), 1)
    for rb in range(t // tw):
        rows = slice(rb * tw, (rb + 1) * tw)
        r0 = q0 + rb * tw
        k0 = jnp.maximum(r0 - WINDOW, 0)
        kw = kw_ref[0, pl.ds(pl.multiple_of(k0, tw), w_len), :]
        d = (r0 - k0) + _iota((tw, w_len), 0) - _iota((tw, w_len), 1)
        keep_w = jnp.where(d >= 0, jnp.where(d <= WINDOW, 1.0, 0.0), 0.0) > 0.5
        for h in range(NSA_HPG):
            s = jnp.where(keep_w, _dot_nt(qn_ref[0, h, rows, :], kw[:, 0:128]), NEG)
            e = jnp.exp2(s - jnp.max(s, axis=1, keepdims=True))
            o_w = _dot(e.astype(BF16), kw[:, 128:256]) / jnp.sum(e, axis=1, keepdims=True)
            o = (gates[rows, 3 * h:3 * h + 1] * oc_scr[h, rows, :] + gates[rows, 3 * h + 1:3 * h + 2] * os_scr[h, rows, :]
                 + gates[rows, 3 * h + 2:3 * h + 3] * o_w)
            out_ref[0, rows, LANES * h:LANES * (h + 1)] = jnp.where(lane // HEAD_DIM == g, o, 0.0).astype(BF16)


def _imp_matrix(nc, width):
    n = np.arange(nc)[:, None]
    b = np.arange(width)[None, :]
    sub = SEL_BLOCK // CMP_STRIDE
    return jnp.asarray((n // sub == b).astype(np.float32) + ((n + 1) // sub == b), BF16)


def _expand_tiles(n_tiles, n_blk, tk):
    key = np.arange(n_tiles * tk).reshape(n_tiles, 1, tk)
    return jnp.asarray(key // SEL_BLOCK == np.arange(n_blk)[None, :, None], BF16)


def _nsa_prompt(qn, ckv, kb_sel, kb_win, gates, *, batch, seq, tq):
    nq = seq // tq
    nc = ckv.shape[1]
    n_sel = seq // SEL_BLOCK
    assert n_sel <= LANES and WINDOW % tq == 0 and seq >= WINDOW + tq
    amat = _imp_matrix(nc, -(-n_sel // 8) * 8).T
    etile = _expand_tiles(nq, LANES, tq)
    m = batch * seq
    return pl.pallas_call(
        functools.partial(_nsa_prompt_kernel, n_sel=n_sel),
        grid=(batch, NSA_GROUPS, nq),
        in_specs=[
            pl.BlockSpec((1, NSA_HPG, tq, LANES), lambda b, g, i: (g, 0, b * nq + i, 0)),
            pl.BlockSpec((1, nc, KV_W), lambda b, g, i: (b, 0, 0)),
            pl.BlockSpec((1, seq, KV_W), lambda b, g, i: (b, 0, 0)),
            pl.BlockSpec((1, seq, KV_W), lambda b, g, i: (b, 0, 0)),
            pl.BlockSpec((1, tq, LANES), lambda b, g, i: (g, b * nq + i, 0)),
            pl.BlockSpec(amat.shape, lambda b, g, i: (0, 0)),
            pl.BlockSpec(etile.shape, lambda b, g, i: (0, 0, 0)),
        ],
        out_specs=pl.BlockSpec((1, tq, NSA_W), lambda b, g, i: (g, b * nq + i, 0)),
        out_shape=jax.ShapeDtypeStruct((NSA_GROUPS, m, NSA_W), BF16),
        scratch_shapes=[pltpu.VMEM((NSA_HPG, tq, LANES), F32)] * 5,
        compiler_params=_cp("arbitrary", "arbitrary", "arbitrary"), name="nsa_prompt",
    )(qn, ckv, kb_sel.reshape(batch, seq, KV_W), kb_win.reshape(batch, seq, KV_W), gates, amat, etile)


def _upper_strict(n):
    return jnp.asarray(np.triu(np.ones((n, n), np.float32), 1).T, BF16)


def _stack_heads(v):
    lane = _iota(v.shape, 1)
    return jnp.concatenate([jnp.where(lane // HEAD_DIM == h, v, jnp.zeros_like(v)) for h in range(SB_HEADS)], axis=0)


def _head_lanes(parts):
    lane = _iota(parts[0].shape, 1)
    return jnp.concatenate([jnp.where(lane < HEAD_DIM, parts[0], parts[1]),
                            jnp.where(lane < HEAD_DIM, parts[2], parts[3])], axis=1)


def _sb_prompt_kernel(q_ref, kv_ref, u_ref, out_ref, acc_scr, r_scr):
    i = pl.program_id(1)
    t = q_ref.shape[1]
    rowi, coli = _iota((t, t), 0), _iota((t, t), 1)
    acc_scr[...] = jnp.zeros_like(acc_scr)
    r_scr[...] = jnp.zeros_like(r_scr)

    def tile(j, diag):
        kt = kv_ref[0, pl.ds(pl.multiple_of(j * t, t), t), :]
        ws = []
        for h in range(SB_HEADS):
            z = _dot_nt(q_ref[h], kt[:, 0:SB_W])
            sp = _softplus2(z)
            cost = jnp.where(coli < rowi, sp, 0.0) if diag else sp
            r = r_scr[h]
            w = jnp.exp2(z - sp - _split_dot(cost, u_ref[...], 1) - _rep(r, t // LANES))
            if diag:
                w = jnp.where(coli < rowi, w, 0.0)
            ws.append(w.astype(BF16))
            r_scr[h] = r + jnp.sum(cost, axis=1, keepdims=True)
        acc_scr[...] += _dot(jnp.concatenate(ws, axis=1), _stack_heads(kt[:, SB_W:2 * SB_W]))

    tile(i, True)

    def four_tiles(n, c):
        for u in range(4):
            tile(i - 1 - 4 * n - u, False)
        return c

    lax.fori_loop(0, i // 4, four_tiles, 0)
    rem = i % 4

    @pl.when(rem >= 2)
    def _():
        tile(rem - 1, False)
        tile(rem - 2, False)

    @pl.when(rem % 2 == 1)
    def _():
        tile(0, False)

    out_ref[...] = acc_scr[...].astype(BF16)


def _sb_prompt(q, kb, *, batch, seq, tq):
    nq = seq // tq
    u = _upper_strict(tq)
    return pl.pallas_call(
        _sb_prompt_kernel, grid=(batch, nq),
        in_specs=[pl.BlockSpec((SB_HEADS, tq, SB_W), lambda b, i: (0, b * nq + i, 0)),
                  pl.BlockSpec((1, seq, 2 * SB_W), lambda b, i: (b, 0, 0)),
                  pl.BlockSpec(u.shape, lambda b, i: (0, 0))],
        out_specs=pl.BlockSpec((tq, SB_W), lambda b, i: (b * nq + i, 0)),
        out_shape=jax.ShapeDtypeStruct((batch * seq, SB_W), BF16),
        scratch_shapes=[pltpu.VMEM((tq, SB_W), F32), pltpu.VMEM((SB_HEADS, tq, LANES), F32)],
        compiler_params=_cp("arbitrary", "arbitrary"), name="sb_prompt",
    )(q, kb.reshape(batch, seq, 2 * SB_W), u)


def _fox_prompt_kernel(q_ref, kv_ref, cq_ref, ck_ref, out_ref, m_scr, l_scr, acc_scr, cq_scr):
    i = pl.program_id(1)
    t = q_ref.shape[1]
    rowi, coli = _iota((t, t), 0), _iota((t, t), 1)
    m_scr[...] = jnp.full_like(m_scr, NEG)
    l_scr[...] = jnp.zeros_like(l_scr)
    acc_scr[...] = jnp.zeros_like(acc_scr)
    for h in range(FOX_HEADS):
        cq_scr[h] = jnp.broadcast_to(cq_ref[:, LOGF_LANE + h:LOGF_LANE + h + 1] * LOG2E, (t, LANES))

    def tile(j, diag):
        kt = kv_ref[0, pl.ds(pl.multiple_of(j * t, t), t), :]
        es, alphas = [], []
        for h in range(FOX_HEADS):
            s = _dot_nt(q_ref[h], kt[:, 0:SB_W]) + (_rep(cq_scr[h], t // LANES) - ck_ref[0, h, j] * LOG2E)
            if diag:
                s = jnp.where(coli <= rowi, s, NEG)
            m_prev = m_scr[h]
            m_next = jnp.maximum(m_prev, jnp.max(s, axis=1, keepdims=True))
            alpha = jnp.exp2(m_prev - m_next)
            e = jnp.exp2(s - _rep(m_next, t // LANES))
            l_scr[h] = alpha * l_scr[h] + jnp.sum(e, axis=1, keepdims=True)
            m_scr[h] = m_next
            es.append(e.astype(BF16))
            alphas.append(alpha)
        acc_scr[...] = acc_scr[...] * _head_lanes(alphas) + _dot(jnp.concatenate(es, axis=1),
                                                                 _stack_heads(kt[:, SB_W:2 * SB_W]))

    def two_tiles(n, c):
        tile(2 * n, False)
        tile(2 * n + 1, False)
        return c

    lax.fori_loop(0, i // 2, two_tiles, 0)

    @pl.when(i % 2 == 1)
    def _():
        tile(i - 1, False)

    tile(i, True)
    out_ref[...] = (acc_scr[...] / _head_lanes([jnp.maximum(l_scr[h], 1e-30) for h in range(FOX_HEADS)])).astype(BF16)


def _fox_prompt(q, kb, cum, cum_k, *, batch, seq, tq):
    nq = seq // tq
    nk = seq // tq
    return pl.pallas_call(
        _fox_prompt_kernel, grid=(batch, nq),
        in_specs=[pl.BlockSpec((FOX_HEADS, tq, SB_W), lambda b, i: (0, b * nq + i, 0)),
                  pl.BlockSpec((1, seq, 2 * SB_W), lambda b, i: (b, 0, 0)),
                  pl.BlockSpec((tq, LANES), lambda b, i: (b * nq + i, 0)),
                  pl.BlockSpec((1, FOX_HEADS, nk, 1, tq), lambda b, i: (b, 0, 0, 0, 0))],
        out_specs=pl.BlockSpec((tq, SB_W), lambda b, i: (b * nq + i, 0)),
        out_shape=jax.ShapeDtypeStruct((batch * seq, SB_W), BF16),
        scratch_shapes=[pltpu.VMEM((FOX_HEADS, tq, LANES), F32), pltpu.VMEM((FOX_HEADS, tq, LANES), F32),
                        pltpu.VMEM((tq, SB_W), F32), pltpu.VMEM((FOX_HEADS, tq, LANES), F32)],
        compiler_params=_cp("arbitrary", "arbitrary"), name="fox_prompt",
    )(q, kb.reshape(batch, seq, 2 * SB_W), cum, cum_k)


def _merge_kernel(x_ref, g_ref, oa_ref, ob_ref, oc_ref, wm_ref, wa_ref, wb_ref, wc_ref, wo_ref, out_ref):
    x = x_ref[...]
    d = x.shape[1]
    hb = _rms(x, g_ref[...]).astype(BF16)
    oa = oa_ref[0] + oa_ref[1]
    u = jax.nn.sigmoid(_dot_nt(hb, wm_ref[0:d, :])) * _dot(oa, wa_ref[...])
    u = u + jax.nn.sigmoid(_dot_nt(hb, wm_ref[d:2 * d, :])) * _dot(ob_ref[...], wb_ref[...])
    u = u + jax.nn.sigmoid(_dot_nt(hb, wm_ref[2 * d:3 * d, :])) * _dot(oc_ref[...], wc_ref[...])
    out_ref[...] = x + _dot(u.astype(BF16), wo_ref[...])


def _merge(x, gain, o_nsa, o_sb, o_fox, w_merge, w_a, w_b, w_c, w_o, *, tm):
    m, d = x.shape
    row = lambda wid: pl.BlockSpec((tm, wid), lambda i: (i, 0))
    const = lambda a: pl.BlockSpec(a.shape, lambda i: (0,) * a.ndim)
    return pl.pallas_call(
        _merge_kernel, grid=(m // tm,),
        in_specs=[row(d), const(gain), pl.BlockSpec((NSA_GROUPS, tm, NSA_W), lambda i: (0, i, 0)),
                  row(SB_W), row(SB_W), const(w_merge), const(w_a), const(w_b), const(w_c), const(w_o)],
        out_specs=row(d), out_shape=jax.ShapeDtypeStruct((m, d), F32),
        compiler_params=_cp("arbitrary"), name="merge",
    )(x, gain, o_nsa, o_sb, o_fox, w_merge, w_a, w_b, w_c, w_o)


def _ffn_kernel(x_ref, g_ref, wg_ref, wu_ref, wd_ref, gf_ref, out_ref, *, final_norm):
    x = x_ref[...]
    hb = _rms(x, g_ref[...]).astype(BF16)
    a = _dot(hb, wg_ref[...])
    act = a * jax.nn.sigmoid(a) * _dot(hb, wu_ref[...])
    y = x + _dot(act.astype(BF16), wd_ref[...])
    out_ref[...] = _rms(y, gf_ref[...]) if final_norm else y


def _ffn(x, gain, w_gate, w_up, w_down, final_gain, *, tm, final_norm):
    m, d = x.shape
    row = pl.BlockSpec((tm, d), lambda i: (i, 0))
    vec = pl.BlockSpec((1, d), lambda i: (0, 0))
    const = lambda a: pl.BlockSpec(a.shape, lambda i: (0,) * a.ndim, pipeline_mode=pl.Buffered(1))
    return pl.pallas_call(
        functools.partial(_ffn_kernel, final_norm=final_norm), grid=(m // tm,),
        in_specs=[row, vec, const(w_gate), const(w_up), const(w_down), vec],
        out_specs=row, out_shape=jax.ShapeDtypeStruct((m, d), F32),
        compiler_params=_cp("arbitrary"), name="ffn",
    )(x, gain, w_gate, w_up, w_down, final_gain)


def _page_specs(rows, n_pages, pp, descending):
    def spec(i):
        if descending:
            return pl.BlockSpec((1, rows, PAGE), lambda b, j, pt: (pt[b, n_pages - 1 - (j * pp + i)], 0, 0))
        return pl.BlockSpec((1, rows, PAGE), lambda b, j, pt: (pt[b, j * pp + i], 0, 0))
    return [spec(i) for i in range(pp)]


def _lanes(parts):
    return jnp.concatenate(parts, axis=1)


def _sb_dec_kernel(pt_ref, q_ref, *refs, pp):
    pages, (u_ref, out_ref, acc_scr, r_scr) = refs[:pp], refs[pp:]
    j = pl.program_id(1)

    @pl.when(j == 0)
    def _():
        acc_scr[...] = jnp.zeros_like(acc_scr)
        r_scr[...] = jnp.zeros_like(r_scr)

    q = q_ref[0]
    z_all = _dot(q, _lanes([p[0, 0:SB_W, :].astype(BF16) for p in pages]))
    zs = [z_all[:, i * PAGE:(i + 1) * PAGE] for i in range(pp)]
    sps = [_softplus2(z) for z in zs]
    afters = [_split_dot(sp, u_ref[...], 2) for sp in sps]
    r = r_scr[...]
    ws = []
    for i in range(pp):
        ws.append(jnp.exp2(zs[i] - sps[i] - afters[i] - r).astype(BF16))
        r = r + jnp.sum(sps[i], axis=1, keepdims=True)
    acc_scr[...] += _dot_nt(_lanes(ws), _lanes([p[0, SB_W:2 * SB_W, :].astype(BF16) for p in pages]))
    r_scr[...] = r

    @pl.when(j == pl.num_programs(1) - 1)
    def _():
        out_ref[0] = acc_scr[...]


def _sb_dec(q8, pool, page_table, *, pp):
    bd, n_pages = page_table.shape
    u = _upper_strict(PAGE)
    grid_spec = pltpu.PrefetchScalarGridSpec(
        num_scalar_prefetch=1, grid=(bd, n_pages // pp),
        in_specs=[pl.BlockSpec((1, 8, SB_W), lambda b, j, pt: (b, 0, 0))]
        + _page_specs(2 * SB_W, n_pages, pp, True) + [pl.BlockSpec(u.shape, lambda b, j, pt: (0, 0))],
        out_specs=pl.BlockSpec((1, 8, SB_W), lambda b, j, pt: (b, 0, 0)),
        scratch_shapes=[pltpu.VMEM((8, SB_W), F32), pltpu.VMEM((8, LANES), F32)])
    return pl.pallas_call(
        functools.partial(_sb_dec_kernel, pp=pp), grid_spec=grid_spec,
        out_shape=jax.ShapeDtypeStruct((bd, 8, SB_W), F32),
        compiler_params=_cp("arbitrary", "arbitrary"), name="sb_decode",
    )(page_table, q8, *([pool] * pp), u)


def _fox_dec_kernel(pt_ref, q_ref, kn_ref, vn_ref, lfn_ref, *refs, pp):
    pages, lf_pages = refs[:pp], refs[pp:2 * pp]
    u_ref, out_ref, m_scr, l_scr, acc_scr, r_scr = refs[2 * pp:]
    j = pl.program_id(1)
    q = q_ref[0]
    reps = SB_W // LANES

    @pl.when(j == 0)
    def _():
        kn = kn_ref[0].astype(BF16).astype(F32)
        m_scr[...] = jnp.broadcast_to(jnp.sum(q.astype(F32) * kn, axis=1, keepdims=True), m_scr.shape)
        l_scr[...] = jnp.ones_like(l_scr)
        acc_scr[...] = jnp.broadcast_to(vn_ref[0].astype(BF16).astype(F32), acc_scr.shape)
        r_scr[...] = lfn_ref[0]

    z_all = _dot(q, _lanes([p[0, 0:SB_W, :].astype(BF16) for p in pages]))
    pad = jnp.zeros((8 - FOX_HEADS, PAGE), F32)
    r = r_scr[...]
    ss = []
    for i in range(pp):
        lf = jnp.concatenate([lf_pages[i][0], pad], axis=0)
        ss.append(z_all[:, i * PAGE:(i + 1) * PAGE] + LOG2E * (_split_dot(lf, u_ref[...], 3) + r))
        r = r + jnp.sum(lf, axis=1, keepdims=True)
    r_scr[...] = r
    m_prev = m_scr[...]
    m_next = m_prev
    for s in ss:
        m_next = jnp.maximum(m_next, jnp.max(s, axis=1, keepdims=True))
    alpha = jnp.exp2(m_prev - m_next)
    es = [jnp.exp2(s - m_next) for s in ss]
    l_new = alpha * l_scr[...]
    for e in es:
        l_new = l_new + jnp.sum(e, axis=1, keepdims=True)
    l_scr[...] = l_new
    m_scr[...] = m_next
    acc_scr[...] = acc_scr[...] * _rep(alpha, reps) + _dot_nt(
        _lanes([e.astype(BF16) for e in es]), _lanes([p[0, SB_W:2 * SB_W, :].astype(BF16) for p in pages]))

    @pl.when(j == pl.num_programs(1) - 1)
    def _():
        out_ref[0] = acc_scr[...] / _rep(l_scr[...], reps)


def _fox_dec(q8, k_new, v_new, lf_new, pool, lf_pool, page_table, *, pp):
    bd, n_pages = page_table.shape
    u = _upper_strict(PAGE)
    req = lambda shape: pl.BlockSpec((1,) + shape, lambda b, j, pt: (b,) + (0,) * len(shape))
    lf_spec = lambda i: pl.BlockSpec((1, FOX_HEADS, PAGE), lambda b, j, pt: (pt[b, n_pages - 1 - (j * pp + i)], 0, 0))
    grid_spec = pltpu.PrefetchScalarGridSpec(
        num_scalar_prefetch=1, grid=(bd, n_pages // pp),
        in_specs=[req((8, SB_W)), req((1, SB_W)), req((1, SB_W)), req((8, LANES))]
        + _page_specs(2 * SB_W, n_pages, pp, True) + [lf_spec(i) for i in range(pp)]
        + [pl.BlockSpec(u.shape, lambda b, j, pt: (0, 0))],
        out_specs=req((8, SB_W)),
        scratch_shapes=[pltpu.VMEM((8, LANES), F32), pltpu.VMEM((8, LANES), F32),
                        pltpu.VMEM((8, SB_W), F32), pltpu.VMEM((8, LANES), F32)])
    return pl.pallas_call(
        functools.partial(_fox_dec_kernel, pp=pp), grid_spec=grid_spec,
        out_shape=jax.ShapeDtypeStruct((bd, 8, SB_W), F32),
        compiler_params=_cp("arbitrary", "arbitrary"), name="fox_decode",
    )(page_table, q8, k_new, v_new, lf_new, *([pool] * pp), *([lf_pool] * pp), u)


def _nsa_dec_kernel(pt_ref, q_ref, ckv_ref, gate_ref, win_ref, kvs_new_ref, kvw_new_ref, kvw_col_ref, amat_ref, e_ref,
                    *refs, pp, past_len, n_sel):
    pages, (out_ref, win_out_ref, m_scr, l_scr, acc_scr, sel_scr, oc_scr, ow_scr) = refs[:pp], refs[pp:]
    j = pl.program_id(1)
    q = q_ref[0]
    rows = q.shape[0]
    row = _iota((rows, LANES), 0)
    row1 = _iota((rows, 1), 0)
    lane = _iota((rows, LANES), 1)
    pos = jnp.full((rows, 1), past_len, I32)

    def new_token(kv_row):
        k = kv_row[:, 0:LANES].astype(BF16).astype(F32)
        return jnp.sum(q.astype(F32) * k, axis=1, keepdims=True), kv_row[:, LANES:2 * LANES].astype(BF16).astype(F32)

    @pl.when(j == 0)
    def _():
        nc = ckv_ref.shape[1]
        cmask = _iota((rows, nc), 1) * CMP_STRIDE + (CMP_BLOCK - 1) <= pos
        p = _masked_softmax_rows(_dot_nt(q, ckv_ref[0, :, 0:128]), cmask)
        oc_scr[...] = _dot(p.astype(BF16), ckv_ref[0, :, 128:256])
        g0 = jnp.sum(p[0:NSA_HPG], axis=0, keepdims=True)
        g1 = jnp.sum(p[NSA_HPG:2 * NSA_HPG], axis=0, keepdims=True)
        p_grp = jnp.where(row1 < NSA_HPG, g0, g1)
        imp = _split_dot(p_grp, amat_ref[...], 2)
        sel_scr[...] = _select_blocks(imp, pos, n_sel)

        wk = win_ref[0]
        n_w = wk.shape[1]
        s = _dot(q, wk[0:LANES, :].astype(BF16))
        kpos = past_len - n_w + _iota((rows, n_w), 1)
        keep = jnp.where(kpos >= 0, jnp.where(pos - kpos <= WINDOW, 1.0, 0.0), 0.0) > 0.5
        s = jnp.where(keep, s, NEG)
        s_new, v_new = new_token(kvw_new_ref[0])
        m = jnp.maximum(s_new, jnp.max(s, axis=1, keepdims=True))
        e = jnp.exp2(s - m)
        e_new = jnp.exp2(s_new - m)
        o_w = _dot_nt(e.astype(BF16), wk[LANES:2 * LANES, :].astype(BF16)) + e_new * v_new
        ow_scr[...] = o_w / (jnp.sum(e, axis=1, keepdims=True) + e_new)

        shifted = pltpu.roll(wk, n_w - 1, 1)
        win_out_ref[0] = jnp.where(_iota(wk.shape, 1) == n_w - 1, kvw_col_ref[0], shifted)
        _online_init(m_scr, l_scr, acc_scr)

    sel = sel_scr[...].astype(BF16)
    z_all = _dot(q, _lanes([p[0, 0:LANES, :].astype(BF16) for p in pages]))
    ss = []
    for i in range(pp):
        chosen = _dot(sel, e_ref[j * pp + i]) > 0.5
        ss.append(jnp.where(chosen, z_all[:, i * PAGE:(i + 1) * PAGE], NEG))
    m_prev = m_scr[...]
    m_next = m_prev
    for s in ss:
        m_next = jnp.maximum(m_next, jnp.max(s, axis=1, keepdims=True))
    alpha = jnp.exp2(m_prev - m_next)
    es = [jnp.exp2(s - m_next) for s in ss]
    l_new = alpha * l_scr[...]
    for e in es:
        l_new = l_new + jnp.sum(e, axis=1, keepdims=True)
    l_scr[...] = l_new
    m_scr[...] = m_next
    acc_scr[...] = acc_scr[...] * alpha + _dot_nt(_lanes([e.astype(BF16) for e in es]),
                                                  _lanes([p[0, LANES:2 * LANES, :].astype(BF16) for p in pages]))

    @pl.when(j == pl.num_programs(1) - 1)
    def _():
        s_new, v_new = new_token(kvs_new_ref[0])
        cur = past_len // SEL_BLOCK
        chosen = jnp.sum(jnp.where(_iota(sel_scr.shape, 1) == cur, sel_scr[...], 0.0), axis=1, keepdims=True)
        s_new = jnp.where(chosen > 0.5, s_new, NEG)
        m_prev = m_scr[...]
        m_next = jnp.maximum(m_prev, s_new)
        alpha = jnp.exp2(m_prev - m_next)
        e_new = jnp.exp2(s_new - m_next)
        o_s = (acc_scr[...] * alpha + e_new * v_new) / jnp.maximum(alpha * l_scr[...] + e_new, 1e-30)
        gates = jnp.where(row < NSA_HPG, gate_ref[0, 0:1, :], gate_ref[0, 1:2, :])
        hh = row % NSA_HPG

        def gate(c):
            return jnp.sum(jnp.where(lane == 3 * hh + c, gates, 0.0), axis=1, keepdims=True)

        out_ref[0] = gate(0) * oc_scr[...] + gate(1) * o_s + gate(2) * ow_scr[...]


def _nsa_dec(q8, ckv, gates, win_state, kvs_new, kvw_new, pool, page_table, *, pp, layer):
    bd, n_pages = page_table.shape
    past_len = n_pages * PAGE
    n_sel = -(-(past_len + 1) // SEL_BLOCK)
    nc = ckv.shape[1]
    wid = -(-n_sel // LANES) * LANES
    amat = _imp_matrix(nc, wid)
    etile = _expand_tiles(n_pages, wid, PAGE)
    n_w = win_state.shape[2]
    req = lambda shape: pl.BlockSpec((1,) + shape, lambda b, j, pt: (b,) + (0,) * len(shape))
    const = lambda a: pl.BlockSpec(a.shape, lambda b, j, pt: (0,) * a.ndim)
    win_spec = pl.BlockSpec((1, KV_W, n_w), lambda b, j, pt: (layer * bd + b, 0, 0))
    grid_spec = pltpu.PrefetchScalarGridSpec(
        num_scalar_prefetch=1, grid=(bd, n_pages // pp),
        in_specs=[req((8, LANES)), req((nc, KV_W)), req((NSA_GROUPS, LANES)), win_spec,
                  req((1, KV_W)), req((1, KV_W)), req((KV_W, 1)), const(amat), const(etile)]
        + _page_specs(KV_W, n_pages, pp, False),
        out_specs=(req((8, LANES)), req((KV_W, n_w))),
        scratch_shapes=[pltpu.VMEM((8, LANES), F32)] * 3 + [pltpu.VMEM((8, wid), F32)] + [pltpu.VMEM((8, LANES), F32)] * 2)
    return pl.pallas_call(
        functools.partial(_nsa_dec_kernel, pp=pp, past_len=past_len, n_sel=n_sel), grid_spec=grid_spec,
        out_shape=(jax.ShapeDtypeStruct((bd, 8, LANES), F32), jax.ShapeDtypeStruct((bd, KV_W, n_w), F32)),
        compiler_params=_cp("arbitrary", "arbitrary"), name="nsa_decode",
    )(page_table, q8, ckv, gates, win_state, kvs_new, kvw_new, kvw_new.reshape(bd, KV_W, 1), amat, etile,
      *([pool] * pp))


def _rope_table(pos):
    half = ROT_DIM // 2
    inv = ROPE_THETA ** (-jnp.arange(half, dtype=F32) / half)
    ang = pos.astype(F32)[:, None] * inv[None, :]
    cos, sin = jnp.cos(ang), jnp.sin(ang)
    n = pos.shape[0]
    pad = HEAD_DIM - ROT_DIM
    one_head = lambda a, b, fill: jnp.concatenate([a, b, jnp.full((n, pad), fill, F32)], axis=1)
    zero = jnp.zeros_like(sin)
    tabs = [one_head(cos, cos, 1.0), one_head(-sin, zero, 0.0), one_head(zero, sin, 0.0)]
    return jnp.concatenate([jnp.tile(t, (1, LANES // HEAD_DIM)) for t in tabs], axis=1)


_Q_HEAD_ORDER = (0, 4, 1, 5, 2, 6, 3, 7)


def _layer_weights(l, w_in, b_forget, cmp_pos, w_cmp_k, w_cmp_v, w_up_nsa):
    w = jnp.swapaxes(w_in, 1, 2)[l]
    d = w.shape[1]
    gate0 = NSA_W + 3 * KV_W
    pf0 = gate0 + GATE_W + 6 * SB_W
    merge0 = pf0 + FOX_HEADS
    rows_ = [w[HEAD_DIM * h:HEAD_DIM * (h + 1)] for h in _Q_HEAD_ORDER]
    rows_ += [w[NSA_W:gate0], w[gate0 + GATE_W:pf0], w[gate0:gate0 + GATE_W], w[pf0:merge0]]
    rows_.append(jnp.zeros((N_PROJ - merge0, d), F32))
    w_proj = jnp.concatenate(rows_, axis=0).astype(BF16)
    w_merge = w[merge0:].astype(BF16)
    zpad = lambda n: jnp.zeros((1, n), F32)
    bias = jnp.concatenate([zpad(LOGF_LANE), b_forget[l].reshape(1, FOX_HEADS), zpad(LANES - LOGF_LANE - FOX_HEADS)], axis=1)
    wk = w_cmp_k[l].reshape(CMP_BLOCK, HEAD_DIM, HEAD_DIM)
    wv = w_cmp_v[l].reshape(CMP_BLOCK, HEAD_DIM, HEAD_DIM)
    zero = jnp.zeros_like(wk)
    w_blk = jnp.concatenate([jnp.concatenate([wsrc if c == n else zero for c in range(4)], axis=2)
                             for n, wsrc in enumerate((wk, wk, wv, wv))], axis=1)
    w_cat = jnp.concatenate([w_blk[0:CMP_BLOCK // 2], w_blk[CMP_BLOCK // 2:]], axis=2).astype(BF16)
    pe4 = jnp.tile(cmp_pos[l], (1, KV_W // HEAD_DIM))
    w_a = jnp.concatenate([w_up_nsa[l][HEAD_DIM * h:HEAD_DIM * (h + 1)] for h in _Q_HEAD_ORDER], axis=0).astype(BF16)
    return w_proj, w_merge, bias, w_cat, pe4, w_a


def _pad_heads(a):
    return jnp.concatenate([a, jnp.zeros((a.shape[0], 8 - a.shape[1], a.shape[2]), a.dtype)], axis=1)


def _token_lanes(cache):
    l, p, t = cache.shape[:3]
    return cache.transpose(0, 1, 3, 4, 5, 2).reshape(l * p, -1, t)


def _own_lanes(o8, n_heads):
    n = o8.shape[0]
    o = o8[:, 0:n_heads].reshape(n, n_heads, n_heads, HEAD_DIM)
    return jnp.stack([o[:, h, h] for h in range(n_heads)], axis=1).reshape(n, n_heads * HEAD_DIM)


def kernel(x_prompt, x_sample, cache_nsa_cmp_kv, cache_nsa_sel_kv, cache_sb_kv, cache_fox_kv, cache_fox_logf,
           state_nsa_win_kv, page_table, norm_mix_g, norm_ffn_g, norm_final_g, w_in, b_forget, cmp_pos,
           w_cmp_k, w_cmp_v, w_up_nsa, w_up_sb, w_up_fox, w_out, w_ffn_gate, w_ffn_up, w_ffn_down):
    batch, seq, d = x_prompt.shape
    bd = x_sample.shape[0]
    depth = w_in.shape[0]
    n_pool = cache_sb_kv.shape[1]
    n_pages = page_table.shape[1]
    past_len = n_pages * PAGE
    m = batch * seq
    tm, tq, tq_soft = min(MATMUL_ROWS, seq), min(STICK_TILE, seq), min(SOFTMAX_TILE, seq)
    pp = min(PAGES_PER_STEP, n_pages)
    pp_prompt = min(PAGES_PER_STEP, seq // PAGE)

    rope_p = _rope_table(jnp.arange(seq, dtype=I32))
    rope_s = _rope_table(jnp.full((bd,), past_len, I32))
    ident_pt = jnp.arange(batch * (seq // PAGE), dtype=I32).reshape(batch, seq // PAGE)
    pool_cmp, pool_sel = _token_lanes(cache_nsa_cmp_kv), _token_lanes(cache_nsa_sel_kv)
    pool_sb, pool_fox = _token_lanes(cache_sb_kv), _token_lanes(cache_fox_kv)
    pool_lf = cache_fox_logf.astype(F32).transpose(0, 1, 3, 2).reshape(depth * n_pool, FOX_HEADS, PAGE)
    win_state = _token_lanes(state_nsa_win_kv)

    xp = x_prompt.reshape(m, d)
    xs = x_sample.reshape(bd, d)
    g_fin = norm_final_g.reshape(1, d)
    p_states, s_states = [], []
    for l in range(depth):
        w_proj, w_merge, bias, w_blk, pe4, w_a = _layer_weights(l, w_in, b_forget, cmp_pos, w_cmp_k, w_cmp_v, w_up_nsa)
        w_b, w_c, w_o = w_up_sb[l].astype(BF16), w_up_fox[l].astype(BF16), w_out[l].astype(BF16)
        w_g, w_u, w_d = w_ffn_gate[l].astype(BF16), w_ffn_up[l].astype(BF16), w_ffn_down[l].astype(BF16)
        g_mix, g_ffn = norm_mix_g[l].reshape(1, d), norm_ffn_g[l].reshape(1, d)

        (qn, kv_cmp, kv_sel, kv_win, kv_sb, kv_fox, kb_sel, kb_win, kb_sb, kb_fox, q_sb, q_fox, gates, logf, cum
         ) = _proj(xp, g_mix, w_proj, rope_p, bias, tm=tm, tiles_per_seq=seq // tm, feature_states=True)
        ckv = _compress(kv_cmp, ident_pt, w_blk, pe4, pp=pp_prompt, paged=False)
        o_nsa = _nsa_prompt(qn, ckv, kb_sel, kb_win, gates, batch=batch, seq=seq, tq=tq_soft)
        o_sb = _sb_prompt(q_sb, kb_sb, batch=batch, seq=seq, tq=tq)
        cum4 = cum[:, LOGF_LANE:LOGF_LANE + FOX_HEADS]
        cum_k = cum4.reshape(batch, seq, FOX_HEADS).transpose(0, 2, 1).reshape(
            batch, FOX_HEADS, seq // tq_soft, 1, tq_soft)
        o_fox = _fox_prompt(q_fox, kb_fox, cum, cum_k, batch=batch, seq=seq, tq=tq_soft)
        xp = _merge(xp, g_mix, o_nsa, o_sb, o_fox, w_merge, w_a, w_b, w_c, w_o, tm=tm)
        xp = _ffn(xp, g_ffn, w_g, w_u, w_d, g_fin, tm=tm, final_norm=(l == depth - 1))
        win_keep = min(WINDOW, seq)
        p_states.append((kv_cmp, kv_sel, kv_sb, kv_fox, logf, kv_win[:, :, seq - win_keep:]))

        (qn, kv_cmp, kv_sel, kv_win, kv_sb, kv_fox, _, _, _, _, q_sb, q_fox, gates, logf, _
         ) = _proj(xs, g_mix, w_proj, rope_s, bias, tm=bd, tiles_per_seq=1, feature_states=False)
        pt_l = page_table + l * n_pool
        ckv = _compress(pool_cmp, pt_l, w_blk, pe4, pp=pp, paged=True)
        q8 = qn.transpose(2, 0, 1, 3).reshape(bd, NSA_HEADS, LANES)
        o_nsa8, win_new = _nsa_dec(q8, ckv, gates.transpose(1, 0, 2), win_state, kv_sel.reshape(bd, 1, KV_W),
                                   kv_win.reshape(bd, 1, KV_W), pool_sel, pt_l, pp=pp, layer=l)
        o_sb8 = _sb_dec(_pad_heads(q_sb.transpose(1, 0, 2)), pool_sb, pt_l, pp=pp)
        lf4 = logf[:, LOGF_LANE:LOGF_LANE + FOX_HEADS]
        lf_new = _pad_heads(jnp.broadcast_to(lf4[:, :, None], (bd, FOX_HEADS, LANES)))
        o_fox8 = _fox_dec(_pad_heads(q_fox.transpose(1, 0, 2)), kv_fox[:, 0:SB_W].reshape(bd, 1, SB_W),
                          kv_fox[:, SB_W:2 * SB_W].reshape(bd, 1, SB_W), lf_new, pool_fox, pool_lf, pt_l, pp=pp)
        o_a = o_nsa8.reshape(bd, NSA_GROUPS, NSA_HPG, NSA_GROUPS, HEAD_DIM)
        o_a = jnp.stack([o_a[:, g, :, g] for g in range(NSA_GROUPS)], axis=2).reshape(bd, NSA_W)
        o_a = jnp.stack([o_a, jnp.zeros_like(o_a)], axis=0).astype(BF16)
        o_b = _own_lanes(o_sb8, SB_HEADS).astype(BF16)
        o_c = _own_lanes(o_fox8, FOX_HEADS).astype(BF16)
        xs = _merge(xs, g_mix, o_a, o_b, o_c, w_merge, w_a, w_b, w_c, w_o, tm=bd)
        xs = _ffn(xs, g_ffn, w_g, w_u, w_d, g_fin, tm=bd, final_norm=(l == depth - 1))
        s_states.append((
            kv_cmp.reshape(bd, 1, 2, NSA_GROUPS, HEAD_DIM), kv_sel.reshape(bd, 1, 2, NSA_GROUPS, HEAD_DIM),
            kv_sb.reshape(bd, 1, 2, SB_HEADS, HEAD_DIM), kv_fox.reshape(bd, 1, 2, FOX_HEADS, HEAD_DIM),
            lf4.reshape(bd, 1, FOX_HEADS),
            win_new.reshape(bd, 2, NSA_GROUPS, HEAD_DIM, -1).transpose(0, 4, 1, 2, 3)))

    y_prompt = xp.reshape(batch, seq, d)
    y_sample = xs.reshape(bd, 1, d)
    stk = lambda sts, i: jnp.stack([st[i] for st in sts], axis=0)

    def kv_state(i, heads):
        a = stk(p_states, i)
        return a.reshape(depth, batch, 2, heads, HEAD_DIM, a.shape[-1]).transpose(0, 1, 5, 2, 3, 4)

    p_out = (kv_state(0, NSA_GROUPS), kv_state(1, NSA_GROUPS), kv_state(2, SB_HEADS), kv_state(3, FOX_HEADS),
             stk(p_states, 4).transpose(0, 1, 3, 2), kv_state(5, NSA_GROUPS))
    return (y_prompt, y_sample) + p_out + tuple(stk(s_states, i) for i in range(6))
```

```python
import functools

import numpy as np
import jax
import jax.numpy as jnp
from jax import lax
from jax.experimental import pallas as pl
from jax.experimental.pallas import tpu as pltpu

F32, BF16, I32 = jnp.float32, jnp.bfloat16, jnp.int32

HEAD_DIM = 64
ROT_DIM = HEAD_DIM // 4
ROPE_THETA = 500000.0
NSA_HEADS = 8
NSA_GROUPS = 2
NSA_HPG = NSA_HEADS // NSA_GROUPS
CMP_STRIDE = 16
CMP_BLOCK = 32
SEL_BLOCK = 64
TOP_N = 16
WINDOW = 512
SB_HEADS = 4
FOX_HEADS = 4
PAGE = 128
SEL_FORCE_SCORE = 1.0e4
NEG = -1.0e30
RMS_EPS = 1e-6
SCALE = HEAD_DIM ** -0.5
LOG2E = 1.4426950408889634

LANES = 128
NSA_W = NSA_HEADS * HEAD_DIM
KV_W = 2 * NSA_GROUPS * HEAD_DIM
SB_W = SB_HEADS * HEAD_DIM
GATE_W = 3 * NSA_HEADS
C_Q, C_CMP, C_SEL, C_WIN, C_SB, C_FOX, C_MISC = 0, 512, 768, 1024, 1280, 2048, 2816
N_PROJ = C_MISC + LANES
LOGF_LANE = GATE_W
V7X_VMEM_BYTES = 64 * 1024 * 1024
VMEM_LIMIT = V7X_VMEM_BYTES * 7 // 8
PAGES_PER_STEP = 32
MATMUL_ROWS = 512
SOFTMAX_TILE = 512
STICK_TILE = 256


def _cp(*sem):
    return pltpu.CompilerParams(dimension_semantics=sem, vmem_limit_bytes=VMEM_LIMIT)


def _dot(a, b):
    return jnp.dot(a, b, preferred_element_type=F32)


def _dot_nt(a, b):
    return lax.dot_general(a, b, (((1,), (1,)), ((), ())), preferred_element_type=F32)


def _split_dot(a, r, parts):
    out, rem = None, a
    for _ in range(parts):
        hi = rem.astype(BF16)
        d = _dot(hi, r)
        out = d if out is None else out + d
        rem = rem - hi.astype(F32)
    return out


def _split_dot_l(l, a, parts):
    out, rem = None, a
    for _ in range(parts):
        hi = rem.astype(BF16)
        d = _dot(l, hi)
        out = d if out is None else out + d
        rem = rem - hi.astype(F32)
    return out


def _rms(x, g):
    return x * lax.rsqrt(jnp.mean(x * x, axis=-1, keepdims=True) + RMS_EPS) * g


def _softplus2(z2):
    return jnp.maximum(z2, 0.0) + jnp.log2(1.0 + jnp.exp2(-jnp.abs(z2)))


def _iota(shape, axis):
    return lax.broadcasted_iota(I32, shape, axis)


def _proj_kernel(x_ref, g_ref, w_ref, rope_ref, b_ref, tri_ref,
                 qn_ref, kvc_ref, kvs_ref, kvw_ref, kvsb_ref, kvfx_ref,
                 kbs_ref, kbw_ref, kbsb_ref, kbfx_ref, qsb_ref, qfx_ref,
                 gate_ref, logf_ref, cum_ref, h_scr, carry_scr, *, tiles_per_seq, feature_states):
    i = pl.program_id(0)
    tm = x_ref.shape[0]
    h_scr[...] = _rms(x_ref[...], g_ref[...]).astype(BF16)
    hb = h_scr[...]
    cos, sin_lo, sin_hi = rope_ref[:, 0:128], rope_ref[:, 128:256], rope_ref[:, 256:384]

    def rope(seg):
        return seg * cos + pltpu.roll(seg, LANES - 8, 1) * sin_lo + pltpu.roll(seg, 8, 1) * sin_hi

    def put_state(f_ref, blocks):
        for n, blk in enumerate(blocks):
            if feature_states:
                f_ref[0, LANES * n:LANES * (n + 1), :] = blk.T
            else:
                f_ref[:, LANES * n:LANES * (n + 1)] = blk

    lane = _iota((tm, LANES), 1)
    pq = _dot_nt(hb, w_ref[C_Q:C_Q + NSA_W, :])
    for s in range(NSA_HPG):
        seg = rope(pq[:, LANES * s:LANES * (s + 1)]) * (SCALE * LOG2E)
        for g in range(NSA_GROUPS):
            qn_ref[g, s] = jnp.where(lane // HEAD_DIM == g, seg, 0.0).astype(BF16)

    for c0, f_ref, h_ref in ((C_CMP, kvc_ref, None), (C_SEL, kvs_ref, kbs_ref), (C_WIN, kvw_ref, kbw_ref)):
        p = _dot_nt(hb, w_ref[c0:c0 + KV_W, :])
        k, v = rope(p[:, 0:128]), p[:, 128:256]
        put_state(f_ref, [k, v])
        if h_ref is not None:
            h_ref[:, 0:128] = k.astype(BF16)
            h_ref[:, 128:256] = v.astype(BF16)

    lane_w = _iota((tm, SB_W), 1)
    for c0, q_ref, f_ref, h_ref in ((C_SB, qsb_ref, kvsb_ref, kbsb_ref), (C_FOX, qfx_ref, kvfx_ref, kbfx_ref)):
        p = _dot_nt(hb, w_ref[c0:c0 + 3 * SB_W, :])
        q = p[:, 0:SB_W] * (SCALE * LOG2E)
        for h in range(SB_HEADS):
            q_ref[h] = jnp.where(lane_w // HEAD_DIM == h, q, 0.0).astype(BF16)
        kv = p[:, SB_W:3 * SB_W]
        put_state(f_ref, [kv[:, LANES * n:LANES * (n + 1)] for n in range(2 * SB_W // LANES)])
        h_ref[...] = kv.astype(BF16)

    pm = _dot_nt(hb, w_ref[C_MISC:C_MISC + LANES, :])
    sg = jax.nn.sigmoid(pm)
    gate_ref[0] = sg
    gate_ref[1] = pltpu.roll(sg, LANES - GATE_W // 2, 1)
    zf = pm + b_ref[...]
    lf = jnp.minimum(zf, 0.0) - jnp.log1p(jnp.exp(-jnp.abs(zf)))
    if feature_states:
        logf_ref[0] = lf.T[LOGF_LANE:LOGF_LANE + FOX_HEADS, :]
    else:
        logf_ref[...] = lf

    @pl.when(i % tiles_per_seq == 0)
    def _():
        carry_scr[...] = jnp.zeros_like(carry_scr)

    c = _split_dot_l(tri_ref[...], lf, 3) + carry_scr[0:1, :]
    cum_ref[...] = c
    carry_scr[...] = jnp.broadcast_to(c[tm - 1:tm, :], carry_scr.shape)


def _proj(x, gain, w, rope_tab, bias, *, tm, tiles_per_seq, feature_states):
    m, d = x.shape
    tri = jnp.tril(jnp.ones((tm, tm), F32)).astype(BF16)
    row = lambda wid: pl.BlockSpec((tm, wid), lambda i: (i, 0))
    const = lambda shape: pl.BlockSpec(shape, lambda i: (0,) * len(shape))
    f32o = lambda wid: jax.ShapeDtypeStruct((m, wid), F32)
    b16o = lambda wid: jax.ShapeDtypeStruct((m, wid), BF16)
    if feature_states:
        nb, seq = m // (tm * tiles_per_seq), tm * tiles_per_seq
        st_shape = lambda wid: jax.ShapeDtypeStruct((nb, wid, seq), F32)
        st_spec = lambda wid: pl.BlockSpec((1, wid, tm), lambda i: (i // tiles_per_seq, 0, i % tiles_per_seq))
        lf_shape, lf_spec = st_shape(FOX_HEADS), st_spec(FOX_HEADS)
    else:
        st_shape, st_spec, lf_shape, lf_spec = f32o, row, f32o(LANES), row(LANES)
    out_shape = (
        jax.ShapeDtypeStruct((NSA_GROUPS, NSA_HPG, m, LANES), BF16),
        st_shape(KV_W), st_shape(KV_W), st_shape(KV_W), st_shape(2 * SB_W), st_shape(2 * SB_W),
        b16o(KV_W), b16o(KV_W), b16o(2 * SB_W), b16o(2 * SB_W),
        jax.ShapeDtypeStruct((SB_HEADS, m, SB_W), BF16), jax.ShapeDtypeStruct((SB_HEADS, m, SB_W), BF16),
        jax.ShapeDtypeStruct((NSA_GROUPS, m, LANES), F32), lf_shape, f32o(LANES),
    )
    out_specs = (
        pl.BlockSpec((NSA_GROUPS, NSA_HPG, tm, LANES), lambda i: (0, 0, i, 0)),
        st_spec(KV_W), st_spec(KV_W), st_spec(KV_W), st_spec(2 * SB_W), st_spec(2 * SB_W),
        row(KV_W), row(KV_W), row(2 * SB_W), row(2 * SB_W),
        pl.BlockSpec((SB_HEADS, tm, SB_W), lambda i: (0, i, 0)),
        pl.BlockSpec((SB_HEADS, tm, SB_W), lambda i: (0, i, 0)),
        pl.BlockSpec((NSA_GROUPS, tm, LANES), lambda i: (0, i, 0)),
        lf_spec, row(LANES),
    )
    return pl.pallas_call(
        functools.partial(_proj_kernel, tiles_per_seq=tiles_per_seq, feature_states=feature_states),
        grid=(m // tm,),
        in_specs=[row(d), const((1, d)), const((N_PROJ, d)),
                  pl.BlockSpec((tm, 3 * LANES), lambda i: (i % tiles_per_seq, 0)),
                  const((1, LANES)), const((tm, tm))],
        out_specs=out_specs, out_shape=out_shape,
        scratch_shapes=[pltpu.VMEM((tm, d), BF16), pltpu.VMEM((8, LANES), F32)],
        compiler_params=_cp("arbitrary"), name="proj",
    )(x, gain, w, rope_tab, bias, tri)


def _compress_kernel(pt_ref, *refs, pp, nch):
    pages, (perm_ref, w_ref, pe_ref, out_ref, x_scr, acc_scr, pos_scr) = refs[:pp], refs[pp:]
    j = pl.program_id(1)
    half = CMP_BLOCK // 2
    chunks = PAGE // CMP_STRIDE

    @pl.when((pl.program_id(0) == 0) & (j == 0))
    def _():
        c = jnp.zeros((8, KV_W), F32)
        for l in range(half):
            for off, cols in ((0, slice(0, KV_W)), (half, slice(KV_W, 2 * KV_W))):
                row = jnp.broadcast_to(pe_ref[off + l:off + l + 1, :], (8, KV_W))
                c = c + _split_dot(row, w_ref[l, :, cols], 2)
        pos_scr[...] = c

    for i in range(pp):
        gt = _dot_nt(perm_ref[...], pages[i][0].astype(BF16))
        base = pl.multiple_of((j * pp + i) * chunks, chunks)
        for l in range(half):
            x_scr[l, pl.ds(base, chunks), :] = gt[chunks * l:chunks * (l + 1), :]

    @pl.when(j == pl.num_programs(1) - 1)
    def _():
        acc_scr[...] = jnp.zeros_like(acc_scr)
        for l in range(half):
            acc_scr[...] += _dot(x_scr[l].astype(BF16), w_ref[l])
        nxt = pltpu.roll(acc_scr[:, KV_W:2 * KV_W], nch - 1, 0)
        nxt = jnp.where(_iota((nch, KV_W), 0) < nch - 1, nxt, 0.0)
        out_ref[0] = (acc_scr[:, 0:KV_W] + nxt + pos_scr[0:1, :]).astype(BF16)


def _compress(pool, page_table, w_cat, pe4, *, pp, paged):
    b, n_pages = page_table.shape
    chunks = PAGE // CMP_STRIDE
    nch = n_pages * chunks
    half = CMP_BLOCK // 2
    tok = np.arange(PAGE)
    perm = jnp.asarray(np.arange(PAGE)[:, None] == ((tok % CMP_STRIDE) * chunks + tok // CMP_STRIDE)[None, :], BF16)
    if paged:
        page_spec = lambda i: pl.BlockSpec((1, KV_W, PAGE), lambda bb, j, pt: (pt[bb, j * pp + i], 0, 0))
    else:
        page_spec = lambda i: pl.BlockSpec((1, KV_W, PAGE), lambda bb, j, pt: (bb, 0, j * pp + i))
    grid_spec = pltpu.PrefetchScalarGridSpec(
        num_scalar_prefetch=1, grid=(b, n_pages // pp),
        in_specs=[page_spec(i) for i in range(pp)] + [
            pl.BlockSpec((PAGE, PAGE), lambda bb, j, pt: (0, 0)),
            pl.BlockSpec((half, KV_W, 2 * KV_W), lambda bb, j, pt: (0, 0, 0)),
            pl.BlockSpec((CMP_BLOCK, KV_W), lambda bb, j, pt: (0, 0))],
        out_specs=pl.BlockSpec((1, nch, KV_W), lambda bb, j, pt: (bb, 0, 0)),
        scratch_shapes=[pltpu.VMEM((half, nch, KV_W), F32), pltpu.VMEM((nch, 2 * KV_W), F32),
                        pltpu.VMEM((8, KV_W), F32)])
    return pl.pallas_call(
        functools.partial(_compress_kernel, pp=pp, nch=nch), grid_spec=grid_spec,
        out_shape=jax.ShapeDtypeStruct((b, nch, KV_W), BF16),
        compiler_params=_cp("arbitrary", "arbitrary"), name="compress",
    )(page_table, *([pool] * pp), perm, w_cat, pe4)


def _masked_softmax_rows(s, mask):
    s = jnp.where(mask, s, NEG)
    e = jnp.where(mask, jnp.exp2(s - jnp.max(s, axis=1, keepdims=True)), 0.0)
    return e / jnp.maximum(jnp.sum(e, axis=1, keepdims=True), 1e-30)


def _online_init(m_scr, l_scr, acc_scr):
    m_scr[...] = jnp.full_like(m_scr, NEG)
    l_scr[...] = jnp.zeros_like(l_scr)
    acc_scr[...] = jnp.zeros_like(acc_scr)


def _select_blocks(imp, pos, n_sel):
    blk = _iota(imp.shape, 1)
    cur = pos // SEL_BLOCK
    valid = blk * SEL_BLOCK <= pos
    forced = jnp.where(blk == 0, 1.0, 0.0) + jnp.where(blk == cur, 1.0, 0.0) + jnp.where(blk == cur - 1, 1.0, 0.0)
    score = jnp.where(valid, jnp.where(forced > 0.5, SEL_FORCE_SCORE, imp), -1.0)
    score = jnp.where(blk < n_sel, score, -2.0)
    rank = jnp.zeros(imp.shape, F32)
    for b2 in range(n_sel):
        col = score[:, b2:b2 + 1]
        ge = jnp.where(col >= score, 1.0, 0.0)
        gt = jnp.where(col > score, 1.0, 0.0)
        rank = rank + jnp.where(blk > b2, ge, gt)
    return jnp.where(rank < min(TOP_N, n_sel), 1.0, 0.0)


def _rep(x, n):
    return x if n == 1 else jnp.concatenate([x] * n, axis=1)


def _select_blocks_t(p_grp, amat_t, q0, n_sel):
    t = p_grp.shape[0]
    nb = amat_t.shape[0]
    hi = p_grp.astype(BF16)
    lo = (p_grp - hi.astype(F32)).astype(BF16)
    imp = _dot_nt(amat_t, hi) + _dot_nt(amat_t, lo)
    pos = q0 + _iota((1, t), 1)
    blk = _iota((nb, t), 0)
    cur = pos // SEL_BLOCK
    forced = jnp.where(blk == 0, 1.0, 0.0) + jnp.where(blk == cur, 1.0, 0.0) + jnp.where(blk == cur - 1, 1.0, 0.0)
    score = jnp.where(blk * SEL_BLOCK <= pos, jnp.where(forced > 0.5, SEL_FORCE_SCORE, imp), -1.0)
    score = jnp.where(blk < n_sel, score, -2.0)
    groups = [score[8 * k:8 * (k + 1)] for k in range(nb // 8)]
    ranks = [jnp.zeros((8, t), F32) for _ in groups]
    sub = _iota((8, t), 0)
    for b2 in range(n_sel):
        rowv = score[b2:b2 + 1, :]
        k2, r2 = divmod(b2, 8)
        for k, grp in enumerate(groups):
            if k > k2:
                inc = jnp.where(rowv >= grp, 1.0, 0.0)
            elif k < k2:
                inc = jnp.where(rowv > grp, 1.0, 0.0)
            else:
                inc = jnp.where(sub > r2, jnp.where(rowv >= grp, 1.0, 0.0), jnp.where(rowv > grp, 1.0, 0.0))
            ranks[k] = ranks[k] + inc
    chosen = [jnp.where(rk < min(TOP_N, n_sel), 1.0, 0.0) for rk in ranks]
    pad = [jnp.zeros((LANES - nb, t), F32)] if nb < LANES else []
    return jnp.concatenate(chosen + pad, axis=0)


def _nsa_prompt_kernel(qn_ref, ckv_ref, ks_ref, kw_ref, gate_ref, amat_ref, e_ref, out_ref,
                       m_scr, l_scr, acc_scr, oc_scr, os_scr, *, n_sel):
    g, i = pl.program_id(1), pl.program_id(2)
    t = qn_ref.shape[2]
    q0 = i * t
    rowi, coli = _iota((t, t), 0), _iota((t, t), 1)
    reps = t // LANES

    nc = ckv_ref.shape[1]
    cmask = _iota((t, nc), 1) * CMP_STRIDE + (CMP_BLOCK - 1) <= q0 + _iota((t, 1), 0)
    ck, cv = ckv_ref[0, :, 0:128], ckv_ref[0, :, 128:256]
    p_grp = None
    for h in range(NSA_HPG):
        p = _masked_softmax_rows(_dot_nt(qn_ref[0, h], ck), cmask)
        oc_scr[h] = _dot(p.astype(BF16), cv)
        p_grp = p if p_grp is None else p_grp + p
    sel = _select_blocks_t(p_grp, amat_ref[...], q0, n_sel).T.astype(BF16)

    def online_head(h, s, v):
        m_prev = m_scr[h]
        m_next = jnp.maximum(m_prev, jnp.max(s, axis=1, keepdims=True))
        alpha = jnp.exp2(m_prev - m_next)
        e = jnp.exp2(s - _rep(m_next, reps))
        l_scr[h] = alpha * l_scr[h] + jnp.sum(e, axis=1, keepdims=True)
        m_scr[h] = m_next
        acc_scr[h] = acc_scr[h] * alpha + _dot(e.astype(BF16), v)

    def reset():
        m_scr[...] = jnp.full_like(m_scr, NEG)
        l_scr[...] = jnp.zeros_like(l_scr)
        acc_scr[...] = jnp.zeros_like(acc_scr)

    def sel_tile(j, diag):
        kt = ks_ref[0, pl.ds(pl.multiple_of(j * t, t), t), :]
        chosen = _dot(sel, e_ref[j])
        if diag:
            chosen = jnp.where(coli <= rowi, chosen, 0.0)
        keep = chosen > 0.5
        for h in range(NSA_HPG):
            online_head(h, jnp.where(keep, _dot_nt(qn_ref[0, h], kt[:, 0:128]), NEG), kt[:, 128:256])

    reset()
    def two_sel_tiles(n, c):
        sel_tile(2 * n, False)
        sel_tile(2 * n + 1, False)
        return c

    lax.fori_loop(0, i // 2, two_sel_tiles, 0)

    @pl.when(i % 2 == 1)
    def _():
        sel_tile(i - 1, False)

    sel_tile(i, True)
    for h in range(NSA_HPG):
        os_scr[h] = acc_scr[h] / jnp.maximum(l_scr[h], 1e-30)

    tw = min(t, 2 * LANES)
    w_len = WINDOW + tw
    gates = gate_ref[0]
    lane = _iota((tw, LANES), 1)
    for rb in range(t // tw):
        rows = slice(rb * tw, (rb + 1) * tw)
        r0 = q0 + rb * tw
        k0 = jnp.maximum(r0 - WINDOW, 0)
        kw = kw_ref[0, pl.ds(pl.multiple_of(k0, tw), w_len), :]
        d = (r0 - k0) + _iota((tw, w_len), 0) - _iota((tw, w_len), 1)
        keep_w = jnp.where(d >= 0, jnp.where(d <= WINDOW, 1.0, 0.0), 0.0) > 0.5
        for h in range(NSA_HPG):
            s = jnp.where(keep_w, _dot_nt(qn_ref[0, h, rows, :], kw[:, 0:128]), NEG)
            e = jnp.exp2(s - jnp.max(s, axis=1, keepdims=True))
            o_w = _dot(e.astype(BF16), kw[:, 128:256]) / jnp.sum(e, axis=1, keepdims=True)
            o = (gates[rows, 3 * h:3 * h + 1] * oc_scr[h, rows, :] + gates[rows, 3 * h + 1:3 * h + 2] * os_scr[h, rows, :]
                 + gates[rows, 3 * h + 2:3 * h + 3] * o_w)
            out_ref[0, rows, LANES * h:LANES * (h + 1)] = jnp.where(lane // HEAD_DIM == g, o, 0.0).astype(BF16)


def _imp_matrix(nc, width):
    n = np.arange(nc)[:, None]
    b = np.arange(width)[None, :]
    sub = SEL_BLOCK // CMP_STRIDE
    return jnp.asarray((n // sub == b).astype(np.float32) + ((n + 1) // sub == b), BF16)


def _expand_tiles(n_tiles, n_blk, tk):
    key = np.arange(n_tiles * tk).reshape(n_tiles, 1, tk)
    return jnp.asarray(key // SEL_BLOCK == np.arange(n_blk)[None, :, None], BF16)


def _nsa_prompt(qn, ckv, kb_sel, kb_win, gates, *, batch, seq, tq):
    nq = seq // tq
    nc = ckv.shape[1]
    n_sel = seq // SEL_BLOCK
    assert n_sel <= LANES and WINDOW % tq == 0 and seq >= WINDOW + tq
    amat = _imp_matrix(nc, -(-n_sel // 8) * 8).T
    etile = _expand_tiles(nq, LANES, tq)
    m = batch * seq
    return pl.pallas_call(
        functools.partial(_nsa_prompt_kernel, n_sel=n_sel),
        grid=(batch, NSA_GROUPS, nq),
        in_specs=[
            pl.BlockSpec((1, NSA_HPG, tq, LANES), lambda b, g, i: (g, 0, b * nq + i, 0)),
            pl.BlockSpec((1, nc, KV_W), lambda b, g, i: (b, 0, 0)),
            pl.BlockSpec((1, seq, KV_W), lambda b, g, i: (b, 0, 0)),
            pl.BlockSpec((1, seq, KV_W), lambda b, g, i: (b, 0, 0)),
            pl.BlockSpec((1, tq, LANES), lambda b, g, i: (g, b * nq + i, 0)),
            pl.BlockSpec(amat.shape, lambda b, g, i: (0, 0)),
            pl.BlockSpec(etile.shape, lambda b, g, i: (0, 0, 0)),
        ],
        out_specs=pl.BlockSpec((1, tq, NSA_W), lambda b, g, i: (g, b * nq + i, 0)),
        out_shape=jax.ShapeDtypeStruct((NSA_GROUPS, m, NSA_W), BF16),
        scratch_shapes=[pltpu.VMEM((NSA_HPG, tq, LANES), F32)] * 5,
        compiler_params=_cp("arbitrary", "arbitrary", "arbitrary"), name="nsa_prompt",
    )(qn, ckv, kb_sel.reshape(batch, seq, KV_W), kb_win.reshape(batch, seq, KV_W), gates, amat, etile)


def _upper_strict(n):
    return jnp.asarray(np.triu(np.ones((n, n), np.float32), 1).T, BF16)


def _stack_heads(v):
    lane = _iota(v.shape, 1)
    return jnp.concatenate([jnp.where(lane // HEAD_DIM == h, v, jnp.zeros_like(v)) for h in range(SB_HEADS)], axis=0)


def _head_lanes(parts):
    lane = _iota(parts[0].shape, 1)
    return jnp.concatenate([jnp.where(lane < HEAD_DIM, parts[0], parts[1]),
                            jnp.where(lane < HEAD_DIM, parts[2], parts[3])], axis=1)


def _sb_prompt_kernel(q_ref, kv_ref, u_ref, out_ref, acc_scr, r_scr):
    i = pl.program_id(1)
    t = q_ref.shape[1]
    rowi, coli = _iota((t, t), 0), _iota((t, t), 1)
    acc_scr[...] = jnp.zeros_like(acc_scr)
    r_scr[...] = jnp.zeros_like(r_scr)

    def tile(j, diag):
        kt = kv_ref[0, pl.ds(pl.multiple_of(j * t, t), t), :]
        ws = []
        for h in range(SB_HEADS):
            z = _dot_nt(q_ref[h], kt[:, 0:SB_W])
            sp = _softplus2(z)
            cost = jnp.where(coli < rowi, sp, 0.0) if diag else sp
            r = r_scr[h]
            w = jnp.exp2(z - sp - _split_dot(cost, u_ref[...], 1) - _rep(r, t // LANES))
            if diag:
                w = jnp.where(coli < rowi, w, 0.0)
            ws.append(w.astype(BF16))
            r_scr[h] = r + jnp.sum(cost, axis=1, keepdims=True)
        acc_scr[...] += _dot(jnp.concatenate(ws, axis=1), _stack_heads(kt[:, SB_W:2 * SB_W]))

    tile(i, True)

    def eight_tiles(n, c):
        for u in range(8):
            tile(i - 1 - 8 * n - u, False)
        return c

    lax.fori_loop(0, i // 8, eight_tiles, 0)
    rem = i % 8

    @pl.when(rem >= 4)
    def _():
        for u in range(4):
            tile(rem - 1 - u, False)

    rem4 = rem % 4

    @pl.when(rem4 >= 2)
    def _():
        tile(rem4 - 1, False)
        tile(rem4 - 2, False)

    @pl.when(rem4 % 2 == 1)
    def _():
        tile(0, False)

    out_ref[...] = acc_scr[...].astype(BF16)


def _sb_prompt(q, kb, *, batch, seq, tq):
    nq = seq // tq
    u = _upper_strict(tq)
    return pl.pallas_call(
        _sb_prompt_kernel, grid=(batch, nq),
        in_specs=[pl.BlockSpec((SB_HEADS, tq, SB_W), lambda b, i: (0, b * nq + i, 0)),
                  pl.BlockSpec((1, seq, 2 * SB_W), lambda b, i: (b, 0, 0)),
                  pl.BlockSpec(u.shape, lambda b, i: (0, 0))],
        out_specs=pl.BlockSpec((tq, SB_W), lambda b, i: (b * nq + i, 0)),
        out_shape=jax.ShapeDtypeStruct((batch * seq, SB_W), BF16),
        scratch_shapes=[pltpu.VMEM((tq, SB_W), F32), pltpu.VMEM((SB_HEADS, tq, LANES), F32)],
        compiler_params=_cp("arbitrary", "arbitrary"), name="sb_prompt",
    )(q, kb.reshape(batch, seq, 2 * SB_W), u)


def _fox_prompt_kernel(q_ref, kv_ref, cq_ref, ck_ref, out_ref, m_scr, l_scr, acc_scr, cq_scr):
    i = pl.program_id(1)
    t = q_ref.shape[1]
    rowi, coli = _iota((t, t), 0), _iota((t, t), 1)
    m_scr[...] = jnp.full_like(m_scr, NEG)
    l_scr[...] = jnp.zeros_like(l_scr)
    acc_scr[...] = jnp.zeros_like(acc_scr)
    for h in range(FOX_HEADS):
        cq_scr[h] = jnp.broadcast_to(cq_ref[:, LOGF_LANE + h:LOGF_LANE + h + 1] * LOG2E, (t, LANES))

    def tile(j, diag):
        kt = kv_ref[0, pl.ds(pl.multiple_of(j * t, t), t), :]
        es, alphas = [], []
        for h in range(FOX_HEADS):
            s = _dot_nt(q_ref[h], kt[:, 0:SB_W]) + (_rep(cq_scr[h], t // LANES) - ck_ref[0, h, j] * LOG2E)
            if diag:
                s = jnp.where(coli <= rowi, s, NEG)
            m_prev = m_scr[h]
            m_next = jnp.maximum(m_prev, jnp.max(s, axis=1, keepdims=True))
            alpha = jnp.exp2(m_prev - m_next)
            e = jnp.exp2(s - _rep(m_next, t // LANES))
            l_scr[h] = alpha * l_scr[h] + jnp.sum(e, axis=1, keepdims=True)
            m_scr[h] = m_next
            es.append(e.astype(BF16))
            alphas.append(alpha)
        acc_scr[...] = acc_scr[...] * _head_lanes(alphas) + _dot(jnp.concatenate(es, axis=1),
                                                                 _stack_heads(kt[:, SB_W:2 * SB_W]))

    def two_tiles(n, c):
        tile(2 * n, False)
        tile(2 * n + 1, False)
        return c

    lax.fori_loop(0, i // 2, two_tiles, 0)

    @pl.when(i % 2 == 1)
    def _():
        tile(i - 1, False)

    tile(i, True)
    out_ref[...] = (acc_scr[...] / _head_lanes([jnp.maximum(l_scr[h], 1e-30) for h in range(FOX_HEADS)])).astype(BF16)


def _fox_prompt(q, kb, cum, cum_k, *, batch, seq, tq):
    nq = seq // tq
    nk = seq // tq
    return pl.pallas_call(
        _fox_prompt_kernel, grid=(batch, nq),
        in_specs=[pl.BlockSpec((FOX_HEADS, tq, SB_W), lambda b, i: (0, b * nq + i, 0)),
                  pl.BlockSpec((1, seq, 2 * SB_W), lambda b, i: (b, 0, 0)),
                  pl.BlockSpec((tq, LANES), lambda b, i: (b * nq + i, 0)),
                  pl.BlockSpec((1, FOX_HEADS, nk, 1, tq), lambda b, i: (b, 0, 0, 0, 0))],
        out_specs=pl.BlockSpec((tq, SB_W), lambda b, i: (b * nq + i, 0)),
        out_shape=jax.ShapeDtypeStruct((batch * seq, SB_W), BF16),
        scratch_shapes=[pltpu.VMEM((FOX_HEADS, tq, LANES), F32), pltpu.VMEM((FOX_HEADS, tq, LANES), F32),
                        pltpu.VMEM((tq, SB_W), F32), pltpu.VMEM((FOX_HEADS, tq, LANES), F32)],
        compiler_params=_cp("arbitrary", "arbitrary"), name="fox_prompt",
    )(q, kb.reshape(batch, seq, 2 * SB_W), cum, cum_k)


def _merge_kernel(x_ref, g_ref, oa_ref, ob_ref, oc_ref, wm_ref, wa_ref, wb_ref, wc_ref, wo_ref, out_ref):
    x = x_ref[...]
    d = x.shape[1]
    hb = _rms(x, g_ref[...]).astype(BF16)
    oa = oa_ref[0] + oa_ref[1]
    u = jax.nn.sigmoid(_dot_nt(hb, wm_ref[0:d, :])) * _dot(oa, wa_ref[...])
    u = u + jax.nn.sigmoid(_dot_nt(hb, wm_ref[d:2 * d, :])) * _dot(ob_ref[...], wb_ref[...])
    u = u + jax.nn.sigmoid(_dot_nt(hb, wm_ref[2 * d:3 * d, :])) * _dot(oc_ref[...], wc_ref[...])
    out_ref[...] = x + _dot(u.astype(BF16), wo_ref[...])


def _merge(x, gain, o_nsa, o_sb, o_fox, w_merge, w_a, w_b, w_c, w_o, *, tm):
    m, d = x.shape
    row = lambda wid: pl.BlockSpec((tm, wid), lambda i: (i, 0))
    const = lambda a: pl.BlockSpec(a.shape, lambda i: (0,) * a.ndim)
    return pl.pallas_call(
        _merge_kernel, grid=(m // tm,),
        in_specs=[row(d), const(gain), pl.BlockSpec((NSA_GROUPS, tm, NSA_W), lambda i: (0, i, 0)),
                  row(SB_W), row(SB_W), const(w_merge), const(w_a), const(w_b), const(w_c), const(w_o)],
        out_specs=row(d), out_shape=jax.ShapeDtypeStruct((m, d), F32),
        compiler_params=_cp("arbitrary"), name="merge",
    )(x, gain, o_nsa, o_sb, o_fox, w_merge, w_a, w_b, w_c, w_o)


def _ffn_kernel(x_ref, g_ref, wg_ref, wu_ref, wd_ref, gf_ref, out_ref, *, final_norm):
    x = x_ref[...]
    hb = _rms(x, g_ref[...]).astype(BF16)
    a = _dot(hb, wg_ref[...])
    act = a * jax.nn.sigmoid(a) * _dot(hb, wu_ref[...])
    y = x + _dot(act.astype(BF16), wd_ref[...])
    out_ref[...] = _rms(y, gf_ref[...]) if final_norm else y


def _ffn(x, gain, w_gate, w_up, w_down, final_gain, *, tm, final_norm):
    m, d = x.shape
    row = pl.BlockSpec((tm, d), lambda i: (i, 0))
    vec = pl.BlockSpec((1, d), lambda i: (0, 0))
    const = lambda a: pl.BlockSpec(a.shape, lambda i: (0,) * a.ndim, pipeline_mode=pl.Buffered(1))
    return pl.pallas_call(
        functools.partial(_ffn_kernel, final_norm=final_norm), grid=(m // tm,),
        in_specs=[row, vec, const(w_gate), const(w_up), const(w_down), vec],
        out_specs=row, out_shape=jax.ShapeDtypeStruct((m, d), F32),
        compiler_params=_cp("arbitrary"), name="ffn",
    )(x, gain, w_gate, w_up, w_down, final_gain)


def _page_specs(rows, n_pages, pp, descending):
    def spec(i):
        if descending:
            return pl.BlockSpec((1, rows, PAGE), lambda b, j, pt: (pt[b, n_pages - 1 - (j * pp + i)], 0, 0))
        return pl.BlockSpec((1, rows, PAGE), lambda b, j, pt: (pt[b, j * pp + i], 0, 0))
    return [spec(i) for i in range(pp)]


def _lanes(parts):
    return jnp.concatenate(parts, axis=1)


def _sb_dec_kernel(pt_ref, q_ref, *refs, pp):
    pages, (u_ref, out_ref, acc_scr, r_scr) = refs[:pp], refs[pp:]
    j = pl.program_id(1)

    @pl.when(j == 0)
    def _():
        acc_scr[...] = jnp.zeros_like(acc_scr)
        r_scr[...] = jnp.zeros_like(r_scr)

    q = q_ref[0]
    z_all = _dot(q, _lanes([p[0, 0:SB_W, :].astype(BF16) for p in pages]))
    zs = [z_all[:, i * PAGE:(i + 1) * PAGE] for i in range(pp)]
    sps = [_softplus2(z) for z in zs]
    afters = [_split_dot(sp, u_ref[...], 2) for sp in sps]
    r = r_scr[...]
    ws = []
    for i in range(pp):
        ws.append(jnp.exp2(zs[i] - sps[i] - afters[i] - r).astype(BF16))
        r = r + jnp.sum(sps[i], axis=1, keepdims=True)
    acc_scr[...] += _dot_nt(_lanes(ws), _lanes([p[0, SB_W:2 * SB_W, :].astype(BF16) for p in pages]))
    r_scr[...] = r

    @pl.when(j == pl.num_programs(1) - 1)
    def _():
        out_ref[0] = acc_scr[...]


def _sb_dec(q8, pool, page_table, *, pp):
    bd, n_pages = page_table.shape
    u = _upper_strict(PAGE)
    grid_spec = pltpu.PrefetchScalarGridSpec(
        num_scalar_prefetch=1, grid=(bd, n_pages // pp),
        in_specs=[pl.BlockSpec((1, 8, SB_W), lambda b, j, pt: (b, 0, 0))]
        + _page_specs(2 * SB_W, n_pages, pp, True) + [pl.BlockSpec(u.shape, lambda b, j, pt: (0, 0))],
        out_specs=pl.BlockSpec((1, 8, SB_W), lambda b, j, pt: (b, 0, 0)),
        scratch_shapes=[pltpu.VMEM((8, SB_W), F32), pltpu.VMEM((8, LANES), F32)])
    return pl.pallas_call(
        functools.partial(_sb_dec_kernel, pp=pp), grid_spec=grid_spec,
        out_shape=jax.ShapeDtypeStruct((bd, 8, SB_W), F32),
        compiler_params=_cp("arbitrary", "arbitrary"), name="sb_decode",
    )(page_table, q8, *([pool] * pp), u)


def _fox_dec_kernel(pt_ref, q_ref, kn_ref, vn_ref, lfn_ref, *refs, pp):
    pages, lf_pages = refs[:pp], refs[pp:2 * pp]
    u_ref, out_ref, m_scr, l_scr, acc_scr, r_scr = refs[2 * pp:]
    j = pl.program_id(1)
    q = q_ref[0]
    reps = SB_W // LANES

    @pl.when(j == 0)
    def _():
        kn = kn_ref[0].astype(BF16).astype(F32)
        m_scr[...] = jnp.broadcast_to(jnp.sum(q.astype(F32) * kn, axis=1, keepdims=True), m_scr.shape)
        l_scr[...] = jnp.ones_like(l_scr)
        acc_scr[...] = jnp.broadcast_to(vn_ref[0].astype(BF16).astype(F32), acc_scr.shape)
        r_scr[...] = lfn_ref[0]

    z_all = _dot(q, _lanes([p[0, 0:SB_W, :].astype(BF16) for p in pages]))
    pad = jnp.zeros((8 - FOX_HEADS, PAGE), F32)
    r = r_scr[...]
    ss = []
    for i in range(pp):
        lf = jnp.concatenate([lf_pages[i][0], pad], axis=0)
        ss.append(z_all[:, i * PAGE:(i + 1) * PAGE] + LOG2E * (_split_dot(lf, u_ref[...], 3) + r))
        r = r + jnp.sum(lf, axis=1, keepdims=True)
    r_scr[...] = r
    m_prev = m_scr[...]
    m_next = m_prev
    for s in ss:
        m_next = jnp.maximum(m_next, jnp.max(s, axis=1, keepdims=True))
    alpha = jnp.exp2(m_prev - m_next)
    es = [jnp.exp2(s - m_next) for s in ss]
    l_new = alpha * l_scr[...]
    for e in es:
        l_new = l_new + jnp.sum(e, axis=1, keepdims=True)
    l_scr[...] = l_new
    m_scr[...] = m_next
    acc_scr[...] = acc_scr[...] * _rep(alpha, reps) + _dot_nt(
        _lanes([e.astype(BF16) for e in es]), _lanes([p[0, SB_W:2 * SB_W, :].astype(BF16) for p in pages]))

    @pl.when(j == pl.num_programs(1) - 1)
    def _():
        out_ref[0] = acc_scr[...] / _rep(l_scr[...], reps)


def _fox_dec(q8, k_new, v_new, lf_new, pool, lf_pool, page_table, *, pp):
    bd, n_pages = page_table.shape
    u = _upper_strict(PAGE)
    req = lambda shape: pl.BlockSpec((1,) + shape, lambda b, j, pt: (b,) + (0,) * len(shape))
    lf_spec = lambda i: pl.BlockSpec((1, FOX_HEADS, PAGE), lambda b, j, pt: (pt[b, n_pages - 1 - (j * pp + i)], 0, 0))
    grid_spec = pltpu.PrefetchScalarGridSpec(
        num_scalar_prefetch=1, grid=(bd, n_pages // pp),
        in_specs=[req((8, SB_W)), req((1, SB_W)), req((1, SB_W)), req((8, LANES))]
        + _page_specs(2 * SB_W, n_pages, pp, True) + [lf_spec(i) for i in range(pp)]
        + [pl.BlockSpec(u.shape, lambda b, j, pt: (0, 0))],
        out_specs=req((8, SB_W)),
        scratch_shapes=[pltpu.VMEM((8, LANES), F32), pltpu.VMEM((8, LANES), F32),
                        pltpu.VMEM((8, SB_W), F32), pltpu.VMEM((8, LANES), F32)])
    return pl.pallas_call(
        functools.partial(_fox_dec_kernel, pp=pp), grid_spec=grid_spec,
        out_shape=jax.ShapeDtypeStruct((bd, 8, SB_W), F32),
        compiler_params=_cp("arbitrary", "arbitrary"), name="fox_decode",
    )(page_table, q8, k_new, v_new, lf_new, *([pool] * pp), *([lf_pool] * pp), u)


def _nsa_dec_kernel(pt_ref, q_ref, ckv_ref, gate_ref, win_ref, kvs_new_ref, kvw_new_ref, kvw_col_ref, amat_ref, e_ref,
                    *refs, pp, past_len, n_sel):
    pages, (out_ref, win_out_ref, m_scr, l_scr, acc_scr, sel_scr, oc_scr, ow_scr) = refs[:pp], refs[pp:]
    j = pl.program_id(1)
    q = q_ref[0]
    rows = q.shape[0]
    row = _iota((rows, LANES), 0)
    row1 = _iota((rows, 1), 0)
    lane = _iota((rows, LANES), 1)
    pos = jnp.full((rows, 1), past_len, I32)

    def new_token(kv_row):
        k = kv_row[:, 0:LANES].astype(BF16).astype(F32)
        return jnp.sum(q.astype(F32) * k, axis=1, keepdims=True), kv_row[:, LANES:2 * LANES].astype(BF16).astype(F32)

    @pl.when(j == 0)
    def _():
        nc = ckv_ref.shape[1]
        cmask = _iota((rows, nc), 1) * CMP_STRIDE + (CMP_BLOCK - 1) <= pos
        p = _masked_softmax_rows(_dot_nt(q, ckv_ref[0, :, 0:128]), cmask)
        oc_scr[...] = _dot(p.astype(BF16), ckv_ref[0, :, 128:256])
        g0 = jnp.sum(p[0:NSA_HPG], axis=0, keepdims=True)
        g1 = jnp.sum(p[NSA_HPG:2 * NSA_HPG], axis=0, keepdims=True)
        p_grp = jnp.where(row1 < NSA_HPG, g0, g1)
        imp = _split_dot(p_grp, amat_ref[...], 2)
        sel_scr[...] = _select_blocks(imp, pos, n_sel)

        wk = win_ref[0]
        n_w = wk.shape[1]
        s = _dot(q, wk[0:LANES, :].astype(BF16))
        kpos = past_len - n_w + _iota((rows, n_w), 1)
        keep = jnp.where(kpos >= 0, jnp.where(pos - kpos <= WINDOW, 1.0, 0.0), 0.0) > 0.5
        s = jnp.where(keep, s, NEG)
        s_new, v_new = new_token(kvw_new_ref[0])
        m = jnp.maximum(s_new, jnp.max(s, axis=1, keepdims=True))
        e = jnp.exp2(s - m)
        e_new = jnp.exp2(s_new - m)
        o_w = _dot_nt(e.astype(BF16), wk[LANES:2 * LANES, :].astype(BF16)) + e_new * v_new
        ow_scr[...] = o_w / (jnp.sum(e, axis=1, keepdims=True) + e_new)

        shifted = pltpu.roll(wk, n_w - 1, 1)
        win_out_ref[0] = jnp.where(_iota(wk.shape, 1) == n_w - 1, kvw_col_ref[0], shifted)
        _online_init(m_scr, l_scr, acc_scr)

    sel = sel_scr[...].astype(BF16)
    z_all = _dot(q, _lanes([p[0, 0:LANES, :].astype(BF16) for p in pages]))
    ss = []
    for i in range(pp):
        chosen = _dot(sel, e_ref[j * pp + i]) > 0.5
        ss.append(jnp.where(chosen, z_all[:, i * PAGE:(i + 1) * PAGE], NEG))
    m_prev = m_scr[...]
    m_next = m_prev
    for s in ss:
        m_next = jnp.maximum(m_next, jnp.max(s, axis=1, keepdims=True))
    alpha = jnp.exp2(m_prev - m_next)
    es = [jnp.exp2(s - m_next) for s in ss]
    l_new = alpha * l_scr[...]
    for e in es:
        l_new = l_new + jnp.sum(e, axis=1, keepdims=True)
    l_scr[...] = l_new
    m_scr[...] = m_next
    acc_scr[...] = acc_scr[...] * alpha + _dot_nt(_lanes([e.astype(BF16) for e in es]),
                                                  _lanes([p[0, LANES:2 * LANES, :].astype(BF16) for p in pages]))

    @pl.when(j == pl.num_programs(1) - 1)
    def _():
        s_new, v_new = new_token(kvs_new_ref[0])
        cur = past_len // SEL_BLOCK
        chosen = jnp.sum(jnp.where(_iota(sel_scr.shape, 1) == cur, sel_scr[...], 0.0), axis=1, keepdims=True)
        s_new = jnp.where(chosen > 0.5, s_new, NEG)
        m_prev = m_scr[...]
        m_next = jnp.maximum(m_prev, s_new)
        alpha = jnp.exp2(m_prev - m_next)
        e_new = jnp.exp2(s_new - m_next)
        o_s = (acc_scr[...] * alpha + e_new * v_new) / jnp.maximum(alpha * l_scr[...] + e_new, 1e-30)
        gates = jnp.where(row < NSA_HPG, gate_ref[0, 0:1, :], gate_ref[0, 1:2, :])
        hh = row % NSA_HPG

        def gate(c):
            return jnp.sum(jnp.where(lane == 3 * hh + c, gates, 0.0), axis=1, keepdims=True)

        out_ref[0] = gate(0) * oc_scr[...] + gate(1) * o_s + gate(2) * ow_scr[...]


def _nsa_dec(q8, ckv, gates, win_state, kvs_new, kvw_new, pool, page_table, *, pp, layer):
    bd, n_pages = page_table.shape
    past_len = n_pages * PAGE
    n_sel = -(-(past_len + 1) // SEL_BLOCK)
    nc = ckv.shape[1]
    wid = -(-n_sel // LANES) * LANES
    amat = _imp_matrix(nc, wid)
    etile = _expand_tiles(n_pages, wid, PAGE)
    n_w = win_state.shape[2]
    req = lambda shape: pl.BlockSpec((1,) + shape, lambda b, j, pt: (b,) + (0,) * len(shape))
    const = lambda a: pl.BlockSpec(a.shape, lambda b, j, pt: (0,) * a.ndim)
    win_spec = pl.BlockSpec((1, KV_W, n_w), lambda b, j, pt: (layer * bd + b, 0, 0))
    grid_spec = pltpu.PrefetchScalarGridSpec(
        num_scalar_prefetch=1, grid=(bd, n_pages // pp),
        in_specs=[req((8, LANES)), req((nc, KV_W)), req((NSA_GROUPS, LANES)), win_spec,
                  req((1, KV_W)), req((1, KV_W)), req((KV_W, 1)), const(amat), const(etile)]
        + _page_specs(KV_W, n_pages, pp, False),
        out_specs=(req((8, LANES)), req((KV_W, n_w))),
        scratch_shapes=[pltpu.VMEM((8, LANES), F32)] * 3 + [pltpu.VMEM((8, wid), F32)] + [pltpu.VMEM((8, LANES), F32)] * 2)
    return pl.pallas_call(
        functools.partial(_nsa_dec_kernel, pp=pp, past_len=past_len, n_sel=n_sel), grid_spec=grid_spec,
        out_shape=(jax.ShapeDtypeStruct((bd, 8, LANES), F32), jax.ShapeDtypeStruct((bd, KV_W, n_w), F32)),
        compiler_params=_cp("arbitrary", "arbitrary"), name="nsa_decode",
    )(page_table, q8, ckv, gates, win_state, kvs_new, kvw_new, kvw_new.reshape(bd, KV_W, 1), amat, etile,
      *([pool] * pp))


def _rope_table(pos):
    half = ROT_DIM // 2
    inv = ROPE_THETA ** (-jnp.arange(half, dtype=F32) / half)
    ang = pos.astype(F32)[:, None] * inv[None, :]
    cos, sin = jnp.cos(ang), jnp.sin(ang)
    n = pos.shape[0]
    pad = HEAD_DIM - ROT_DIM
    one_head = lambda a, b, fill: jnp.concatenate([a, b, jnp.full((n, pad), fill, F32)], axis=1)
    zero = jnp.zeros_like(sin)
    tabs = [one_head(cos, cos, 1.0), one_head(-sin, zero, 0.0), one_head(zero, sin, 0.0)]
    return jnp.concatenate([jnp.tile(t, (1, LANES // HEAD_DIM)) for t in tabs], axis=1)


_Q_HEAD_ORDER = (0, 4, 1, 5, 2, 6, 3, 7)


def _layer_weights(l, w_in, b_forget, cmp_pos, w_cmp_k, w_cmp_v, w_up_nsa):
    w = jnp.swapaxes(w_in, 1, 2)[l]
    d = w.shape[1]
    gate0 = NSA_W + 3 * KV_W
    pf0 = gate0 + GATE_W + 6 * SB_W
    merge0 = pf0 + FOX_HEADS
    rows_ = [w[HEAD_DIM * h:HEAD_DIM * (h + 1)] for h in _Q_HEAD_ORDER]
    rows_ += [w[NSA_W:gate0], w[gate0 + GATE_W:pf0], w[gate0:gate0 + GATE_W], w[pf0:merge0]]
    rows_.append(jnp.zeros((N_PROJ - merge0, d), F32))
    w_proj = jnp.concatenate(rows_, axis=0).astype(BF16)
    w_merge = w[merge0:].astype(BF16)
    zpad = lambda n: jnp.zeros((1, n), F32)
    bias = jnp.concatenate([zpad(LOGF_LANE), b_forget[l].reshape(1, FOX_HEADS), zpad(LANES - LOGF_LANE - FOX_HEADS)], axis=1)
    wk = w_cmp_k[l].reshape(CMP_BLOCK, HEAD_DIM, HEAD_DIM)
    wv = w_cmp_v[l].reshape(CMP_BLOCK, HEAD_DIM, HEAD_DIM)
    zero = jnp.zeros_like(wk)
    w_blk = jnp.concatenate([jnp.concatenate([wsrc if c == n else zero for c in range(4)], axis=2)
                             for n, wsrc in enumerate((wk, wk, wv, wv))], axis=1)
    w_cat = jnp.concatenate([w_blk[0:CMP_BLOCK // 2], w_blk[CMP_BLOCK // 2:]], axis=2).astype(BF16)
    pe4 = jnp.tile(cmp_pos[l], (1, KV_W // HEAD_DIM))
    w_a = jnp.concatenate([w_up_nsa[l][HEAD_DIM * h:HEAD_DIM * (h + 1)] for h in _Q_HEAD_ORDER], axis=0).astype(BF16)
    return w_proj, w_merge, bias, w_cat, pe4, w_a


def _pad_heads(a):
    return jnp.concatenate([a, jnp.zeros((a.shape[0], 8 - a.shape[1], a.shape[2]), a.dtype)], axis=1)


def _token_lanes(cache):
    l, p, t = cache.shape[:3]
    return cache.transpose(0, 1, 3, 4, 5, 2).reshape(l * p, -1, t)


def _own_lanes(o8, n_heads):
    n = o8.shape[0]
    o = o8[:, 0:n_heads].reshape(n, n_heads, n_heads, HEAD_DIM)
    return jnp.stack([o[:, h, h] for h in range(n_heads)], axis=1).reshape(n, n_heads * HEAD_DIM)


def kernel(x_prompt, x_sample, cache_nsa_cmp_kv, cache_nsa_sel_kv, cache_sb_kv, cache_fox_kv, cache_fox_logf,
           state_nsa_win_kv, page_table, norm_mix_g, norm_ffn_g, norm_final_g, w_in, b_forget, cmp_pos,
           w_cmp_k, w_cmp_v, w_up_nsa, w_up_sb, w_up_fox, w_out, w_ffn_gate, w_ffn_up, w_ffn_down):
    batch, seq, d = x_prompt.shape
    bd = x_sample.shape[0]
    depth = w_in.shape[0]
    n_pool = cache_sb_kv.shape[1]
    n_pages = page_table.shape[1]
    past_len = n_pages * PAGE
    m = batch * seq
    tm, tq, tq_soft = min(MATMUL_ROWS, seq), min(STICK_TILE, seq), min(SOFTMAX_TILE, seq)
    pp = min(PAGES_PER_STEP, n_pages)
    pp_prompt = min(PAGES_PER_STEP, seq // PAGE)

    rope_p = _rope_table(jnp.arange(seq, dtype=I32))
    rope_s = _rope_table(jnp.full((bd,), past_len, I32))
    ident_pt = jnp.arange(batch * (seq // PAGE), dtype=I32).reshape(batch, seq // PAGE)
    pool_cmp, pool_sel = _token_lanes(cache_nsa_cmp_kv), _token_lanes(cache_nsa_sel_kv)
    pool_sb, pool_fox = _token_lanes(cache_sb_kv), _token_lanes(cache_fox_kv)
    pool_lf = cache_fox_logf.astype(F32).transpose(0, 1, 3, 2).reshape(depth * n_pool, FOX_HEADS, PAGE)
    win_state = _token_lanes(state_nsa_win_kv)

    xp = x_prompt.reshape(m, d)
    xs = x_sample.reshape(bd, d)
    g_fin = norm_final_g.reshape(1, d)
    p_states, s_states = [], []
    for l in range(depth):
        w_proj, w_merge, bias, w_blk, pe4, w_a = _layer_weights(l, w_in, b_forget, cmp_pos, w_cmp_k, w_cmp_v, w_up_nsa)
        w_b, w_c, w_o = w_up_sb[l].astype(BF16), w_up_fox[l].astype(BF16), w_out[l].astype(BF16)
        w_g, w_u, w_d = w_ffn_gate[l].astype(BF16), w_ffn_up[l].astype(BF16), w_ffn_down[l].astype(BF16)
        g_mix, g_ffn = norm_mix_g[l].reshape(1, d), norm_ffn_g[l].reshape(1, d)

        (qn, kv_cmp, kv_sel, kv_win, kv_sb, kv_fox, kb_sel, kb_win, kb_sb, kb_fox, q_sb, q_fox, gates, logf, cum
         ) = _proj(xp, g_mix, w_proj, rope_p, bias, tm=tm, tiles_per_seq=seq // tm, feature_states=True)
        ckv = _compress(kv_cmp, ident_pt, w_blk, pe4, pp=pp_prompt, paged=False)
        o_nsa = _nsa_prompt(qn, ckv, kb_sel, kb_win, gates, batch=batch, seq=seq, tq=tq_soft)
        o_sb = _sb_prompt(q_sb, kb_sb, batch=batch, seq=seq, tq=tq)
        cum4 = cum[:, LOGF_LANE:LOGF_LANE + FOX_HEADS]
        cum_k = cum4.reshape(batch, seq, FOX_HEADS).transpose(0, 2, 1).reshape(
            batch, FOX_HEADS, seq // tq_soft, 1, tq_soft)
        o_fox = _fox_prompt(q_fox, kb_fox, cum, cum_k, batch=batch, seq=seq, tq=tq_soft)
        xp = _merge(xp, g_mix, o_nsa, o_sb, o_fox, w_merge, w_a, w_b, w_c, w_o, tm=tm)
        xp = _ffn(xp, g_ffn, w_g, w_u, w_d, g_fin, tm=tm, final_norm=(l == depth - 1))
        win_keep = min(WINDOW, seq)
        p_states.append((kv_cmp, kv_sel, kv_sb, kv_fox, logf, kv_win[:, :, seq - win_keep:]))

        (qn, kv_cmp, kv_sel, kv_win, kv_sb, kv_fox, _, _, _, _, q_sb, q_fox, gates, logf, _
         ) = _proj(xs, g_mix, w_proj, rope_s, bias, tm=bd, tiles_per_seq=1, feature_states=False)
        pt_l = page_table + l * n_pool
        ckv = _compress(pool_cmp, pt_l, w_blk, pe4, pp=pp, paged=True)
        q8 = qn.transpose(2, 0, 1, 3).reshape(bd, NSA_HEADS, LANES)
        o_nsa8, win_new = _nsa_dec(q8, ckv, gates.transpose(1, 0, 2), win_state, kv_sel.reshape(bd, 1, KV_W),
                                   kv_win.reshape(bd, 1, KV_W), pool_sel, pt_l, pp=pp, layer=l)
        o_sb8 = _sb_dec(_pad_heads(q_sb.transpose(1, 0, 2)), pool_sb, pt_l, pp=pp)
        lf4 = logf[:, LOGF_LANE:LOGF_LANE + FOX_HEADS]
        lf_new = _pad_heads(jnp.broadcast_to(lf4[:, :, None], (bd, FOX_HEADS, LANES)))
        o_fox8 = _fox_dec(_pad_heads(q_fox.transpose(1, 0, 2)), kv_fox[:, 0:SB_W].reshape(bd, 1, SB_W),
                          kv_fox[:, SB_W:2 * SB_W].reshape(bd, 1, SB_W), lf_new, pool_fox, pool_lf, pt_l, pp=pp)
        o_a = o_nsa8.reshape(bd, NSA_GROUPS, NSA_HPG, NSA_GROUPS, HEAD_DIM)
        o_a = jnp.stack([o_a[:, g, :, g] for g in range(NSA_GROUPS)], axis=2).reshape(bd, NSA_W)
        o_a = jnp.stack([o_a, jnp.zeros_like(o_a)], axis=0).astype(BF16)
        o_b = _own_lanes(o_sb8, SB_HEADS).astype(BF16)
        o_c = _own_lanes(o_fox8, FOX_HEADS).astype(BF16)
        xs = _merge(xs, g_mix, o_a, o_b, o_c, w_merge, w_a, w_b, w_c, w_o, tm=bd)
        xs = _ffn(xs, g_ffn, w_g, w_u, w_d, g_fin, tm=bd, final_norm=(l == depth - 1))
        s_states.append((
            kv_cmp.reshape(bd, 1, 2, NSA_GROUPS, HEAD_DIM), kv_sel.reshape(bd, 1, 2, NSA_GROUPS, HEAD_DIM),
            kv_sb.reshape(bd, 1, 2, SB_HEADS, HEAD_DIM), kv_fox.reshape(bd, 1, 2, FOX_HEADS, HEAD_DIM),
            lf4.reshape(bd, 1, FOX_HEADS),
            win_new.reshape(bd, 2, NSA_GROUPS, HEAD_DIM, -1).transpose(0, 4, 1, 2, 3)))

    y_prompt = xp.reshape(batch, seq, d)
    y_sample = xs.reshape(bd, 1, d)
    stk = lambda sts, i: jnp.stack([st[i] for st in sts], axis=0)

    def kv_state(i, heads):
        a = stk(p_states, i)
        return a.reshape(depth, batch, 2, heads, HEAD_DIM, a.shape[-1]).transpose(0, 1, 5, 2, 3, 4)

    p_out = (kv_state(0, NSA_GROUPS), kv_state(1, NSA_GROUPS), kv_state(2, SB_HEADS), kv_state(3, FOX_HEADS),
             stk(p_states, 4).transpose(0, 1, 3, 2), kv_state(5, NSA_GROUPS))
    return (y_prompt, y_sample) + p_out + tuple(stk(s_states, i) for i in range(6))
```

```python
import functools

import numpy as np
import jax
import jax.numpy as jnp
from jax import lax
from jax.experimental import pallas as pl
from jax.experimental.pallas import tpu as pltpu

F32, BF16, I32 = jnp.float32, jnp.bfloat16, jnp.int32

HEAD_DIM = 64
ROT_DIM = HEAD_DIM // 4
ROPE_THETA = 500000.0
NSA_HEADS = 8
NSA_GROUPS = 2
NSA_HPG = NSA_HEADS // NSA_GROUPS
CMP_STRIDE = 16
CMP_BLOCK = 32
SEL_BLOCK = 64
TOP_N = 16
WINDOW = 512
SB_HEADS = 4
FOX_HEADS = 4
PAGE = 128
SEL_FORCE_SCORE = 1.0e4
NEG = -1.0e30
RMS_EPS = 1e-6
SCALE = HEAD_DIM ** -0.5
LOG2E = 1.4426950408889634

LANES = 128
NSA_W = NSA_HEADS * HEAD_DIM
KV_W = 2 * NSA_GROUPS * HEAD_DIM
SB_W = SB_HEADS * HEAD_DIM
GATE_W = 3 * NSA_HEADS
C_Q, C_CMP, C_SEL, C_WIN, C_SB, C_FOX, C_MISC = 0, 512, 768, 1024, 1280, 2048, 2816
N_PROJ = C_MISC + LANES
LOGF_LANE = GATE_W
V7X_VMEM_BYTES = 64 * 1024 * 1024
VMEM_LIMIT = V7X_VMEM_BYTES * 7 // 8
PAGES_PER_STEP = 32
MATMUL_ROWS = 512
SOFTMAX_TILE = 512
STICK_TILE = 256


def _cp(*sem):
    return pltpu.CompilerParams(dimension_semantics=sem, vmem_limit_bytes=VMEM_LIMIT)


def _dot(a, b):
    return jnp.dot(a, b, preferred_element_type=F32)


def _dot_nt(a, b):
    return lax.dot_general(a, b, (((1,), (1,)), ((), ())), preferred_element_type=F32)


def _split_dot(a, r, parts):
    out, rem = None, a
    for _ in range(parts):
        hi = rem.astype(BF16)
        d = _dot(hi, r)
        out = d if out is None else out + d
        rem = rem - hi.astype(F32)
    return out


def _split_dot_l(l, a, parts):
    out, rem = None, a
    for _ in range(parts):
        hi = rem.astype(BF16)
        d = _dot(l, hi)
        out = d if out is None else out + d
        rem = rem - hi.astype(F32)
    return out


def _rms(x, g):
    return x * lax.rsqrt(jnp.mean(x * x, axis=-1, keepdims=True) + RMS_EPS) * g


def _softplus2(z2):
    return jnp.maximum(z2, 0.0) + jnp.log2(1.0 + jnp.exp2(-jnp.abs(z2)))


def _iota(shape, axis):
    return lax.broadcasted_iota(I32, shape, axis)


def _proj_kernel(x_ref, g_ref, w_ref, rope_ref, b_ref, tri_ref,
                 qn_ref, kvc_ref, kvs_ref, kvw_ref, kvsb_ref, kvfx_ref,
                 kbs_ref, kbw_ref, kbsb_ref, kbfx_ref, qsb_ref, qfx_ref,
                 gate_ref, logf_ref, cum_ref, h_scr, carry_scr, *, tiles_per_seq, feature_states):
    i = pl.program_id(0)
    tm = x_ref.shape[0]
    h_scr[...] = _rms(x_ref[...], g_ref[...]).astype(BF16)
    hb = h_scr[...]
    cos, sin_lo, sin_hi = rope_ref[:, 0:128], rope_ref[:, 128:256], rope_ref[:, 256:384]

    def rope(seg):
        return seg * cos + pltpu.roll(seg, LANES - 8, 1) * sin_lo + pltpu.roll(seg, 8, 1) * sin_hi

    def put_state(f_ref, blocks):
        for n, blk in enumerate(blocks):
            if feature_states:
                f_ref[0, LANES * n:LANES * (n + 1), :] = blk.T
            else:
                f_ref[:, LANES * n:LANES * (n + 1)] = blk

    lane = _iota((tm, LANES), 1)
    pq = _dot_nt(hb, w_ref[C_Q:C_Q + NSA_W, :])
    for s in range(NSA_HPG):
        seg = rope(pq[:, LANES * s:LANES * (s + 1)]) * (SCALE * LOG2E)
        for g in range(NSA_GROUPS):
            qn_ref[g, s] = jnp.where(lane // HEAD_DIM == g, seg, 0.0).astype(BF16)

    for c0, f_ref, h_ref in ((C_CMP, kvc_ref, None), (C_SEL, kvs_ref, kbs_ref), (C_WIN, kvw_ref, kbw_ref)):
        p = _dot_nt(hb, w_ref[c0:c0 + KV_W, :])
        k, v = rope(p[:, 0:128]), p[:, 128:256]
        put_state(f_ref, [k, v])
        if h_ref is not None:
            h_ref[:, 0:128] = k.astype(BF16)
            h_ref[:, 128:256] = v.astype(BF16)

    lane_w = _iota((tm, SB_W), 1)
    for c0, q_ref, f_ref, h_ref in ((C_SB, qsb_ref, kvsb_ref, kbsb_ref), (C_FOX, qfx_ref, kvfx_ref, kbfx_ref)):
        p = _dot_nt(hb, w_ref[c0:c0 + 3 * SB_W, :])
        q = p[:, 0:SB_W] * (SCALE * LOG2E)
        for h in range(SB_HEADS):
            q_ref[h] = jnp.where(lane_w // HEAD_DIM == h, q, 0.0).astype(BF16)
        kv = p[:, SB_W:3 * SB_W]
        put_state(f_ref, [kv[:, LANES * n:LANES * (n + 1)] for n in range(2 * SB_W // LANES)])
        h_ref[...] = kv.astype(BF16)

    pm = _dot_nt(hb, w_ref[C_MISC:C_MISC + LANES, :])
    sg = jax.nn.sigmoid(pm)
    gate_ref[0] = sg
    gate_ref[1] = pltpu.roll(sg, LANES - GATE_W // 2, 1)
    zf = pm + b_ref[...]
    lf = jnp.minimum(zf, 0.0) - jnp.log1p(jnp.exp(-jnp.abs(zf)))
    if feature_states:
        logf_ref[0] = lf.T[LOGF_LANE:LOGF_LANE + FOX_HEADS, :]
    else:
        logf_ref[...] = lf

    @pl.when(i % tiles_per_seq == 0)
    def _():
        carry_scr[...] = jnp.zeros_like(carry_scr)

    c = _split_dot_l(tri_ref[...], lf, 3) + carry_scr[0:1, :]
    cum_ref[...] = c
    carry_scr[...] = jnp.broadcast_to(c[tm - 1:tm, :], carry_scr.shape)


def _proj(x, gain, w, rope_tab, bias, *, tm, tiles_per_seq, feature_states):
    m, d = x.shape
    tri = jnp.tril(jnp.ones((tm, tm), F32)).astype(BF16)
    row = lambda wid: pl.BlockSpec((tm, wid), lambda i: (i, 0))
    const = lambda shape: pl.BlockSpec(shape, lambda i: (0,) * len(shape))
    f32o = lambda wid: jax.ShapeDtypeStruct((m, wid), F32)
    b16o = lambda wid: jax.ShapeDtypeStruct((m, wid), BF16)
    if feature_states:
        nb, seq = m // (tm * tiles_per_seq), tm * tiles_per_seq
        st_shape = lambda wid: jax.ShapeDtypeStruct((nb, wid, seq), F32)
        st_spec = lambda wid: pl.BlockSpec((1, wid, tm), lambda i: (i // tiles_per_seq, 0, i % tiles_per_seq))
        lf_shape, lf_spec = st_shape(FOX_HEADS), st_spec(FOX_HEADS)
    else:
        st_shape, st_spec, lf_shape, lf_spec = f32o, row, f32o(LANES), row(LANES)
    out_shape = (
        jax.ShapeDtypeStruct((NSA_GROUPS, NSA_HPG, m, LANES), BF16),
        st_shape(KV_W), st_shape(KV_W), st_shape(KV_W), st_shape(2 * SB_W), st_shape(2 * SB_W),
        b16o(KV_W), b16o(KV_W), b16o(2 * SB_W), b16o(2 * SB_W),
        jax.ShapeDtypeStruct((SB_HEADS, m, SB_W), BF16), jax.ShapeDtypeStruct((SB_HEADS, m, SB_W), BF16),
        jax.ShapeDtypeStruct((NSA_GROUPS, m, LANES), F32), lf_shape, f32o(LANES),
    )
    out_specs = (
        pl.BlockSpec((NSA_GROUPS, NSA_HPG, tm, LANES), lambda i: (0, 0, i, 0)),
        st_spec(KV_W), st_spec(KV_W), st_spec(KV_W), st_spec(2 * SB_W), st_spec(2 * SB_W),
        row(KV_W), row(KV_W), row(2 * SB_W), row(2 * SB_W),
        pl.BlockSpec((SB_HEADS, tm, SB_W), lambda i: (0, i, 0)),
        pl.BlockSpec((SB_HEADS, tm, SB_W), lambda i: (0, i, 0)),
        pl.BlockSpec((NSA_GROUPS, tm, LANES), lambda i: (0, i, 0)),
        lf_spec, row(LANES),
    )
    return pl.pallas_call(
        functools.partial(_proj_kernel, tiles_per_seq=tiles_per_seq, feature_states=feature_states),
        grid=(m // tm,),
        in_specs=[row(d), const((1, d)), const((N_PROJ, d)),
                  pl.BlockSpec((tm, 3 * LANES), lambda i: (i % tiles_per_seq, 0)),
                  const((1, LANES)), const((tm, tm))],
        out_specs=out_specs, out_shape=out_shape,
        scratch_shapes=[pltpu.VMEM((tm, d), BF16), pltpu.VMEM((8, LANES), F32)],
        compiler_params=_cp("arbitrary"), name="proj",
    )(x, gain, w, rope_tab, bias, tri)


def _compress_kernel(pt_ref, *refs, pp, nch):
    pages, (perm_ref, w_ref, pe_ref, out_ref, x_scr, acc_scr, pos_scr) = refs[:pp], refs[pp:]
    j = pl.program_id(1)
    half = CMP_BLOCK // 2
    chunks = PAGE // CMP_STRIDE

    @pl.when((pl.program_id(0) == 0) & (j == 0))
    def _():
        c = jnp.zeros((8, KV_W), F32)
        for l in range(half):
            for off, cols in ((0, slice(0, KV_W)), (half, slice(KV_W, 2 * KV_W))):
                row = jnp.broadcast_to(pe_ref[off + l:off + l + 1, :], (8, KV_W))
                c = c + _split_dot(row, w_ref[l, :, cols], 2)
        pos_scr[...] = c

    for i in range(pp):
        gt = _dot_nt(perm_ref[...], pages[i][0].astype(BF16))
        base = pl.multiple_of((j * pp + i) * chunks, chunks)
        for l in range(half):
            x_scr[l, pl.ds(base, chunks), :] = gt[chunks * l:chunks * (l + 1), :]

    @pl.when(j == pl.num_programs(1) - 1)
    def _():
        acc_scr[...] = jnp.zeros_like(acc_scr)
        for l in range(half):
            acc_scr[...] += _dot(x_scr[l].astype(BF16), w_ref[l])
        nxt = pltpu.roll(acc_scr[:, KV_W:2 * KV_W], nch - 1, 0)
        nxt = jnp.where(_iota((nch, KV_W), 0) < nch - 1, nxt, 0.0)
        out_ref[0] = (acc_scr[:, 0:KV_W] + nxt + pos_scr[0:1, :]).astype(BF16)


def _compress(pool, page_table, w_cat, pe4, *, pp, paged):
    b, n_pages = page_table.shape
    chunks = PAGE // CMP_STRIDE
    nch = n_pages * chunks
    half = CMP_BLOCK // 2
    tok = np.arange(PAGE)
    perm = jnp.asarray(np.arange(PAGE)[:, None] == ((tok % CMP_STRIDE) * chunks + tok // CMP_STRIDE)[None, :], BF16)
    if paged:
        page_spec = lambda i: pl.BlockSpec((1, KV_W, PAGE), lambda bb, j, pt: (pt[bb, j * pp + i], 0, 0))
    else:
        page_spec = lambda i: pl.BlockSpec((1, KV_W, PAGE), lambda bb, j, pt: (bb, 0, j * pp + i))
    grid_spec = pltpu.PrefetchScalarGridSpec(
        num_scalar_prefetch=1, grid=(b, n_pages // pp),
        in_specs=[page_spec(i) for i in range(pp)] + [
            pl.BlockSpec((PAGE, PAGE), lambda bb, j, pt: (0, 0)),
            pl.BlockSpec((half, KV_W, 2 * KV_W), lambda bb, j, pt: (0, 0, 0)),
            pl.BlockSpec((CMP_BLOCK, KV_W), lambda bb, j, pt: (0, 0))],
        out_specs=pl.BlockSpec((1, nch, KV_W), lambda bb, j, pt: (bb, 0, 0)),
        scratch_shapes=[pltpu.VMEM((half, nch, KV_W), F32), pltpu.VMEM((nch, 2 * KV_W), F32),
                        pltpu.VMEM((8, KV_W), F32)])
    return pl.pallas_call(
        functools.partial(_compress_kernel, pp=pp, nch=nch), grid_spec=grid_spec,
        out_shape=jax.ShapeDtypeStruct((b, nch, KV_W), BF16),
        compiler_params=_cp("arbitrary", "arbitrary"), name="compress",
    )(page_table, *([pool] * pp), perm, w_cat, pe4)


def _masked_softmax_rows(s, mask):
    s = jnp.where(mask, s, NEG)
    e = jnp.where(mask, jnp.exp2(s - jnp.max(s, axis=1, keepdims=True)), 0.0)
    return e / jnp.maximum(jnp.sum(e, axis=1, keepdims=True), 1e-30)


def _online_init(m_scr, l_scr, acc_scr):
    m_scr[...] = jnp.full_like(m_scr, NEG)
    l_scr[...] = jnp.zeros_like(l_scr)
    acc_scr[...] = jnp.zeros_like(acc_scr)


def _select_blocks(imp, pos, n_sel):
    blk = _iota(imp.shape, 1)
    cur = pos // SEL_BLOCK
    valid = blk * SEL_BLOCK <= pos
    forced = jnp.where(blk == 0, 1.0, 0.0) + jnp.where(blk == cur, 1.0, 0.0) + jnp.where(blk == cur - 1, 1.0, 0.0)
    score = jnp.where(valid, jnp.where(forced > 0.5, SEL_FORCE_SCORE, imp), -1.0)
    score = jnp.where(blk < n_sel, score, -2.0)
    rank = jnp.zeros(imp.shape, F32)
    for b2 in range(n_sel):
        col = score[:, b2:b2 + 1]
        ge = jnp.where(col >= score, 1.0, 0.0)
        gt = jnp.where(col > score, 1.0, 0.0)
        rank = rank + jnp.where(blk > b2, ge, gt)
    return jnp.where(rank < min(TOP_N, n_sel), 1.0, 0.0)


def _rep(x, n):
    return x if n == 1 else jnp.concatenate([x] * n, axis=1)


def _select_blocks_t(p_grp, amat_t, q0, n_sel):
    t = p_grp.shape[0]
    nb = amat_t.shape[0]
    hi = p_grp.astype(BF16)
    lo = (p_grp - hi.astype(F32)).astype(BF16)
    imp = _dot_nt(amat_t, hi) + _dot_nt(amat_t, lo)
    pos = q0 + _iota((1, t), 1)
    blk = _iota((nb, t), 0)
    cur = pos // SEL_BLOCK
    forced = jnp.where(blk == 0, 1.0, 0.0) + jnp.where(blk == cur, 1.0, 0.0) + jnp.where(blk == cur - 1, 1.0, 0.0)
    score = jnp.where(blk * SEL_BLOCK <= pos, jnp.where(forced > 0.5, SEL_FORCE_SCORE, imp), -1.0)
    score = jnp.where(blk < n_sel, score, -2.0)
    groups = [score[8 * k:8 * (k + 1)] for k in range(nb // 8)]
    ranks = [jnp.zeros((8, t), F32) for _ in groups]
    sub = _iota((8, t), 0)
    for b2 in range(n_sel):
        rowv = score[b2:b2 + 1, :]
        k2, r2 = divmod(b2, 8)
        for k, grp in enumerate(groups):
            if k > k2:
                inc = jnp.where(rowv >= grp, 1.0, 0.0)
            elif k < k2:
                inc = jnp.where(rowv > grp, 1.0, 0.0)
            else:
                inc = jnp.where(sub > r2, jnp.where(rowv >= grp, 1.0, 0.0), jnp.where(rowv > grp, 1.0, 0.0))
            ranks[k] = ranks[k] + inc
    chosen = [jnp.where(rk < min(TOP_N, n_sel), 1.0, 0.0) for rk in ranks]
    pad = [jnp.zeros((LANES - nb, t), F32)] if nb < LANES else []
    return jnp.concatenate(chosen + pad, axis=0)


def _nsa_prompt_kernel(qn_ref, ckv_ref, ks_ref, kw_ref, gate_ref, amat_ref, e_ref, out_ref,
                       m_scr, l_scr, acc_scr, oc_scr, os_scr, *, n_sel):
    g, i = pl.program_id(1), pl.program_id(2)
    t = qn_ref.shape[2]
    q0 = i * t
    rowi, coli = _iota((t, t), 0), _iota((t, t), 1)
    reps = t // LANES

    nc = ckv_ref.shape[1]
    cmask = _iota((t, nc), 1) * CMP_STRIDE + (CMP_BLOCK - 1) <= q0 + _iota((t, 1), 0)
    ck, cv = ckv_ref[0, :, 0:128], ckv_ref[0, :, 128:256]
    p_grp = None
    for h in range(NSA_HPG):
        p = _masked_softmax_rows(_dot_nt(qn_ref[0, h], ck), cmask)
        oc_scr[h] = _dot(p.astype(BF16), cv)
        p_grp = p if p_grp is None else p_grp + p
    sel = _select_blocks_t(p_grp, amat_ref[...], q0, n_sel).T.astype(BF16)

    def online_head(h, s, v):
        m_prev = m_scr[h]
        m_next = jnp.maximum(m_prev, jnp.max(s, axis=1, keepdims=True))
        alpha = jnp.exp2(m_prev - m_next)
        e = jnp.exp2(s - _rep(m_next, reps))
        l_scr[h] = alpha * l_scr[h] + jnp.sum(e, axis=1, keepdims=True)
        m_scr[h] = m_next
        acc_scr[h] = acc_scr[h] * alpha + _dot(e.astype(BF16), v)

    def reset():
        m_scr[...] = jnp.full_like(m_scr, NEG)
        l_scr[...] = jnp.zeros_like(l_scr)
        acc_scr[...] = jnp.zeros_like(acc_scr)

    def sel_tile(j, diag):
        kt = ks_ref[0, pl.ds(pl.multiple_of(j * t, t), t), :]
        chosen = _dot(sel, e_ref[j])
        if diag:
            chosen = jnp.where(coli <= rowi, chosen, 0.0)
        keep = chosen > 0.5
        for h in range(NSA_HPG):
            online_head(h, jnp.where(keep, _dot_nt(qn_ref[0, h], kt[:, 0:128]), NEG), kt[:, 128:256])

    reset()
    def two_sel_tiles(n, c):
        sel_tile(2 * n, False)
        sel_tile(2 * n + 1, False)
        return c

    lax.fori_loop(0, i // 2, two_sel_tiles, 0)

    @pl.when(i % 2 == 1)
    def _():
        sel_tile(i - 1, False)

    sel_tile(i, True)
    for h in range(NSA_HPG):
        os_scr[h] = acc_scr[h] / jnp.maximum(l_scr[h], 1e-30)

    tw = min(t, 2 * LANES)
    w_len = WINDOW + tw
    gates = gate_ref[0]
    lane = _iota((tw, LANES), 1)
    for rb in range(t // tw):
        rows = slice(rb * tw, (rb + 1) * tw)
        r0 = q0 + rb * tw
        k0 = jnp.maximum(r0 - WINDOW, 0)
        kw = kw_ref[0, pl.ds(pl.multiple_of(k0, tw), w_len), :]
        d = (r0 - k0) + _iota((tw, w_len), 0) - _iota((tw, w_len), 1)
        keep_w = jnp.where(d >= 0, jnp.where(d <= WINDOW, 1.0, 0.0), 0.0) > 0.5
        for h in range(NSA_HPG):
            s = jnp.where(keep_w, _dot_nt(qn_ref[0, h, rows, :], kw[:, 0:128]), NEG)
            e = jnp.exp2(s - jnp.max(s, axis=1, keepdims=True))
            o_w = _dot(e.astype(BF16), kw[:, 128:256]) / jnp.sum(e, axis=1, keepdims=True)
            o = (gates[rows, 3 * h:3 * h + 1] * oc_scr[h, rows, :] + gates[rows, 3 * h + 1:3 * h + 2] * os_scr[h, rows, :]
                 + gates[rows, 3 * h + 2:3 * h + 3] * o_w)
            out_ref[0, rows, LANES * h:LANES * (h + 1)] = jnp.where(lane // HEAD_DIM == g, o, 0.0).astype(BF16)


def _imp_matrix(nc, width):
    n = np.arange(nc)[:, None]
    b = np.arange(width)[None, :]
    sub = SEL_BLOCK // CMP_STRIDE
    return jnp.asarray((n // sub == b).astype(np.float32) + ((n + 1) // sub == b), BF16)


def _expand_tiles(n_tiles, n_blk, tk):
    key = np.arange(n_tiles * tk).reshape(n_tiles, 1, tk)
    return jnp.asarray(key // SEL_BLOCK == np.arange(n_blk)[None, :, None], BF16)


def _nsa_prompt(qn, ckv, kb_sel, kb_win, gates, *, batch, seq, tq):
    nq = seq // tq
    nc = ckv.shape[1]
    n_sel = seq // SEL_BLOCK
    assert n_sel <= LANES and WINDOW % tq == 0 and seq >= WINDOW + tq
    amat = _imp_matrix(nc, -(-n_sel // 8) * 8).T
    etile = _expand_tiles(nq, LANES, tq)
    m = batch * seq
    return pl.pallas_call(
        functools.partial(_nsa_prompt_kernel, n_sel=n_sel),
        grid=(batch, NSA_GROUPS, nq),
        in_specs=[
            pl.BlockSpec((1, NSA_HPG, tq, LANES), lambda b, g, i: (g, 0, b * nq + i, 0)),
            pl.BlockSpec((1, nc, KV_W), lambda b, g, i: (b, 0, 0)),
            pl.BlockSpec((1, seq, KV_W), lambda b, g, i: (b, 0, 0)),
            pl.BlockSpec((1, seq, KV_W), lambda b, g, i: (b, 0, 0)),
            pl.BlockSpec((1, tq, LANES), lambda b, g, i: (g, b * nq + i, 0)),
            pl.BlockSpec(amat.shape, lambda b, g, i: (0, 0)),
            pl.BlockSpec(etile.shape, lambda b, g, i: (0, 0, 0)),
        ],
        out_specs=pl.BlockSpec((1, tq, NSA_W), lambda b, g, i: (g, b * nq + i, 0)),
        out_shape=jax.ShapeDtypeStruct((NSA_GROUPS, m, NSA_W), BF16),
        scratch_shapes=[pltpu.VMEM((NSA_HPG, tq, LANES), F32)] * 5,
        compiler_params=_cp("arbitrary", "arbitrary", "arbitrary"), name="nsa_prompt",
    )(qn, ckv, kb_sel.reshape(batch, seq, KV_W), kb_win.reshape(batch, seq, KV_W), gates, amat, etile)


def _upper_strict(n):
    return jnp.asarray(np.triu(np.ones((n, n), np.float32), 1).T, BF16)


def _stack_heads(v):
    lane = _iota(v.shape, 1)
    return jnp.concatenate([jnp.where(lane // HEAD_DIM == h, v, jnp.zeros_like(v)) for h in range(SB_HEADS)], axis=0)


def _head_lanes(parts):
    lane = _iota(parts[0].shape, 1)
    return jnp.concatenate([jnp.where(lane < HEAD_DIM, parts[0], parts[1]),
                            jnp.where(lane < HEAD_DIM, parts[2], parts[3])], axis=1)


def _sb_prompt_kernel(q_ref, kv_ref, u_ref, out_ref, acc_scr, r_scr):
    i = pl.program_id(1)
    t = q_ref.shape[1]
    rowi, coli = _iota((t, t), 0), _iota((t, t), 1)
    acc_scr[...] = jnp.zeros_like(acc_scr)
    r_scr[...] = jnp.zeros_like(r_scr)

    def tile(j, diag):
        kt = kv_ref[0, pl.ds(pl.multiple_of(j * t, t), t), :]
        ws = []
        for h in range(SB_HEADS):
            z = _dot_nt(q_ref[h], kt[:, 0:SB_W])
            sp = _softplus2(z)
            cost = jnp.where(coli < rowi, sp, 0.0) if diag else sp
            r = r_scr[h]
            w = jnp.exp2(z - sp - _split_dot(cost, u_ref[...], 1) - _rep(r, t // LANES))
            if diag:
                w = jnp.where(coli < rowi, w, 0.0)
            ws.append(w.astype(BF16))
            r_scr[h] = r + jnp.sum(cost, axis=1, keepdims=True)
        acc_scr[...] += _dot(jnp.concatenate(ws, axis=1), _stack_heads(kt[:, SB_W:2 * SB_W]))

    tile(i, True)

    def eight_tiles(n, c):
        for u in range(8):
            tile(i - 1 - 8 * n - u, False)
        return c

    lax.fori_loop(0, i // 8, eight_tiles, 0)
    rem = i % 8

    @pl.when(rem >= 4)
    def _():
        for u in range(4):
            tile(rem - 1 - u, False)

    rem4 = rem % 4

    @pl.when(rem4 >= 2)
    def _():
        tile(rem4 - 1, False)
        tile(rem4 - 2, False)

    @pl.when(rem4 % 2 == 1)
    def _():
        tile(0, False)

    out_ref[...] = acc_scr[...].astype(BF16)


def _sb_prompt(q, kb, *, batch, seq, tq):
    nq = seq // tq
    u = _upper_strict(tq)
    return pl.pallas_call(
        _sb_prompt_kernel, grid=(batch, nq),
        in_specs=[pl.BlockSpec((SB_HEADS, tq, SB_W), lambda b, i: (0, b * nq + i, 0)),
                  pl.BlockSpec((1, seq, 2 * SB_W), lambda b, i: (b, 0, 0)),
                  pl.BlockSpec(u.shape, lambda b, i: (0, 0))],
        out_specs=pl.BlockSpec((tq, SB_W), lambda b, i: (b * nq + i, 0)),
        out_shape=jax.ShapeDtypeStruct((batch * seq, SB_W), BF16),
        scratch_shapes=[pltpu.VMEM((tq, SB_W), F32), pltpu.VMEM((SB_HEADS, tq, LANES), F32)],
        compiler_params=_cp("arbitrary", "arbitrary"), name="sb_prompt",
    )(q, kb.reshape(batch, seq, 2 * SB_W), u)


def _fox_prompt_kernel(q_ref, kv_ref, cq_ref, ck_ref, out_ref, m_scr, l_scr, acc_scr, cq_scr):
    i = pl.program_id(1)
    t = q_ref.shape[1]
    rowi, coli = _iota((t, t), 0), _iota((t, t), 1)
    m_scr[...] = jnp.full_like(m_scr, NEG)
    l_scr[...] = jnp.zeros_like(l_scr)
    acc_scr[...] = jnp.zeros_like(acc_scr)
    for h in range(FOX_HEADS):
        cq_scr[h] = jnp.broadcast_to(cq_ref[:, LOGF_LANE + h:LOGF_LANE + h + 1] * LOG2E, (t, LANES))

    def tile(j, diag):
        kt = kv_ref[0, pl.ds(pl.multiple_of(j * t, t), t), :]
        es, alphas = [], []
        for h in range(FOX_HEADS):
            s = _dot_nt(q_ref[h], kt[:, 0:SB_W]) + (_rep(cq_scr[h], t // LANES) - ck_ref[0, h, j] * LOG2E)
            if diag:
                s = jnp.where(coli <= rowi, s, NEG)
            m_prev = m_scr[h]
            m_next = jnp.maximum(m_prev, jnp.max(s, axis=1, keepdims=True))
            alpha = jnp.exp2(m_prev - m_next)
            e = jnp.exp2(s - _rep(m_next, t // LANES))
            l_scr[h] = alpha * l_scr[h] + jnp.sum(e, axis=1, keepdims=True)
            m_scr[h] = m_next
            es.append(e.astype(BF16))
            alphas.append(alpha)
        acc_scr[...] = acc_scr[...] * _head_lanes(alphas) + _dot(jnp.concatenate(es, axis=1),
                                                                 _stack_heads(kt[:, SB_W:2 * SB_W]))

    def two_tiles(n, c):
        tile(2 * n, False)
        tile(2 * n + 1, False)
        return c

    lax.fori_loop(0, i // 2, two_tiles, 0)

    @pl.when(i % 2 == 1)
    def _():
        tile(i - 1, False)

    tile(i, True)
    out_ref[...] = (acc_scr[...] / _head_lanes([jnp.maximum(l_scr[h], 1e-30) for h in range(FOX_HEADS)])).astype(BF16)


def _fox_prompt(q, kb, cum, cum_k, *, batch, seq, tq):
    nq = seq // tq
    nk = seq // tq
    return pl.pallas_call(
        _fox_prompt_kernel, grid=(batch, nq),
        in_specs=[pl.BlockSpec((FOX_HEADS, tq, SB_W), lambda b, i: (0, b * nq + i, 0)),
                  pl.BlockSpec((1, seq, 2 * SB_W), lambda b, i: (b, 0, 0)),
                  pl.BlockSpec((tq, LANES), lambda b, i: (b * nq + i, 0)),
                  pl.BlockSpec((1, FOX_HEADS, nk, 1, tq), lambda b, i: (b, 0, 0, 0, 0))],
        out_specs=pl.BlockSpec((tq, SB_W), lambda b, i: (b * nq + i, 0)),
        out_shape=jax.ShapeDtypeStruct((batch * seq, SB_W), BF16),
        scratch_shapes=[pltpu.VMEM((FOX_HEADS, tq, LANES), F32), pltpu.VMEM((FOX_HEADS, tq, LANES), F32),
                        pltpu.VMEM((tq, SB_W), F32), pltpu.VMEM((FOX_HEADS, tq, LANES), F32)],
        compiler_params=_cp("arbitrary", "arbitrary"), name="fox_prompt",
    )(q, kb.reshape(batch, seq, 2 * SB_W), cum, cum_k)


def _merge_kernel(x_ref, g_ref, oa_ref, ob_ref, oc_ref, wm_ref, wa_ref, wb_ref, wc_ref, wo_ref, out_ref):
    x = x_ref[...]
    d = x.shape[1]
    hb = _rms(x, g_ref[...]).astype(BF16)
    oa = oa_ref[0] + oa_ref[1]
    u = jax.nn.sigmoid(_dot_nt(hb, wm_ref[0:d, :])) * _dot(oa, wa_ref[...])
    u = u + jax.nn.sigmoid(_dot_nt(hb, wm_ref[d:2 * d, :])) * _dot(ob_ref[...], wb_ref[...])
    u = u + jax.nn.sigmoid(_dot_nt(hb, wm_ref[2 * d:3 * d, :])) * _dot(oc_ref[...], wc_ref[...])
    out_ref[...] = x + _dot(u.astype(BF16), wo_ref[...])


def _merge(x, gain, o_nsa, o_sb, o_fox, w_merge, w_a, w_b, w_c, w_o, *, tm):
    m, d = x.shape
    row = lambda wid: pl.BlockSpec((tm, wid), lambda i: (i, 0))
    const = lambda a: pl.BlockSpec(a.shape, lambda i: (0,) * a.ndim)
    return pl.pallas_call(
        _merge_kernel, grid=(m // tm,),
        in_specs=[row(d), const(gain), pl.BlockSpec((NSA_GROUPS, tm, NSA_W), lambda i: (0, i, 0)),
                  row(SB_W), row(SB_W), const(w_merge), const(w_a), const(w_b), const(w_c), const(w_o)],
        out_specs=row(d), out_shape=jax.ShapeDtypeStruct((m, d), F32),
        compiler_params=_cp("arbitrary"), name="merge",
    )(x, gain, o_nsa, o_sb, o_fox, w_merge, w_a, w_b, w_c, w_o)


def _merge_ffn_kernel(x_ref, g_ref, oa_ref, ob_ref, oc_ref, wm_ref, wa_ref, wb_ref, wc_ref, wo_ref,
                      g2_ref, wg_ref, wu_ref, wd_ref, gf_ref, out_ref, *, final_norm):
    x = x_ref[...]
    d = x.shape[1]
    hb = _rms(x, g_ref[...]).astype(BF16)
    oa = oa_ref[0] + oa_ref[1]
    u = jax.nn.sigmoid(_dot_nt(hb, wm_ref[0:d, :])) * _dot(oa, wa_ref[...])
    u = u + jax.nn.sigmoid(_dot_nt(hb, wm_ref[d:2 * d, :])) * _dot(ob_ref[...], wb_ref[...])
    u = u + jax.nn.sigmoid(_dot_nt(hb, wm_ref[2 * d:3 * d, :])) * _dot(oc_ref[...], wc_ref[...])
    x1 = x + _dot(u.astype(BF16), wo_ref[...])
    h2 = _rms(x1, g2_ref[...]).astype(BF16)
    a = _dot(h2, wg_ref[...])
    act = a * jax.nn.sigmoid(a) * _dot(h2, wu_ref[...])
    y = x1 + _dot(act.astype(BF16), wd_ref[...])
    out_ref[...] = _rms(y, gf_ref[...]) if final_norm else y


def _merge_ffn(x, gain, o_nsa, o_sb, o_fox, w_merge, w_a, w_b, w_c, w_o, gain2, w_gate, w_up, w_down, final_gain,
               *, tm, final_norm):
    m, d = x.shape
    row = lambda wid: pl.BlockSpec((tm, wid), lambda i: (i, 0))
    vec = pl.BlockSpec((1, d), lambda i: (0, 0))
    const = lambda a: pl.BlockSpec(a.shape, lambda i: (0,) * a.ndim, pipeline_mode=pl.Buffered(1))
    return pl.pallas_call(
        functools.partial(_merge_ffn_kernel, final_norm=final_norm), grid=(m // tm,),
        in_specs=[row(d), vec, pl.BlockSpec((NSA_GROUPS, tm, NSA_W), lambda i: (0, i, 0)), row(SB_W), row(SB_W),
                  const(w_merge), const(w_a), const(w_b), const(w_c), const(w_o),
                  vec, const(w_gate), const(w_up), const(w_down), vec],
        out_specs=row(d), out_shape=jax.ShapeDtypeStruct((m, d), F32),
        compiler_params=_cp("arbitrary"), name="merge_ffn",
    )(x, gain, o_nsa, o_sb, o_fox, w_merge, w_a, w_b, w_c, w_o, gain2, w_gate, w_up, w_down, final_gain)


def _ffn_kernel(x_ref, g_ref, wg_ref, wu_ref, wd_ref, gf_ref, out_ref, *, final_norm):
    x = x_ref[...]
    hb = _rms(x, g_ref[...]).astype(BF16)
    a = _dot(hb, wg_ref[...])
    act = a * jax.nn.sigmoid(a) * _dot(hb, wu_ref[...])
    y = x + _dot(act.astype(BF16), wd_ref[...])
    out_ref[...] = _rms(y, gf_ref[...]) if final_norm else y


def _ffn(x, gain, w_gate, w_up, w_down, final_gain, *, tm, final_norm):
    m, d = x.shape
    row = pl.BlockSpec((tm, d), lambda i: (i, 0))
    vec = pl.BlockSpec((1, d), lambda i: (0, 0))
    const = lambda a: pl.BlockSpec(a.shape, lambda i: (0,) * a.ndim, pipeline_mode=pl.Buffered(1))
    return pl.pallas_call(
        functools.partial(_ffn_kernel, final_norm=final_norm), grid=(m // tm,),
        in_specs=[row, vec, const(w_gate), const(w_up), const(w_down), vec],
        out_specs=row, out_shape=jax.ShapeDtypeStruct((m, d), F32),
        compiler_params=_cp("arbitrary"), name="ffn",
    )(x, gain, w_gate, w_up, w_down, final_gain)


def _page_specs(rows, n_pages, pp, descending):
    def spec(i):
        if descending:
            return pl.BlockSpec((1, rows, PAGE), lambda b, j, pt: (pt[b, n_pages - 1 - (j * pp + i)], 0, 0))
        return pl.BlockSpec((1, rows, PAGE), lambda b, j, pt: (pt[b, j * pp + i], 0, 0))
    return [spec(i) for i in range(pp)]


def _lanes(parts):
    return jnp.concatenate(parts, axis=1)


def _sb_dec_kernel(pt_ref, q_ref, *refs, pp):
    pages, (u_ref, out_ref, acc_scr, r_scr) = refs[:pp], refs[pp:]
    j = pl.program_id(1)

    @pl.when(j == 0)
    def _():
        acc_scr[...] = jnp.zeros_like(acc_scr)
        r_scr[...] = jnp.zeros_like(r_scr)

    q = q_ref[0]
    z_all = _dot(q, _lanes([p[0, 0:SB_W, :].astype(BF16) for p in pages]))
    zs = [z_all[:, i * PAGE:(i + 1) * PAGE] for i in range(pp)]
    sps = [_softplus2(z) for z in zs]
    afters = [_split_dot(sp, u_ref[...], 2) for sp in sps]
    r = r_scr[...]
    ws = []
    for i in range(pp):
        ws.append(jnp.exp2(zs[i] - sps[i] - afters[i] - r).astype(BF16))
        r = r + jnp.sum(sps[i], axis=1, keepdims=True)
    acc_scr[...] += _dot_nt(_lanes(ws), _lanes([p[0, SB_W:2 * SB_W, :].astype(BF16) for p in pages]))
    r_scr[...] = r

    @pl.when(j == pl.num_programs(1) - 1)
    def _():
        out_ref[0] = acc_scr[...]


def _sb_dec(q8, pool, page_table, *, pp):
    bd, n_pages = page_table.shape
    u = _upper_strict(PAGE)
    grid_spec = pltpu.PrefetchScalarGridSpec(
        num_scalar_prefetch=1, grid=(bd, n_pages // pp),
        in_specs=[pl.BlockSpec((1, 8, SB_W), lambda b, j, pt: (b, 0, 0))]
        + _page_specs(2 * SB_W, n_pages, pp, True) + [pl.BlockSpec(u.shape, lambda b, j, pt: (0, 0))],
        out_specs=pl.BlockSpec((1, 8, SB_W), lambda b, j, pt: (b, 0, 0)),
        scratch_shapes=[pltpu.VMEM((8, SB_W), F32), pltpu.VMEM((8, LANES), F32)])
    return pl.pallas_call(
        functools.partial(_sb_dec_kernel, pp=pp), grid_spec=grid_spec,
        out_shape=jax.ShapeDtypeStruct((bd, 8, SB_W), F32),
        compiler_params=_cp("arbitrary", "arbitrary"), name="sb_decode",
    )(page_table, q8, *([pool] * pp), u)


def _fox_dec_kernel(pt_ref, q_ref, kn_ref, vn_ref, lfn_ref, *refs, pp):
    pages, lf_pages = refs[:pp], refs[pp:2 * pp]
    u_ref, out_ref, m_scr, l_scr, acc_scr, r_scr = refs[2 * pp:]
    j = pl.program_id(1)
    q = q_ref[0]
    reps = SB_W // LANES

    @pl.when(j == 0)
    def _():
        kn = kn_ref[0].astype(BF16).astype(F32)
        m_scr[...] = jnp.broadcast_to(jnp.sum(q.astype(F32) * kn, axis=1, keepdims=True), m_scr.shape)
        l_scr[...] = jnp.ones_like(l_scr)
        acc_scr[...] = jnp.broadcast_to(vn_ref[0].astype(BF16).astype(F32), acc_scr.shape)
        r_scr[...] = lfn_ref[0]

    z_all = _dot(q, _lanes([p[0, 0:SB_W, :].astype(BF16) for p in pages]))
    pad = jnp.zeros((8 - FOX_HEADS, PAGE), F32)
    r = r_scr[...]
    ss = []
    for i in range(pp):
        lf = jnp.concatenate([lf_pages[i][0], pad], axis=0)
        ss.append(z_all[:, i * PAGE:(i + 1) * PAGE] + LOG2E * (_split_dot(lf, u_ref[...], 3) + r))
        r = r + jnp.sum(lf, axis=1, keepdims=True)
    r_scr[...] = r
    m_prev = m_scr[...]
    m_next = m_prev
    for s in ss:
        m_next = jnp.maximum(m_next, jnp.max(s, axis=1, keepdims=True))
    alpha = jnp.exp2(m_prev - m_next)
    es = [jnp.exp2(s - m_next) for s in ss]
    l_new = alpha * l_scr[...]
    for e in es:
        l_new = l_new + jnp.sum(e, axis=1, keepdims=True)
    l_scr[...] = l_new
    m_scr[...] = m_next
    acc_scr[...] = acc_scr[...] * _rep(alpha, reps) + _dot_nt(
        _lanes([e.astype(BF16) for e in es]), _lanes([p[0, SB_W:2 * SB_W, :].astype(BF16) for p in pages]))

    @pl.when(j == pl.num_programs(1) - 1)
    def _():
        out_ref[0] = acc_scr[...] / _rep(l_scr[...], reps)


def _fox_dec(q8, k_new, v_new, lf_new, pool, lf_pool, page_table, *, pp):
    bd, n_pages = page_table.shape
    u = _upper_strict(PAGE)
    req = lambda shape: pl.BlockSpec((1,) + shape, lambda b, j, pt: (b,) + (0,) * len(shape))
    lf_spec = lambda i: pl.BlockSpec((1, FOX_HEADS, PAGE), lambda b, j, pt: (pt[b, n_pages - 1 - (j * pp + i)], 0, 0))
    grid_spec = pltpu.PrefetchScalarGridSpec(
        num_scalar_prefetch=1, grid=(bd, n_pages // pp),
        in_specs=[req((8, SB_W)), req((1, SB_W)), req((1, SB_W)), req((8, LANES))]
        + _page_specs(2 * SB_W, n_pages, pp, True) + [lf_spec(i) for i in range(pp)]
        + [pl.BlockSpec(u.shape, lambda b, j, pt: (0, 0))],
        out_specs=req((8, SB_W)),
        scratch_shapes=[pltpu.VMEM((8, LANES), F32), pltpu.VMEM((8, LANES), F32),
                        pltpu.VMEM((8, SB_W), F32), pltpu.VMEM((8, LANES), F32)])
    return pl.pallas_call(
        functools.partial(_fox_dec_kernel, pp=pp), grid_spec=grid_spec,
        out_shape=jax.ShapeDtypeStruct((bd, 8, SB_W), F32),
        compiler_params=_cp("arbitrary", "arbitrary"), name="fox_decode",
    )(page_table, q8, k_new, v_new, lf_new, *([pool] * pp), *([lf_pool] * pp), u)


def _nsa_dec_kernel(pt_ref, q_ref, ckv_ref, gate_ref, win_ref, kvs_new_ref, kvw_new_ref, kvw_col_ref, amat_ref, e_ref,
                    *refs, pp, past_len, n_sel):
    pages, (out_ref, win_out_ref, m_scr, l_scr, acc_scr, sel_scr, oc_scr, ow_scr) = refs[:pp], refs[pp:]
    j = pl.program_id(1)
    q = q_ref[0]
    rows = q.shape[0]
    row = _iota((rows, LANES), 0)
    row1 = _iota((rows, 1), 0)
    lane = _iota((rows, LANES), 1)
    pos = jnp.full((rows, 1), past_len, I32)

    def new_token(kv_row):
        k = kv_row[:, 0:LANES].astype(BF16).astype(F32)
        return jnp.sum(q.astype(F32) * k, axis=1, keepdims=True), kv_row[:, LANES:2 * LANES].astype(BF16).astype(F32)

    @pl.when(j == 0)
    def _():
        nc = ckv_ref.shape[1]
        cmask = _iota((rows, nc), 1) * CMP_STRIDE + (CMP_BLOCK - 1) <= pos
        p = _masked_softmax_rows(_dot_nt(q, ckv_ref[0, :, 0:128]), cmask)
        oc_scr[...] = _dot(p.astype(BF16), ckv_ref[0, :, 128:256])
        g0 = jnp.sum(p[0:NSA_HPG], axis=0, keepdims=True)
        g1 = jnp.sum(p[NSA_HPG:2 * NSA_HPG], axis=0, keepdims=True)
        p_grp = jnp.where(row1 < NSA_HPG, g0, g1)
        imp = _split_dot(p_grp, amat_ref[...], 2)
        sel_scr[...] = _select_blocks(imp, pos, n_sel)

        wk = win_ref[0]
        n_w = wk.shape[1]
        s = _dot(q, wk[0:LANES, :].astype(BF16))
        kpos = past_len - n_w + _iota((rows, n_w), 1)
        keep = jnp.where(kpos >= 0, jnp.where(pos - kpos <= WINDOW, 1.0, 0.0), 0.0) > 0.5
        s = jnp.where(keep, s, NEG)
        s_new, v_new = new_token(kvw_new_ref[0])
        m = jnp.maximum(s_new, jnp.max(s, axis=1, keepdims=True))
        e = jnp.exp2(s - m)
        e_new = jnp.exp2(s_new - m)
        o_w = _dot_nt(e.astype(BF16), wk[LANES:2 * LANES, :].astype(BF16)) + e_new * v_new
        ow_scr[...] = o_w / (jnp.sum(e, axis=1, keepdims=True) + e_new)

        shifted = pltpu.roll(wk, n_w - 1, 1)
        win_out_ref[0] = jnp.where(_iota(wk.shape, 1) == n_w - 1, kvw_col_ref[0], shifted)
        _online_init(m_scr, l_scr, acc_scr)

    sel = sel_scr[...].astype(BF16)
    z_all = _dot(q, _lanes([p[0, 0:LANES, :].astype(BF16) for p in pages]))
    ss = []
    for i in range(pp):
        chosen = _dot(sel, e_ref[j * pp + i]) > 0.5
        ss.append(jnp.where(chosen, z_all[:, i * PAGE:(i + 1) * PAGE], NEG))
    m_prev = m_scr[...]
    m_next = m_prev
    for s in ss:
        m_next = jnp.maximum(m_next, jnp.max(s, axis=1, keepdims=True))
    alpha = jnp.exp2(m_prev - m_next)
    es = [jnp.exp2(s - m_next) for s in ss]
    l_new = alpha * l_scr[...]
    for e in es:
        l_new = l_new + jnp.sum(e, axis=1, keepdims=True)
    l_scr[...] = l_new
    m_scr[...] = m_next
    acc_scr[...] = acc_scr[...] * alpha + _dot_nt(_lanes([e.astype(BF16) for e in es]),
                                                  _lanes([p[0, LANES:2 * LANES, :].astype(BF16) for p in pages]))

    @pl.when(j == pl.num_programs(1) - 1)
    def _():
        s_new, v_new = new_token(kvs_new_ref[0])
        cur = past_len // SEL_BLOCK
        chosen = jnp.sum(jnp.where(_iota(sel_scr.shape, 1) == cur, sel_scr[...], 0.0), axis=1, keepdims=True)
        s_new = jnp.where(chosen > 0.5, s_new, NEG)
        m_prev = m_scr[...]
        m_next = jnp.maximum(m_prev, s_new)
        alpha = jnp.exp2(m_prev - m_next)
        e_new = jnp.exp2(s_new - m_next)
        o_s = (acc_scr[...] * alpha + e_new * v_new) / jnp.maximum(alpha * l_scr[...] + e_new, 1e-30)
        gates = jnp.where(row < NSA_HPG, gate_ref[0, 0:1, :], gate_ref[0, 1:2, :])
        hh = row % NSA_HPG

        def gate(c):
            return jnp.sum(jnp.where(lane == 3 * hh + c, gates, 0.0), axis=1, keepdims=True)

        out_ref[0] = gate(0) * oc_scr[...] + gate(1) * o_s + gate(2) * ow_scr[...]


def _nsa_dec(q8, ckv, gates, win_state, kvs_new, kvw_new, pool, page_table, *, pp, layer):
    bd, n_pages = page_table.shape
    past_len = n_pages * PAGE
    n_sel = -(-(past_len + 1) // SEL_BLOCK)
    nc = ckv.shape[1]
    wid = -(-n_sel // LANES) * LANES
    amat = _imp_matrix(nc, wid)
    etile = _expand_tiles(n_pages, wid, PAGE)
    n_w = win_state.shape[2]
    req = lambda shape: pl.BlockSpec((1,) + shape, lambda b, j, pt: (b,) + (0,) * len(shape))
    const = lambda a: pl.BlockSpec(a.shape, lambda b, j, pt: (0,) * a.ndim)
    win_spec = pl.BlockSpec((1, KV_W, n_w), lambda b, j, pt: (layer * bd + b, 0, 0))
    grid_spec = pltpu.PrefetchScalarGridSpec(
        num_scalar_prefetch=1, grid=(bd, n_pages // pp),
        in_specs=[req((8, LANES)), req((nc, KV_W)), req((NSA_GROUPS, LANES)), win_spec,
                  req((1, KV_W)), req((1, KV_W)), req((KV_W, 1)), const(amat), const(etile)]
        + _page_specs(KV_W, n_pages, pp, False),
        out_specs=(req((8, LANES)), req((KV_W, n_w))),
        scratch_shapes=[pltpu.VMEM((8, LANES), F32)] * 3 + [pltpu.VMEM((8, wid), F32)] + [pltpu.VMEM((8, LANES), F32)] * 2)
    return pl.pallas_call(
        functools.partial(_nsa_dec_kernel, pp=pp, past_len=past_len, n_sel=n_sel), grid_spec=grid_spec,
        out_shape=(jax.ShapeDtypeStruct((bd, 8, LANES), F32), jax.ShapeDtypeStruct((bd, KV_W, n_w), F32)),
        compiler_params=_cp("arbitrary", "arbitrary"), name="nsa_decode",
    )(page_table, q8, ckv, gates, win_state, kvs_new, kvw_new, kvw_new.reshape(bd, KV_W, 1), amat, etile,
      *([pool] * pp))


def _rope_table(pos):
    half = ROT_DIM // 2
    inv = ROPE_THETA ** (-jnp.arange(half, dtype=F32) / half)
    ang = pos.astype(F32)[:, None] * inv[None, :]
    cos, sin = jnp.cos(ang), jnp.sin(ang)
    n = pos.shape[0]
    pad = HEAD_DIM - ROT_DIM
    one_head = lambda a, b, fill: jnp.concatenate([a, b, jnp.full((n, pad), fill, F32)], axis=1)
    zero = jnp.zeros_like(sin)
    tabs = [one_head(cos, cos, 1.0), one_head(-sin, zero, 0.0), one_head(zero, sin, 0.0)]
    return jnp.concatenate([jnp.tile(t, (1, LANES // HEAD_DIM)) for t in tabs], axis=1)


_Q_HEAD_ORDER = (0, 4, 1, 5, 2, 6, 3, 7)


def _layer_weights(l, w_in, b_forget, cmp_pos, w_cmp_k, w_cmp_v, w_up_nsa):
    w = jnp.swapaxes(w_in, 1, 2)[l]
    d = w.shape[1]
    gate0 = NSA_W + 3 * KV_W
    pf0 = gate0 + GATE_W + 6 * SB_W
    merge0 = pf0 + FOX_HEADS
    rows_ = [w[HEAD_DIM * h:HEAD_DIM * (h + 1)] for h in _Q_HEAD_ORDER]
    rows_ += [w[NSA_W:gate0], w[gate0 + GATE_W:pf0], w[gate0:gate0 + GATE_W], w[pf0:merge0]]
    rows_.append(jnp.zeros((N_PROJ - merge0, d), F32))
    w_proj = jnp.concatenate(rows_, axis=0).astype(BF16)
    w_merge = w[merge0:].astype(BF16)
    zpad = lambda n: jnp.zeros((1, n), F32)
    bias = jnp.concatenate([zpad(LOGF_LANE), b_forget[l].reshape(1, FOX_HEADS), zpad(LANES - LOGF_LANE - FOX_HEADS)], axis=1)
    wk = w_cmp_k[l].reshape(CMP_BLOCK, HEAD_DIM, HEAD_DIM)
    wv = w_cmp_v[l].reshape(CMP_BLOCK, HEAD_DIM, HEAD_DIM)
    zero = jnp.zeros_like(wk)
    w_blk = jnp.concatenate([jnp.concatenate([wsrc if c == n else zero for c in range(4)], axis=2)
                             for n, wsrc in enumerate((wk, wk, wv, wv))], axis=1)
    w_cat = jnp.concatenate([w_blk[0:CMP_BLOCK // 2], w_blk[CMP_BLOCK // 2:]], axis=2).astype(BF16)
    pe4 = jnp.tile(cmp_pos[l], (1, KV_W // HEAD_DIM))
    w_a = jnp.concatenate([w_up_nsa[l][HEAD_DIM * h:HEAD_DIM * (h + 1)] for h in _Q_HEAD_ORDER], axis=0).astype(BF16)
    return w_proj, w_merge, bias, w_cat, pe4, w_a


def _pad_heads(a):
    return jnp.concatenate([a, jnp.zeros((a.shape[0], 8 - a.shape[1], a.shape[2]), a.dtype)], axis=1)


def _token_lanes(cache):
    l, p, t = cache.shape[:3]
    return cache.transpose(0, 1, 3, 4, 5, 2).reshape(l * p, -1, t)


def _own_lanes(o8, n_heads):
    n = o8.shape[0]
    o = o8[:, 0:n_heads].reshape(n, n_heads, n_heads, HEAD_DIM)
    return jnp.stack([o[:, h, h] for h in range(n_heads)], axis=1).reshape(n, n_heads * HEAD_DIM)


def kernel(x_prompt, x_sample, cache_nsa_cmp_kv, cache_nsa_sel_kv, cache_sb_kv, cache_fox_kv, cache_fox_logf,
           state_nsa_win_kv, page_table, norm_mix_g, norm_ffn_g, norm_final_g, w_in, b_forget, cmp_pos,
           w_cmp_k, w_cmp_v, w_up_nsa, w_up_sb, w_up_fox, w_out, w_ffn_gate, w_ffn_up, w_ffn_down):
    batch, seq, d = x_prompt.shape
    bd = x_sample.shape[0]
    depth = w_in.shape[0]
    n_pool = cache_sb_kv.shape[1]
    n_pages = page_table.shape[1]
    past_len = n_pages * PAGE
    m = batch * seq
    tm, tq, tq_soft = min(MATMUL_ROWS, seq), min(STICK_TILE, seq), min(SOFTMAX_TILE, seq)
    pp = min(PAGES_PER_STEP, n_pages)
    pp_prompt = min(PAGES_PER_STEP, seq // PAGE)

    rope_p = _rope_table(jnp.arange(seq, dtype=I32))
    rope_s = _rope_table(jnp.full((bd,), past_len, I32))
    ident_pt = jnp.arange(batch * (seq // PAGE), dtype=I32).reshape(batch, seq // PAGE)
    pool_cmp, pool_sel = _token_lanes(cache_nsa_cmp_kv), _token_lanes(cache_nsa_sel_kv)
    pool_sb, pool_fox = _token_lanes(cache_sb_kv), _token_lanes(cache_fox_kv)
    pool_lf = cache_fox_logf.astype(F32).transpose(0, 1, 3, 2).reshape(depth * n_pool, FOX_HEADS, PAGE)
    win_state = _token_lanes(state_nsa_win_kv)

    xp = x_prompt.reshape(m, d)
    xs = x_sample.reshape(bd, d)
    g_fin = norm_final_g.reshape(1, d)
    p_states, s_states = [], []
    for l in range(depth):
        w_proj, w_merge, bias, w_blk, pe4, w_a = _layer_weights(l, w_in, b_forget, cmp_pos, w_cmp_k, w_cmp_v, w_up_nsa)
        w_b, w_c, w_o = w_up_sb[l].astype(BF16), w_up_fox[l].astype(BF16), w_out[l].astype(BF16)
        w_g, w_u, w_d = w_ffn_gate[l].astype(BF16), w_ffn_up[l].astype(BF16), w_ffn_down[l].astype(BF16)
        g_mix, g_ffn = norm_mix_g[l].reshape(1, d), norm_ffn_g[l].reshape(1, d)

        (qn, kv_cmp, kv_sel, kv_win, kv_sb, kv_fox, kb_sel, kb_win, kb_sb, kb_fox, q_sb, q_fox, gates, logf, cum
         ) = _proj(xp, g_mix, w_proj, rope_p, bias, tm=tm, tiles_per_seq=seq // tm, feature_states=True)
        ckv = _compress(kv_cmp, ident_pt, w_blk, pe4, pp=pp_prompt, paged=False)
        o_nsa = _nsa_prompt(qn, ckv, kb_sel, kb_win, gates, batch=batch, seq=seq, tq=tq_soft)
        o_sb = _sb_prompt(q_sb, kb_sb, batch=batch, seq=seq, tq=tq)
        cum4 = cum[:, LOGF_LANE:LOGF_LANE + FOX_HEADS]
        cum_k = cum4.reshape(batch, seq, FOX_HEADS).transpose(0, 2, 1).reshape(
            batch, FOX_HEADS, seq // tq_soft, 1, tq_soft)
        o_fox = _fox_prompt(q_fox, kb_fox, cum, cum_k, batch=batch, seq=seq, tq=tq_soft)
        xp = _merge_ffn(xp, g_mix, o_nsa, o_sb, o_fox, w_merge, w_a, w_b, w_c, w_o, g_ffn, w_g, w_u, w_d, g_fin,
                        tm=tm, final_norm=(l == depth - 1))
        win_keep = min(WINDOW, seq)
        p_states.append((kv_cmp, kv_sel, kv_sb, kv_fox, logf, kv_win[:, :, seq - win_keep:]))

        (qn, kv_cmp, kv_sel, kv_win, kv_sb, kv_fox, _, _, _, _, q_sb, q_fox, gates, logf, _
         ) = _proj(xs, g_mix, w_proj, rope_s, bias, tm=bd, tiles_per_seq=1, feature_states=False)
        pt_l = page_table + l * n_pool
        ckv = _compress(pool_cmp, pt_l, w_blk, pe4, pp=pp, paged=True)
        q8 = qn.transpose(2, 0, 1, 3).reshape(bd, NSA_HEADS, LANES)
        o_nsa8, win_new = _nsa_dec(q8, ckv, gates.transpose(1, 0, 2), win_state, kv_sel.reshape(bd, 1, KV_W),
                                   kv_win.reshape(bd, 1, KV_W), pool_sel, pt_l, pp=pp, layer=l)
        o_sb8 = _sb_dec(_pad_heads(q_sb.transpose(1, 0, 2)), pool_sb, pt_l, pp=pp)
        lf4 = logf[:, LOGF_LANE:LOGF_LANE + FOX_HEADS]
        lf_new = _pad_heads(jnp.broadcast_to(lf4[:, :, None], (bd, FOX_HEADS, LANES)))
        o_fox8 = _fox_dec(_pad_heads(q_fox.transpose(1, 0, 2)), kv_fox[:, 0:SB_W].reshape(bd, 1, SB_W),
                          kv_fox[:, SB_W:2 * SB_W].reshape(bd, 1, SB_W), lf_new, pool_fox, pool_lf, pt_l, pp=pp)
        o_a = o_nsa8.reshape(bd, NSA_GROUPS, NSA_HPG, NSA_GROUPS, HEAD_DIM)
        o_a = jnp.stack([o_a[:, g, :, g] for g in range(NSA_GROUPS)], axis=2).reshape(bd, NSA_W)
        o_a = jnp.stack([o_a, jnp.zeros_like(o_a)], axis=0).astype(BF16)
        o_b = _own_lanes(o_sb8, SB_HEADS).astype(BF16)
        o_c = _own_lanes(o_fox8, FOX_HEADS).astype(BF16)
        xs = _merge(xs, g_mix, o_a, o_b, o_c, w_merge, w_a, w_b, w_c, w_o, tm=bd)
        xs = _ffn(xs, g_ffn, w_g, w_u, w_d, g_fin, tm=bd, final_norm=(l == depth - 1))
        s_states.append((
            kv_cmp.reshape(bd, 1, 2, NSA_GROUPS, HEAD_DIM), kv_sel.reshape(bd, 1, 2, NSA_GROUPS, HEAD_DIM),
            kv_sb.reshape(bd, 1, 2, SB_HEADS, HEAD_DIM), kv_fox.reshape(bd, 1, 2, FOX_HEADS, HEAD_DIM),
            lf4.reshape(bd, 1, FOX_HEADS),
            win_new.reshape(bd, 2, NSA_GROUPS, HEAD_DIM, -1).transpose(0, 4, 1, 2, 3)))

    y_prompt = xp.reshape(batch, seq, d)
    y_sample = xs.reshape(bd, 1, d)
    stk = lambda sts, i: jnp.stack([st[i] for st in sts], axis=0)

    def kv_state(i, heads):
        a = stk(p_states, i)
        return a.reshape(depth, batch, 2, heads, HEAD_DIM, a.shape[-1]).transpose(0, 1, 5, 2, 3, 4)

    p_out = (kv_state(0, NSA_GROUPS), kv_state(1, NSA_GROUPS), kv_state(2, SB_HEADS), kv_state(3, FOX_HEADS),
             stk(p_states, 4).transpose(0, 1, 3, 2), kv_state(5, NSA_GROUPS))
    return (y_prompt, y_sample) + p_out + tuple(stk(s_states, i) for i in range(6))
```
